```python
import math
import jax, jax.numpy as jnp
from jax import lax
import numpy as np

D_MODEL = 2048
BATCH = 4
SEQ = 2048
DEPTH = 1
DEC_BATCH = 8
DEC_SEQ = 32
PAST_LEN = 4096

CHUNK = 64
D_RNN = D_MODEL
N_RNN_BLOCKS = 16
RNN_BLOCK = D_RNN // N_RNN_BLOCKS
CONV_W = 4
LRU_C = 8.0
N_HEADS = 16
N_KV_HEADS = 4
HEAD_DIM = D_MODEL // N_HEADS
GQA_GROUP = N_HEADS // N_KV_HEADS
Q_DIM = N_HEADS * HEAD_DIM
KV_DIM = N_KV_HEADS * HEAD_DIM
WINDOW = 128
WIN_CHUNKS = WINDOW // CHUNK
CACHE_WIN = min(WINDOW, PAST_LEN)
N_BUCKETS = 32
MAX_DISTANCE = 128
N_GROUPS = 4
E_PER_GROUP = 8
N_EXPERTS = N_GROUPS * E_PER_GROUP
TOP_K = 2
D_EXPERT = 512
EPS = 1e-6
NEG_INF = -1e30
SPLITS = [D_RNN, 2 * D_RNN, 2 * D_RNN + Q_DIM, 2 * D_RNN + Q_DIM + KV_DIM,
          2 * D_RNN + Q_DIM + 2 * KV_DIM, 2 * D_RNN + Q_DIM + 2 * KV_DIM + D_MODEL]
D_IN = 2 * D_RNN + Q_DIM + 2 * KV_DIM + 2 * D_MODEL

kernel_name = 'hybrid_rglru_swa_hmoe_stream_step'


def rms_norm(x, g):
    xf = x.astype(jnp.float32)
    y = xf * lax.rsqrt(jnp.mean(xf * xf, axis=-1, keepdims=True) + EPS)
    return (y * g.astype(jnp.float32)).astype(x.dtype)


def modulate(n, shift, scale):
    return n * (1 + scale[:, None, :]) + shift[:, None, :]


def t5_bucket(rel):
    nb = N_BUCKETS // 2
    ret = jnp.where(rel > 0, nb, 0)
    n = jnp.abs(rel)
    max_exact = nb // 2
    nf = jnp.maximum(n, 1).astype(jnp.float32)
    large = max_exact + (jnp.log(nf / max_exact) / math.log(MAX_DISTANCE / max_exact)
                         * (nb - max_exact)).astype(jnp.int32)
    large = jnp.minimum(large, nb - 1)
    return ret + jnp.where(n < max_exact, n, large)


def rel_bias(table, q_pos, k_pos):
    b = t5_bucket(k_pos[None, :] - q_pos[:, None])
    bias = jnp.transpose(table.astype(jnp.float32)[b], (2, 0, 1))
    return bias.reshape(N_KV_HEADS, GQA_GROUP, q_pos.shape[0], k_pos.shape[0])


def sink_softmax(s, sinks):
    sk = sinks.astype(jnp.float32).reshape(N_KV_HEADS, GQA_GROUP)[:, :, None, None]
    m = jnp.maximum(jnp.max(s, axis=-1, keepdims=True), sk)
    p = jnp.exp(s - m)
    return p / (jnp.sum(p, axis=-1, keepdims=True) + jnp.exp(sk - m))


def band_attention(q, k, v, sinks, table):
    B, S = q.shape[:2]
    nc = S // CHUNK
    span = (WIN_CHUNKS + 1) * CHUNK
    qc = q.reshape(B, nc, CHUNK, N_KV_HEADS, GQA_GROUP, HEAD_DIM)

    def band(t):
        tc = t.reshape(B, nc, CHUNK, N_KV_HEADS, HEAD_DIM)
        tp = jnp.pad(tc, ((0, 0), (WIN_CHUNKS, 0), (0, 0), (0, 0), (0, 0)))
        return jnp.concatenate([tp[:, j:j + nc] for j in range(WIN_CHUNKS + 1)], axis=2)

    kb, vb = band(k), band(v)
    s = jnp.einsum('bcqhgd,bckhd->bchgqk', qc, kb, preferred_element_type=jnp.float32) * (HEAD_DIM ** -0.5)
    q_pos = WIN_CHUNKS * CHUNK + jnp.arange(CHUNK)
    k_pos = jnp.arange(span)
    s = s + rel_bias(table, q_pos, k_pos)[None, None]
    key_chunk = jnp.arange(nc)[:, None] - WIN_CHUNKS + (jnp.arange(span) // CHUNK)[None, :]
    s = jnp.where((key_chunk >= 0)[None, :, None, None, None, :], s, NEG_INF)
    p = sink_softmax(s, sinks)
    o = jnp.einsum('bchgqk,bckhd->bcqhgd', p.astype(vb.dtype), vb)
    return o.reshape(B, S, Q_DIM)


def window_attention_step(q, k, v, k_prev, v_prev, sinks, table):
    B, L = q.shape[:2]
    cw = k_prev.shape[1]
    k_all = jnp.concatenate([k_prev.astype(k.dtype), k], axis=1)
    v_all = jnp.concatenate([v_prev.astype(v.dtype), v], axis=1)
    q_pos = PAST_LEN + jnp.arange(L)
    k_pos = jnp.concatenate([PAST_LEN - cw + jnp.arange(cw), q_pos])
    qg = q.reshape(B, L, N_KV_HEADS, GQA_GROUP, HEAD_DIM)
    s = jnp.einsum('bqhgd,bkhd->bhgqk', qg, k_all, preferred_element_type=jnp.float32) * (HEAD_DIM ** -0.5)
    s = s + rel_bias(table, q_pos, k_pos)[None]
    p = sink_softmax(s, sinks)
    o = jnp.einsum('bhgqk,bkhd->bqhgd', p.astype(v_all.dtype), v_all).reshape(B, L, Q_DIM)
    return o, k_all[:, -cw:], v_all[:, -cw:]


def causal_conv(ext, w, b):
    L = ext.shape[1] - (CONV_W - 1)
    return sum(ext[:, j:j + L] * w[j] for j in range(CONV_W)) + b


def rglru(xc, h0, w_a, b_a, w_x, b_x, lam):
    B, L, _ = xc.shape
    xf = xc.astype(jnp.float32)
    xb = xf.reshape(B, L, N_RNN_BLOCKS, RNN_BLOCK)
    r = jax.nn.sigmoid(jnp.einsum('blnd,nde->blne', xb, w_a.astype(jnp.float32)) + b_a.astype(jnp.float32))
    i = jax.nn.sigmoid(jnp.einsum('blnd,nde->blne', xb, w_x.astype(jnp.float32)) + b_x.astype(jnp.float32))
    r = r.reshape(B, L, D_RNN)
    i = i.reshape(B, L, D_RNN)
    log_a = -LRU_C * r * jax.nn.softplus(-lam.astype(jnp.float32))
    a = jnp.exp(log_a)
    u = jnp.sqrt(-jnp.expm1(2.0 * log_a)) * (i * xf)

    def step(h, au):
        a_t, u_t = au
        h = a_t * h + u_t
        return h, h

    h_last, hs = lax.scan(step, h0.astype(jnp.float32), (jnp.swapaxes(a, 0, 1), jnp.swapaxes(u, 0, 1)))
    return jnp.swapaxes(hs, 0, 1), h_last


def mixer(n, conv_prev, h_prev, k_prev, v_prev, p, rel_table):
    B, L, _ = n.shape
    z = jnp.einsum('bld,de->ble', n, p['w_in']) + p['b_in']
    xr, gr, q, k, v, ga, gb = jnp.split(z, SPLITS, axis=-1)
    ext = jnp.concatenate([conv_prev.astype(xr.dtype), xr], axis=1)
    xc = causal_conv(ext, p['w_conv'], p['b_conv'])
    hs, h_last = rglru(xc, h_prev, p['w_rg_a'], p['b_rg_a'], p['w_rg_x'], p['b_rg_x'], p['lru_lambda'])
    y_a = jnp.einsum('blc,cd->bld', hs.astype(n.dtype) * jax.nn.gelu(gr), p['w_rnn_out'])
    q = q.reshape(B, L, N_HEADS, HEAD_DIM)
    k = k.reshape(B, L, N_KV_HEADS, HEAD_DIM)
    v = v.reshape(B, L, N_KV_HEADS, HEAD_DIM)
    if k_prev is None:
        o = band_attention(q, k, v, p['attn_sinks'], rel_table)
        k_new, v_new = k[:, -min(WINDOW, L):], v[:, -min(WINDOW, L):]
    else:
        o, k_new, v_new = window_attention_step(q, k, v, k_prev, v_prev, p['attn_sinks'], rel_table)
    y_b = jnp.einsum('blc,cd->bld', o, p['w_attn_out'])
    merged = jax.nn.sigmoid(ga) * y_a + jax.nn.sigmoid(gb) * y_b
    out = jnp.einsum('bld,de->ble', merged, p['w_out'])
    return out, (ext[:, -(CONV_W - 1):], h_last, k_new, v_new)


def hier_moe(n, p):
    B, L, D = n.shape
    t = n.reshape(B * L, D)
    T = B * L
    gl = (t @ p['w_route_group'] + p['b_route_group']).astype(jnp.float32)
    gp = jax.nn.softmax(gl, axis=-1)
    _, g_idx = lax.top_k(gl, 1)
    g_w = jnp.take_along_axis(gp, g_idx, axis=-1)
    el = (t @ p['w_route_expert'] + p['b_route_expert']).astype(jnp.float32).reshape(T, N_GROUPS, E_PER_GROUP)
    el_g = jnp.take_along_axis(el, jnp.broadcast_to(g_idx[:, :, None], (T, 1, E_PER_GROUP)), axis=1)[:, 0]
    top_v, top_i = lax.top_k(el_g, TOP_K)
    top_w = jax.nn.softmax(top_v, axis=-1) * g_w
    expert_id = g_idx * E_PER_GROUP + top_i
    gate = jnp.sum(jax.nn.one_hot(expert_id, N_EXPERTS, dtype=jnp.float32) * top_w[..., None], axis=1)
    hg = jnp.einsum('td,edf->tef', t, p['w_exp_gate'])
    hu = jnp.einsum('td,edf->tef', t, p['w_exp_up'])
    act = jax.nn.silu(hg) * hu * gate.astype(n.dtype)[..., None]
    out = jnp.einsum('tef,efd->td', act, p['w_exp_down'])
    return out.reshape(B, L, D)


def block(x, c, conv_prev, h_prev, k_prev, v_prev, p, rel_table):
    mod = jnp.einsum('bd,de->be', jax.nn.silu(c), p['w_ada']) + p['b_ada']
    sh1, sc1, g1, sh2, sc2, g2 = jnp.split(mod, 6, axis=-1)
    n1 = modulate(rms_norm(x, p['norm1_g']), sh1, sc1)
    mix, new_state = mixer(n1, conv_prev, h_prev, k_prev, v_prev, p, rel_table)
    x = x + g1[:, None, :] * mix
    n2 = modulate(rms_norm(x, p['norm2_g']), sh2, sc2)
    x = x + g2[:, None, :] * hier_moe(n2, p)
    return x, new_state


def setup_inputs(seed: int = 0) -> dict:
    key = jax.random.key(seed)
    ks = iter(jax.random.split(key, 40))
    f32 = jnp.float32

    def nrm(shape, scale):
        return jax.random.normal(next(ks), shape, f32) * scale

    a0 = jax.random.uniform(next(ks), (DEPTH, D_RNN), f32, 0.9, 0.999)
    pa = a0 ** (1.0 / LRU_C)
    lam = jnp.log(pa) - jnp.log1p(-pa)
    return {
        'x_prompt': nrm((BATCH, SEQ, D_MODEL), 1.0),
        'x_sample': nrm((DEC_BATCH, DEC_SEQ, D_MODEL), 1.0),
        'cache_k_win': nrm((DEPTH, DEC_BATCH, CACHE_WIN, N_KV_HEADS, HEAD_DIM), 1.0),
        'cache_v_win': nrm((DEPTH, DEC_BATCH, CACHE_WIN, N_KV_HEADS, HEAD_DIM), 1.0),
        'state_conv': nrm((DEPTH, DEC_BATCH, CONV_W - 1, D_RNN), 1.0),
        'state_rglru': nrm((DEPTH, DEC_BATCH, D_RNN), 0.5),
        'c_prompt': nrm((BATCH, D_MODEL), 1.0),
        'c_sample': nrm((DEC_BATCH, D_MODEL), 1.0),
        'w_ada': nrm((DEPTH, D_MODEL, 6 * D_MODEL), 0.5 * D_MODEL ** -0.5),
        'b_ada': nrm((DEPTH, 6 * D_MODEL), 0.02),
        'norm1_g': 1.0 + nrm((DEPTH, D_MODEL), 0.02),
        'norm2_g': 1.0 + nrm((DEPTH, D_MODEL), 0.02),
        'w_in': nrm((DEPTH, D_MODEL, D_IN), D_MODEL ** -0.5),
        'b_in': nrm((DEPTH, D_IN), 0.02),
        'w_conv': nrm((DEPTH, CONV_W, D_RNN), CONV_W ** -0.5),
        'b_conv': nrm((DEPTH, D_RNN), 0.02),
        'w_rg_a': nrm((DEPTH, N_RNN_BLOCKS, RNN_BLOCK, RNN_BLOCK), RNN_BLOCK ** -0.5),
        'b_rg_a': nrm((DEPTH, N_RNN_BLOCKS, RNN_BLOCK), 0.02),
        'w_rg_x': nrm((DEPTH, N_RNN_BLOCKS, RNN_BLOCK, RNN_BLOCK), RNN_BLOCK ** -0.5),
        'b_rg_x': nrm((DEPTH, N_RNN_BLOCKS, RNN_BLOCK), 0.02),
        'lru_lambda': lam,
        'w_rnn_out': nrm((DEPTH, D_RNN, D_MODEL), D_RNN ** -0.5),
        'w_attn_out': nrm((DEPTH, Q_DIM, D_MODEL), Q_DIM ** -0.5),
        'w_out': nrm((DEPTH, D_MODEL, D_MODEL), D_MODEL ** -0.5),
        'attn_sinks': nrm((DEPTH, N_HEADS), 0.5),
        'w_route_group': nrm((DEPTH, D_MODEL, N_GROUPS), D_MODEL ** -0.5),
        'b_route_group': nrm((DEPTH, N_GROUPS), 0.01),
        'w_route_expert': nrm((DEPTH, D_MODEL, N_EXPERTS), D_MODEL ** -0.5),
        'b_route_expert': nrm((DEPTH, N_EXPERTS), 0.01),
        'w_exp_gate': nrm((DEPTH, N_EXPERTS, D_MODEL, D_EXPERT), D_MODEL ** -0.5),
        'w_exp_up': nrm((DEPTH, N_EXPERTS, D_MODEL, D_EXPERT), D_MODEL ** -0.5),
        'w_exp_down': nrm((DEPTH, N_EXPERTS, D_EXPERT, D_MODEL), D_EXPERT ** -0.5),
        'rel_bias_table': nrm((N_BUCKETS, N_HEADS), 0.5),
        'final_norm_g': 1.0 + nrm((D_MODEL,), 0.02),
    }


def reference(x_prompt, x_sample, cache_k_win, cache_v_win, state_conv, state_rglru, c_prompt, c_sample,
              w_ada, b_ada, norm1_g, norm2_g, w_in, b_in, w_conv, b_conv, w_rg_a, b_rg_a, w_rg_x, b_rg_x,
              lru_lambda, w_rnn_out, w_attn_out, w_out, attn_sinks, w_route_group, b_route_group,
              w_route_expert, b_route_expert, w_exp_gate, w_exp_up, w_exp_down, rel_bias_table, final_norm_g):
    hp, hs = x_prompt, x_sample
    kp_l, vp_l, cp_l, rp_l = [], [], [], []
    ks_l, vs_l, cs_l, rs_l = [], [], [], []
    for l in range(DEPTH):
        p = {
            'w_ada': w_ada[l], 'b_ada': b_ada[l], 'norm1_g': norm1_g[l], 'norm2_g': norm2_g[l],
            'w_in': w_in[l], 'b_in': b_in[l], 'w_conv': w_conv[l], 'b_conv': b_conv[l],
            'w_rg_a': w_rg_a[l], 'b_rg_a': b_rg_a[l], 'w_rg_x': w_rg_x[l], 'b_rg_x': b_rg_x[l],
            'lru_lambda': lru_lambda[l], 'w_rnn_out': w_rnn_out[l], 'w_attn_out': w_attn_out[l],
            'w_out': w_out[l], 'attn_sinks': attn_sinks[l],
            'w_route_group': w_route_group[l], 'b_route_group': b_route_group[l],
            'w_route_expert': w_route_expert[l], 'b_route_expert': b_route_expert[l],
            'w_exp_gate': w_exp_gate[l], 'w_exp_up': w_exp_up[l], 'w_exp_down': w_exp_down[l],
        }
        conv0 = jnp.zeros((hp.shape[0], CONV_W - 1, D_RNN), hp.dtype)
        h0 = jnp.zeros((hp.shape[0], D_RNN), jnp.float32)
        hp, (cp, rp, kp, vp) = block(hp, c_prompt, conv0, h0, None, None, p, rel_bias_table)
        hs, (cs, rs, kk, vv) = block(hs, c_sample, state_conv[l], state_rglru[l],
                                     cache_k_win[l], cache_v_win[l], p, rel_bias_table)
        kp_l.append(kp); vp_l.append(vp); cp_l.append(cp); rp_l.append(rp)
        ks_l.append(kk); vs_l.append(vv); cs_l.append(cs); rs_l.append(rs)
    y_prompt = rms_norm(hp, final_norm_g)
    y_sample = rms_norm(hs, final_norm_g)
    return (y_prompt, y_sample,
            jnp.stack(kp_l), jnp.stack(vp_l), jnp.stack(cp_l), jnp.stack(rp_l),
            jnp.stack(ks_l), jnp.stack(vs_l), jnp.stack(cs_l), jnp.stack(rs_l))
```

```python
import functools
import math

import jax
import jax.numpy as jnp
from jax import lax
from jax.experimental import pallas as pl
from jax.experimental.pallas import tpu as pltpu

F32 = jnp.float32
BF16 = jnp.bfloat16

D_MODEL = 2048
D_RNN = 2048
RNN_BLOCK = 128
N_RNN_BLOCKS = D_RNN // RNN_BLOCK
CONV_W = 4
LRU_C = 8.0
N_HEADS = 16
N_KV_HEADS = 4
HEAD_DIM = 128
GQA_GROUP = N_HEADS // N_KV_HEADS
Q_DIM = N_HEADS * HEAD_DIM
KV_DIM = N_KV_HEADS * HEAD_DIM
CHUNK = 64
WINDOW = 128
WIN_CHUNKS = WINDOW // CHUNK
N_BUCKETS = 32
MAX_DISTANCE = 128
N_GROUPS = 4
E_PER_GROUP = 8
N_EXPERTS = N_GROUPS * E_PER_GROUP
D_EXPERT = 512
EPS = 1e-6
NEG_INF = -1e30
D_IN = 2 * D_RNN + Q_DIM + 2 * KV_DIM + 2 * D_MODEL

LANES = 128
SUBLANES = 8
MOD_ROWS = 32
KEY_TILE = 256
VMEM_LIMIT = 56 * 1024 * 1024

IN_TN = 512
MOE_TM = 128
FIN_TM = 256


def _sigmoid(x):
    return 1.0 / (1.0 + jnp.exp(-x))


def _gelu_tanh(x):
    return 0.5 * x * (1.0 + jnp.tanh(math.sqrt(2.0 / math.pi) * (x + 0.044715 * (x * x * x))))


def _bcast_rows(v, rows):
    n, d = v.shape
    return jnp.broadcast_to(v[:, None, :], (n, MOD_ROWS, d)).reshape(rows, d)


def _cparams(sem):
    return pltpu.CompilerParams(dimension_semantics=sem, vmem_limit_bytes=VMEM_LIMIT)


def _ada_kernel(c_ref, w_ref, b_ref, o_ref):
    c = c_ref[...]
    s = (c * _sigmoid(c)).astype(BF16)
    o_ref[...] = jnp.dot(s, w_ref[...].astype(BF16), preferred_element_type=F32) + b_ref[...]


def _ada(c_all, w_ada, b_ada):
    rows = c_all.shape[0]
    n = w_ada.shape[1]
    tn = 1024
    return pl.pallas_call(
        _ada_kernel,
        grid=(n // tn,),
        in_specs=[pl.BlockSpec((rows, D_MODEL), lambda j: (0, 0)),
                  pl.BlockSpec((D_MODEL, tn), lambda j: (0, j)),
                  pl.BlockSpec((1, tn), lambda j: (0, j))],
        out_specs=pl.BlockSpec((rows, tn), lambda j: (0, j)),
        out_shape=jax.ShapeDtypeStruct((rows, n), F32),
        compiler_params=_cparams(("arbitrary",)),
        name="ada",
    )(c_all, w_ada, b_ada.reshape(1, n))


_J_XR = (0, D_RNN // IN_TN)
_J_GR = (_J_XR[1], _J_XR[1] + D_RNN // IN_TN)
_J_Q = (_J_GR[1], _J_GR[1] + Q_DIM // IN_TN)
_J_K = (_J_Q[1], _J_Q[1] + KV_DIM // IN_TN)
_J_V = (_J_K[1], _J_K[1] + KV_DIM // IN_TN)
_J_GA = (_J_V[1], _J_V[1] + D_MODEL // IN_TN)
_J_GB = (_J_GA[1], _J_GA[1] + D_MODEL // IN_TN)


def _inproj_kernel(x_ref, sc_ref, sh_ref, g_ref, w_ref, b_ref,
                   xr_ref, gg_ref, q_ref, k_ref, v_ref, sa_ref, sb_ref, n1_scr):
    j = pl.program_id(1)
    tm = x_ref.shape[0]

    @pl.when(j == 0)
    def _():
        x = x_ref[...]
        y = x * lax.rsqrt(jnp.mean(x * x, axis=-1, keepdims=True) + EPS) * g_ref[...]
        n1 = y * (1.0 + _bcast_rows(sc_ref[...], tm)) + _bcast_rows(sh_ref[...], tm)
        n1_scr[...] = n1.astype(BF16)

    acc = jnp.dot(n1_scr[...], w_ref[...].astype(BF16), preferred_element_type=F32) + b_ref[...]

    def in_range(r):
        return (j >= r[0]) & (j < r[1])

    @pl.when(in_range(_J_XR))
    def _():
        xr_ref[...] = acc

    @pl.when(in_range(_J_GR))
    def _():
        gg_ref[...] = _gelu_tanh(acc).astype(BF16)

    @pl.when(in_range(_J_Q))
    def _():
        q_ref[...] = acc.astype(BF16)

    @pl.when(in_range(_J_K))
    def _():
        k_ref[...] = acc

    @pl.when(in_range(_J_V))
    def _():
        v_ref[...] = acc

    @pl.when(in_range(_J_GA))
    def _():
        sa_ref[...] = _sigmoid(acc).astype(BF16)

    @pl.when(in_range(_J_GB))
    def _():
        sb_ref[...] = _sigmoid(acc).astype(BF16)


def _inproj(x, sc1, sh1, norm_g, w_in, b_in, tm):
    t = x.shape[0]
    nj = D_IN // IN_TN

    def out_spec(r):
        return pl.BlockSpec((tm, IN_TN), lambda i, j: (i, jnp.clip(j, r[0], r[1] - 1) - r[0]))

    def sds(n, dt):
        return jax.ShapeDtypeStruct((t, n), dt)

    seg = tm // MOD_ROWS
    return pl.pallas_call(
        _inproj_kernel,
        grid=(t // tm, nj),
        in_specs=[pl.BlockSpec((tm, D_MODEL), lambda i, j: (i, 0)),
                  pl.BlockSpec((seg, D_MODEL), lambda i, j: (i, 0)),
                  pl.BlockSpec((seg, D_MODEL), lambda i, j: (i, 0)),
                  pl.BlockSpec((1, D_MODEL), lambda i, j: (0, 0)),
                  pl.BlockSpec((D_MODEL, IN_TN), lambda i, j: (0, j)),
                  pl.BlockSpec((1, IN_TN), lambda i, j: (0, j))],
        out_specs=[out_spec(_J_XR), out_spec(_J_GR), out_spec(_J_Q), out_spec(_J_K),
                   out_spec(_J_V), out_spec(_J_GA), out_spec(_J_GB)],
        out_shape=[sds(D_RNN, F32), sds(D_RNN, BF16), sds(Q_DIM, BF16), sds(KV_DIM, F32),
                   sds(KV_DIM, F32), sds(D_MODEL, BF16), sds(D_MODEL, BF16)],
        scratch_shapes=[pltpu.VMEM((tm, D_MODEL), BF16)],
        compiler_params=_cparams(("arbitrary", "arbitrary")),
        name="inproj",
    )(x, sc1, sh1, norm_g.reshape(1, D_MODEL), w_in, b_in.reshape(1, D_IN))


RG_CW = 256
RG_SLABS = RG_CW // LANES
N_CHAINS = SUBLANES
CHAIN_PAD = 8


def _rglru_kernel(xr_ref, gg_ref, st_ref, h0_ref, wc_ref, bc_ref, wg_ref, bg_ref, lam_ref,
                  hg_ref, hl_ref, ext_scr, a_scr, u_scr, *, n_seq, seq_len, link):
    rows = n_seq * seq_len
    cl = rows // N_CHAINS
    pitch = cl + CHAIN_PAD

    for s in range(n_seq):
        ext_scr[s, 0:SUBLANES, :] = st_ref[s]
        ext_scr[s, SUBLANES:SUBLANES + seq_len, :] = xr_ref[s * seq_len:(s + 1) * seq_len, :]

    lam = lam_ref[...]
    z = -lam
    softplus = jnp.maximum(z, 0.0) + jnp.log1p(jnp.exp(-jnp.abs(z)))
    nsl = -LRU_C * softplus

    for c in range(N_CHAINS):
        s, r0 = divmod(c * cl, seq_len)
        xc = bc_ref[...]
        for j in range(CONV_W):
            xc = xc + wc_ref[j:j + 1, :] * ext_scr[s, pl.ds(SUBLANES - (CONV_W - 1) + j + r0, cl), :]
        for sl in range(RG_SLABS):
            lanes = slice(sl * LANES, (sl + 1) * LANES)
            xb = xc[:, lanes]
            g = jnp.dot(xb.astype(BF16), wg_ref[sl].astype(BF16), preferred_element_type=F32) + bg_ref[sl]
            r = _sigmoid(g[:, :LANES])
            i = _sigmoid(g[:, LANES:])
            th = jnp.tanh(0.5 * (nsl[:, lanes] * r))
            rcp = 1.0 / (1.0 - th)
            a_scr[sl, c * pitch:c * pitch + cl, :] = (1.0 + th) * rcp
            u_scr[sl, c * pitch:c * pitch + cl, :] = (2.0 * jnp.sqrt(-th) * rcp) * (i * xb)

    def step(t, carry):
        hs, ps = carry
        new_h, new_p = [], []
        for sl in range(RG_SLABS):
            a = a_scr[sl, pl.ds(t, N_CHAINS, stride=pitch), :]
            u = u_scr[sl, pl.ds(t, N_CHAINS, stride=pitch), :]
            h = a * hs[sl] + u
            u_scr[sl, pl.ds(t, N_CHAINS, stride=pitch), :] = h
            new_h.append(h)
            if link:
                p = a * ps[sl]
                a_scr[sl, pl.ds(t, N_CHAINS, stride=pitch), :] = p
                new_p.append(p)
            else:
                new_p.append(ps[sl])
        return tuple(new_h), tuple(new_p)

    h_init = tuple(h0_ref[0, :, sl * LANES:(sl + 1) * LANES] for sl in range(RG_SLABS))
    p_init = tuple(jnp.ones((N_CHAINS, LANES), F32) for _ in range(RG_SLABS))
    h_end, p_end = lax.fori_loop(0, cl, step, (h_init, p_init), unroll=8)

    row = lax.broadcasted_iota(jnp.int32, (N_CHAINS, LANES), 0)
    for sl in range(RG_SLABS):
        lanes = slice(sl * LANES, (sl + 1) * LANES)
        if link:
            def shift_down(v):
                return jnp.where(row == 0, 0.0, pltpu.roll(v, 1, axis=0))
            hh = h_end[sl]
            for _ in range(N_CHAINS - 1):
                hh = h_end[sl] + p_end[sl] * shift_down(hh)
            carry_in = shift_down(hh)
        else:
            hh = h_end[sl]
        hl_ref[0, :, lanes] = hh
        for c in range(N_CHAINS):
            h = u_scr[sl, c * pitch:c * pitch + cl, :]
            if link:
                h = h + a_scr[sl, c * pitch:c * pitch + cl, :] * carry_in[c:c + 1, :]
            gg = gg_ref[c * cl:(c + 1) * cl, lanes].astype(F32)
            hg_ref[c * cl:(c + 1) * cl, lanes] = (h * gg).astype(BF16)


def _rglru(xr, gg, state8, h0, w_conv, b_conv, w_gate, b_gate, lam, *,
           n_blocks, n_seq, seq_len, row_block0, link):
    rows = n_seq * seq_len
    cl = rows // N_CHAINS
    ncb = D_RNN // RG_CW
    kern = functools.partial(_rglru_kernel, n_seq=n_seq, seq_len=seq_len, link=link)
    in_specs = [pl.BlockSpec((rows, RG_CW), lambda b, n: (row_block0 + b, n)),
                pl.BlockSpec((rows, RG_CW), lambda b, n: (row_block0 + b, n)),
                pl.BlockSpec((n_seq, SUBLANES, RG_CW), lambda b, n: (b, 0, n)),
                pl.BlockSpec((1, N_CHAINS, RG_CW), lambda b, n: (b, 0, n)),
                pl.BlockSpec((CONV_W, RG_CW), lambda b, n: (0, n)),
                pl.BlockSpec((1, RG_CW), lambda b, n: (0, n)),
                pl.BlockSpec((RG_SLABS, RNN_BLOCK, 2 * RNN_BLOCK), lambda b, n: (n, 0, 0)),
                pl.BlockSpec((RG_SLABS, 1, 2 * RNN_BLOCK), lambda b, n: (n, 0, 0)),
                pl.BlockSpec((1, RG_CW), lambda b, n: (0, n))]
    args = [xr, gg, state8, h0, w_conv, b_conv.reshape(1, D_RNN), w_gate, b_gate, lam.reshape(1, D_RNN)]
    return pl.pallas_call(
        kern,
        grid=(n_blocks, ncb),
        in_specs=in_specs,
        out_specs=[pl.BlockSpec((rows, RG_CW), lambda b, n: (b, n)),
                   pl.BlockSpec((1, N_CHAINS, RG_CW), lambda b, n: (b, 0, n))],
        out_shape=[jax.ShapeDtypeStruct((n_blocks * rows, D_RNN), BF16),
                   jax.ShapeDtypeStruct((n_blocks, N_CHAINS, D_RNN), F32)],
        scratch_shapes=[pltpu.VMEM((n_seq, SUBLANES + seq_len, RG_CW), F32),
                        pltpu.VMEM((RG_SLABS, N_CHAINS * (cl + CHAIN_PAD), LANES), F32),
                        pltpu.VMEM((RG_SLABS, N_CHAINS * (cl + CHAIN_PAD), LANES), F32)],
        compiler_params=_cparams(("arbitrary", "arbitrary")),
        name="rglru_link" if link else "rglru_step",
    )(*args)


def _build_bias(tbl_ref, bkt_ref, bias_scr):
    bkt = bkt_ref[...]
    for h in range(N_HEADS):
        acc = jnp.zeros(bkt.shape, F32)
        for bk in range(N_BUCKETS):
            acc = jnp.where(bkt == bk, tbl_ref[bk, h], acc)
        bias_scr[h] = acc


def _attend(q_of, kw, vw, bias_scr, sink_ref, hk, nq, key_ok, store):
    qg = jnp.concatenate([q_of(hk * GQA_GROUP + g) for g in range(GQA_GROUP)], axis=0)
    s = lax.dot_general(qg, kw, (((1,), (1,)), ((), ())), preferred_element_type=F32) * (HEAD_DIM ** -0.5)
    s = s + jnp.concatenate([bias_scr[hk * GQA_GROUP + g] for g in range(GQA_GROUP)], axis=0)
    s = jnp.where(key_ok, s, NEG_INF)
    sink = jnp.concatenate([jnp.full((nq, 1), sink_ref[hk * GQA_GROUP + g], F32) for g in range(GQA_GROUP)], axis=0)
    m = jnp.maximum(jnp.max(s, axis=-1, keepdims=True), sink)
    p = jnp.exp(s - m)
    denom = jnp.sum(p, axis=-1, keepdims=True) + jnp.exp(sink - m)
    o = jnp.dot(p.astype(BF16), vw, preferred_element_type=F32) / denom
    for g in range(GQA_GROUP):
        store(hk * GQA_GROUP + g, o[g * nq:(g + 1) * nq, :].astype(BF16))


ATT_CPB = 8
ATT_LEAD = WIN_CHUNKS * CHUNK
ATT_TAIL = KEY_TILE - (WIN_CHUNKS + 1) * CHUNK


def _attn_band_kernel(tbl_ref, sink_ref, q_ref, k_ref, v_ref, bkt_ref, o_ref, kpad, vpad, bias_scr, *, seq_len):
    b = pl.program_id(0)
    cg = pl.program_id(1)

    @pl.when((b == 0) & (cg == 0))
    def _():
        _build_bias(tbl_ref, bkt_ref, bias_scr)

    @pl.when(cg == 0)
    def _():
        for ref, pad in ((k_ref, kpad), (v_ref, vpad)):
            pad[0:ATT_LEAD, :] = jnp.zeros((ATT_LEAD, KV_DIM), BF16)
            pad[ATT_LEAD:ATT_LEAD + seq_len, :] = ref[...].astype(BF16)
            pad[ATT_LEAD + seq_len:, :] = jnp.zeros((ATT_TAIL, KV_DIM), BF16)

    kidx = lax.broadcasted_iota(jnp.int32, (1, KEY_TILE), 1)

    def chunk(c, carry):
        q0 = pl.multiple_of(c * CHUNK, CHUNK)
        start = pl.multiple_of((cg * ATT_CPB + c) * CHUNK, CHUNK)
        key_ok = (start + kidx >= ATT_LEAD) & (kidx < (WIN_CHUNKS + 1) * CHUNK)
        for hk in range(N_KV_HEADS):
            cols = slice(hk * HEAD_DIM, (hk + 1) * HEAD_DIM)
            kw = kpad[pl.ds(start, KEY_TILE), cols]
            vw = vpad[pl.ds(start, KEY_TILE), cols]

            def q_of(h):
                return q_ref[pl.ds(q0, CHUNK), h * HEAD_DIM:(h + 1) * HEAD_DIM]

            def store(h, val):
                o_ref[pl.ds(q0, CHUNK), h * HEAD_DIM:(h + 1) * HEAD_DIM] = val

            _attend(q_of, kw, vw, bias_scr, sink_ref, hk, CHUNK, key_ok, store)
        return carry

    lax.fori_loop(0, ATT_CPB, chunk, 0)


def _attn_band(q, k, v, table, sinks, bkt, *, n_batch, seq_len):
    t = n_batch * seq_len
    rows = ATT_CPB * CHUNK
    ng = seq_len // rows
    kern = functools.partial(_attn_band_kernel, seq_len=seq_len)
    smem = pl.BlockSpec(memory_space=pltpu.SMEM)
    return pl.pallas_call(
        kern,
        grid=(n_batch, ng),
        in_specs=[smem, smem,
                  pl.BlockSpec((rows, Q_DIM), lambda b, c: (b * ng + c, 0)),
                  pl.BlockSpec((seq_len, KV_DIM), lambda b, c: (b, 0)),
                  pl.BlockSpec((seq_len, KV_DIM), lambda b, c: (b, 0)),
                  pl.BlockSpec((CHUNK, KEY_TILE), lambda b, c: (0, 0))],
        out_specs=pl.BlockSpec((rows, Q_DIM), lambda b, c: (b * ng + c, 0)),
        out_shape=jax.ShapeDtypeStruct((t, Q_DIM), BF16),
        scratch_shapes=[pltpu.VMEM((ATT_LEAD + seq_len + ATT_TAIL, KV_DIM), BF16),
                        pltpu.VMEM((ATT_LEAD + seq_len + ATT_TAIL, KV_DIM), BF16),
                        pltpu.VMEM((N_HEADS, CHUNK, KEY_TILE), F32)],
        compiler_params=_cparams(("arbitrary", "arbitrary")),
        name="attn_band",
    )(table, sinks, q, k, v, bkt)


def _attn_step_kernel(tbl_ref, sink_ref, q_ref, k_ref, v_ref, bkt_ref, o_ref, kbuf, vbuf, bias_scr,
                      *, n_keys, n_q):
    b = pl.program_id(0)

    @pl.when(b == 0)
    def _():
        _build_bias(tbl_ref, bkt_ref, bias_scr)
        kbuf[n_keys:, :] = jnp.zeros((KEY_TILE - n_keys, KV_DIM), BF16)
        vbuf[n_keys:, :] = jnp.zeros((KEY_TILE - n_keys, KV_DIM), BF16)

    kbuf[0:n_keys, :] = k_ref[0].astype(BF16)
    vbuf[0:n_keys, :] = v_ref[0].astype(BF16)
    key_ok = lax.broadcasted_iota(jnp.int32, (1, KEY_TILE), 1) < n_keys
    for hk in range(N_KV_HEADS):
        cols = slice(hk * HEAD_DIM, (hk + 1) * HEAD_DIM)

        def q_of(h):
            return q_ref[:, h * HEAD_DIM:(h + 1) * HEAD_DIM]

        def store(h, val):
            o_ref[:, h * HEAD_DIM:(h + 1) * HEAD_DIM] = val

        _attend(q_of, kbuf[:, cols], vbuf[:, cols], bias_scr, sink_ref, hk, n_q, key_ok, store)


def _attn_step(q, k_all, v_all, table, sinks, bkt, *, n_batch, n_q, row_block0):
    n_keys = k_all.shape[1]
    kern = functools.partial(_attn_step_kernel, n_keys=n_keys, n_q=n_q)
    smem = pl.BlockSpec(memory_space=pltpu.SMEM)
    return pl.pallas_call(
        kern,
        grid=(n_batch,),
        in_specs=[smem, smem,
                  pl.BlockSpec((n_q, Q_DIM), lambda b: (row_block0 + b, 0)),
                  pl.BlockSpec((1, n_keys, KV_DIM), lambda b: (b, 0, 0)),
                  pl.BlockSpec((1, n_keys, KV_DIM), lambda b: (b, 0, 0)),
                  pl.BlockSpec((n_q, KEY_TILE), lambda b: (0, 0))],
        out_specs=pl.BlockSpec((n_q, Q_DIM), lambda b: (b, 0)),
        out_shape=jax.ShapeDtypeStruct((n_batch * n_q, Q_DIM), BF16),
        scratch_shapes=[pltpu.VMEM((KEY_TILE, KV_DIM), BF16),
                        pltpu.VMEM((KEY_TILE, KV_DIM), BF16),
                        pltpu.VMEM((N_HEADS, n_q, KEY_TILE), F32)],
        compiler_params=_cparams(("arbitrary",)),
        name="attn_step",
    )(table, sinks, q, k_all, v_all, bkt)


def _t5_bucket(rel):
    nb = N_BUCKETS // 2
    ret = jnp.where(rel > 0, nb, 0)
    n = jnp.abs(rel)
    max_exact = nb // 2
    nf = jnp.maximum(n, 1).astype(jnp.float32)
    large = max_exact + (jnp.log(nf / max_exact) / math.log(MAX_DISTANCE / max_exact)
                         * (nb - max_exact)).astype(jnp.int32)
    large = jnp.minimum(large, nb - 1)
    return ret + jnp.where(n < max_exact, n, large)


def _bucket_map(q_pos, k_pos):
    bkt = _t5_bucket(k_pos[None, :] - q_pos[:, None]).astype(jnp.int32)
    return jnp.pad(bkt, ((0, 0), (0, KEY_TILE - k_pos.shape[0])))


OUT_TM = 256
ROUTE_LANES = LANES


def _split_bf16(x):
    hi = x.astype(BF16)
    lo = (x - hi.astype(F32)).astype(BF16)
    return hi, lo


def _outproj_kernel(hgp_ref, hgs_ref, op_ref, os_ref, sa_ref, sb_ref, x_ref, g1_ref, sh2_ref, sc2_ref, ng_ref,
                    wr_ref, wa_ref, wo_ref, wrt_ref, brt_ref, x1_ref, n2_ref, rt_ref, *, n_prompt_tiles):
    tm = x_ref.shape[0]
    is_prompt = pl.program_id(0) < n_prompt_tiles
    hg = jnp.where(is_prompt, hgp_ref[...], hgs_ref[...])
    o = jnp.where(is_prompt, op_ref[...], os_ref[...])
    ya = jnp.dot(hg, wr_ref[...], preferred_element_type=F32)
    yb = jnp.dot(o, wa_ref[...], preferred_element_type=F32)
    merged = (sa_ref[...].astype(F32) * ya + sb_ref[...].astype(F32) * yb).astype(BF16)
    mix = jnp.dot(merged, wo_ref[...], preferred_element_type=F32)
    x1 = x_ref[...] + _bcast_rows(g1_ref[...], tm) * mix
    x1_ref[...] = x1
    y = x1 * lax.rsqrt(jnp.mean(x1 * x1, axis=-1, keepdims=True) + EPS) * ng_ref[...]
    n2 = y * (1.0 + _bcast_rows(sc2_ref[...], tm)) + _bcast_rows(sh2_ref[...], tm)
    n2_ref[...] = n2

    n_hi, n_lo = _split_bf16(n2)
    w_hi, w_lo = _split_bf16(wrt_ref[...])
    lg = (jnp.dot(n_hi, w_hi, preferred_element_type=F32) + jnp.dot(n_hi, w_lo, preferred_element_type=F32)
          + jnp.dot(n_lo, w_hi, preferred_element_type=F32)) + brt_ref[...]

    lane = lax.broadcasted_iota(jnp.int32, (tm, ROUTE_LANES), 1)
    lane_f = lane.astype(F32)
    e_lane = lane - N_GROUPS
    lane_group = (e_lane >> 3).astype(F32)

    def first_argmax(vals, vmax):
        return jnp.min(jnp.where(vals == vmax, lane_f, float(ROUTE_LANES)), axis=-1, keepdims=True)

    gl = jnp.where(lane < N_GROUPS, lg, NEG_INF)
    gmax = jnp.max(gl, axis=-1, keepdims=True)
    g_idx = first_argmax(gl, gmax)
    g_w = 1.0 / jnp.sum(jnp.exp(gl - gmax), axis=-1, keepdims=True)
    in_group = jnp.where((e_lane >= 0) & (e_lane < N_EXPERTS), lane_group, -1.0) == g_idx
    el = jnp.where(in_group, lg, NEG_INF)
    v1 = jnp.max(el, axis=-1, keepdims=True)
    i1 = first_argmax(el, v1)
    el2 = jnp.where(lane_f == i1, NEG_INF, el)
    v2 = jnp.max(el2, axis=-1, keepdims=True)
    i2 = first_argmax(el2, v2)
    e21 = jnp.exp(v2 - v1)
    w1 = g_w / (1.0 + e21)
    w2 = g_w * e21 / (1.0 + e21)
    e1 = i1 - float(N_GROUPS)
    e2 = i2 - float(N_GROUPS)
    rt_ref[...] = jnp.where(lane == 0, e1, jnp.where(lane == 1, e2, jnp.where(lane == 2, w1,
                            jnp.where(lane == 3, w2, 0.0))))


def _outproj(hg_p, hg_s, o_p, o_s, sa, sb, x, g1, sh2, sc2, norm_g, wr, wa, wo, w_route, b_route):
    t = x.shape[0]
    tm = OUT_TM
    seg = tm // MOD_ROWS
    n_p = hg_p.shape[0] // tm
    row = lambda i: (i, 0)
    fix = lambda i: (0, 0)
    row_p = lambda i: (jnp.minimum(i, n_p - 1), 0)
    row_s = lambda i: (jnp.maximum(i - n_p, 0), 0)
    once = pl.Buffered(1)
    return pl.pallas_call(
        functools.partial(_outproj_kernel, n_prompt_tiles=n_p),
        grid=(t // tm,),
        in_specs=[pl.BlockSpec((tm, D_RNN), row_p), pl.BlockSpec((tm, D_RNN), row_s),
                  pl.BlockSpec((tm, Q_DIM), row_p), pl.BlockSpec((tm, Q_DIM), row_s),
                  pl.BlockSpec((tm, D_MODEL), row), pl.BlockSpec((tm, D_MODEL), row),
                  pl.BlockSpec((tm, D_MODEL), row),
                  pl.BlockSpec((seg, D_MODEL), row), pl.BlockSpec((seg, D_MODEL), row),
                  pl.BlockSpec((seg, D_MODEL), row),
                  pl.BlockSpec((1, D_MODEL), fix),
                  pl.BlockSpec((D_RNN, D_MODEL), fix, pipeline_mode=once),
                  pl.BlockSpec((Q_DIM, D_MODEL), fix, pipeline_mode=once),
                  pl.BlockSpec((D_MODEL, D_MODEL), fix, pipeline_mode=once),
                  pl.BlockSpec((D_MODEL, ROUTE_LANES), fix),
                  pl.BlockSpec((1, ROUTE_LANES), fix)],
        out_specs=[pl.BlockSpec((tm, D_MODEL), row), pl.BlockSpec((tm, D_MODEL), row),
                   pl.BlockSpec((tm, ROUTE_LANES), row)],
        out_shape=[jax.ShapeDtypeStruct((t, D_MODEL), F32), jax.ShapeDtypeStruct((t, D_MODEL), F32),
                   jax.ShapeDtypeStruct((t, ROUTE_LANES), F32)],
        compiler_params=_cparams(("arbitrary",)),
        name="outproj",
    )(hg_p, hg_s, o_p, o_s, sa, sb, x, g1, sh2, sc2, norm_g.reshape(1, D_MODEL), wr, wa, wo, w_route, b_route)


def _row_gather(src_hbm, idx_ref, base, dst, sem, n_rows):
    def body(r, carry):
        tok = idx_ref[base + r]
        pltpu.make_async_copy(src_hbm.at[pl.ds(tok, 1), :], dst.at[pl.ds(r, 1), :], sem).start()
        return carry
    lax.fori_loop(0, n_rows, body, 0, unroll=8)


def _row_gather_wait(src_hbm, dst, sem, n_rows):
    pltpu.make_async_copy(src_hbm.at[pl.ds(0, n_rows), :], dst, sem).wait()


def _moe_kernel(te_ref, tfirst_ref, tvalid_ref, src_ref, n2_hbm, wg_ref, wu_ref, wd_ref, y_ref,
                xbuf, sem, wg_bf, wu_bf, wd_bf):
    i = pl.program_id(0)
    n_tiles = pl.num_programs(0)
    slot = i % 2

    @pl.when(i == 0)
    def _():
        _row_gather(n2_hbm, src_ref, 0, xbuf.at[0], sem.at[0], MOE_TM)

    @pl.when(i + 1 < n_tiles)
    def _():
        _row_gather(n2_hbm, src_ref, (i + 1) * MOE_TM, xbuf.at[1 - slot], sem.at[1 - slot], MOE_TM)

    @pl.when(tfirst_ref[i] == 1)
    def _():
        wg_bf[...] = wg_ref[0].astype(BF16)
        wu_bf[...] = wu_ref[0].astype(BF16)
        wd_bf[...] = wd_ref[0].astype(BF16)

    _row_gather_wait(n2_hbm, xbuf.at[slot], sem.at[slot], MOE_TM)

    @pl.when(tvalid_ref[i] == 1)
    def _():
        x = xbuf[slot].astype(BF16)
        hgate = jnp.dot(x, wg_bf[...], preferred_element_type=F32)
        hup = jnp.dot(x, wu_bf[...], preferred_element_type=F32)
        act = (hgate * _sigmoid(hgate) * hup).astype(BF16)
        y_ref[...] = jnp.dot(act, wd_bf[...], preferred_element_type=F32)

    @pl.when(tvalid_ref[i] == 0)
    def _():
        y_ref[...] = jnp.zeros(y_ref.shape, F32)


def _moe(n2, w_gate, w_up, w_down, tile_expert, tile_first, tile_valid, src_tok):
    n_tiles = tile_expert.shape[0]
    grid_spec = pltpu.PrefetchScalarGridSpec(
        num_scalar_prefetch=4,
        grid=(n_tiles,),
        in_specs=[pl.BlockSpec(memory_space=pl.ANY),
                  pl.BlockSpec((1, D_MODEL, D_EXPERT), lambda i, te, tf, tv, st: (te[i], 0, 0)),
                  pl.BlockSpec((1, D_MODEL, D_EXPERT), lambda i, te, tf, tv, st: (te[i], 0, 0)),
                  pl.BlockSpec((1, D_EXPERT, D_MODEL), lambda i, te, tf, tv, st: (te[i], 0, 0))],
        out_specs=pl.BlockSpec((MOE_TM, D_MODEL), lambda i, te, tf, tv, st: (i, 0)),
        scratch_shapes=[pltpu.VMEM((2, MOE_TM, D_MODEL), F32),
                        pltpu.SemaphoreType.DMA((2,)),
                        pltpu.VMEM((D_MODEL, D_EXPERT), BF16),
                        pltpu.VMEM((D_MODEL, D_EXPERT), BF16),
                        pltpu.VMEM((D_EXPERT, D_MODEL), BF16)],
    )
    return pl.pallas_call(
        _moe_kernel,
        grid_spec=grid_spec,
        out_shape=jax.ShapeDtypeStruct((n_tiles * MOE_TM, D_MODEL), F32),
        compiler_params=_cparams(("arbitrary",)),
        name="moe",
    )(tile_expert, tile_first, tile_valid, src_tok, n2, w_gate, w_up, w_down)


def _route_plan(e1, e2, n_tok):
    flat_e = jnp.concatenate([e1, e2])
    onehot = (flat_e[:, None] == jnp.arange(N_EXPERTS, dtype=jnp.int32)[None, :]).astype(jnp.int32)
    csum = jnp.cumsum(onehot, axis=0)
    rank = jnp.take_along_axis(csum, flat_e[:, None], axis=1)[:, 0] - 1
    counts = csum[-1]
    tiles_per = (counts + MOE_TM - 1) // MOE_TM
    tile_end = jnp.cumsum(tiles_per)
    tile_off = tile_end - tiles_per
    slot = tile_off[flat_e] * MOE_TM + rank
    n_tiles = (2 * n_tok) // MOE_TM + N_EXPERTS
    tok = jnp.tile(jnp.arange(n_tok, dtype=jnp.int32), 2)
    src_tok = jnp.zeros((n_tiles * MOE_TM,), jnp.int32).at[slot].set(tok)
    tile_id = jnp.arange(n_tiles, dtype=jnp.int32)
    n_used = tile_end[-1]
    tile_valid = (tile_id < n_used).astype(jnp.int32)
    te = jnp.searchsorted(tile_end, tile_id, side="right").astype(jnp.int32)
    last_e = jnp.searchsorted(tile_end, n_used - 1, side="right").astype(jnp.int32)
    tile_expert = jnp.where(tile_valid == 1, te, last_e)
    prev = jnp.concatenate([jnp.full((1,), -1, jnp.int32), tile_expert[:-1]])
    tile_first = (tile_expert != prev).astype(jnp.int32)
    return tile_expert, tile_first, tile_valid, src_tok, slot[:n_tok], slot[n_tok:]


def _final_kernel(p1_ref, p2_ref, ys_hbm, x1_ref, g2_ref, rt_ref, fg_ref, y_ref, ybuf, sem):
    i = pl.program_id(0)
    n_tiles = pl.num_programs(0)
    slot = i % 2

    def start(tile, s):
        _row_gather(ys_hbm, p1_ref, tile * FIN_TM, ybuf.at[s, 0], sem.at[s], FIN_TM)
        _row_gather(ys_hbm, p2_ref, tile * FIN_TM, ybuf.at[s, 1], sem.at[s], FIN_TM)

    @pl.when(i == 0)
    def _():
        start(0, 0)

    @pl.when(i + 1 < n_tiles)
    def _():
        start(i + 1, 1 - slot)

    _row_gather_wait(ys_hbm, ybuf.at[slot, 0], sem.at[slot], FIN_TM)
    _row_gather_wait(ys_hbm, ybuf.at[slot, 1], sem.at[slot], FIN_TM)
    rt = rt_ref[...]
    moe = rt[:, 2:3] * ybuf[slot, 0] + rt[:, 3:4] * ybuf[slot, 1]
    x2 = x1_ref[...] + _bcast_rows(g2_ref[...], FIN_TM) * moe
    y_ref[...] = x2 * lax.rsqrt(jnp.mean(x2 * x2, axis=-1, keepdims=True) + EPS) * fg_ref[...]


def _final(ys, x1, g2, rt, final_g, p1, p2):
    t = x1.shape[0]
    seg = FIN_TM // MOD_ROWS
    grid_spec = pltpu.PrefetchScalarGridSpec(
        num_scalar_prefetch=2,
        grid=(t // FIN_TM,),
        in_specs=[pl.BlockSpec(memory_space=pl.ANY),
                  pl.BlockSpec((FIN_TM, D_MODEL), lambda i, a, b: (i, 0)),
                  pl.BlockSpec((seg, D_MODEL), lambda i, a, b: (i, 0)),
                  pl.BlockSpec((FIN_TM, ROUTE_LANES), lambda i, a, b: (i, 0)),
                  pl.BlockSpec((1, D_MODEL), lambda i, a, b: (0, 0))],
        out_specs=pl.BlockSpec((FIN_TM, D_MODEL), lambda i, a, b: (i, 0)),
        scratch_shapes=[pltpu.VMEM((2, 2, FIN_TM, D_MODEL), F32),
                        pltpu.SemaphoreType.DMA((2,))],
    )
    return pl.pallas_call(
        _final_kernel,
        grid_spec=grid_spec,
        out_shape=jax.ShapeDtypeStruct((t, D_MODEL), F32),
        compiler_params=_cparams(("arbitrary",)),
        name="final",
    )(p1, p2, ys, x1, g2, rt, final_g.reshape(1, D_MODEL))


def _pick_tile(t, candidates):
    for c in candidates:
        if t % c == 0:
            return c
    raise ValueError(f"no row tile for {t} rows")


def _layer(x, c_all, seg_batch, state_conv, state_rglru, cache_k, cache_v, p, rel_table, dims):
    n_b, seq, d_b, d_seq = dims
    t_p = n_b * seq
    t = x.shape[0]

    mod = _ada(c_all, p["w_ada"], p["b_ada"])
    sh1, sc1, g1, sh2, sc2, g2 = [m[seg_batch] for m in jnp.split(mod, 6, axis=-1)]

    tm = _pick_tile(t, (768, 512, 256))
    xr, gg, q, k, v, sa, sb = _inproj(x, sc1, sh1, p["norm1_g"], p["w_in"], p["b_in"], tm)

    w_gate = jnp.concatenate([p["w_rg_a"], p["w_rg_x"]], axis=-1)
    b_gate = jnp.concatenate([p["b_rg_a"], p["b_rg_x"]], axis=-1)[:, None, :]
    zeros_state = jnp.zeros((n_b, SUBLANES, D_RNN), F32)
    hg_p, hl_p = _rglru(xr, gg, zeros_state, jnp.zeros((n_b, N_CHAINS, D_RNN), F32), p["w_conv"], p["b_conv"],
                        w_gate, b_gate, p["lru_lambda"],
                        n_blocks=n_b, n_seq=1, seq_len=seq, row_block0=0, link=True)
    state8 = jnp.pad(state_conv, ((0, 0), (SUBLANES - (CONV_W - 1), 0), (0, 0)))
    hg_s, hl_s = _rglru(xr, gg, state8, state_rglru[None], p["w_conv"], p["b_conv"],
                        w_gate, b_gate, p["lru_lambda"],
                        n_blocks=1, n_seq=d_b, seq_len=d_seq, row_block0=t_p // (d_b * d_seq), link=False)

    span = (WIN_CHUNKS + 1) * CHUNK
    bkt_p = _bucket_map(WIN_CHUNKS * CHUNK + jnp.arange(CHUNK), jnp.arange(span))
    o_p = _attn_band(q, k, v, rel_table, p["attn_sinks"], bkt_p, n_batch=n_b, seq_len=seq)
    cw = cache_k.shape[1]
    k_all = jnp.concatenate([cache_k.reshape(d_b, cw, KV_DIM), k[t_p:].reshape(d_b, d_seq, KV_DIM)], axis=1)
    v_all = jnp.concatenate([cache_v.reshape(d_b, cw, KV_DIM), v[t_p:].reshape(d_b, d_seq, KV_DIM)], axis=1)
    bkt_s = _bucket_map(cw + jnp.arange(d_seq), jnp.arange(cw + d_seq))
    o_s = _attn_step(q, k_all, v_all, rel_table, p["attn_sinks"], bkt_s,
                     n_batch=d_b, n_q=d_seq, row_block0=t_p // d_seq)

    w_route = jnp.pad(jnp.concatenate([p["w_route_group"], p["w_route_expert"]], axis=1),
                      ((0, 0), (0, ROUTE_LANES - N_GROUPS - N_EXPERTS)))
    b_route = jnp.pad(jnp.concatenate([p["b_route_group"], p["b_route_expert"]]),
                      (0, ROUTE_LANES - N_GROUPS - N_EXPERTS)).reshape(1, ROUTE_LANES)
    x1, n2, rt = _outproj(hg_p, hg_s, o_p, o_s, sa, sb, x, g1, sh2, sc2, p["norm2_g"],
                          p["w_rnn_out"].astype(BF16), p["w_attn_out"].astype(BF16), p["w_out"].astype(BF16),
                          w_route, b_route)

    e1 = rt[:, 0].astype(jnp.int32)
    e2 = rt[:, 1].astype(jnp.int32)
    tile_expert, tile_first, tile_valid, src_tok, p1, p2 = _route_plan(e1, e2, t)
    ys = _moe(n2, p["w_exp_gate"], p["w_exp_up"], p["w_exp_down"], tile_expert, tile_first, tile_valid, src_tok)
    return ys, x1, g2, rt, p1, p2, (xr, k, v, k_all, v_all, hl_p, hl_s)


def kernel(x_prompt, x_sample, cache_k_win, cache_v_win, state_conv, state_rglru, c_prompt, c_sample, w_ada, b_ada, norm1_g, norm2_g, w_in, b_in, w_conv, b_conv, w_rg_a, b_rg_a, w_rg_x, b_rg_x, lru_lambda, w_rnn_out, w_attn_out, w_out, attn_sinks, w_route_group, b_route_group, w_route_expert, b_route_expert, w_exp_gate, w_exp_up, w_exp_down, rel_bias_table, final_norm_g):
    n_b, seq, _ = x_prompt.shape
    d_b, d_seq, _ = x_sample.shape
    depth = w_ada.shape[0]
    assert depth == 1, "single trunk layer"
    assert seq % (ATT_CPB * CHUNK) == 0 and seq % MOD_ROWS == 0 and d_seq % MOD_ROWS == 0
    assert d_seq >= CONV_W - 1 and d_b == N_CHAINS and d_seq == MOD_ROWS
    t_p = n_b * seq
    assert t_p % (d_b * d_seq) == 0 and t_p % OUT_TM == 0 and (d_b * d_seq) % OUT_TM == 0
    dims = (n_b, seq, d_b, d_seq)
    cw = cache_k_win.shape[2]

    x = jnp.concatenate([x_prompt.reshape(t_p, D_MODEL), x_sample.reshape(d_b * d_seq, D_MODEL)], axis=0)
    n_c = n_b + d_b
    c_all = jnp.pad(jnp.concatenate([c_prompt, c_sample], axis=0), ((0, -n_c % SUBLANES), (0, 0)))
    seg_batch = jnp.concatenate([jnp.repeat(jnp.arange(n_b), seq // MOD_ROWS),
                                 n_b + jnp.repeat(jnp.arange(d_b), d_seq // MOD_ROWS)])

    l = 0
    p = {
        "w_ada": w_ada[l], "b_ada": b_ada[l], "norm1_g": norm1_g[l], "norm2_g": norm2_g[l],
        "w_in": w_in[l], "b_in": b_in[l], "w_conv": w_conv[l], "b_conv": b_conv[l],
        "w_rg_a": w_rg_a[l], "b_rg_a": b_rg_a[l], "w_rg_x": w_rg_x[l], "b_rg_x": b_rg_x[l],
        "lru_lambda": lru_lambda[l], "w_rnn_out": w_rnn_out[l], "w_attn_out": w_attn_out[l],
        "w_out": w_out[l], "attn_sinks": attn_sinks[l],
        "w_route_group": w_route_group[l], "b_route_group": b_route_group[l],
        "w_route_expert": w_route_expert[l], "b_route_expert": b_route_expert[l],
        "w_exp_gate": w_exp_gate[l], "w_exp_up": w_exp_up[l], "w_exp_down": w_exp_down[l],
    }
    ys, x1, g2, rt, p1, p2, (xr, k, v, k_all, v_all, hl_p, hl_s) = _layer(
        x, c_all, seg_batch, state_conv[l], state_rglru[l], cache_k_win[l], cache_v_win[l], p,
        rel_bias_table, dims)
    y = _final(ys, x1, g2, rt, final_norm_g, p1, p2)

    y_prompt = y[:t_p].reshape(n_b, seq, D_MODEL)
    y_sample = y[t_p:].reshape(d_b, d_seq, D_MODEL)
    win = min(WINDOW, seq)
    kp = k[:t_p].reshape(n_b, seq, N_KV_HEADS, HEAD_DIM)[:, -win:]
    vp = v[:t_p].reshape(n_b, seq, N_KV_HEADS, HEAD_DIM)[:, -win:]
    cp = xr[:t_p].reshape(n_b, seq, D_RNN)[:, -(CONV_W - 1):]
    rp = hl_p[:, N_CHAINS - 1, :]
    ks = k_all[:, -cw:].reshape(d_b, cw, N_KV_HEADS, HEAD_DIM)
    vs = v_all[:, -cw:].reshape(d_b, cw, N_KV_HEADS, HEAD_DIM)
    cs = xr[t_p:].reshape(d_b, d_seq, D_RNN)[:, -(CONV_W - 1):]
    rs = hl_s[0]
    return (y_prompt, y_sample, kp[None], vp[None], cp[None], rp[None], ks[None], vs[None], cs[None], rs[None])
```

```python
import functools
import math

import jax
import jax.numpy as jnp
from jax import lax
from jax.experimental import pallas as pl
from jax.experimental.pallas import tpu as pltpu

F32 = jnp.float32
BF16 = jnp.bfloat16

D_MODEL = 2048
D_RNN = 2048
RNN_BLOCK = 128
CONV_W = 4
LRU_C = 8.0
N_HEADS = 16
N_KV_HEADS = 4
HEAD_DIM = 128
GQA_GROUP = N_HEADS // N_KV_HEADS
Q_DIM = N_HEADS * HEAD_DIM
KV_DIM = N_KV_HEADS * HEAD_DIM
CHUNK = 64
WINDOW = 128
WIN_CHUNKS = WINDOW // CHUNK
SPAN = (WIN_CHUNKS + 1) * CHUNK
N_BUCKETS = 32
MAX_DISTANCE = 128
N_GROUPS = 4
E_PER_GROUP = 8
N_EXPERTS = N_GROUPS * E_PER_GROUP
D_EXPERT = 512
EPS = 1e-6
NEG_INF = -1e30
D_IN = 2 * D_RNN + Q_DIM + 2 * KV_DIM + 2 * D_MODEL
COL_XR, COL_GR, COL_Q = 0, D_RNN, 2 * D_RNN
COL_K = COL_Q + Q_DIM
COL_V = COL_K + KV_DIM
COL_GA = COL_V + KV_DIM
COL_GB = COL_GA + D_MODEL

LANES = 128
SUBLANES = 8
MOD_ROWS = 32
KEY_TILE = 256
VMEM_LIMIT = 56 * 1024 * 1024

ROW_TILE = 256
IN_NSUB = 3
IN_TN = 1024
MOE_TM = 128


def _sigmoid(x):
    return 1.0 / (1.0 + jnp.exp(-x))


def _gelu_tanh(x):
    return 0.5 * x * (1.0 + jnp.tanh(math.sqrt(2.0 / math.pi) * (x + 0.044715 * (x * x * x))))


def _bcast_rows(v, rows):
    n, d = v.shape
    return jnp.broadcast_to(v[:, None, :], (n, MOD_ROWS, d)).reshape(rows, d)


def _rms_modulate(x, gain, scale_seg, shift_seg):
    rows = x.shape[0]
    y = x * lax.rsqrt(jnp.mean(x * x, axis=-1, keepdims=True) + EPS) * gain
    return y * (1.0 + _bcast_rows(scale_seg, rows)) + _bcast_rows(shift_seg, rows)


def _cparams(sem, vmem_limit=VMEM_LIMIT):
    return pltpu.CompilerParams(dimension_semantics=sem, vmem_limit_bytes=vmem_limit)


def _ada_kernel(c_ref, w_ref, b_ref, o_ref):
    c = c_ref[...]
    s = (c * _sigmoid(c)).astype(BF16)
    o_ref[...] = jnp.dot(s, w_ref[...].astype(BF16), preferred_element_type=F32) + b_ref[...]


def _ada(c_all, w_ada, b_ada):
    rows = c_all.shape[0]
    n = w_ada.shape[1]
    tn = 1024
    return pl.pallas_call(
        _ada_kernel,
        grid=(n // tn,),
        in_specs=[pl.BlockSpec((rows, D_MODEL), lambda j: (0, 0)),
                  pl.BlockSpec((D_MODEL, tn), lambda j: (0, j)),
                  pl.BlockSpec((1, tn), lambda j: (0, j))],
        out_specs=pl.BlockSpec((rows, tn), lambda j: (0, j)),
        out_shape=jax.ShapeDtypeStruct((rows, n), F32),
        compiler_params=_cparams(("arbitrary",)),
        name="ada",
    )(c_all, w_ada, b_ada.reshape(1, n))


def _inproj_kernel(x0_ref, x1_ref, x2_ref, xs_ref, sc_ref, sh_ref, g_ref, w_ref, b_ref, z_ref, n1_scr,
                   *, n_prompt_tiles):
    i = pl.program_id(0)
    j = pl.program_id(1)

    @pl.when(j == 0)
    def _():
        tail_is_sample = i * IN_NSUB + (IN_NSUB - 1) >= n_prompt_tiles
        subs = (x0_ref[...], x1_ref[...], jnp.where(tail_is_sample, xs_ref[...], x2_ref[...]))
        seg = ROW_TILE // MOD_ROWS
        for r, x in enumerate(subs):
            n1 = _rms_modulate(x, g_ref[...], sc_ref[r * seg:(r + 1) * seg, :], sh_ref[r * seg:(r + 1) * seg, :])
            n1_scr[r * ROW_TILE:(r + 1) * ROW_TILE, :] = n1.astype(BF16)

    z_ref[...] = jnp.dot(n1_scr[...], w_ref[...].astype(BF16), preferred_element_type=F32) + b_ref[...]


def _inproj(x_p, x_s, sc1, sh1, norm_g, w_in, b_in):
    n_p = x_p.shape[0] // ROW_TILE
    t = x_p.shape[0] + x_s.shape[0]
    tm = IN_NSUB * ROW_TILE
    assert x_s.shape[0] == ROW_TILE and t % tm == 0
    seg = tm // MOD_ROWS

    def sub(r):
        return pl.BlockSpec((ROW_TILE, D_MODEL), lambda i, j: (jnp.minimum(i * IN_NSUB + r, n_p - 1), 0))

    return pl.pallas_call(
        functools.partial(_inproj_kernel, n_prompt_tiles=n_p),
        grid=(t // tm, D_IN // IN_TN),
        in_specs=[sub(0), sub(1), sub(2),
                  pl.BlockSpec((ROW_TILE, D_MODEL), lambda i, j: (0, 0)),
                  pl.BlockSpec((seg, D_MODEL), lambda i, j: (i, 0)),
                  pl.BlockSpec((seg, D_MODEL), lambda i, j: (i, 0)),
                  pl.BlockSpec((1, D_MODEL), lambda i, j: (0, 0)),
                  pl.BlockSpec((D_MODEL, IN_TN), lambda i, j: (0, j)),
                  pl.BlockSpec((1, IN_TN), lambda i, j: (0, j))],
        out_specs=pl.BlockSpec((tm, IN_TN), lambda i, j: (i, j)),
        out_shape=jax.ShapeDtypeStruct((t, D_IN), F32),
        scratch_shapes=[pltpu.VMEM((tm, D_MODEL), BF16)],
        compiler_params=_cparams(("arbitrary", "arbitrary")),
        name="inproj",
    )(x_p, x_p, x_p, x_s, sc1, sh1, norm_g.reshape(1, D_MODEL), w_in, b_in.reshape(1, D_IN))


RG_CW = 256
RG_SLABS = RG_CW // LANES
N_CHAINS = SUBLANES
CHAIN_PAD = 8


def _rglru_kernel(xr_ref, gr_ref, st_ref, h0_ref, wc_ref, bc_ref, wg_ref, bg_ref, lam_ref,
                  hg_ref, hl_ref, ext_scr, a_scr, u_scr, *, n_seq, seq_len, link):
    rows = n_seq * seq_len
    cl = rows // N_CHAINS
    pitch = cl + CHAIN_PAD

    for s in range(n_seq):
        ext_scr[s, 0:SUBLANES, :] = st_ref[s]
        ext_scr[s, SUBLANES:SUBLANES + seq_len, :] = xr_ref[s * seq_len:(s + 1) * seq_len, :]

    z = -lam_ref[...]
    softplus = jnp.maximum(z, 0.0) + jnp.log1p(jnp.exp(-jnp.abs(z)))
    nsl = -LRU_C * softplus

    for c in range(N_CHAINS):
        s, r0 = divmod(c * cl, seq_len)
        xc = bc_ref[...]
        for j in range(CONV_W):
            xc = xc + wc_ref[j:j + 1, :] * ext_scr[s, pl.ds(SUBLANES - (CONV_W - 1) + j + r0, cl), :]
        for sl in range(RG_SLABS):
            lanes = slice(sl * LANES, (sl + 1) * LANES)
            xb = xc[:, lanes]
            g = jnp.dot(xb.astype(BF16), wg_ref[sl].astype(BF16), preferred_element_type=F32) + bg_ref[sl]
            r = _sigmoid(g[:, :LANES])
            i = _sigmoid(g[:, LANES:])
            th = jnp.tanh(0.5 * (nsl[:, lanes] * r))
            rcp = 1.0 / (1.0 - th)
            a_scr[sl, c * pitch:c * pitch + cl, :] = (1.0 + th) * rcp
            u_scr[sl, c * pitch:c * pitch + cl, :] = (2.0 * jnp.sqrt(-th) * rcp) * (i * xb)

    def step(t, carry):
        hs, ps = carry
        new_h, new_p = [], []
        for sl in range(RG_SLABS):
            a = a_scr[sl, pl.ds(t, N_CHAINS, stride=pitch), :]
            u = u_scr[sl, pl.ds(t, N_CHAINS, stride=pitch), :]
            h = a * hs[sl] + u
            u_scr[sl, pl.ds(t, N_CHAINS, stride=pitch), :] = h
            new_h.append(h)
            if link:
                p = a * ps[sl]
                a_scr[sl, pl.ds(t, N_CHAINS, stride=pitch), :] = p
                new_p.append(p)
            else:
                new_p.append(ps[sl])
        return tuple(new_h), tuple(new_p)

    h_init = tuple(h0_ref[0, :, sl * LANES:(sl + 1) * LANES] for sl in range(RG_SLABS))
    p_init = tuple(jnp.ones((N_CHAINS, LANES), F32) for _ in range(RG_SLABS))
    h_end, p_end = lax.fori_loop(0, cl, step, (h_init, p_init), unroll=8)

    row = lax.broadcasted_iota(jnp.int32, (N_CHAINS, LANES), 0)
    for sl in range(RG_SLABS):
        lanes = slice(sl * LANES, (sl + 1) * LANES)
        if link:
            def shift_down(v):
                return jnp.where(row == 0, 0.0, pltpu.roll(v, 1, axis=0))
            hh = h_end[sl]
            for _ in range(N_CHAINS - 1):
                hh = h_end[sl] + p_end[sl] * shift_down(hh)
            carry_in = shift_down(hh)
        else:
            hh = h_end[sl]
        hl_ref[0, :, lanes] = hh
        for c in range(N_CHAINS):
            h = u_scr[sl, c * pitch:c * pitch + cl, :]
            if link:
                h = h + a_scr[sl, c * pitch:c * pitch + cl, :] * carry_in[c:c + 1, :]
            gg = _gelu_tanh(gr_ref[c * cl:(c + 1) * cl, lanes])
            hg_ref[c * cl:(c + 1) * cl, lanes] = (h * gg).astype(BF16)


def _rglru(z, state8, h0, w_conv, b_conv, w_gate, b_gate, lam, *, n_blocks, n_seq, seq_len, row_block0, link):
    rows = n_seq * seq_len
    cl = rows // N_CHAINS
    ncb = D_RNN // RG_CW
    gr0 = COL_GR // RG_CW
    kern = functools.partial(_rglru_kernel, n_seq=n_seq, seq_len=seq_len, link=link)
    return pl.pallas_call(
        kern,
        grid=(n_blocks, ncb),
        in_specs=[pl.BlockSpec((rows, RG_CW), lambda b, n: (row_block0 + b, n)),
                  pl.BlockSpec((rows, RG_CW), lambda b, n: (row_block0 + b, gr0 + n)),
                  pl.BlockSpec((n_seq, SUBLANES, RG_CW), lambda b, n: (b, 0, n)),
                  pl.BlockSpec((1, N_CHAINS, RG_CW), lambda b, n: (b, 0, n)),
                  pl.BlockSpec((CONV_W, RG_CW), lambda b, n: (0, n)),
                  pl.BlockSpec((1, RG_CW), lambda b, n: (0, n)),
                  pl.BlockSpec((RG_SLABS, RNN_BLOCK, 2 * RNN_BLOCK), lambda b, n: (n, 0, 0)),
                  pl.BlockSpec((RG_SLABS, 1, 2 * RNN_BLOCK), lambda b, n: (n, 0, 0)),
                  pl.BlockSpec((1, RG_CW), lambda b, n: (0, n))],
        out_specs=[pl.BlockSpec((rows, RG_CW), lambda b, n: (b, n)),
                   pl.BlockSpec((1, N_CHAINS, RG_CW), lambda b, n: (b, 0, n))],
        out_shape=[jax.ShapeDtypeStruct((n_blocks * rows, D_RNN), BF16),
                   jax.ShapeDtypeStruct((n_blocks, N_CHAINS, D_RNN), F32)],
        scratch_shapes=[pltpu.VMEM((n_seq, SUBLANES + seq_len, RG_CW), F32),
                        pltpu.VMEM((RG_SLABS, N_CHAINS * (cl + CHAIN_PAD), LANES), F32),
                        pltpu.VMEM((RG_SLABS, N_CHAINS * (cl + CHAIN_PAD), LANES), F32)],
        compiler_params=_cparams(("arbitrary", "arbitrary")),
        name="rglru_link" if link else "rglru_step",
    )(z, z, state8, h0, w_conv, b_conv.reshape(1, D_RNN), w_gate, b_gate, lam.reshape(1, D_RNN))


def _build_bias(tbl_ref, bkt_ref, bias_scr):
    bkt = bkt_ref[...]
    base = jnp.where(bkt < 0, NEG_INF, 0.0)
    for h in range(N_HEADS):
        acc = base
        for bk in range(N_BUCKETS):
            acc = jnp.where(bkt == bk, tbl_ref[bk, h], acc)
        bias_scr[h] = acc


def _attend(qh, kw, vw, bias, sink, key_ok):
    s = lax.dot_general(qh, kw, (((1,), (1,)), ((), ())), preferred_element_type=F32) * (HEAD_DIM ** -0.5) + bias
    if key_ok is not None:
        s = jnp.where(key_ok, s, NEG_INF)
    m = jnp.maximum(jnp.max(s, axis=-1, keepdims=True), sink)
    p = jnp.exp(s - m)
    inv = 1.0 / (jnp.sum(p, axis=-1, keepdims=True) + jnp.exp(sink - m))
    return jnp.dot((p * inv).astype(BF16), vw, preferred_element_type=F32)


ATT_CPB = 8
ATT_LEAD = WIN_CHUNKS * CHUNK
ATT_TAIL = KEY_TILE - SPAN


def _attn_band_kernel(tbl_ref, sink_ref, q_ref, k_ref, v_ref, bkt_ref, o_ref, kpad, vpad, bias_scr, *, seq_len):
    b = pl.program_id(0)
    cg = pl.program_id(1)

    @pl.when((b == 0) & (cg == 0))
    def _():
        _build_bias(tbl_ref, bkt_ref, bias_scr)

    @pl.when(cg == 0)
    def _():
        for ref, pad in ((k_ref, kpad), (v_ref, vpad)):
            pad[0:ATT_LEAD, :] = jnp.zeros((ATT_LEAD, KV_DIM), BF16)
            pad[ATT_LEAD:ATT_LEAD + seq_len, :] = ref[...].astype(BF16)
            pad[ATT_LEAD + seq_len:, :] = jnp.zeros((ATT_TAIL, KV_DIM), BF16)

    kidx = lax.broadcasted_iota(jnp.int32, (1, KEY_TILE), 1)

    def chunk(c, carry):
        q0 = pl.multiple_of(c * CHUNK, CHUNK)
        start = pl.multiple_of((cg * ATT_CPB + c) * CHUNK, CHUNK)
        key_ok = start + kidx >= ATT_LEAD
        for hk in range(N_KV_HEADS):
            kw = kpad[pl.ds(start, KEY_TILE), hk * HEAD_DIM:(hk + 1) * HEAD_DIM]
            vw = vpad[pl.ds(start, KEY_TILE), hk * HEAD_DIM:(hk + 1) * HEAD_DIM]
            for g in range(GQA_GROUP):
                h = hk * GQA_GROUP + g
                cols = slice(h * HEAD_DIM, (h + 1) * HEAD_DIM)
                qh = q_ref[pl.ds(q0, CHUNK), cols].astype(BF16)
                o = _attend(qh, kw, vw, bias_scr[h], sink_ref[h], key_ok)
                o_ref[pl.ds(q0, CHUNK), cols] = o.astype(BF16)
        return carry

    lax.fori_loop(0, ATT_CPB, chunk, 0)


def _attn_band(z, table, sinks, bkt, *, n_batch, seq_len):
    rows = ATT_CPB * CHUNK
    ng = seq_len // rows
    kern = functools.partial(_attn_band_kernel, seq_len=seq_len)
    smem = pl.BlockSpec(memory_space=pltpu.SMEM)
    pad_rows = ATT_LEAD + seq_len + ATT_TAIL
    return pl.pallas_call(
        kern,
        grid=(n_batch, ng),
        in_specs=[smem, smem,
                  pl.BlockSpec((rows, Q_DIM), lambda b, c: (b * ng + c, COL_Q // Q_DIM)),
                  pl.BlockSpec((seq_len, KV_DIM), lambda b, c: (b, COL_K // KV_DIM)),
                  pl.BlockSpec((seq_len, KV_DIM), lambda b, c: (b, COL_V // KV_DIM)),
                  pl.BlockSpec((CHUNK, KEY_TILE), lambda b, c: (0, 0))],
        out_specs=pl.BlockSpec((rows, Q_DIM), lambda b, c: (b * ng + c, 0)),
        out_shape=jax.ShapeDtypeStruct((n_batch * seq_len, Q_DIM), BF16),
        scratch_shapes=[pltpu.VMEM((pad_rows, KV_DIM), BF16),
                        pltpu.VMEM((pad_rows, KV_DIM), BF16),
                        pltpu.VMEM((N_HEADS, CHUNK, KEY_TILE), F32)],
        compiler_params=_cparams(("arbitrary", "arbitrary")),
        name="attn_band",
    )(table, sinks, z, z, z, bkt)


def _attn_step_kernel(tbl_ref, sink_ref, q_ref, k_ref, v_ref, bkt_ref, o_ref, kbuf, vbuf, bias_scr,
                      *, n_keys, n_q):
    b = pl.program_id(0)

    @pl.when(b == 0)
    def _():
        _build_bias(tbl_ref, bkt_ref, bias_scr)
        kbuf[n_keys:, :] = jnp.zeros((KEY_TILE - n_keys, KV_DIM), BF16)
        vbuf[n_keys:, :] = jnp.zeros((KEY_TILE - n_keys, KV_DIM), BF16)

    kbuf[0:n_keys, :] = k_ref[0].astype(BF16)
    vbuf[0:n_keys, :] = v_ref[0].astype(BF16)
    for hk in range(N_KV_HEADS):
        kw = kbuf[:, hk * HEAD_DIM:(hk + 1) * HEAD_DIM]
        vw = vbuf[:, hk * HEAD_DIM:(hk + 1) * HEAD_DIM]
        for g in range(GQA_GROUP):
            h = hk * GQA_GROUP + g
            cols = slice(h * HEAD_DIM, (h + 1) * HEAD_DIM)
            o = _attend(q_ref[:, cols].astype(BF16), kw, vw, bias_scr[h], sink_ref[h], None)
            o_ref[:, cols] = o.astype(BF16)


def _attn_step(z, k_all, v_all, table, sinks, bkt, *, n_batch, n_q, row_block0):
    n_keys = k_all.shape[1]
    kern = functools.partial(_attn_step_kernel, n_keys=n_keys, n_q=n_q)
    smem = pl.BlockSpec(memory_space=pltpu.SMEM)
    return pl.pallas_call(
        kern,
        grid=(n_batch,),
        in_specs=[smem, smem,
                  pl.BlockSpec((n_q, Q_DIM), lambda b: (row_block0 + b, COL_Q // Q_DIM)),
                  pl.BlockSpec((1, n_keys, KV_DIM), lambda b: (b, 0, 0)),
                  pl.BlockSpec((1, n_keys, KV_DIM), lambda b: (b, 0, 0)),
                  pl.BlockSpec((n_q, KEY_TILE), lambda b: (0, 0))],
        out_specs=pl.BlockSpec((n_q, Q_DIM), lambda b: (b, 0)),
        out_shape=jax.ShapeDtypeStruct((n_batch * n_q, Q_DIM), BF16),
        scratch_shapes=[pltpu.VMEM((KEY_TILE, KV_DIM), BF16),
                        pltpu.VMEM((KEY_TILE, KV_DIM), BF16),
                        pltpu.VMEM((N_HEADS, n_q, KEY_TILE), F32)],
        compiler_params=_cparams(("arbitrary",)),
        name="attn_step",
    )(table, sinks, z, k_all, v_all, bkt)


def _t5_bucket(rel):
    nb = N_BUCKETS // 2
    ret = jnp.where(rel > 0, nb, 0)
    n = jnp.abs(rel)
    max_exact = nb // 2
    nf = jnp.maximum(n, 1).astype(jnp.float32)
    large = max_exact + (jnp.log(nf / max_exact) / math.log(MAX_DISTANCE / max_exact)
                         * (nb - max_exact)).astype(jnp.int32)
    large = jnp.minimum(large, nb - 1)
    return ret + jnp.where(n < max_exact, n, large)


def _bucket_map(q_pos, k_pos):
    bkt = _t5_bucket(k_pos[None, :] - q_pos[:, None]).astype(jnp.int32)
    return jnp.pad(bkt, ((0, 0), (0, KEY_TILE - k_pos.shape[0])), constant_values=-1)


ROUTE_LANES = LANES
GATE_TN = 1024
OUT_VMEM_LIMIT = 60 * 1024 * 1024


def _outproj_kernel(hgp_ref, hgs_ref, op_ref, os_ref, ga0_ref, ga1_ref, gb0_ref, gb1_ref, xp_ref, xs_ref,
                    g1_ref, sh2_ref, sc2_ref, ng_ref, wr_ref, wa_ref, wo_ref, wrt_ref, brt_ref,
                    x1_ref, n2_ref, rt_ref, *, n_prompt_tiles):
    tm = x1_ref.shape[0]
    is_prompt = pl.program_id(0) < n_prompt_tiles
    hg = jnp.where(is_prompt, hgp_ref[...], hgs_ref[...])
    o = jnp.where(is_prompt, op_ref[...], os_ref[...])
    x = jnp.where(is_prompt, xp_ref[...], xs_ref[...])
    ya = jnp.dot(hg, wr_ref[...], preferred_element_type=F32)
    yb = jnp.dot(o, wa_ref[...], preferred_element_type=F32)
    halves = []
    for k, (ga_ref, gb_ref) in enumerate(((ga0_ref, gb0_ref), (ga1_ref, gb1_ref))):
        cols = slice(k * GATE_TN, (k + 1) * GATE_TN)
        halves.append((_sigmoid(ga_ref[...]) * ya[:, cols] + _sigmoid(gb_ref[...]) * yb[:, cols]).astype(BF16))
    merged = jnp.concatenate(halves, axis=1)
    mix = jnp.dot(merged, wo_ref[...], preferred_element_type=F32)
    x1 = x + _bcast_rows(g1_ref[...], tm) * mix
    x1_ref[...] = x1
    n2 = _rms_modulate(x1, ng_ref[...], sc2_ref[...], sh2_ref[...])
    n2_ref[...] = n2

    lg = jnp.dot(n2.astype(BF16), wrt_ref[...].astype(BF16), preferred_element_type=F32) + brt_ref[...]

    lane = lax.broadcasted_iota(jnp.int32, (tm, ROUTE_LANES), 1)
    lane_f = lane.astype(F32)
    e_lane = lane - N_GROUPS
    lane_group = (e_lane >> 3).astype(F32)

    def first_argmax(vals, vmax):
        return jnp.min(jnp.where(vals == vmax, lane_f, float(ROUTE_LANES)), axis=-1, keepdims=True)

    gl = jnp.where(lane < N_GROUPS, lg, NEG_INF)
    gmax = jnp.max(gl, axis=-1, keepdims=True)
    g_idx = first_argmax(gl, gmax)
    g_w = 1.0 / jnp.sum(jnp.exp(gl - gmax), axis=-1, keepdims=True)
    in_group = jnp.where((e_lane >= 0) & (e_lane < N_EXPERTS), lane_group, -1.0) == g_idx
    el = jnp.where(in_group, lg, NEG_INF)
    v1 = jnp.max(el, axis=-1, keepdims=True)
    i1 = first_argmax(el, v1)
    el2 = jnp.where(lane_f == i1, NEG_INF, el)
    v2 = jnp.max(el2, axis=-1, keepdims=True)
    i2 = first_argmax(el2, v2)
    e21 = jnp.exp(v2 - v1)
    w1 = g_w / (1.0 + e21)
    w2 = g_w * e21 / (1.0 + e21)
    e1 = i1 - float(N_GROUPS)
    e2 = i2 - float(N_GROUPS)
    rt_ref[...] = jnp.where(lane == 0, e1, jnp.where(lane == 1, e2, jnp.where(lane == 2, w1,
                            jnp.where(lane == 3, w2, 0.0))))


def _outproj(hg_p, hg_s, o_p, o_s, z, x_p, x_s, g1, sh2, sc2, norm_g, wr, wa, wo, w_route, b_route):
    tm = ROW_TILE
    t = z.shape[0]
    seg = tm // MOD_ROWS
    n_p = x_p.shape[0] // tm
    row = lambda i: (i, 0)
    fix = lambda i: (0, 0)
    row_p = lambda i: (jnp.minimum(i, n_p - 1), 0)
    row_s = lambda i: (jnp.maximum(i - n_p, 0), 0)
    once = pl.Buffered(1)

    def gate(col):
        return pl.BlockSpec((tm, GATE_TN), lambda i: (i, col // GATE_TN))

    return pl.pallas_call(
        functools.partial(_outproj_kernel, n_prompt_tiles=n_p),
        grid=(t // tm,),
        in_specs=[pl.BlockSpec((tm, D_RNN), row_p), pl.BlockSpec((tm, D_RNN), row_s, pipeline_mode=once),
                  pl.BlockSpec((tm, Q_DIM), row_p), pl.BlockSpec((tm, Q_DIM), row_s, pipeline_mode=once),
                  gate(COL_GA), gate(COL_GA + GATE_TN), gate(COL_GB), gate(COL_GB + GATE_TN),
                  pl.BlockSpec((tm, D_MODEL), row_p), pl.BlockSpec((tm, D_MODEL), row_s, pipeline_mode=once),
                  pl.BlockSpec((seg, D_MODEL), row), pl.BlockSpec((seg, D_MODEL), row),
                  pl.BlockSpec((seg, D_MODEL), row),
                  pl.BlockSpec((1, D_MODEL), fix),
                  pl.BlockSpec((D_RNN, D_MODEL), fix, pipeline_mode=once),
                  pl.BlockSpec((Q_DIM, D_MODEL), fix, pipeline_mode=once),
                  pl.BlockSpec((D_MODEL, D_MODEL), fix, pipeline_mode=once),
                  pl.BlockSpec((D_MODEL, ROUTE_LANES), fix),
                  pl.BlockSpec((1, ROUTE_LANES), fix)],
        out_specs=[pl.BlockSpec((tm, D_MODEL), row), pl.BlockSpec((tm, D_MODEL), row),
                   pl.BlockSpec((tm, ROUTE_LANES), row)],
        out_shape=[jax.ShapeDtypeStruct((t, D_MODEL), F32), jax.ShapeDtypeStruct((t, D_MODEL), F32),
                   jax.ShapeDtypeStruct((t, ROUTE_LANES), F32)],
        compiler_params=_cparams(("arbitrary",), OUT_VMEM_LIMIT),
        name="outproj",
    )(hg_p, hg_s, o_p, o_s, z, z, z, z, x_p, x_s, g1, sh2, sc2, norm_g.reshape(1, D_MODEL),
      wr, wa, wo, w_route, b_route)


def _row_gather(src_hbm, idx_ref, base, dst, sem, n_rows):
    def body(r, carry):
        tok = idx_ref[base + r]
        pltpu.make_async_copy(src_hbm.at[pl.ds(tok, 1), :], dst.at[pl.ds(r, 1), :], sem).start()
        return carry
    lax.fori_loop(0, n_rows, body, 0, unroll=8)


def _row_gather_wait(src_hbm, dst, sem, n_rows):
    pltpu.make_async_copy(src_hbm.at[pl.ds(0, n_rows), :], dst, sem).wait()


def _moe_kernel(te_ref, tfirst_ref, tvalid_ref, src_ref, n2_hbm, wg_ref, wu_ref, wd_ref, y_ref,
                xbuf, sem, wg_bf, wu_bf, wd_bf):
    i = pl.program_id(0)
    n_tiles = pl.num_programs(0)
    slot = i % 2

    @pl.when(i == 0)
    def _():
        _row_gather(n2_hbm, src_ref, 0, xbuf.at[0], sem.at[0], MOE_TM)

    @pl.when(i + 1 < n_tiles)
    def _():
        _row_gather(n2_hbm, src_ref, (i + 1) * MOE_TM, xbuf.at[1 - slot], sem.at[1 - slot], MOE_TM)

    @pl.when(tfirst_ref[i] == 1)
    def _():
        wg_bf[...] = wg_ref[0].astype(BF16)
        wu_bf[...] = wu_ref[0].astype(BF16)
        wd_bf[...] = wd_ref[0].astype(BF16)

    _row_gather_wait(n2_hbm, xbuf.at[slot], sem.at[slot], MOE_TM)

    @pl.when(tvalid_ref[i] == 1)
    def _():
        x = xbuf[slot].astype(BF16)
        hgate = jnp.dot(x, wg_bf[...], preferred_element_type=F32)
        hup = jnp.dot(x, wu_bf[...], preferred_element_type=F32)
        act = (hgate * _sigmoid(hgate) * hup).astype(BF16)
        y_ref[...] = jnp.dot(act, wd_bf[...], preferred_element_type=F32)

    @pl.when(tvalid_ref[i] == 0)
    def _():
        y_ref[...] = jnp.zeros(y_ref.shape, F32)


def _moe(n2, w_gate, w_up, w_down, tile_expert, tile_first, tile_valid, src_tok):
    n_tiles = tile_expert.shape[0]
    grid_spec = pltpu.PrefetchScalarGridSpec(
        num_scalar_prefetch=4,
        grid=(n_tiles,),
        in_specs=[pl.BlockSpec(memory_space=pl.ANY),
                  pl.BlockSpec((1, D_MODEL, D_EXPERT), lambda i, te, tf, tv, st: (te[i], 0, 0)),
                  pl.BlockSpec((1, D_MODEL, D_EXPERT), lambda i, te, tf, tv, st: (te[i], 0, 0)),
                  pl.BlockSpec((1, D_EXPERT, D_MODEL), lambda i, te, tf, tv, st: (te[i], 0, 0))],
        out_specs=pl.BlockSpec((MOE_TM, D_MODEL), lambda i, te, tf, tv, st: (i, 0)),
        scratch_shapes=[pltpu.VMEM((2, MOE_TM, D_MODEL), F32),
                        pltpu.SemaphoreType.DMA((2,)),
                        pltpu.VMEM((D_MODEL, D_EXPERT), BF16),
                        pltpu.VMEM((D_MODEL, D_EXPERT), BF16),
                        pltpu.VMEM((D_EXPERT, D_MODEL), BF16)],
    )
    return pl.pallas_call(
        _moe_kernel,
        grid_spec=grid_spec,
        out_shape=jax.ShapeDtypeStruct((n_tiles * MOE_TM, D_MODEL), F32),
        compiler_params=_cparams(("arbitrary",)),
        name="moe",
    )(tile_expert, tile_first, tile_valid, src_tok, n2, w_gate, w_up, w_down)


def _route_plan(e1, e2, n_tok):
    experts = jnp.arange(N_EXPERTS, dtype=jnp.int32)
    flat_e = jnp.concatenate([e1, e2])
    onehot = (flat_e[:, None] == experts[None, :]).astype(jnp.int32)
    csum = jnp.cumsum(onehot, axis=0)
    rank = jnp.sum(csum * onehot, axis=1) - 1
    counts = csum[-1]
    tiles_per = (counts + MOE_TM - 1) // MOE_TM
    tile_end = jnp.cumsum(tiles_per)
    tile_off = tile_end - tiles_per
    slot = jnp.sum(onehot * tile_off[None, :], axis=1) * MOE_TM + rank
    n_tiles = (2 * n_tok) // MOE_TM + N_EXPERTS
    tok = jnp.tile(jnp.arange(n_tok, dtype=jnp.int32), 2)
    src_tok = jnp.zeros((n_tiles * MOE_TM,), jnp.int32).at[slot].set(tok)
    tile_id = jnp.arange(n_tiles, dtype=jnp.int32)
    n_used = tile_end[-1]
    tile_valid = (tile_id < n_used).astype(jnp.int32)
    te = jnp.sum((tile_end[None, :] <= jnp.minimum(tile_id, n_used - 1)[:, None]).astype(jnp.int32), axis=1)
    tile_expert = jnp.minimum(te, N_EXPERTS - 1)
    prev = jnp.concatenate([jnp.full((1,), -1, jnp.int32), tile_expert[:-1]])
    tile_first = (tile_expert != prev).astype(jnp.int32)
    return tile_expert, tile_first, tile_valid, src_tok, slot[:n_tok], slot[n_tok:]


def _final_kernel(p1_ref, p2_ref, ys_hbm, x1_ref, g2_ref, rt_ref, fg_ref, yp_ref, ysm_ref, ybuf, sem,
                  *, n_prompt_tiles):
    i = pl.program_id(0)
    n_tiles = pl.num_programs(0)
    slot = i % 2

    def start(tile, s):
        _row_gather(ys_hbm, p1_ref, tile * ROW_TILE, ybuf.at[s, 0], sem.at[s], ROW_TILE)
        _row_gather(ys_hbm, p2_ref, tile * ROW_TILE, ybuf.at[s, 1], sem.at[s], ROW_TILE)

    @pl.when(i == 0)
    def _():
        start(0, 0)

    @pl.when(i + 1 < n_tiles)
    def _():
        start(i + 1, 1 - slot)

    _row_gather_wait(ys_hbm, ybuf.at[slot, 0], sem.at[slot], ROW_TILE)
    _row_gather_wait(ys_hbm, ybuf.at[slot, 1], sem.at[slot], ROW_TILE)
    rt = rt_ref[...]
    moe = rt[:, 2:3] * ybuf[slot, 0] + rt[:, 3:4] * ybuf[slot, 1]
    x2 = x1_ref[...] + _bcast_rows(g2_ref[...], ROW_TILE) * moe
    y = x2 * lax.rsqrt(jnp.mean(x2 * x2, axis=-1, keepdims=True) + EPS) * fg_ref[...]

    @pl.when(i < n_prompt_tiles)
    def _():
        yp_ref[...] = y

    @pl.when(i >= n_prompt_tiles)
    def _():
        ysm_ref[...] = y


def _final(ys, x1, g2, rt, final_g, p1, p2, n_prompt_rows):
    t = x1.shape[0]
    tm = ROW_TILE
    seg = tm // MOD_ROWS
    n_p = n_prompt_rows // tm
    grid_spec = pltpu.PrefetchScalarGridSpec(
        num_scalar_prefetch=2,
        grid=(t // tm,),
        in_specs=[pl.BlockSpec(memory_space=pl.ANY),
                  pl.BlockSpec((tm, D_MODEL), lambda i, a, b: (i, 0)),
                  pl.BlockSpec((seg, D_MODEL), lambda i, a, b: (i, 0)),
                  pl.BlockSpec((tm, ROUTE_LANES), lambda i, a, b: (i, 0)),
                  pl.BlockSpec((1, D_MODEL), lambda i, a, b: (0, 0))],
        out_specs=[pl.BlockSpec((tm, D_MODEL), lambda i, a, b: (jnp.minimum(i, n_p - 1), 0)),
                   pl.BlockSpec((tm, D_MODEL), lambda i, a, b: (jnp.maximum(i - n_p, 0), 0))],
        scratch_shapes=[pltpu.VMEM((2, 2, tm, D_MODEL), F32),
                        pltpu.SemaphoreType.DMA((2,))],
    )
    return pl.pallas_call(
        functools.partial(_final_kernel, n_prompt_tiles=n_p),
        grid_spec=grid_spec,
        out_shape=[jax.ShapeDtypeStruct((n_prompt_rows, D_MODEL), F32),
                   jax.ShapeDtypeStruct((t - n_prompt_rows, D_MODEL), F32)],
        compiler_params=_cparams(("arbitrary",)),
        name="final",
    )(p1, p2, ys, x1, g2, rt, final_g.reshape(1, D_MODEL))


def kernel(x_prompt, x_sample, cache_k_win, cache_v_win, state_conv, state_rglru, c_prompt, c_sample, w_ada, b_ada, norm1_g, norm2_g, w_in, b_in, w_conv, b_conv, w_rg_a, b_rg_a, w_rg_x, b_rg_x, lru_lambda, w_rnn_out, w_attn_out, w_out, attn_sinks, w_route_group, b_route_group, w_route_expert, b_route_expert, w_exp_gate, w_exp_up, w_exp_down, rel_bias_table, final_norm_g):
    n_b, seq, _ = x_prompt.shape
    d_b, d_seq, _ = x_sample.shape
    assert w_ada.shape[0] == 1, "single trunk layer"
    assert seq % (ATT_CPB * CHUNK) == 0 and seq % MOD_ROWS == 0 and d_seq == MOD_ROWS
    assert d_seq >= CONV_W - 1 and d_b == N_CHAINS
    t_p, t_s = n_b * seq, d_b * d_seq
    t = t_p + t_s
    assert t_s == ROW_TILE and t_p % ROW_TILE == 0 and t % (IN_NSUB * ROW_TILE) == 0
    cw = cache_k_win.shape[2]
    l = 0

    x_p = x_prompt.reshape(t_p, D_MODEL)
    x_s = x_sample.reshape(t_s, D_MODEL)

    n_c = n_b + d_b
    c_all = jnp.pad(jnp.concatenate([c_prompt, c_sample], axis=0), ((0, -n_c % SUBLANES), (0, 0)))
    mod = _ada(c_all, w_ada[l], b_ada[l])

    def per_segment(m):
        return jnp.concatenate([jnp.repeat(m[:n_b], seq // MOD_ROWS, axis=0),
                                jnp.repeat(m[n_b:n_c], d_seq // MOD_ROWS, axis=0)], axis=0)

    sh1, sc1, g1, sh2, sc2, g2 = [per_segment(m) for m in jnp.split(mod, 6, axis=-1)]

    z = _inproj(x_p, x_s, sc1, sh1, norm1_g[l], w_in[l], b_in[l])

    w_gate = jnp.concatenate([w_rg_a[l], w_rg_x[l]], axis=-1)
    b_gate = jnp.concatenate([b_rg_a[l], b_rg_x[l]], axis=-1)[:, None, :]
    hg_p, hl_p = _rglru(z, jnp.zeros((n_b, SUBLANES, D_RNN), F32), jnp.zeros((n_b, N_CHAINS, D_RNN), F32),
                        w_conv[l], b_conv[l], w_gate, b_gate, lru_lambda[l],
                        n_blocks=n_b, n_seq=1, seq_len=seq, row_block0=0, link=True)
    state8 = jnp.pad(state_conv[l], ((0, 0), (SUBLANES - (CONV_W - 1), 0), (0, 0)))
    hg_s, hl_s = _rglru(z, state8, state_rglru[l][None], w_conv[l], b_conv[l], w_gate, b_gate, lru_lambda[l],
                        n_blocks=1, n_seq=d_b, seq_len=d_seq, row_block0=t_p // t_s, link=False)

    sinks = attn_sinks[l]
    bkt_p = _bucket_map(WIN_CHUNKS * CHUNK + jnp.arange(CHUNK), jnp.arange(SPAN))
    o_p = _attn_band(z, rel_bias_table, sinks, bkt_p, n_batch=n_b, seq_len=seq)
    k_new = z[t_p:, COL_K:COL_K + KV_DIM].reshape(d_b, d_seq, KV_DIM)
    v_new = z[t_p:, COL_V:COL_V + KV_DIM].reshape(d_b, d_seq, KV_DIM)
    k_all = jnp.concatenate([cache_k_win[l].reshape(d_b, cw, KV_DIM), k_new], axis=1)
    v_all = jnp.concatenate([cache_v_win[l].reshape(d_b, cw, KV_DIM), v_new], axis=1)
    bkt_s = _bucket_map(cw + jnp.arange(d_seq), jnp.arange(cw + d_seq))
    o_s = _attn_step(z, k_all, v_all, rel_bias_table, sinks, bkt_s, n_batch=d_b, n_q=d_seq, row_block0=t_p // d_seq)

    n_route = N_GROUPS + N_EXPERTS
    w_route = jnp.pad(jnp.concatenate([w_route_group[l], w_route_expert[l]], axis=1),
                      ((0, 0), (0, ROUTE_LANES - n_route)))
    b_route = jnp.pad(jnp.concatenate([b_route_group[l], b_route_expert[l]]),
                      (0, ROUTE_LANES - n_route)).reshape(1, ROUTE_LANES)
    x1, n2, rt = _outproj(hg_p, hg_s, o_p, o_s, z, x_p, x_s, g1, sh2, sc2, norm2_g[l],
                          w_rnn_out[l].astype(BF16), w_attn_out[l].astype(BF16), w_out[l].astype(BF16),
                          w_route, b_route)

    e1 = rt[:, 0].astype(jnp.int32)
    e2 = rt[:, 1].astype(jnp.int32)
    tile_expert, tile_first, tile_valid, src_tok, p1, p2 = _route_plan(e1, e2, t)
    ys = _moe(n2, w_exp_gate[l], w_exp_up[l], w_exp_down[l], tile_expert, tile_first, tile_valid, src_tok)
    y_p, y_s = _final(ys, x1, g2, rt, final_norm_g, p1, p2, t_p)

    win = min(WINDOW, seq)

    def tail(col0, width, n_rows):
        return jnp.stack([z[(b + 1) * seq - n_rows:(b + 1) * seq, col0:col0 + width] for b in range(n_b)])

    kp = tail(COL_K, KV_DIM, win).reshape(n_b, win, N_KV_HEADS, HEAD_DIM)
    vp = tail(COL_V, KV_DIM, win).reshape(n_b, win, N_KV_HEADS, HEAD_DIM)
    cp = tail(COL_XR, D_RNN, CONV_W - 1)
    rp = hl_p[:, N_CHAINS - 1, :]
    ks = k_all[:, -cw:].reshape(d_b, cw, N_KV_HEADS, HEAD_DIM)
    vs = v_all[:, -cw:].reshape(d_b, cw, N_KV_HEADS, HEAD_DIM)
    cs = z[t_p:, COL_XR:COL_XR + D_RNN].reshape(d_b, d_seq, D_RNN)[:, -(CONV_W - 1):]
    rs = hl_s[0]
    return (y_p.reshape(n_b, seq, D_MODEL), y_s.reshape(d_b, d_seq, D_MODEL),
            kp[None], vp[None], cp[None], rp[None], ks[None], vs[None], cs[None], rs[None])
```

```python
import functools
import math

import jax
import jax.numpy as jnp
from jax import lax
from jax.experimental import pallas as pl
from jax.experimental.pallas import tpu as pltpu

F32 = jnp.float32
BF16 = jnp.bfloat16

D_MODEL = 2048
D_RNN = 2048
RNN_BLOCK = 128
CONV_W = 4
LRU_C = 8.0
N_HEADS = 16
N_KV_HEADS = 4
HEAD_DIM = 128
GQA_GROUP = N_HEADS // N_KV_HEADS
Q_DIM = N_HEADS * HEAD_DIM
KV_DIM = N_KV_HEADS * HEAD_DIM
CHUNK = 64
WINDOW = 128
WIN_CHUNKS = WINDOW // CHUNK
SPAN = (WIN_CHUNKS + 1) * CHUNK
N_BUCKETS = 32
MAX_DISTANCE = 128
N_GROUPS = 4
E_PER_GROUP = 8
N_EXPERTS = N_GROUPS * E_PER_GROUP
D_EXPERT = 512
EPS = 1e-6
NEG_INF = -1e30
D_IN = 2 * D_RNN + Q_DIM + 2 * KV_DIM + 2 * D_MODEL
COL_XR, COL_GR, COL_Q = 0, D_RNN, 2 * D_RNN
COL_K = COL_Q + Q_DIM
COL_V = COL_K + KV_DIM
COL_GA = COL_V + KV_DIM
COL_GB = COL_GA + D_MODEL

LANES = 128
SUBLANES = 8
MOD_ROWS = 32
KEY_TILE = 256
VMEM_LIMIT = 56 * 1024 * 1024

ROW_TILE = 256
IN_NSUB = 3
IN_TN = 1024
MOE_TM = 128


def _sigmoid(x):
    return 1.0 / (1.0 + jnp.exp(-x))


def _gelu_tanh(x):
    return 0.5 * x * (1.0 + jnp.tanh(math.sqrt(2.0 / math.pi) * (x + 0.044715 * (x * x * x))))


def _bcast_rows(v, rows):
    n, d = v.shape
    return jnp.broadcast_to(v[:, None, :], (n, MOD_ROWS, d)).reshape(rows, d)


def _rms_modulate(x, gain, scale_seg, shift_seg):
    rows = x.shape[0]
    y = x * lax.rsqrt(jnp.mean(x * x, axis=-1, keepdims=True) + EPS) * gain
    return y * (1.0 + _bcast_rows(scale_seg, rows)) + _bcast_rows(shift_seg, rows)


def _cparams(sem, vmem_limit=VMEM_LIMIT):
    return pltpu.CompilerParams(dimension_semantics=sem, vmem_limit_bytes=vmem_limit)


def _ada_kernel(c_ref, w_ref, b_ref, o_ref):
    c = c_ref[...]
    s = (c * _sigmoid(c)).astype(BF16)
    o_ref[...] = jnp.dot(s, w_ref[...].astype(BF16), preferred_element_type=F32) + b_ref[...]


def _ada(c_all, w_ada, b_ada):
    rows = c_all.shape[0]
    n = w_ada.shape[1]
    tn = 1024
    return pl.pallas_call(
        _ada_kernel,
        grid=(n // tn,),
        in_specs=[pl.BlockSpec((rows, D_MODEL), lambda j: (0, 0)),
                  pl.BlockSpec((D_MODEL, tn), lambda j: (0, j)),
                  pl.BlockSpec((1, tn), lambda j: (0, j))],
        out_specs=pl.BlockSpec((rows, tn), lambda j: (0, j)),
        out_shape=jax.ShapeDtypeStruct((rows, n), F32),
        compiler_params=_cparams(("arbitrary",)),
        name="ada",
    )(c_all, w_ada, b_ada.reshape(1, n))


def _inproj_kernel(x0_ref, x1_ref, x2_ref, xs_ref, sc_ref, sh_ref, g_ref, w_ref, b_ref, z_ref, n1_scr,
                   *, n_prompt_tiles):
    i = pl.program_id(0)
    j = pl.program_id(1)

    @pl.when(j == 0)
    def _():
        tail_is_sample = i * IN_NSUB + (IN_NSUB - 1) >= n_prompt_tiles
        subs = (x0_ref[...], x1_ref[...], jnp.where(tail_is_sample, xs_ref[...], x2_ref[...]))
        seg = ROW_TILE // MOD_ROWS
        for r, x in enumerate(subs):
            n1 = _rms_modulate(x, g_ref[...], sc_ref[r * seg:(r + 1) * seg, :], sh_ref[r * seg:(r + 1) * seg, :])
            n1_scr[r * ROW_TILE:(r + 1) * ROW_TILE, :] = n1.astype(BF16)

    z_ref[...] = jnp.dot(n1_scr[...], w_ref[...].astype(BF16), preferred_element_type=F32) + b_ref[...]


def _inproj(x_p, x_s, sc1, sh1, norm_g, w_in, b_in):
    n_p = x_p.shape[0] // ROW_TILE
    t = x_p.shape[0] + x_s.shape[0]
    tm = IN_NSUB * ROW_TILE
    assert x_s.shape[0] == ROW_TILE and t % tm == 0
    seg = tm // MOD_ROWS

    def sub(r):
        return pl.BlockSpec((ROW_TILE, D_MODEL), lambda i, j: (jnp.minimum(i * IN_NSUB + r, n_p - 1), 0))

    return pl.pallas_call(
        functools.partial(_inproj_kernel, n_prompt_tiles=n_p),
        grid=(t // tm, D_IN // IN_TN),
        in_specs=[sub(0), sub(1), sub(2),
                  pl.BlockSpec((ROW_TILE, D_MODEL), lambda i, j: (0, 0)),
                  pl.BlockSpec((seg, D_MODEL), lambda i, j: (i, 0)),
                  pl.BlockSpec((seg, D_MODEL), lambda i, j: (i, 0)),
                  pl.BlockSpec((1, D_MODEL), lambda i, j: (0, 0)),
                  pl.BlockSpec((D_MODEL, IN_TN), lambda i, j: (0, j)),
                  pl.BlockSpec((1, IN_TN), lambda i, j: (0, j))],
        out_specs=pl.BlockSpec((tm, IN_TN), lambda i, j: (i, j)),
        out_shape=jax.ShapeDtypeStruct((t, D_IN), F32),
        scratch_shapes=[pltpu.VMEM((tm, D_MODEL), BF16)],
        compiler_params=_cparams(("arbitrary", "arbitrary")),
        name="inproj",
    )(x_p, x_p, x_p, x_s, sc1, sh1, norm_g.reshape(1, D_MODEL), w_in, b_in.reshape(1, D_IN))


RG_CW = 256
RG_SLABS = RG_CW // LANES
N_CHAINS = SUBLANES
CHAIN_PAD = 8


def _rglru_kernel(xr_ref, gr_ref, st_ref, h0_ref, wc_ref, bc_ref, wg_ref, bg_ref, lam_ref,
                  hg_ref, hl_ref, ext_scr, a_scr, u_scr, *, n_seq, seq_len, link):
    rows = n_seq * seq_len
    cl = rows // N_CHAINS
    pitch = cl + CHAIN_PAD

    for s in range(n_seq):
        ext_scr[s, 0:SUBLANES, :] = st_ref[s]
        ext_scr[s, SUBLANES:SUBLANES + seq_len, :] = xr_ref[s * seq_len:(s + 1) * seq_len, :]

    z = -lam_ref[...]
    softplus = jnp.maximum(z, 0.0) + jnp.log1p(jnp.exp(-jnp.abs(z)))
    nsl = -LRU_C * softplus

    for c in range(N_CHAINS):
        s, r0 = divmod(c * cl, seq_len)
        xc = bc_ref[...]
        for j in range(CONV_W):
            xc = xc + wc_ref[j:j + 1, :] * ext_scr[s, pl.ds(SUBLANES - (CONV_W - 1) + j + r0, cl), :]
        for sl in range(RG_SLABS):
            lanes = slice(sl * LANES, (sl + 1) * LANES)
            xb = xc[:, lanes]
            g = jnp.dot(xb.astype(BF16), wg_ref[sl].astype(BF16), preferred_element_type=F32) + bg_ref[sl]
            r = _sigmoid(g[:, :LANES])
            i = _sigmoid(g[:, LANES:])
            th = jnp.tanh(0.5 * (nsl[:, lanes] * r))
            rcp = 1.0 / (1.0 - th)
            a_scr[sl, c * pitch:c * pitch + cl, :] = (1.0 + th) * rcp
            u_scr[sl, c * pitch:c * pitch + cl, :] = (2.0 * jnp.sqrt(-th) * rcp) * (i * xb)

    def step(t, carry):
        hs, ps = carry
        new_h, new_p = [], []
        for sl in range(RG_SLABS):
            a = a_scr[sl, pl.ds(t, N_CHAINS, stride=pitch), :]
            u = u_scr[sl, pl.ds(t, N_CHAINS, stride=pitch), :]
            h = a * hs[sl] + u
            u_scr[sl, pl.ds(t, N_CHAINS, stride=pitch), :] = h
            new_h.append(h)
            if link:
                p = a * ps[sl]
                a_scr[sl, pl.ds(t, N_CHAINS, stride=pitch), :] = p
                new_p.append(p)
            else:
                new_p.append(ps[sl])
        return tuple(new_h), tuple(new_p)

    h_init = tuple(h0_ref[0, :, sl * LANES:(sl + 1) * LANES] for sl in range(RG_SLABS))
    p_init = tuple(jnp.ones((N_CHAINS, LANES), F32) for _ in range(RG_SLABS))
    h_end, p_end = lax.fori_loop(0, cl, step, (h_init, p_init), unroll=8)

    row = lax.broadcasted_iota(jnp.int32, (N_CHAINS, LANES), 0)
    for sl in range(RG_SLABS):
        lanes = slice(sl * LANES, (sl + 1) * LANES)
        if link:
            def shift_down(v):
                return jnp.where(row == 0, 0.0, pltpu.roll(v, 1, axis=0))
            hh = h_end[sl]
            for _ in range(N_CHAINS - 1):
                hh = h_end[sl] + p_end[sl] * shift_down(hh)
            carry_in = shift_down(hh)
        else:
            hh = h_end[sl]
        hl_ref[0, :, lanes] = hh
        for c in range(N_CHAINS):
            h = u_scr[sl, c * pitch:c * pitch + cl, :]
            if link:
                h = h + a_scr[sl, c * pitch:c * pitch + cl, :] * carry_in[c:c + 1, :]
            gg = _gelu_tanh(gr_ref[c * cl:(c + 1) * cl, lanes])
            hg_ref[c * cl:(c + 1) * cl, lanes] = (h * gg).astype(BF16)


def _rglru(z, state8, h0, w_conv, b_conv, w_gate, b_gate, lam, *, n_blocks, n_seq, seq_len, row_block0, link):
    rows = n_seq * seq_len
    cl = rows // N_CHAINS
    ncb = D_RNN // RG_CW
    gr0 = COL_GR // RG_CW
    kern = functools.partial(_rglru_kernel, n_seq=n_seq, seq_len=seq_len, link=link)
    return pl.pallas_call(
        kern,
        grid=(n_blocks, ncb),
        in_specs=[pl.BlockSpec((rows, RG_CW), lambda b, n: (row_block0 + b, n)),
                  pl.BlockSpec((rows, RG_CW), lambda b, n: (row_block0 + b, gr0 + n)),
                  pl.BlockSpec((n_seq, SUBLANES, RG_CW), lambda b, n: (b, 0, n)),
                  pl.BlockSpec((1, N_CHAINS, RG_CW), lambda b, n: (b, 0, n)),
                  pl.BlockSpec((CONV_W, RG_CW), lambda b, n: (0, n)),
                  pl.BlockSpec((1, RG_CW), lambda b, n: (0, n)),
                  pl.BlockSpec((RG_SLABS, RNN_BLOCK, 2 * RNN_BLOCK), lambda b, n: (n, 0, 0)),
                  pl.BlockSpec((RG_SLABS, 1, 2 * RNN_BLOCK), lambda b, n: (n, 0, 0)),
                  pl.BlockSpec((1, RG_CW), lambda b, n: (0, n))],
        out_specs=[pl.BlockSpec((rows, RG_CW), lambda b, n: (b, n)),
                   pl.BlockSpec((1, N_CHAINS, RG_CW), lambda b, n: (b, 0, n))],
        out_shape=[jax.ShapeDtypeStruct((n_blocks * rows, D_RNN), BF16),
                   jax.ShapeDtypeStruct((n_blocks, N_CHAINS, D_RNN), F32)],
        scratch_shapes=[pltpu.VMEM((n_seq, SUBLANES + seq_len, RG_CW), F32),
                        pltpu.VMEM((RG_SLABS, N_CHAINS * (cl + CHAIN_PAD), LANES), F32),
                        pltpu.VMEM((RG_SLABS, N_CHAINS * (cl + CHAIN_PAD), LANES), F32)],
        compiler_params=_cparams(("arbitrary", "arbitrary")),
        name="rglru_link" if link else "rglru_step",
    )(z, z, state8, h0, w_conv, b_conv.reshape(1, D_RNN), w_gate, b_gate, lam.reshape(1, D_RNN))


def _build_bias(tbl_ref, bkt_ref, bias_scr, n_q):
    bkt = bkt_ref[...]
    base = jnp.where(bkt < 0, NEG_INF, 0.0)
    for h in range(N_HEADS):
        acc = base
        for bk in range(N_BUCKETS):
            acc = jnp.where(bkt == bk, tbl_ref[bk, h], acc)
        hk, g = divmod(h, GQA_GROUP)
        bias_scr[hk, g * n_q:(g + 1) * n_q, :] = acc


def _sink_columns(sink_ref, n_q):
    row = lax.broadcasted_iota(jnp.int32, (GQA_GROUP * n_q, 1), 0)
    cols = []
    for hk in range(N_KV_HEADS):
        col = jnp.full((GQA_GROUP * n_q, 1), sink_ref[hk * GQA_GROUP], F32)
        for g in range(1, GQA_GROUP):
            col = jnp.where(row >= g * n_q, sink_ref[hk * GQA_GROUP + g], col)
        cols.append(col)
    return cols


def _attend(q_of, k_of, v_of, bias_scr, sinks, key_ok, store, n_q):
    scores = []
    for hk in range(N_KV_HEADS):
        qg = jnp.concatenate([q_of(hk * GQA_GROUP + g) for g in range(GQA_GROUP)], axis=0)
        s = lax.dot_general(qg, k_of(hk), (((1,), (1,)), ((), ())), preferred_element_type=F32)
        s = s * (HEAD_DIM ** -0.5) + bias_scr[hk]
        if key_ok is not None:
            s = jnp.where(key_ok, s, NEG_INF)
        scores.append(s)
    probs = []
    for hk in range(N_KV_HEADS):
        s, sink = scores[hk], sinks[hk]
        m = jnp.maximum(jnp.max(s, axis=-1, keepdims=True), sink)
        p = jnp.exp(s - m)
        inv = 1.0 / (jnp.sum(p, axis=-1, keepdims=True) + jnp.exp(sink - m))
        probs.append((p * inv).astype(BF16))
    for hk in range(N_KV_HEADS):
        o = jnp.dot(probs[hk], v_of(hk), preferred_element_type=F32)
        for g in range(GQA_GROUP):
            store(hk * GQA_GROUP + g, o[g * n_q:(g + 1) * n_q, :].astype(BF16))


ATT_CPB = 8
ATT_LEAD = WIN_CHUNKS * CHUNK
ATT_TAIL = KEY_TILE - SPAN


def _attn_band_kernel(tbl_ref, sink_ref, q_ref, k_ref, v_ref, bkt_ref, o_ref, kpad, vpad, bias_scr, *, seq_len):
    b = pl.program_id(0)
    cg = pl.program_id(1)

    @pl.when((b == 0) & (cg == 0))
    def _():
        _build_bias(tbl_ref, bkt_ref, bias_scr, CHUNK)

    @pl.when(cg == 0)
    def _():
        for ref, pad in ((k_ref, kpad), (v_ref, vpad)):
            pad[0:ATT_LEAD, :] = jnp.zeros((ATT_LEAD, KV_DIM), BF16)
            pad[ATT_LEAD:ATT_LEAD + seq_len, :] = ref[...].astype(BF16)
            pad[ATT_LEAD + seq_len:, :] = jnp.zeros((ATT_TAIL, KV_DIM), BF16)

    kidx = lax.broadcasted_iota(jnp.int32, (1, KEY_TILE), 1)
    sinks = _sink_columns(sink_ref, CHUNK)

    def chunk(c, carry):
        q0 = pl.multiple_of(c * CHUNK, CHUNK)
        start = pl.multiple_of((cg * ATT_CPB + c) * CHUNK, CHUNK)
        key_ok = start + kidx >= ATT_LEAD

        def head_cols(h):
            return slice(h * HEAD_DIM, (h + 1) * HEAD_DIM)

        def store(h, val):
            o_ref[pl.ds(q0, CHUNK), head_cols(h)] = val

        _attend(lambda h: q_ref[pl.ds(q0, CHUNK), head_cols(h)].astype(BF16),
                lambda hk: kpad[pl.ds(start, KEY_TILE), head_cols(hk)],
                lambda hk: vpad[pl.ds(start, KEY_TILE), head_cols(hk)],
                bias_scr, sinks, key_ok, store, CHUNK)
        return carry

    lax.fori_loop(0, ATT_CPB, chunk, 0)


def _attn_band(z, table, sinks, bkt, *, n_batch, seq_len):
    rows = ATT_CPB * CHUNK
    ng = seq_len // rows
    kern = functools.partial(_attn_band_kernel, seq_len=seq_len)
    smem = pl.BlockSpec(memory_space=pltpu.SMEM)
    pad_rows = ATT_LEAD + seq_len + ATT_TAIL
    return pl.pallas_call(
        kern,
        grid=(n_batch, ng),
        in_specs=[smem, smem,
                  pl.BlockSpec((rows, Q_DIM), lambda b, c: (b * ng + c, COL_Q // Q_DIM)),
                  pl.BlockSpec((seq_len, KV_DIM), lambda b, c: (b, COL_K // KV_DIM)),
                  pl.BlockSpec((seq_len, KV_DIM), lambda b, c: (b, COL_V // KV_DIM)),
                  pl.BlockSpec((CHUNK, KEY_TILE), lambda b, c: (0, 0))],
        out_specs=pl.BlockSpec((rows, Q_DIM), lambda b, c: (b * ng + c, 0)),
        out_shape=jax.ShapeDtypeStruct((n_batch * seq_len, Q_DIM), BF16),
        scratch_shapes=[pltpu.VMEM((pad_rows, KV_DIM), BF16),
                        pltpu.VMEM((pad_rows, KV_DIM), BF16),
                        pltpu.VMEM((N_KV_HEADS, GQA_GROUP * CHUNK, KEY_TILE), F32)],
        compiler_params=_cparams(("arbitrary", "arbitrary")),
        name="attn_band",
    )(table, sinks, z, z, z, bkt)


def _attn_step_kernel(tbl_ref, sink_ref, q_ref, k_ref, v_ref, bkt_ref, o_ref, kbuf, vbuf, bias_scr,
                      *, n_keys, n_q):
    b = pl.program_id(0)

    @pl.when(b == 0)
    def _():
        _build_bias(tbl_ref, bkt_ref, bias_scr, n_q)
        kbuf[n_keys:, :] = jnp.zeros((KEY_TILE - n_keys, KV_DIM), BF16)
        vbuf[n_keys:, :] = jnp.zeros((KEY_TILE - n_keys, KV_DIM), BF16)

    kbuf[0:n_keys, :] = k_ref[0].astype(BF16)
    vbuf[0:n_keys, :] = v_ref[0].astype(BF16)

    def head_cols(h):
        return slice(h * HEAD_DIM, (h + 1) * HEAD_DIM)

    def store(h, val):
        o_ref[:, head_cols(h)] = val

    _attend(lambda h: q_ref[:, head_cols(h)].astype(BF16),
            lambda hk: kbuf[:, head_cols(hk)], lambda hk: vbuf[:, head_cols(hk)],
            bias_scr, _sink_columns(sink_ref, n_q), None, store, n_q)


def _attn_step(z, k_all, v_all, table, sinks, bkt, *, n_batch, n_q, row_block0):
    n_keys = k_all.shape[1]
    kern = functools.partial(_attn_step_kernel, n_keys=n_keys, n_q=n_q)
    smem = pl.BlockSpec(memory_space=pltpu.SMEM)
    return pl.pallas_call(
        kern,
        grid=(n_batch,),
        in_specs=[smem, smem,
                  pl.BlockSpec((n_q, Q_DIM), lambda b: (row_block0 + b, COL_Q // Q_DIM)),
                  pl.BlockSpec((1, n_keys, KV_DIM), lambda b: (b, 0, 0)),
                  pl.BlockSpec((1, n_keys, KV_DIM), lambda b: (b, 0, 0)),
                  pl.BlockSpec((n_q, KEY_TILE), lambda b: (0, 0))],
        out_specs=pl.BlockSpec((n_q, Q_DIM), lambda b: (b, 0)),
        out_shape=jax.ShapeDtypeStruct((n_batch * n_q, Q_DIM), BF16),
        scratch_shapes=[pltpu.VMEM((KEY_TILE, KV_DIM), BF16),
                        pltpu.VMEM((KEY_TILE, KV_DIM), BF16),
                        pltpu.VMEM((N_KV_HEADS, GQA_GROUP * n_q, KEY_TILE), F32)],
        compiler_params=_cparams(("arbitrary",)),
        name="attn_step",
    )(table, sinks, z, k_all, v_all, bkt)


def _t5_bucket(rel):
    nb = N_BUCKETS // 2
    ret = jnp.where(rel > 0, nb, 0)
    n = jnp.abs(rel)
    max_exact = nb // 2
    nf = jnp.maximum(n, 1).astype(jnp.float32)
    large = max_exact + (jnp.log(nf / max_exact) / math.log(MAX_DISTANCE / max_exact)
                         * (nb - max_exact)).astype(jnp.int32)
    large = jnp.minimum(large, nb - 1)
    return ret + jnp.where(n < max_exact, n, large)


def _bucket_map(q_pos, k_pos):
    bkt = _t5_bucket(k_pos[None, :] - q_pos[:, None]).astype(jnp.int32)
    return jnp.pad(bkt, ((0, 0), (0, KEY_TILE - k_pos.shape[0])), constant_values=-1)


ROUTE_LANES = LANES
GATE_TN = 1024
OUT_VMEM_LIMIT = 60 * 1024 * 1024


def _outproj_kernel(hgp_ref, hgs_ref, op_ref, os_ref, ga0_ref, ga1_ref, gb0_ref, gb1_ref, xp_ref, xs_ref,
                    g1_ref, sh2_ref, sc2_ref, ng_ref, wr_ref, wa_ref, wo_ref, wrt_ref, brt_ref,
                    x1_ref, n2_ref, rt_ref, *, n_prompt_tiles):
    tm = x1_ref.shape[0]
    is_prompt = pl.program_id(0) < n_prompt_tiles
    hg = jnp.where(is_prompt, hgp_ref[...], hgs_ref[...])
    o = jnp.where(is_prompt, op_ref[...], os_ref[...])
    x = jnp.where(is_prompt, xp_ref[...], xs_ref[...])
    ya = jnp.dot(hg, wr_ref[...], preferred_element_type=F32)
    yb = jnp.dot(o, wa_ref[...], preferred_element_type=F32)
    halves = []
    for k, (ga_ref, gb_ref) in enumerate(((ga0_ref, gb0_ref), (ga1_ref, gb1_ref))):
        cols = slice(k * GATE_TN, (k + 1) * GATE_TN)
        halves.append((_sigmoid(ga_ref[...]) * ya[:, cols] + _sigmoid(gb_ref[...]) * yb[:, cols]).astype(BF16))
    merged = jnp.concatenate(halves, axis=1)
    mix = jnp.dot(merged, wo_ref[...], preferred_element_type=F32)
    x1 = x + _bcast_rows(g1_ref[...], tm) * mix
    x1_ref[...] = x1
    n2 = _rms_modulate(x1, ng_ref[...], sc2_ref[...], sh2_ref[...])
    n2_ref[...] = n2

    lg = jnp.dot(n2.astype(BF16), wrt_ref[...].astype(BF16), preferred_element_type=F32) + brt_ref[...]

    lane = lax.broadcasted_iota(jnp.int32, (tm, ROUTE_LANES), 1)
    lane_f = lane.astype(F32)
    e_lane = lane - N_GROUPS
    lane_group = (e_lane >> 3).astype(F32)

    def first_argmax(vals, vmax):
        return jnp.min(jnp.where(vals == vmax, lane_f, float(ROUTE_LANES)), axis=-1, keepdims=True)

    gl = jnp.where(lane < N_GROUPS, lg, NEG_INF)
    gmax = jnp.max(gl, axis=-1, keepdims=True)
    g_idx = first_argmax(gl, gmax)
    g_w = 1.0 / jnp.sum(jnp.exp(gl - gmax), axis=-1, keepdims=True)
    in_group = jnp.where((e_lane >= 0) & (e_lane < N_EXPERTS), lane_group, -1.0) == g_idx
    el = jnp.where(in_group, lg, NEG_INF)
    v1 = jnp.max(el, axis=-1, keepdims=True)
    i1 = first_argmax(el, v1)
    el2 = jnp.where(lane_f == i1, NEG_INF, el)
    v2 = jnp.max(el2, axis=-1, keepdims=True)
    i2 = first_argmax(el2, v2)
    e21 = jnp.exp(v2 - v1)
    w1 = g_w / (1.0 + e21)
    w2 = g_w * e21 / (1.0 + e21)
    e1 = i1 - float(N_GROUPS)
    e2 = i2 - float(N_GROUPS)
    rt_ref[...] = jnp.where(lane == 0, e1, jnp.where(lane == 1, e2, jnp.where(lane == 2, w1,
                            jnp.where(lane == 3, w2, 0.0))))


def _outproj(hg_p, hg_s, o_p, o_s, z, x_p, x_s, g1, sh2, sc2, norm_g, wr, wa, wo, w_route, b_route):
    tm = ROW_TILE
    t = z.shape[0]
    seg = tm // MOD_ROWS
    n_p = x_p.shape[0] // tm
    row = lambda i: (i, 0)
    fix = lambda i: (0, 0)
    row_p = lambda i: (jnp.minimum(i, n_p - 1), 0)
    row_s = lambda i: (jnp.maximum(i - n_p, 0), 0)
    once = pl.Buffered(1)

    def gate(col):
        return pl.BlockSpec((tm, GATE_TN), lambda i: (i, col // GATE_TN))

    return pl.pallas_call(
        functools.partial(_outproj_kernel, n_prompt_tiles=n_p),
        grid=(t // tm,),
        in_specs=[pl.BlockSpec((tm, D_RNN), row_p), pl.BlockSpec((tm, D_RNN), row_s, pipeline_mode=once),
                  pl.BlockSpec((tm, Q_DIM), row_p), pl.BlockSpec((tm, Q_DIM), row_s, pipeline_mode=once),
                  gate(COL_GA), gate(COL_GA + GATE_TN), gate(COL_GB), gate(COL_GB + GATE_TN),
                  pl.BlockSpec((tm, D_MODEL), row_p), pl.BlockSpec((tm, D_MODEL), row_s, pipeline_mode=once),
                  pl.BlockSpec((seg, D_MODEL), row), pl.BlockSpec((seg, D_MODEL), row),
                  pl.BlockSpec((seg, D_MODEL), row),
                  pl.BlockSpec((1, D_MODEL), fix),
                  pl.BlockSpec((D_RNN, D_MODEL), fix, pipeline_mode=once),
                  pl.BlockSpec((Q_DIM, D_MODEL), fix, pipeline_mode=once),
                  pl.BlockSpec((D_MODEL, D_MODEL), fix, pipeline_mode=once),
                  pl.BlockSpec((D_MODEL, ROUTE_LANES), fix),
                  pl.BlockSpec((1, ROUTE_LANES), fix)],
        out_specs=[pl.BlockSpec((tm, D_MODEL), row), pl.BlockSpec((tm, D_MODEL), row),
                   pl.BlockSpec((tm, ROUTE_LANES), row)],
        out_shape=[jax.ShapeDtypeStruct((t, D_MODEL), F32), jax.ShapeDtypeStruct((t, D_MODEL), F32),
                   jax.ShapeDtypeStruct((t, ROUTE_LANES), F32)],
        compiler_params=_cparams(("arbitrary",), OUT_VMEM_LIMIT),
        name="outproj",
    )(hg_p, hg_s, o_p, o_s, z, z, z, z, x_p, x_s, g1, sh2, sc2, norm_g.reshape(1, D_MODEL),
      wr, wa, wo, w_route, b_route)


def _row_gather(src_hbm, idx_ref, base, dst, sem, n_rows):
    def body(r, carry):
        tok = idx_ref[base + r]
        pltpu.make_async_copy(src_hbm.at[pl.ds(tok, 1), :], dst.at[pl.ds(r, 1), :], sem).start()
        return carry
    lax.fori_loop(0, n_rows, body, 0, unroll=8)


def _row_gather_wait(src_hbm, dst, sem, n_rows):
    pltpu.make_async_copy(src_hbm.at[pl.ds(0, n_rows), :], dst, sem).wait()


def _moe_kernel(te_ref, tfirst_ref, tvalid_ref, src_ref, n2_hbm, wg_ref, wu_ref, wd_ref, y_ref,
                xbuf, sem, wg_bf, wu_bf, wd_bf):
    i = pl.program_id(0)
    n_tiles = pl.num_programs(0)
    slot = i % 2

    @pl.when(i == 0)
    def _():
        _row_gather(n2_hbm, src_ref, 0, xbuf.at[0], sem.at[0], MOE_TM)

    @pl.when(i + 1 < n_tiles)
    def _():
        _row_gather(n2_hbm, src_ref, (i + 1) * MOE_TM, xbuf.at[1 - slot], sem.at[1 - slot], MOE_TM)

    @pl.when(tfirst_ref[i] == 1)
    def _():
        wg_bf[...] = wg_ref[0].astype(BF16)
        wu_bf[...] = wu_ref[0].astype(BF16)
        wd_bf[...] = wd_ref[0].astype(BF16)

    _row_gather_wait(n2_hbm, xbuf.at[slot], sem.at[slot], MOE_TM)

    @pl.when(tvalid_ref[i] == 1)
    def _():
        x = xbuf[slot].astype(BF16)
        hgate = jnp.dot(x, wg_bf[...], preferred_element_type=F32)
        hup = jnp.dot(x, wu_bf[...], preferred_element_type=F32)
        act = (hgate * _sigmoid(hgate) * hup).astype(BF16)
        y_ref[...] = jnp.dot(act, wd_bf[...], preferred_element_type=F32)

    @pl.when(tvalid_ref[i] == 0)
    def _():
        y_ref[...] = jnp.zeros(y_ref.shape, F32)


def _moe(n2, w_gate, w_up, w_down, tile_expert, tile_first, tile_valid, src_tok):
    n_tiles = tile_expert.shape[0]
    grid_spec = pltpu.PrefetchScalarGridSpec(
        num_scalar_prefetch=4,
        grid=(n_tiles,),
        in_specs=[pl.BlockSpec(memory_space=pl.ANY),
                  pl.BlockSpec((1, D_MODEL, D_EXPERT), lambda i, te, tf, tv, st: (te[i], 0, 0)),
                  pl.BlockSpec((1, D_MODEL, D_EXPERT), lambda i, te, tf, tv, st: (te[i], 0, 0)),
                  pl.BlockSpec((1, D_EXPERT, D_MODEL), lambda i, te, tf, tv, st: (te[i], 0, 0))],
        out_specs=pl.BlockSpec((MOE_TM, D_MODEL), lambda i, te, tf, tv, st: (i, 0)),
        scratch_shapes=[pltpu.VMEM((2, MOE_TM, D_MODEL), F32),
                        pltpu.SemaphoreType.DMA((2,)),
                        pltpu.VMEM((D_MODEL, D_EXPERT), BF16),
                        pltpu.VMEM((D_MODEL, D_EXPERT), BF16),
                        pltpu.VMEM((D_EXPERT, D_MODEL), BF16)],
    )
    return pl.pallas_call(
        _moe_kernel,
        grid_spec=grid_spec,
        out_shape=jax.ShapeDtypeStruct((n_tiles * MOE_TM, D_MODEL), F32),
        compiler_params=_cparams(("arbitrary",)),
        name="moe",
    )(tile_expert, tile_first, tile_valid, src_tok, n2, w_gate, w_up, w_down)


def _route_plan(e1, e2, n_tok):
    experts = jnp.arange(N_EXPERTS, dtype=jnp.int32)
    flat_e = jnp.concatenate([e1, e2])
    onehot = (flat_e[:, None] == experts[None, :]).astype(jnp.int32)
    csum = jnp.cumsum(onehot, axis=0)
    rank = jnp.sum(csum * onehot, axis=1) - 1
    counts = csum[-1]
    tiles_per = (counts + MOE_TM - 1) // MOE_TM
    tile_end = jnp.cumsum(tiles_per)
    tile_off = tile_end - tiles_per
    slot = jnp.sum(onehot * tile_off[None, :], axis=1) * MOE_TM + rank
    n_tiles = (2 * n_tok) // MOE_TM + N_EXPERTS
    tok = jnp.tile(jnp.arange(n_tok, dtype=jnp.int32), 2)
    src_tok = jnp.zeros((n_tiles * MOE_TM,), jnp.int32).at[slot].set(tok)
    tile_id = jnp.arange(n_tiles, dtype=jnp.int32)
    n_used = tile_end[-1]
    tile_valid = (tile_id < n_used).astype(jnp.int32)
    te = jnp.sum((tile_end[None, :] <= jnp.minimum(tile_id, n_used - 1)[:, None]).astype(jnp.int32), axis=1)
    tile_expert = jnp.minimum(te, N_EXPERTS - 1)
    prev = jnp.concatenate([jnp.full((1,), -1, jnp.int32), tile_expert[:-1]])
    tile_first = (tile_expert != prev).astype(jnp.int32)
    return tile_expert, tile_first, tile_valid, src_tok, slot[:n_tok], slot[n_tok:]


def _final_kernel(p1_ref, p2_ref, ys_hbm, x1_ref, g2_ref, rt_ref, fg_ref, yp_ref, ysm_ref, ybuf, sem,
                  *, n_prompt_tiles):
    i = pl.program_id(0)
    n_tiles = pl.num_programs(0)
    slot = i % 2

    def start(tile, s):
        _row_gather(ys_hbm, p1_ref, tile * ROW_TILE, ybuf.at[s, 0], sem.at[s], ROW_TILE)
        _row_gather(ys_hbm, p2_ref, tile * ROW_TILE, ybuf.at[s, 1], sem.at[s], ROW_TILE)

    @pl.when(i == 0)
    def _():
        start(0, 0)

    @pl.when(i + 1 < n_tiles)
    def _():
        start(i + 1, 1 - slot)

    _row_gather_wait(ys_hbm, ybuf.at[slot, 0], sem.at[slot], ROW_TILE)
    _row_gather_wait(ys_hbm, ybuf.at[slot, 1], sem.at[slot], ROW_TILE)
    rt = rt_ref[...]
    moe = rt[:, 2:3] * ybuf[slot, 0] + rt[:, 3:4] * ybuf[slot, 1]
    x2 = x1_ref[...] + _bcast_rows(g2_ref[...], ROW_TILE) * moe
    y = x2 * lax.rsqrt(jnp.mean(x2 * x2, axis=-1, keepdims=True) + EPS) * fg_ref[...]

    @pl.when(i < n_prompt_tiles)
    def _():
        yp_ref[...] = y

    @pl.when(i >= n_prompt_tiles)
    def _():
        ysm_ref[...] = y


def _final(ys, x1, g2, rt, final_g, p1, p2, n_prompt_rows):
    t = x1.shape[0]
    tm = ROW_TILE
    seg = tm // MOD_ROWS
    n_p = n_prompt_rows // tm
    grid_spec = pltpu.PrefetchScalarGridSpec(
        num_scalar_prefetch=2,
        grid=(t // tm,),
        in_specs=[pl.BlockSpec(memory_space=pl.ANY),
                  pl.BlockSpec((tm, D_MODEL), lambda i, a, b: (i, 0)),
                  pl.BlockSpec((seg, D_MODEL), lambda i, a, b: (i, 0)),
                  pl.BlockSpec((tm, ROUTE_LANES), lambda i, a, b: (i, 0)),
                  pl.BlockSpec((1, D_MODEL), lambda i, a, b: (0, 0))],
        out_specs=[pl.BlockSpec((tm, D_MODEL), lambda i, a, b: (jnp.minimum(i, n_p - 1), 0)),
                   pl.BlockSpec((tm, D_MODEL), lambda i, a, b: (jnp.maximum(i - n_p, 0), 0))],
        scratch_shapes=[pltpu.VMEM((2, 2, tm, D_MODEL), F32),
                        pltpu.SemaphoreType.DMA((2,))],
    )
    return pl.pallas_call(
        functools.partial(_final_kernel, n_prompt_tiles=n_p),
        grid_spec=grid_spec,
        out_shape=[jax.ShapeDtypeStruct((n_prompt_rows, D_MODEL), F32),
                   jax.ShapeDtypeStruct((t - n_prompt_rows, D_MODEL), F32)],
        compiler_params=_cparams(("arbitrary",)),
        name="final",
    )(p1, p2, ys, x1, g2, rt, final_g.reshape(1, D_MODEL))


def kernel(x_prompt, x_sample, cache_k_win, cache_v_win, state_conv, state_rglru, c_prompt, c_sample, w_ada, b_ada, norm1_g, norm2_g, w_in, b_in, w_conv, b_conv, w_rg_a, b_rg_a, w_rg_x, b_rg_x, lru_lambda, w_rnn_out, w_attn_out, w_out, attn_sinks, w_route_group, b_route_group, w_route_expert, b_route_expert, w_exp_gate, w_exp_up, w_exp_down, rel_bias_table, final_norm_g):
    n_b, seq, _ = x_prompt.shape
    d_b, d_seq, _ = x_sample.shape
    assert w_ada.shape[0] == 1, "single trunk layer"
    assert seq % (ATT_CPB * CHUNK) == 0 and seq % MOD_ROWS == 0 and d_seq == MOD_ROWS
    assert d_seq >= CONV_W - 1 and d_b == N_CHAINS
    t_p, t_s = n_b * seq, d_b * d_seq
    t = t_p + t_s
    assert t_s == ROW_TILE and t_p % ROW_TILE == 0 and t % (IN_NSUB * ROW_TILE) == 0
    cw = cache_k_win.shape[2]
    l = 0

    x_p = x_prompt.reshape(t_p, D_MODEL)
    x_s = x_sample.reshape(t_s, D_MODEL)

    n_c = n_b + d_b
    c_all = jnp.pad(jnp.concatenate([c_prompt, c_sample], axis=0), ((0, -n_c % SUBLANES), (0, 0)))
    mod = _ada(c_all, w_ada[l], b_ada[l])

    def per_segment(m):
        return jnp.concatenate([jnp.repeat(m[:n_b], seq // MOD_ROWS, axis=0),
                                jnp.repeat(m[n_b:n_c], d_seq // MOD_ROWS, axis=0)], axis=0)

    sh1, sc1, g1, sh2, sc2, g2 = [per_segment(m) for m in jnp.split(mod, 6, axis=-1)]

    z = _inproj(x_p, x_s, sc1, sh1, norm1_g[l], w_in[l], b_in[l])

    w_gate = jnp.concatenate([w_rg_a[l], w_rg_x[l]], axis=-1)
    b_gate = jnp.concatenate([b_rg_a[l], b_rg_x[l]], axis=-1)[:, None, :]
    hg_p, hl_p = _rglru(z, jnp.zeros((n_b, SUBLANES, D_RNN), F32), jnp.zeros((n_b, N_CHAINS, D_RNN), F32),
                        w_conv[l], b_conv[l], w_gate, b_gate, lru_lambda[l],
                        n_blocks=n_b, n_seq=1, seq_len=seq, row_block0=0, link=True)
    state8 = jnp.pad(state_conv[l], ((0, 0), (SUBLANES - (CONV_W - 1), 0), (0, 0)))
    hg_s, hl_s = _rglru(z, state8, state_rglru[l][None], w_conv[l], b_conv[l], w_gate, b_gate, lru_lambda[l],
                        n_blocks=1, n_seq=d_b, seq_len=d_seq, row_block0=t_p // t_s, link=False)

    sinks = attn_sinks[l]
    bkt_p = _bucket_map(WIN_CHUNKS * CHUNK + jnp.arange(CHUNK), jnp.arange(SPAN))
    o_p = _attn_band(z, rel_bias_table, sinks, bkt_p, n_batch=n_b, seq_len=seq)
    k_new = z[t_p:, COL_K:COL_K + KV_DIM].reshape(d_b, d_seq, KV_DIM)
    v_new = z[t_p:, COL_V:COL_V + KV_DIM].reshape(d_b, d_seq, KV_DIM)
    k_all = jnp.concatenate([cache_k_win[l].reshape(d_b, cw, KV_DIM), k_new], axis=1)
    v_all = jnp.concatenate([cache_v_win[l].reshape(d_b, cw, KV_DIM), v_new], axis=1)
    bkt_s = _bucket_map(cw + jnp.arange(d_seq), jnp.arange(cw + d_seq))
    o_s = _attn_step(z, k_all, v_all, rel_bias_table, sinks, bkt_s, n_batch=d_b, n_q=d_seq, row_block0=t_p // d_seq)

    n_route = N_GROUPS + N_EXPERTS
    w_route = jnp.pad(jnp.concatenate([w_route_group[l], w_route_expert[l]], axis=1),
                      ((0, 0), (0, ROUTE_LANES - n_route)))
    b_route = jnp.pad(jnp.concatenate([b_route_group[l], b_route_expert[l]]),
                      (0, ROUTE_LANES - n_route)).reshape(1, ROUTE_LANES)
    x1, n2, rt = _outproj(hg_p, hg_s, o_p, o_s, z, x_p, x_s, g1, sh2, sc2, norm2_g[l],
                          w_rnn_out[l].astype(BF16), w_attn_out[l].astype(BF16), w_out[l].astype(BF16),
                          w_route, b_route)

    e1 = rt[:, 0].astype(jnp.int32)
    e2 = rt[:, 1].astype(jnp.int32)
    tile_expert, tile_first, tile_valid, src_tok, p1, p2 = _route_plan(e1, e2, t)
    ys = _moe(n2, w_exp_gate[l], w_exp_up[l], w_exp_down[l], tile_expert, tile_first, tile_valid, src_tok)
    y_p, y_s = _final(ys, x1, g2, rt, final_norm_g, p1, p2, t_p)

    win = min(WINDOW, seq)

    def tail(col0, width, n_rows):
        return jnp.stack([z[(b + 1) * seq - n_rows:(b + 1) * seq, col0:col0 + width] for b in range(n_b)])

    kp = tail(COL_K, KV_DIM, win).reshape(n_b, win, N_KV_HEADS, HEAD_DIM)
    vp = tail(COL_V, KV_DIM, win).reshape(n_b, win, N_KV_HEADS, HEAD_DIM)
    cp = tail(COL_XR, D_RNN, CONV_W - 1)
    rp = hl_p[:, N_CHAINS - 1, :]
    ks = k_all[:, -cw:].reshape(d_b, cw, N_KV_HEADS, HEAD_DIM)
    vs = v_all[:, -cw:].reshape(d_b, cw, N_KV_HEADS, HEAD_DIM)
    cs = z[t_p:, COL_XR:COL_XR + D_RNN].reshape(d_b, d_seq, D_RNN)[:, -(CONV_W - 1):]
    rs = hl_s[0]
    return (y_p.reshape(n_b, seq, D_MODEL), y_s.reshape(d_b, d_seq, D_MODEL),
            kp[None], vp[None], cp[None], rp[None], ks[None], vs[None], cs[None], rs[None])
```

```python
import functools
import math

import jax
import jax.numpy as jnp
from jax import lax
from jax.experimental import pallas as pl
from jax.experimental.pallas import tpu as pltpu

F32 = jnp.float32
BF16 = jnp.bfloat16

D_MODEL = 2048
D_RNN = 2048
RNN_BLOCK = 128
CONV_W = 4
LRU_C = 8.0
N_HEADS = 16
N_KV_HEADS = 4
HEAD_DIM = 128
GQA_GROUP = N_HEADS // N_KV_HEADS
Q_DIM = N_HEADS * HEAD_DIM
KV_DIM = N_KV_HEADS * HEAD_DIM
CHUNK = 64
WINDOW = 128
WIN_CHUNKS = WINDOW // CHUNK
SPAN = (WIN_CHUNKS + 1) * CHUNK
N_BUCKETS = 32
MAX_DISTANCE = 128
N_GROUPS = 4
E_PER_GROUP = 8
N_EXPERTS = N_GROUPS * E_PER_GROUP
D_EXPERT = 512
EPS = 1e-6
NEG_INF = -1e30
D_IN = 2 * D_RNN + Q_DIM + 2 * KV_DIM + 2 * D_MODEL
COL_XR, COL_GR, COL_Q = 0, D_RNN, 2 * D_RNN
COL_K = COL_Q + Q_DIM
COL_V = COL_K + KV_DIM
COL_GA = COL_V + KV_DIM
COL_GB = COL_GA + D_MODEL

LANES = 128
SUBLANES = 8
MOD_ROWS = 32
KEY_TILE = 256
VMEM_LIMIT = 56 * 1024 * 1024

ROW_TILE = 256
IN_NSUB = 3
IN_TN = 1024
MOE_TM = 128


def _sigmoid(x):
    return 1.0 / (1.0 + jnp.exp(-x))


def _gelu_tanh(x):
    return 0.5 * x * (1.0 + jnp.tanh(math.sqrt(2.0 / math.pi) * (x + 0.044715 * (x * x * x))))


def _bcast_rows(v, rows):
    n, d = v.shape
    return jnp.broadcast_to(v[:, None, :], (n, MOD_ROWS, d)).reshape(rows, d)


def _rms_modulate(x, gain, scale_seg, shift_seg):
    rows = x.shape[0]
    y = x * lax.rsqrt(jnp.mean(x * x, axis=-1, keepdims=True) + EPS) * gain
    return y * (1.0 + _bcast_rows(scale_seg, rows)) + _bcast_rows(shift_seg, rows)


def _cparams(sem, vmem_limit=VMEM_LIMIT):
    return pltpu.CompilerParams(dimension_semantics=sem, vmem_limit_bytes=vmem_limit)


def _ada_kernel(c_ref, w_ref, b_ref, o_ref):
    c = c_ref[...]
    s = (c * _sigmoid(c)).astype(BF16)
    o_ref[...] = jnp.dot(s, w_ref[...].astype(BF16), preferred_element_type=F32) + b_ref[...]


def _ada(c_all, w_ada, b_ada):
    rows = c_all.shape[0]
    n = w_ada.shape[1]
    tn = 1024
    return pl.pallas_call(
        _ada_kernel,
        grid=(n // tn,),
        in_specs=[pl.BlockSpec((rows, D_MODEL), lambda j: (0, 0)),
                  pl.BlockSpec((D_MODEL, tn), lambda j: (0, j)),
                  pl.BlockSpec((1, tn), lambda j: (0, j))],
        out_specs=pl.BlockSpec((rows, tn), lambda j: (0, j)),
        out_shape=jax.ShapeDtypeStruct((rows, n), F32),
        compiler_params=_cparams(("arbitrary",)),
        name="ada",
    )(c_all, w_ada, b_ada.reshape(1, n))


def _inproj_kernel(x0_ref, x1_ref, x2_ref, xs_ref, sc_ref, sh_ref, g_ref, w_ref, b_ref, z_ref, n1_scr,
                   *, n_prompt_tiles):
    i = pl.program_id(0)
    j = pl.program_id(1)

    @pl.when(j == 0)
    def _():
        tail_is_sample = i * IN_NSUB + (IN_NSUB - 1) >= n_prompt_tiles
        subs = (x0_ref[...], x1_ref[...], jnp.where(tail_is_sample, xs_ref[...], x2_ref[...]))
        seg = ROW_TILE // MOD_ROWS
        for r, x in enumerate(subs):
            n1 = _rms_modulate(x, g_ref[...], sc_ref[r * seg:(r + 1) * seg, :], sh_ref[r * seg:(r + 1) * seg, :])
            n1_scr[r * ROW_TILE:(r + 1) * ROW_TILE, :] = n1.astype(BF16)

    z_ref[...] = jnp.dot(n1_scr[...], w_ref[...], preferred_element_type=F32) + b_ref[...]


def _inproj(x_p, x_s, sc1, sh1, norm_g, w_in, b_in):
    n_p = x_p.shape[0] // ROW_TILE
    t = x_p.shape[0] + x_s.shape[0]
    tm = IN_NSUB * ROW_TILE
    assert x_s.shape[0] == ROW_TILE and t % tm == 0
    seg = tm // MOD_ROWS

    def sub(r):
        return pl.BlockSpec((ROW_TILE, D_MODEL), lambda i, j: (jnp.minimum(i * IN_NSUB + r, n_p - 1), 0))

    return pl.pallas_call(
        functools.partial(_inproj_kernel, n_prompt_tiles=n_p),
        grid=(t // tm, D_IN // IN_TN),
        in_specs=[sub(0), sub(1), sub(2),
                  pl.BlockSpec((ROW_TILE, D_MODEL), lambda i, j: (0, 0)),
                  pl.BlockSpec((seg, D_MODEL), lambda i, j: (i, 0)),
                  pl.BlockSpec((seg, D_MODEL), lambda i, j: (i, 0)),
                  pl.BlockSpec((1, D_MODEL), lambda i, j: (0, 0)),
                  pl.BlockSpec((D_MODEL, IN_TN), lambda i, j: (0, j)),
                  pl.BlockSpec((1, IN_TN), lambda i, j: (0, j))],
        out_specs=pl.BlockSpec((tm, IN_TN), lambda i, j: (i, j)),
        out_shape=jax.ShapeDtypeStruct((t, D_IN), F32),
        scratch_shapes=[pltpu.VMEM((tm, D_MODEL), BF16)],
        compiler_params=_cparams(("arbitrary", "arbitrary")),
        name="inproj",
    )(x_p, x_p, x_p, x_s, sc1, sh1, norm_g.reshape(1, D_MODEL), w_in, b_in.reshape(1, D_IN))


RG_CW = 256
RG_SLABS = RG_CW // LANES
N_CHAINS = SUBLANES
CHAIN_PAD = 8


def _rglru_kernel(xr_ref, gr_ref, st_ref, h0_ref, wc_ref, bc_ref, wg_ref, bg_ref, lam_ref,
                  hg_ref, hl_ref, ext_scr, a_scr, u_scr, *, n_seq, seq_len, link):
    rows = n_seq * seq_len
    cl = rows // N_CHAINS
    pitch = cl + CHAIN_PAD

    for s in range(n_seq):
        ext_scr[s, 0:SUBLANES, :] = st_ref[s]
        ext_scr[s, SUBLANES:SUBLANES + seq_len, :] = xr_ref[s * seq_len:(s + 1) * seq_len, :]

    z = -lam_ref[...]
    softplus = jnp.maximum(z, 0.0) + jnp.log1p(jnp.exp(-jnp.abs(z)))
    nsl = -LRU_C * softplus

    for c in range(N_CHAINS):
        s, r0 = divmod(c * cl, seq_len)
        xc = bc_ref[...]
        for j in range(CONV_W):
            xc = xc + wc_ref[j:j + 1, :] * ext_scr[s, pl.ds(SUBLANES - (CONV_W - 1) + j + r0, cl), :]
        for sl in range(RG_SLABS):
            lanes = slice(sl * LANES, (sl + 1) * LANES)
            xb = xc[:, lanes]
            g = jnp.dot(xb.astype(BF16), wg_ref[sl].astype(BF16), preferred_element_type=F32) + bg_ref[sl]
            r = _sigmoid(g[:, :LANES])
            i = _sigmoid(g[:, LANES:])
            th = jnp.tanh(0.5 * (nsl[:, lanes] * r))
            rcp = 1.0 / (1.0 - th)
            a_scr[sl, c * pitch:c * pitch + cl, :] = (1.0 + th) * rcp
            u_scr[sl, c * pitch:c * pitch + cl, :] = (2.0 * jnp.sqrt(-th) * rcp) * (i * xb)

    def step(t, carry):
        hs, ps = carry
        new_h, new_p = [], []
        for sl in range(RG_SLABS):
            a = a_scr[sl, pl.ds(t, N_CHAINS, stride=pitch), :]
            u = u_scr[sl, pl.ds(t, N_CHAINS, stride=pitch), :]
            h = a * hs[sl] + u
            u_scr[sl, pl.ds(t, N_CHAINS, stride=pitch), :] = h
            new_h.append(h)
            if link:
                p = a * ps[sl]
                a_scr[sl, pl.ds(t, N_CHAINS, stride=pitch), :] = p
                new_p.append(p)
            else:
                new_p.append(ps[sl])
        return tuple(new_h), tuple(new_p)

    h_init = tuple(h0_ref[0, :, sl * LANES:(sl + 1) * LANES] for sl in range(RG_SLABS))
    p_init = tuple(jnp.ones((N_CHAINS, LANES), F32) for _ in range(RG_SLABS))
    h_end, p_end = lax.fori_loop(0, cl, step, (h_init, p_init), unroll=8)

    row = lax.broadcasted_iota(jnp.int32, (N_CHAINS, LANES), 0)
    for sl in range(RG_SLABS):
        lanes = slice(sl * LANES, (sl + 1) * LANES)
        if link:
            def shift_down(v):
                return jnp.where(row == 0, 0.0, pltpu.roll(v, 1, axis=0))
            hh = h_end[sl]
            for _ in range(N_CHAINS - 1):
                hh = h_end[sl] + p_end[sl] * shift_down(hh)
            carry_in = shift_down(hh)
        else:
            hh = h_end[sl]
        hl_ref[0, :, lanes] = hh
        for c in range(N_CHAINS):
            h = u_scr[sl, c * pitch:c * pitch + cl, :]
            if link:
                h = h + a_scr[sl, c * pitch:c * pitch + cl, :] * carry_in[c:c + 1, :]
            gg = _gelu_tanh(gr_ref[c * cl:(c + 1) * cl, lanes])
            hg_ref[c * cl:(c + 1) * cl, lanes] = (h * gg).astype(BF16)


def _rglru(z, state8, h0, w_conv, b_conv, w_gate, b_gate, lam, *, n_blocks, n_seq, seq_len, row_block0, link):
    rows = n_seq * seq_len
    cl = rows // N_CHAINS
    ncb = D_RNN // RG_CW
    gr0 = COL_GR // RG_CW
    kern = functools.partial(_rglru_kernel, n_seq=n_seq, seq_len=seq_len, link=link)
    return pl.pallas_call(
        kern,
        grid=(n_blocks, ncb),
        in_specs=[pl.BlockSpec((rows, RG_CW), lambda b, n: (row_block0 + b, n)),
                  pl.BlockSpec((rows, RG_CW), lambda b, n: (row_block0 + b, gr0 + n)),
                  pl.BlockSpec((n_seq, SUBLANES, RG_CW), lambda b, n: (b, 0, n)),
                  pl.BlockSpec((1, N_CHAINS, RG_CW), lambda b, n: (b, 0, n)),
                  pl.BlockSpec((CONV_W, RG_CW), lambda b, n: (0, n)),
                  pl.BlockSpec((1, RG_CW), lambda b, n: (0, n)),
                  pl.BlockSpec((RG_SLABS, RNN_BLOCK, 2 * RNN_BLOCK), lambda b, n: (n, 0, 0)),
                  pl.BlockSpec((RG_SLABS, 1, 2 * RNN_BLOCK), lambda b, n: (n, 0, 0)),
                  pl.BlockSpec((1, RG_CW), lambda b, n: (0, n))],
        out_specs=[pl.BlockSpec((rows, RG_CW), lambda b, n: (b, n)),
                   pl.BlockSpec((1, N_CHAINS, RG_CW), lambda b, n: (b, 0, n))],
        out_shape=[jax.ShapeDtypeStruct((n_blocks * rows, D_RNN), BF16),
                   jax.ShapeDtypeStruct((n_blocks, N_CHAINS, D_RNN), F32)],
        scratch_shapes=[pltpu.VMEM((n_seq, SUBLANES + seq_len, RG_CW), F32),
                        pltpu.VMEM((RG_SLABS, N_CHAINS * (cl + CHAIN_PAD), LANES), F32),
                        pltpu.VMEM((RG_SLABS, N_CHAINS * (cl + CHAIN_PAD), LANES), F32)],
        compiler_params=_cparams(("arbitrary", "arbitrary")),
        name="rglru_link" if link else "rglru_step",
    )(z, z, state8, h0, w_conv, b_conv.reshape(1, D_RNN), w_gate, b_gate, lam.reshape(1, D_RNN))


def _build_bias(tbl_ref, bkt_ref, bias_scr, n_q):
    bkt = bkt_ref[...]
    base = jnp.where(bkt < 0, NEG_INF, 0.0)
    for h in range(N_HEADS):
        acc = base
        for bk in range(N_BUCKETS):
            acc = jnp.where(bkt == bk, tbl_ref[bk, h], acc)
        hk, g = divmod(h, GQA_GROUP)
        bias_scr[hk, g * n_q:(g + 1) * n_q, :] = acc


def _sink_columns(sink_ref, n_q):
    row = lax.broadcasted_iota(jnp.int32, (GQA_GROUP * n_q, 1), 0)
    cols = []
    for hk in range(N_KV_HEADS):
        col = jnp.full((GQA_GROUP * n_q, 1), sink_ref[hk * GQA_GROUP], F32)
        for g in range(1, GQA_GROUP):
            col = jnp.where(row >= g * n_q, sink_ref[hk * GQA_GROUP + g], col)
        cols.append(col)
    return cols


def _attend(q_of, k_of, v_of, bias_scr, sinks, key_ok, store, n_q):
    scores = []
    for hk in range(N_KV_HEADS):
        qg = jnp.concatenate([q_of(hk * GQA_GROUP + g) for g in range(GQA_GROUP)], axis=0)
        s = lax.dot_general(qg, k_of(hk), (((1,), (1,)), ((), ())), preferred_element_type=F32)
        s = s * (HEAD_DIM ** -0.5) + bias_scr[hk]
        if key_ok is not None:
            s = jnp.where(key_ok, s, NEG_INF)
        scores.append(s)
    probs = []
    for hk in range(N_KV_HEADS):
        s, sink = scores[hk], sinks[hk]
        m = jnp.maximum(jnp.max(s, axis=-1, keepdims=True), sink)
        p = jnp.exp(s - m)
        inv = 1.0 / (jnp.sum(p, axis=-1, keepdims=True) + jnp.exp(sink - m))
        probs.append((p * inv).astype(BF16))
    for hk in range(N_KV_HEADS):
        o = jnp.dot(probs[hk], v_of(hk), preferred_element_type=F32)
        for g in range(GQA_GROUP):
            store(hk * GQA_GROUP + g, o[g * n_q:(g + 1) * n_q, :].astype(BF16))


ATT_CPB = 8
ATT_LEAD = WIN_CHUNKS * CHUNK
ATT_TAIL = KEY_TILE - SPAN


def _attn_band_kernel(tbl_ref, sink_ref, q_ref, k_ref, v_ref, bkt_ref, o_ref, kpad, vpad, bias_scr, *, seq_len):
    b = pl.program_id(0)
    cg = pl.program_id(1)

    @pl.when((b == 0) & (cg == 0))
    def _():
        _build_bias(tbl_ref, bkt_ref, bias_scr, CHUNK)

    @pl.when(cg == 0)
    def _():
        for ref, pad in ((k_ref, kpad), (v_ref, vpad)):
            pad[0:ATT_LEAD, :] = jnp.zeros((ATT_LEAD, KV_DIM), BF16)
            pad[ATT_LEAD:ATT_LEAD + seq_len, :] = ref[...].astype(BF16)
            pad[ATT_LEAD + seq_len:, :] = jnp.zeros((ATT_TAIL, KV_DIM), BF16)

    kidx = lax.broadcasted_iota(jnp.int32, (1, KEY_TILE), 1)
    sinks = _sink_columns(sink_ref, CHUNK)

    def chunk(c, carry):
        q0 = pl.multiple_of(c * CHUNK, CHUNK)
        start = pl.multiple_of((cg * ATT_CPB + c) * CHUNK, CHUNK)
        key_ok = start + kidx >= ATT_LEAD

        def head_cols(h):
            return slice(h * HEAD_DIM, (h + 1) * HEAD_DIM)

        def store(h, val):
            o_ref[pl.ds(q0, CHUNK), head_cols(h)] = val

        _attend(lambda h: q_ref[pl.ds(q0, CHUNK), head_cols(h)].astype(BF16),
                lambda hk: kpad[pl.ds(start, KEY_TILE), head_cols(hk)],
                lambda hk: vpad[pl.ds(start, KEY_TILE), head_cols(hk)],
                bias_scr, sinks, key_ok, store, CHUNK)
        return carry

    lax.fori_loop(0, ATT_CPB, chunk, 0)


def _attn_band(z, table, sinks, bkt, *, n_batch, seq_len):
    rows = ATT_CPB * CHUNK
    ng = seq_len // rows
    kern = functools.partial(_attn_band_kernel, seq_len=seq_len)
    smem = pl.BlockSpec(memory_space=pltpu.SMEM)
    pad_rows = ATT_LEAD + seq_len + ATT_TAIL
    return pl.pallas_call(
        kern,
        grid=(n_batch, ng),
        in_specs=[smem, smem,
                  pl.BlockSpec((rows, Q_DIM), lambda b, c: (b * ng + c, COL_Q // Q_DIM)),
                  pl.BlockSpec((seq_len, KV_DIM), lambda b, c: (b, COL_K // KV_DIM)),
                  pl.BlockSpec((seq_len, KV_DIM), lambda b, c: (b, COL_V // KV_DIM)),
                  pl.BlockSpec((CHUNK, KEY_TILE), lambda b, c: (0, 0))],
        out_specs=pl.BlockSpec((rows, Q_DIM), lambda b, c: (b * ng + c, 0)),
        out_shape=jax.ShapeDtypeStruct((n_batch * seq_len, Q_DIM), BF16),
        scratch_shapes=[pltpu.VMEM((pad_rows, KV_DIM), BF16),
                        pltpu.VMEM((pad_rows, KV_DIM), BF16),
                        pltpu.VMEM((N_KV_HEADS, GQA_GROUP * CHUNK, KEY_TILE), F32)],
        compiler_params=_cparams(("arbitrary", "arbitrary")),
        name="attn_band",
    )(table, sinks, z, z, z, bkt)


def _attn_step_kernel(tbl_ref, sink_ref, q_ref, k_ref, v_ref, bkt_ref, o_ref, kbuf, vbuf, bias_scr,
                      *, n_keys, n_q):
    b = pl.program_id(0)

    @pl.when(b == 0)
    def _():
        _build_bias(tbl_ref, bkt_ref, bias_scr, n_q)
        kbuf[n_keys:, :] = jnp.zeros((KEY_TILE - n_keys, KV_DIM), BF16)
        vbuf[n_keys:, :] = jnp.zeros((KEY_TILE - n_keys, KV_DIM), BF16)

    kbuf[0:n_keys, :] = k_ref[0].astype(BF16)
    vbuf[0:n_keys, :] = v_ref[0].astype(BF16)

    def head_cols(h):
        return slice(h * HEAD_DIM, (h + 1) * HEAD_DIM)

    def store(h, val):
        o_ref[:, head_cols(h)] = val

    _attend(lambda h: q_ref[:, head_cols(h)].astype(BF16),
            lambda hk: kbuf[:, head_cols(hk)], lambda hk: vbuf[:, head_cols(hk)],
            bias_scr, _sink_columns(sink_ref, n_q), None, store, n_q)


def _attn_step(z, k_all, v_all, table, sinks, bkt, *, n_batch, n_q, row_block0):
    n_keys = k_all.shape[1]
    kern = functools.partial(_attn_step_kernel, n_keys=n_keys, n_q=n_q)
    smem = pl.BlockSpec(memory_space=pltpu.SMEM)
    return pl.pallas_call(
        kern,
        grid=(n_batch,),
        in_specs=[smem, smem,
                  pl.BlockSpec((n_q, Q_DIM), lambda b: (row_block0 + b, COL_Q // Q_DIM)),
                  pl.BlockSpec((1, n_keys, KV_DIM), lambda b: (b, 0, 0)),
                  pl.BlockSpec((1, n_keys, KV_DIM), lambda b: (b, 0, 0)),
                  pl.BlockSpec((n_q, KEY_TILE), lambda b: (0, 0))],
        out_specs=pl.BlockSpec((n_q, Q_DIM), lambda b: (b, 0)),
        out_shape=jax.ShapeDtypeStruct((n_batch * n_q, Q_DIM), BF16),
        scratch_shapes=[pltpu.VMEM((KEY_TILE, KV_DIM), BF16),
                        pltpu.VMEM((KEY_TILE, KV_DIM), BF16),
                        pltpu.VMEM((N_KV_HEADS, GQA_GROUP * n_q, KEY_TILE), F32)],
        compiler_params=_cparams(("arbitrary",)),
        name="attn_step",
    )(table, sinks, z, k_all, v_all, bkt)


def _t5_bucket(rel):
    nb = N_BUCKETS // 2
    ret = jnp.where(rel > 0, nb, 0)
    n = jnp.abs(rel)
    max_exact = nb // 2
    nf = jnp.maximum(n, 1).astype(jnp.float32)
    large = max_exact + (jnp.log(nf / max_exact) / math.log(MAX_DISTANCE / max_exact)
                         * (nb - max_exact)).astype(jnp.int32)
    large = jnp.minimum(large, nb - 1)
    return ret + jnp.where(n < max_exact, n, large)


def _bucket_map(q_pos, k_pos):
    bkt = _t5_bucket(k_pos[None, :] - q_pos[:, None]).astype(jnp.int32)
    return jnp.pad(bkt, ((0, 0), (0, KEY_TILE - k_pos.shape[0])), constant_values=-1)


ROUTE_LANES = LANES
GATE_TN = 1024
OUT_VMEM_LIMIT = 60 * 1024 * 1024


def _outproj_kernel(hgp_ref, hgs_ref, op_ref, os_ref, ga0_ref, ga1_ref, gb0_ref, gb1_ref, xp_ref, xs_ref,
                    g1_ref, sh2_ref, sc2_ref, ng_ref, wr_ref, wa_ref, wo_ref, wrt_ref, brt_ref,
                    x1_ref, n2_ref, rt_ref, *, n_prompt_tiles):
    tm = x1_ref.shape[0]
    is_prompt = pl.program_id(0) < n_prompt_tiles
    hg = jnp.where(is_prompt, hgp_ref[...], hgs_ref[...])
    o = jnp.where(is_prompt, op_ref[...], os_ref[...])
    x = jnp.where(is_prompt, xp_ref[...], xs_ref[...])
    ya = jnp.dot(hg, wr_ref[...], preferred_element_type=F32)
    yb = jnp.dot(o, wa_ref[...], preferred_element_type=F32)
    halves = []
    for k, (ga_ref, gb_ref) in enumerate(((ga0_ref, gb0_ref), (ga1_ref, gb1_ref))):
        cols = slice(k * GATE_TN, (k + 1) * GATE_TN)
        halves.append((_sigmoid(ga_ref[...]) * ya[:, cols] + _sigmoid(gb_ref[...]) * yb[:, cols]).astype(BF16))
    merged = jnp.concatenate(halves, axis=1)
    mix = jnp.dot(merged, wo_ref[...], preferred_element_type=F32)
    x1 = x + _bcast_rows(g1_ref[...], tm) * mix
    x1_ref[...] = x1
    n2 = _rms_modulate(x1, ng_ref[...], sc2_ref[...], sh2_ref[...])
    n2_ref[...] = n2

    lg = jnp.dot(n2.astype(BF16), wrt_ref[...].astype(BF16), preferred_element_type=F32) + brt_ref[...]

    lane = lax.broadcasted_iota(jnp.int32, (tm, ROUTE_LANES), 1)
    lane_f = lane.astype(F32)
    e_lane = lane - N_GROUPS
    lane_group = (e_lane >> 3).astype(F32)

    def first_argmax(vals, vmax):
        return jnp.min(jnp.where(vals == vmax, lane_f, float(ROUTE_LANES)), axis=-1, keepdims=True)

    gl = jnp.where(lane < N_GROUPS, lg, NEG_INF)
    gmax = jnp.max(gl, axis=-1, keepdims=True)
    g_idx = first_argmax(gl, gmax)
    g_w = 1.0 / jnp.sum(jnp.exp(gl - gmax), axis=-1, keepdims=True)
    in_group = jnp.where((e_lane >= 0) & (e_lane < N_EXPERTS), lane_group, -1.0) == g_idx
    el = jnp.where(in_group, lg, NEG_INF)
    v1 = jnp.max(el, axis=-1, keepdims=True)
    i1 = first_argmax(el, v1)
    el2 = jnp.where(lane_f == i1, NEG_INF, el)
    v2 = jnp.max(el2, axis=-1, keepdims=True)
    i2 = first_argmax(el2, v2)
    e21 = jnp.exp(v2 - v1)
    w1 = g_w / (1.0 + e21)
    w2 = g_w * e21 / (1.0 + e21)
    e1 = i1 - float(N_GROUPS)
    e2 = i2 - float(N_GROUPS)
    rt_ref[...] = jnp.where(lane == 0, e1, jnp.where(lane == 1, e2, jnp.where(lane == 2, w1,
                            jnp.where(lane == 3, w2, 0.0))))


def _outproj(hg_p, hg_s, o_p, o_s, z, x_p, x_s, g1, sh2, sc2, norm_g, wr, wa, wo, w_route, b_route):
    tm = ROW_TILE
    t = z.shape[0]
    seg = tm // MOD_ROWS
    n_p = x_p.shape[0] // tm
    row = lambda i: (i, 0)
    fix = lambda i: (0, 0)
    row_p = lambda i: (jnp.minimum(i, n_p - 1), 0)
    row_s = lambda i: (jnp.maximum(i - n_p, 0), 0)
    once = pl.Buffered(1)

    def gate(col):
        return pl.BlockSpec((tm, GATE_TN), lambda i: (i, col // GATE_TN))

    return pl.pallas_call(
        functools.partial(_outproj_kernel, n_prompt_tiles=n_p),
        grid=(t // tm,),
        in_specs=[pl.BlockSpec((tm, D_RNN), row_p), pl.BlockSpec((tm, D_RNN), row_s, pipeline_mode=once),
                  pl.BlockSpec((tm, Q_DIM), row_p), pl.BlockSpec((tm, Q_DIM), row_s, pipeline_mode=once),
                  gate(COL_GA), gate(COL_GA + GATE_TN), gate(COL_GB), gate(COL_GB + GATE_TN),
                  pl.BlockSpec((tm, D_MODEL), row_p), pl.BlockSpec((tm, D_MODEL), row_s, pipeline_mode=once),
                  pl.BlockSpec((seg, D_MODEL), row), pl.BlockSpec((seg, D_MODEL), row),
                  pl.BlockSpec((seg, D_MODEL), row),
                  pl.BlockSpec((1, D_MODEL), fix),
                  pl.BlockSpec((D_RNN, D_MODEL), fix, pipeline_mode=once),
                  pl.BlockSpec((Q_DIM, D_MODEL), fix, pipeline_mode=once),
                  pl.BlockSpec((D_MODEL, D_MODEL), fix, pipeline_mode=once),
                  pl.BlockSpec((D_MODEL, ROUTE_LANES), fix),
                  pl.BlockSpec((1, ROUTE_LANES), fix)],
        out_specs=[pl.BlockSpec((tm, D_MODEL), row), pl.BlockSpec((tm, D_MODEL), row),
                   pl.BlockSpec((tm, ROUTE_LANES), row)],
        out_shape=[jax.ShapeDtypeStruct((t, D_MODEL), F32), jax.ShapeDtypeStruct((t, D_MODEL), F32),
                   jax.ShapeDtypeStruct((t, ROUTE_LANES), F32)],
        compiler_params=_cparams(("arbitrary",), OUT_VMEM_LIMIT),
        name="outproj",
    )(hg_p, hg_s, o_p, o_s, z, z, z, z, x_p, x_s, g1, sh2, sc2, norm_g.reshape(1, D_MODEL),
      wr, wa, wo, w_route, b_route)


def _row_gather(src_hbm, idx_ref, base, dst, sem, n_rows):
    for r in range(n_rows):
        tok = idx_ref[base + r]
        pltpu.make_async_copy(src_hbm.at[pl.ds(tok, 1), :], dst.at[pl.ds(r, 1), :], sem).start()


def _row_gather_wait(src_hbm, dst, sem, n_rows):
    pltpu.make_async_copy(src_hbm.at[pl.ds(0, n_rows), :], dst, sem).wait()


def _moe_kernel(te_ref, tfirst_ref, tvalid_ref, tnext_ref, src_ref, n2_hbm, wg_hbm, wu_hbm, wd_hbm, y_ref,
                xbuf, xsem, wg_st, wu_st, wd_st, wsem, wg_bf, wu_bf, wd_bf):
    i = pl.program_id(0)
    n_tiles = pl.num_programs(0)
    slot = i % 2
    stages = ((wg_hbm, wg_st, wg_bf), (wu_hbm, wu_st, wu_bf), (wd_hbm, wd_st, wd_bf))

    def weight_copy(k, e):
        hbm, st, _ = stages[k]
        return pltpu.make_async_copy(hbm.at[e], st, wsem.at[k])

    @pl.when(i == 0)
    def _():
        for k in range(len(stages)):
            weight_copy(k, te_ref[0]).start(priority=1)
        _row_gather(n2_hbm, src_ref, 0, xbuf.at[0], xsem.at[0], MOE_TM)

    @pl.when(tvalid_ref[i] == 1)
    def _():
        nxt = jnp.minimum(i + 1, n_tiles - 1)

        @pl.when((i + 1 < n_tiles) & (tvalid_ref[nxt] == 1))
        def _():
            _row_gather(n2_hbm, src_ref, (i + 1) * MOE_TM, xbuf.at[1 - slot], xsem.at[1 - slot], MOE_TM)

        @pl.when(tfirst_ref[i] == 1)
        def _():
            for k, (_, st, bf) in enumerate(stages):
                weight_copy(k, te_ref[i]).wait()
                bf[...] = st[...].astype(BF16)

            @pl.when(tnext_ref[i] >= 0)
            def _():
                for k in range(len(stages)):
                    weight_copy(k, tnext_ref[i]).start(priority=1)

        _row_gather_wait(n2_hbm, xbuf.at[slot], xsem.at[slot], MOE_TM)
        x = xbuf[slot].astype(BF16)
        hgate = jnp.dot(x, wg_bf[...], preferred_element_type=F32)
        hup = jnp.dot(x, wu_bf[...], preferred_element_type=F32)
        act = (hgate * _sigmoid(hgate) * hup).astype(BF16)
        y_ref[...] = jnp.dot(act, wd_bf[...], preferred_element_type=F32)

    @pl.when(tvalid_ref[i] == 0)
    def _():
        y_ref[...] = jnp.zeros(y_ref.shape, F32)


def _moe(n2, w_gate, w_up, w_down, tile_expert, tile_first, tile_valid, tile_next, src_tok):
    n_tiles = tile_expert.shape[0]
    hbm = pl.BlockSpec(memory_space=pl.ANY)
    grid_spec = pltpu.PrefetchScalarGridSpec(
        num_scalar_prefetch=5,
        grid=(n_tiles,),
        in_specs=[hbm, hbm, hbm, hbm],
        out_specs=pl.BlockSpec((MOE_TM, D_MODEL), lambda i, te, tf, tv, tn, st: (i, 0)),
        scratch_shapes=[pltpu.VMEM((2, MOE_TM, D_MODEL), F32),
                        pltpu.SemaphoreType.DMA((2,)),
                        pltpu.VMEM((D_MODEL, D_EXPERT), F32),
                        pltpu.VMEM((D_MODEL, D_EXPERT), F32),
                        pltpu.VMEM((D_EXPERT, D_MODEL), F32),
                        pltpu.SemaphoreType.DMA((3,)),
                        pltpu.VMEM((D_MODEL, D_EXPERT), BF16),
                        pltpu.VMEM((D_MODEL, D_EXPERT), BF16),
                        pltpu.VMEM((D_EXPERT, D_MODEL), BF16)],
    )
    return pl.pallas_call(
        _moe_kernel,
        grid_spec=grid_spec,
        out_shape=jax.ShapeDtypeStruct((n_tiles * MOE_TM, D_MODEL), F32),
        compiler_params=_cparams(("arbitrary",)),
        name="moe",
    )(tile_expert, tile_first, tile_valid, tile_next, src_tok, n2, w_gate, w_up, w_down)


def _route_plan(e1, e2, n_tok):
    experts = jnp.arange(N_EXPERTS, dtype=jnp.int32)
    flat_e = jnp.concatenate([e1, e2])
    onehot = (flat_e[:, None] == experts[None, :]).astype(jnp.int32)
    csum = jnp.cumsum(onehot, axis=0)
    rank = jnp.sum(csum * onehot, axis=1) - 1
    counts = csum[-1]
    tiles_per = (counts + MOE_TM - 1) // MOE_TM
    tile_end = jnp.cumsum(tiles_per)
    tile_off = tile_end - tiles_per
    slot = jnp.sum(onehot * tile_off[None, :], axis=1) * MOE_TM + rank
    n_tiles = (2 * n_tok) // MOE_TM + N_EXPERTS
    tok = jnp.tile(jnp.arange(n_tok, dtype=jnp.int32), 2)
    src_tok = jnp.zeros((n_tiles * MOE_TM,), jnp.int32).at[slot].set(tok)
    tile_id = jnp.arange(n_tiles, dtype=jnp.int32)
    n_used = tile_end[-1]
    tile_valid = (tile_id < n_used).astype(jnp.int32)
    te = jnp.sum((tile_end[None, :] <= jnp.minimum(tile_id, n_used - 1)[:, None]).astype(jnp.int32), axis=1)
    tile_expert = jnp.minimum(te, N_EXPERTS - 1)
    prev = jnp.concatenate([jnp.full((1,), -1, jnp.int32), tile_expert[:-1]])
    tile_first = (tile_expert != prev).astype(jnp.int32)
    next_tile = tile_end[tile_expert]
    tile_next = jnp.where(next_tile < n_used, tile_expert[jnp.minimum(next_tile, n_tiles - 1)], -1)
    return tile_expert, tile_first, tile_valid, tile_next, src_tok, slot[:n_tok], slot[n_tok:]


def _final_kernel(p1_ref, p2_ref, ys_hbm, x1_ref, g2_ref, rt_ref, fg_ref, yp_ref, ysm_ref, ybuf, sem,
                  *, n_prompt_tiles):
    i = pl.program_id(0)
    n_tiles = pl.num_programs(0)
    slot = i % 2

    def start(tile, s):
        _row_gather(ys_hbm, p1_ref, tile * ROW_TILE, ybuf.at[s, 0], sem.at[s], ROW_TILE)
        _row_gather(ys_hbm, p2_ref, tile * ROW_TILE, ybuf.at[s, 1], sem.at[s], ROW_TILE)

    @pl.when(i == 0)
    def _():
        start(0, 0)

    @pl.when(i + 1 < n_tiles)
    def _():
        start(i + 1, 1 - slot)

    _row_gather_wait(ys_hbm, ybuf.at[slot, 0], sem.at[slot], ROW_TILE)
    _row_gather_wait(ys_hbm, ybuf.at[slot, 1], sem.at[slot], ROW_TILE)
    rt = rt_ref[...]
    moe = rt[:, 2:3] * ybuf[slot, 0] + rt[:, 3:4] * ybuf[slot, 1]
    x2 = x1_ref[...] + _bcast_rows(g2_ref[...], ROW_TILE) * moe
    y = x2 * lax.rsqrt(jnp.mean(x2 * x2, axis=-1, keepdims=True) + EPS) * fg_ref[...]

    @pl.when(i < n_prompt_tiles)
    def _():
        yp_ref[...] = y

    @pl.when(i >= n_prompt_tiles)
    def _():
        ysm_ref[...] = y


def _final(ys, x1, g2, rt, final_g, p1, p2, n_prompt_rows):
    t = x1.shape[0]
    tm = ROW_TILE
    seg = tm // MOD_ROWS
    n_p = n_prompt_rows // tm
    grid_spec = pltpu.PrefetchScalarGridSpec(
        num_scalar_prefetch=2,
        grid=(t // tm,),
        in_specs=[pl.BlockSpec(memory_space=pl.ANY),
                  pl.BlockSpec((tm, D_MODEL), lambda i, a, b: (i, 0)),
                  pl.BlockSpec((seg, D_MODEL), lambda i, a, b: (i, 0)),
                  pl.BlockSpec((tm, ROUTE_LANES), lambda i, a, b: (i, 0)),
                  pl.BlockSpec((1, D_MODEL), lambda i, a, b: (0, 0))],
        out_specs=[pl.BlockSpec((tm, D_MODEL), lambda i, a, b: (jnp.minimum(i, n_p - 1), 0)),
                   pl.BlockSpec((tm, D_MODEL), lambda i, a, b: (jnp.maximum(i - n_p, 0), 0))],
        scratch_shapes=[pltpu.VMEM((2, 2, tm, D_MODEL), F32),
                        pltpu.SemaphoreType.DMA((2,))],
    )
    return pl.pallas_call(
        functools.partial(_final_kernel, n_prompt_tiles=n_p),
        grid_spec=grid_spec,
        out_shape=[jax.ShapeDtypeStruct((n_prompt_rows, D_MODEL), F32),
                   jax.ShapeDtypeStruct((t - n_prompt_rows, D_MODEL), F32)],
        compiler_params=_cparams(("arbitrary",)),
        name="final",
    )(p1, p2, ys, x1, g2, rt, final_g.reshape(1, D_MODEL))


def kernel(x_prompt, x_sample, cache_k_win, cache_v_win, state_conv, state_rglru, c_prompt, c_sample, w_ada, b_ada, norm1_g, norm2_g, w_in, b_in, w_conv, b_conv, w_rg_a, b_rg_a, w_rg_x, b_rg_x, lru_lambda, w_rnn_out, w_attn_out, w_out, attn_sinks, w_route_group, b_route_group, w_route_expert, b_route_expert, w_exp_gate, w_exp_up, w_exp_down, rel_bias_table, final_norm_g):
    n_b, seq, _ = x_prompt.shape
    d_b, d_seq, _ = x_sample.shape
    assert w_ada.shape[0] == 1, "single trunk layer"
    assert seq % (ATT_CPB * CHUNK) == 0 and seq % MOD_ROWS == 0 and d_seq == MOD_ROWS
    assert d_seq >= CONV_W - 1 and d_b == N_CHAINS
    t_p, t_s = n_b * seq, d_b * d_seq
    t = t_p + t_s
    assert t_s == ROW_TILE and t_p % ROW_TILE == 0 and t % (IN_NSUB * ROW_TILE) == 0
    cw = cache_k_win.shape[2]
    l = 0

    x_p = x_prompt.reshape(t_p, D_MODEL)
    x_s = x_sample.reshape(t_s, D_MODEL)

    n_c = n_b + d_b
    c_all = jnp.pad(jnp.concatenate([c_prompt, c_sample], axis=0), ((0, -n_c % SUBLANES), (0, 0)))
    mod = _ada(c_all, w_ada[l], b_ada[l])

    def per_segment(m):
        return jnp.concatenate([jnp.repeat(m[:n_b], seq // MOD_ROWS, axis=0),
                                jnp.repeat(m[n_b:n_c], d_seq // MOD_ROWS, axis=0)], axis=0)

    sh1, sc1, g1, sh2, sc2, g2 = [per_segment(m) for m in jnp.split(mod, 6, axis=-1)]

    z = _inproj(x_p, x_s, sc1, sh1, norm1_g[l], w_in[l].astype(BF16), b_in[l])

    w_gate = jnp.concatenate([w_rg_a[l], w_rg_x[l]], axis=-1)
    b_gate = jnp.concatenate([b_rg_a[l], b_rg_x[l]], axis=-1)[:, None, :]
    hg_p, hl_p = _rglru(z, jnp.zeros((n_b, SUBLANES, D_RNN), F32), jnp.zeros((n_b, N_CHAINS, D_RNN), F32),
                        w_conv[l], b_conv[l], w_gate, b_gate, lru_lambda[l],
                        n_blocks=n_b, n_seq=1, seq_len=seq, row_block0=0, link=True)
    state8 = jnp.pad(state_conv[l], ((0, 0), (SUBLANES - (CONV_W - 1), 0), (0, 0)))
    hg_s, hl_s = _rglru(z, state8, state_rglru[l][None], w_conv[l], b_conv[l], w_gate, b_gate, lru_lambda[l],
                        n_blocks=1, n_seq=d_b, seq_len=d_seq, row_block0=t_p // t_s, link=False)

    sinks = attn_sinks[l]
    bkt_p = _bucket_map(WIN_CHUNKS * CHUNK + jnp.arange(CHUNK), jnp.arange(SPAN))
    o_p = _attn_band(z, rel_bias_table, sinks, bkt_p, n_batch=n_b, seq_len=seq)
    k_new = z[t_p:, COL_K:COL_K + KV_DIM].reshape(d_b, d_seq, KV_DIM)
    v_new = z[t_p:, COL_V:COL_V + KV_DIM].reshape(d_b, d_seq, KV_DIM)
    k_all = jnp.concatenate([cache_k_win[l].reshape(d_b, cw, KV_DIM), k_new], axis=1)
    v_all = jnp.concatenate([cache_v_win[l].reshape(d_b, cw, KV_DIM), v_new], axis=1)
    bkt_s = _bucket_map(cw + jnp.arange(d_seq), jnp.arange(cw + d_seq))
    o_s = _attn_step(z, k_all, v_all, rel_bias_table, sinks, bkt_s, n_batch=d_b, n_q=d_seq, row_block0=t_p // d_seq)

    n_route = N_GROUPS + N_EXPERTS
    w_route = jnp.pad(jnp.concatenate([w_route_group[l], w_route_expert[l]], axis=1),
                      ((0, 0), (0, ROUTE_LANES - n_route)))
    b_route = jnp.pad(jnp.concatenate([b_route_group[l], b_route_expert[l]]),
                      (0, ROUTE_LANES - n_route)).reshape(1, ROUTE_LANES)
    x1, n2, rt = _outproj(hg_p, hg_s, o_p, o_s, z, x_p, x_s, g1, sh2, sc2, norm2_g[l],
                          w_rnn_out[l].astype(BF16), w_attn_out[l].astype(BF16), w_out[l].astype(BF16),
                          w_route, b_route)

    e1 = rt[:, 0].astype(jnp.int32)
    e2 = rt[:, 1].astype(jnp.int32)
    tile_expert, tile_first, tile_valid, tile_next, src_tok, p1, p2 = _route_plan(e1, e2, t)
    ys = _moe(n2, w_exp_gate[l], w_exp_up[l], w_exp_down[l], tile_expert, tile_first, tile_valid, tile_next,
              src_tok)
    y_p, y_s = _final(ys, x1, g2, rt, final_norm_g, p1, p2, t_p)

    win = min(WINDOW, seq)

    def tail(col0, width, n_rows):
        return jnp.stack([z[(b + 1) * seq - n_rows:(b + 1) * seq, col0:col0 + width] for b in range(n_b)])

    kp = tail(COL_K, KV_DIM, win).reshape(n_b, win, N_KV_HEADS, HEAD_DIM)
    vp = tail(COL_V, KV_DIM, win).reshape(n_b, win, N_KV_HEADS, HEAD_DIM)
    cp = tail(COL_XR, D_RNN, CONV_W - 1)
    rp = hl_p[:, N_CHAINS - 1, :]
    ks = k_all[:, -cw:].reshape(d_b, cw, N_KV_HEADS, HEAD_DIM)
    vs = v_all[:, -cw:].reshape(d_b, cw, N_KV_HEADS, HEAD_DIM)
    cs = z[t_p:, COL_XR:COL_XR + D_RNN].reshape(d_b, d_seq, D_RNN)[:, -(CONV_W - 1):]
    rs = hl_s[0]
    return (y_p.reshape(n_b, seq, D_MODEL), y_s.reshape(d_b, d_seq, D_MODEL),
            kp[None], vp[None], cp[None], rp[None], ks[None], vs[None], cs[None], rs[None])
```

```python
import functools
import math

import jax
import jax.numpy as jnp
from jax import lax
from jax.experimental import pallas as pl
from jax.experimental.pallas import tpu as pltpu

F32 = jnp.float32
BF16 = jnp.bfloat16

D_MODEL = 2048
D_RNN = 2048
RNN_BLOCK = 128
CONV_W = 4
LRU_C = 8.0
N_HEADS = 16
N_KV_HEADS = 4
HEAD_DIM = 128
GQA_GROUP = N_HEADS // N_KV_HEADS
Q_DIM = N_HEADS * HEAD_DIM
KV_DIM = N_KV_HEADS * HEAD_DIM
CHUNK = 64
WINDOW = 128
WIN_CHUNKS = WINDOW // CHUNK
SPAN = (WIN_CHUNKS + 1) * CHUNK
N_BUCKETS = 32
MAX_DISTANCE = 128
N_GROUPS = 4
E_PER_GROUP = 8
N_EXPERTS = N_GROUPS * E_PER_GROUP
D_EXPERT = 512
EPS = 1e-6
NEG_INF = -1e30
D_IN = 2 * D_RNN + Q_DIM + 2 * KV_DIM + 2 * D_MODEL
COL_XR, COL_GR, COL_Q = 0, D_RNN, 2 * D_RNN
COL_K = COL_Q + Q_DIM
COL_V = COL_K + KV_DIM
COL_GA = COL_V + KV_DIM
COL_GB = COL_GA + D_MODEL

LANES = 128
SUBLANES = 8
MOD_ROWS = 32
KEY_TILE = 256
VMEM_LIMIT = 56 * 1024 * 1024

ROW_TILE = 256
IN_NSUB = 3
IN_TN = 1024
MOE_TM = 128


def _sigmoid(x):
    return 1.0 / (1.0 + jnp.exp(-x))


def _gelu_tanh(x):
    return 0.5 * x * (1.0 + jnp.tanh(math.sqrt(2.0 / math.pi) * (x + 0.044715 * (x * x * x))))


def _bcast_rows(v, rows):
    n, d = v.shape
    return jnp.broadcast_to(v[:, None, :], (n, MOD_ROWS, d)).reshape(rows, d)


def _rms_modulate(x, gain, scale_seg, shift_seg):
    rows = x.shape[0]
    y = x * lax.rsqrt(jnp.mean(x * x, axis=-1, keepdims=True) + EPS) * gain
    return y * (1.0 + _bcast_rows(scale_seg, rows)) + _bcast_rows(shift_seg, rows)


def _cparams(sem, vmem_limit=VMEM_LIMIT):
    return pltpu.CompilerParams(dimension_semantics=sem, vmem_limit_bytes=vmem_limit)


def _ada_kernel(c_ref, w_ref, b_ref, o_ref):
    c = c_ref[...]
    s = (c * _sigmoid(c)).astype(BF16)
    o_ref[...] = jnp.dot(s, w_ref[...].astype(BF16), preferred_element_type=F32) + b_ref[...]


def _ada(c_all, w_ada, b_ada):
    rows = c_all.shape[0]
    n = w_ada.shape[1]
    tn = 1024
    return pl.pallas_call(
        _ada_kernel,
        grid=(n // tn,),
        in_specs=[pl.BlockSpec((rows, D_MODEL), lambda j: (0, 0)),
                  pl.BlockSpec((D_MODEL, tn), lambda j: (0, j)),
                  pl.BlockSpec((1, tn), lambda j: (0, j))],
        out_specs=pl.BlockSpec((rows, tn), lambda j: (0, j)),
        out_shape=jax.ShapeDtypeStruct((rows, n), F32),
        compiler_params=_cparams(("arbitrary",)),
        name="ada",
    )(c_all, w_ada, b_ada.reshape(1, n))


def _inproj_kernel(x0_ref, x1_ref, x2_ref, xs_ref, sc_ref, sh_ref, g_ref, w_ref, b_ref, z_ref, n1_scr,
                   *, n_prompt_tiles):
    i = pl.program_id(0)
    j = pl.program_id(1)

    @pl.when(j == 0)
    def _():
        tail_is_sample = i * IN_NSUB + (IN_NSUB - 1) >= n_prompt_tiles
        subs = (x0_ref[...], x1_ref[...], jnp.where(tail_is_sample, xs_ref[...], x2_ref[...]))
        seg = ROW_TILE // MOD_ROWS
        for r, x in enumerate(subs):
            n1 = _rms_modulate(x, g_ref[...], sc_ref[r * seg:(r + 1) * seg, :], sh_ref[r * seg:(r + 1) * seg, :])
            n1_scr[r * ROW_TILE:(r + 1) * ROW_TILE, :] = n1.astype(BF16)

    z_ref[...] = jnp.dot(n1_scr[...], w_ref[...], preferred_element_type=F32) + b_ref[...]


def _inproj(x_p, x_s, sc1, sh1, norm_g, w_in, b_in):
    n_p = x_p.shape[0] // ROW_TILE
    t = x_p.shape[0] + x_s.shape[0]
    tm = IN_NSUB * ROW_TILE
    assert x_s.shape[0] == ROW_TILE and t % tm == 0
    seg = tm // MOD_ROWS

    def sub(r):
        return pl.BlockSpec((ROW_TILE, D_MODEL), lambda i, j: (jnp.minimum(i * IN_NSUB + r, n_p - 1), 0))

    return pl.pallas_call(
        functools.partial(_inproj_kernel, n_prompt_tiles=n_p),
        grid=(t // tm, D_IN // IN_TN),
        in_specs=[sub(0), sub(1), sub(2),
                  pl.BlockSpec((ROW_TILE, D_MODEL), lambda i, j: (0, 0)),
                  pl.BlockSpec((seg, D_MODEL), lambda i, j: (i, 0)),
                  pl.BlockSpec((seg, D_MODEL), lambda i, j: (i, 0)),
                  pl.BlockSpec((1, D_MODEL), lambda i, j: (0, 0)),
                  pl.BlockSpec((D_MODEL, IN_TN), lambda i, j: (0, j)),
                  pl.BlockSpec((1, IN_TN), lambda i, j: (0, j))],
        out_specs=pl.BlockSpec((tm, IN_TN), lambda i, j: (i, j)),
        out_shape=jax.ShapeDtypeStruct((t, D_IN), F32),
        scratch_shapes=[pltpu.VMEM((tm, D_MODEL), BF16)],
        compiler_params=_cparams(("arbitrary", "arbitrary")),
        name="inproj",
    )(x_p, x_p, x_p, x_s, sc1, sh1, norm_g.reshape(1, D_MODEL), w_in, b_in.reshape(1, D_IN))


RG_CW = 256
RG_SLABS = RG_CW // LANES
N_CHAINS = SUBLANES
CHAIN_PAD = 8


def _rglru_kernel(xr_ref, gr_ref, st_ref, h0_ref, wc_ref, bc_ref, wg_ref, bg_ref, lam_ref,
                  hg_ref, hl_ref, ext_scr, a_scr, u_scr, *, n_seq, seq_len, link):
    rows = n_seq * seq_len
    cl = rows // N_CHAINS
    pitch = cl + CHAIN_PAD

    for s in range(n_seq):
        ext_scr[s, 0:SUBLANES, :] = st_ref[s]
        ext_scr[s, SUBLANES:SUBLANES + seq_len, :] = xr_ref[s * seq_len:(s + 1) * seq_len, :]

    z = -lam_ref[...]
    softplus = jnp.maximum(z, 0.0) + jnp.log1p(jnp.exp(-jnp.abs(z)))
    nsl = -LRU_C * softplus

    for c in range(N_CHAINS):
        s, r0 = divmod(c * cl, seq_len)
        xc = bc_ref[...]
        for j in range(CONV_W):
            xc = xc + wc_ref[j:j + 1, :] * ext_scr[s, pl.ds(SUBLANES - (CONV_W - 1) + j + r0, cl), :]
        for sl in range(RG_SLABS):
            lanes = slice(sl * LANES, (sl + 1) * LANES)
            xb = xc[:, lanes]
            g = jnp.dot(xb.astype(BF16), wg_ref[sl].astype(BF16), preferred_element_type=F32) + bg_ref[sl]
            r = _sigmoid(g[:, :LANES])
            i = _sigmoid(g[:, LANES:])
            th = jnp.tanh(0.5 * (nsl[:, lanes] * r))
            rcp = 1.0 / (1.0 - th)
            a_scr[sl, c * pitch:c * pitch + cl, :] = (1.0 + th) * rcp
            u_scr[sl, c * pitch:c * pitch + cl, :] = (2.0 * jnp.sqrt(-th) * rcp) * (i * xb)

    def step(t, carry):
        hs, ps = carry
        new_h, new_p = [], []
        for sl in range(RG_SLABS):
            a = a_scr[sl, pl.ds(t, N_CHAINS, stride=pitch), :]
            u = u_scr[sl, pl.ds(t, N_CHAINS, stride=pitch), :]
            h = a * hs[sl] + u
            u_scr[sl, pl.ds(t, N_CHAINS, stride=pitch), :] = h
            new_h.append(h)
            if link:
                p = a * ps[sl]
                a_scr[sl, pl.ds(t, N_CHAINS, stride=pitch), :] = p
                new_p.append(p)
            else:
                new_p.append(ps[sl])
        return tuple(new_h), tuple(new_p)

    h_init = tuple(h0_ref[0, :, sl * LANES:(sl + 1) * LANES] for sl in range(RG_SLABS))
    p_init = tuple(jnp.ones((N_CHAINS, LANES), F32) for _ in range(RG_SLABS))
    h_end, p_end = lax.fori_loop(0, cl, step, (h_init, p_init), unroll=8)

    row = lax.broadcasted_iota(jnp.int32, (N_CHAINS, LANES), 0)
    for sl in range(RG_SLABS):
        lanes = slice(sl * LANES, (sl + 1) * LANES)
        if link:
            def shift_down(v):
                return jnp.where(row == 0, 0.0, pltpu.roll(v, 1, axis=0))
            hh = h_end[sl]
            for _ in range(N_CHAINS - 1):
                hh = h_end[sl] + p_end[sl] * shift_down(hh)
            carry_in = shift_down(hh)
        else:
            hh = h_end[sl]
        hl_ref[0, :, lanes] = hh
        for c in range(N_CHAINS):
            h = u_scr[sl, c * pitch:c * pitch + cl, :]
            if link:
                h = h + a_scr[sl, c * pitch:c * pitch + cl, :] * carry_in[c:c + 1, :]
            gg = _gelu_tanh(gr_ref[c * cl:(c + 1) * cl, lanes])
            hg_ref[c * cl:(c + 1) * cl, lanes] = (h * gg).astype(BF16)


def _rglru(z, state8, h0, w_conv, b_conv, w_gate, b_gate, lam, *, n_blocks, n_seq, seq_len, row_block0, link):
    rows = n_seq * seq_len
    cl = rows // N_CHAINS
    ncb = D_RNN // RG_CW
    gr0 = COL_GR // RG_CW
    kern = functools.partial(_rglru_kernel, n_seq=n_seq, seq_len=seq_len, link=link)
    return pl.pallas_call(
        kern,
        grid=(n_blocks, ncb),
        in_specs=[pl.BlockSpec((rows, RG_CW), lambda b, n: (row_block0 + b, n)),
                  pl.BlockSpec((rows, RG_CW), lambda b, n: (row_block0 + b, gr0 + n)),
                  pl.BlockSpec((n_seq, SUBLANES, RG_CW), lambda b, n: (b, 0, n)),
                  pl.BlockSpec((1, N_CHAINS, RG_CW), lambda b, n: (b, 0, n)),
                  pl.BlockSpec((CONV_W, RG_CW), lambda b, n: (0, n)),
                  pl.BlockSpec((1, RG_CW), lambda b, n: (0, n)),
                  pl.BlockSpec((RG_SLABS, RNN_BLOCK, 2 * RNN_BLOCK), lambda b, n: (n, 0, 0)),
                  pl.BlockSpec((RG_SLABS, 1, 2 * RNN_BLOCK), lambda b, n: (n, 0, 0)),
                  pl.BlockSpec((1, RG_CW), lambda b, n: (0, n))],
        out_specs=[pl.BlockSpec((rows, RG_CW), lambda b, n: (b, n)),
                   pl.BlockSpec((1, N_CHAINS, RG_CW), lambda b, n: (b, 0, n))],
        out_shape=[jax.ShapeDtypeStruct((n_blocks * rows, D_RNN), BF16),
                   jax.ShapeDtypeStruct((n_blocks, N_CHAINS, D_RNN), F32)],
        scratch_shapes=[pltpu.VMEM((n_seq, SUBLANES + seq_len, RG_CW), F32),
                        pltpu.VMEM((RG_SLABS, N_CHAINS * (cl + CHAIN_PAD), LANES), F32),
                        pltpu.VMEM((RG_SLABS, N_CHAINS * (cl + CHAIN_PAD), LANES), F32)],
        compiler_params=_cparams(("arbitrary", "arbitrary")),
        name="rglru_link" if link else "rglru_step",
    )(z, z, state8, h0, w_conv, b_conv.reshape(1, D_RNN), w_gate, b_gate, lam.reshape(1, D_RNN))


def _build_bias(tbl_ref, bkt_ref, bias_scr, n_q):
    bkt = bkt_ref[...]
    base = jnp.where(bkt < 0, NEG_INF, 0.0)
    for h in range(N_HEADS):
        acc = base
        for bk in range(N_BUCKETS):
            acc = jnp.where(bkt == bk, tbl_ref[bk, h], acc)
        hk, g = divmod(h, GQA_GROUP)
        bias_scr[hk, g * n_q:(g + 1) * n_q, :] = acc


def _sink_columns(sink_ref, n_q):
    row = lax.broadcasted_iota(jnp.int32, (GQA_GROUP * n_q, 1), 0)
    cols = []
    for hk in range(N_KV_HEADS):
        col = jnp.full((GQA_GROUP * n_q, 1), sink_ref[hk * GQA_GROUP], F32)
        for g in range(1, GQA_GROUP):
            col = jnp.where(row >= g * n_q, sink_ref[hk * GQA_GROUP + g], col)
        cols.append(col)
    return cols


def _attend(q_of, k_of, v_of, bias_scr, sinks, key_ok, store, n_q):
    scores = []
    for hk in range(N_KV_HEADS):
        qg = jnp.concatenate([q_of(hk * GQA_GROUP + g) for g in range(GQA_GROUP)], axis=0)
        s = lax.dot_general(qg, k_of(hk), (((1,), (1,)), ((), ())), preferred_element_type=F32)
        s = s * (HEAD_DIM ** -0.5) + bias_scr[hk]
        if key_ok is not None:
            s = jnp.where(key_ok, s, NEG_INF)
        scores.append(s)
    probs = []
    for hk in range(N_KV_HEADS):
        s, sink = scores[hk], sinks[hk]
        m = jnp.maximum(jnp.max(s, axis=-1, keepdims=True), sink)
        p = jnp.exp(s - m)
        inv = 1.0 / (jnp.sum(p, axis=-1, keepdims=True) + jnp.exp(sink - m))
        probs.append((p * inv).astype(BF16))
    for hk in range(N_KV_HEADS):
        o = jnp.dot(probs[hk], v_of(hk), preferred_element_type=F32)
        for g in range(GQA_GROUP):
            store(hk * GQA_GROUP + g, o[g * n_q:(g + 1) * n_q, :].astype(BF16))


ATT_CPB = 8
ATT_LEAD = WIN_CHUNKS * CHUNK
ATT_TAIL = KEY_TILE - SPAN


def _attn_band_kernel(tbl_ref, sink_ref, q_ref, k_ref, v_ref, bkt_ref, o_ref, kpad, vpad, bias_scr, *, seq_len):
    b = pl.program_id(0)
    cg = pl.program_id(1)

    @pl.when((b == 0) & (cg == 0))
    def _():
        _build_bias(tbl_ref, bkt_ref, bias_scr, CHUNK)

    @pl.when(cg == 0)
    def _():
        for ref, pad in ((k_ref, kpad), (v_ref, vpad)):
            pad[0:ATT_LEAD, :] = jnp.zeros((ATT_LEAD, KV_DIM), BF16)
            pad[ATT_LEAD:ATT_LEAD + seq_len, :] = ref[...].astype(BF16)
            pad[ATT_LEAD + seq_len:, :] = jnp.zeros((ATT_TAIL, KV_DIM), BF16)

    kidx = lax.broadcasted_iota(jnp.int32, (1, KEY_TILE), 1)
    sinks = _sink_columns(sink_ref, CHUNK)

    def chunk(c, carry):
        q0 = pl.multiple_of(c * CHUNK, CHUNK)
        start = pl.multiple_of((cg * ATT_CPB + c) * CHUNK, CHUNK)
        key_ok = start + kidx >= ATT_LEAD

        def head_cols(h):
            return slice(h * HEAD_DIM, (h + 1) * HEAD_DIM)

        def store(h, val):
            o_ref[pl.ds(q0, CHUNK), head_cols(h)] = val

        _attend(lambda h: q_ref[pl.ds(q0, CHUNK), head_cols(h)].astype(BF16),
                lambda hk: kpad[pl.ds(start, KEY_TILE), head_cols(hk)],
                lambda hk: vpad[pl.ds(start, KEY_TILE), head_cols(hk)],
                bias_scr, sinks, key_ok, store, CHUNK)
        return carry

    lax.fori_loop(0, ATT_CPB, chunk, 0)


def _attn_band(z, table, sinks, bkt, *, n_batch, seq_len):
    rows = ATT_CPB * CHUNK
    ng = seq_len // rows
    kern = functools.partial(_attn_band_kernel, seq_len=seq_len)
    smem = pl.BlockSpec(memory_space=pltpu.SMEM)
    pad_rows = ATT_LEAD + seq_len + ATT_TAIL
    return pl.pallas_call(
        kern,
        grid=(n_batch, ng),
        in_specs=[smem, smem,
                  pl.BlockSpec((rows, Q_DIM), lambda b, c: (b * ng + c, COL_Q // Q_DIM)),
                  pl.BlockSpec((seq_len, KV_DIM), lambda b, c: (b, COL_K // KV_DIM)),
                  pl.BlockSpec((seq_len, KV_DIM), lambda b, c: (b, COL_V // KV_DIM)),
                  pl.BlockSpec((CHUNK, KEY_TILE), lambda b, c: (0, 0))],
        out_specs=pl.BlockSpec((rows, Q_DIM), lambda b, c: (b * ng + c, 0)),
        out_shape=jax.ShapeDtypeStruct((n_batch * seq_len, Q_DIM), BF16),
        scratch_shapes=[pltpu.VMEM((pad_rows, KV_DIM), BF16),
                        pltpu.VMEM((pad_rows, KV_DIM), BF16),
                        pltpu.VMEM((N_KV_HEADS, GQA_GROUP * CHUNK, KEY_TILE), F32)],
        compiler_params=_cparams(("arbitrary", "arbitrary")),
        name="attn_band",
    )(table, sinks, z, z, z, bkt)


def _attn_step_kernel(tbl_ref, sink_ref, q_ref, k_ref, v_ref, bkt_ref, o_ref, kbuf, vbuf, bias_scr,
                      *, n_keys, n_q):
    b = pl.program_id(0)

    @pl.when(b == 0)
    def _():
        _build_bias(tbl_ref, bkt_ref, bias_scr, n_q)
        kbuf[n_keys:, :] = jnp.zeros((KEY_TILE - n_keys, KV_DIM), BF16)
        vbuf[n_keys:, :] = jnp.zeros((KEY_TILE - n_keys, KV_DIM), BF16)

    kbuf[0:n_keys, :] = k_ref[0].astype(BF16)
    vbuf[0:n_keys, :] = v_ref[0].astype(BF16)

    def head_cols(h):
        return slice(h * HEAD_DIM, (h + 1) * HEAD_DIM)

    def store(h, val):
        o_ref[:, head_cols(h)] = val

    _attend(lambda h: q_ref[:, head_cols(h)].astype(BF16),
            lambda hk: kbuf[:, head_cols(hk)], lambda hk: vbuf[:, head_cols(hk)],
            bias_scr, _sink_columns(sink_ref, n_q), None, store, n_q)


def _attn_step(z, k_all, v_all, table, sinks, bkt, *, n_batch, n_q, row_block0):
    n_keys = k_all.shape[1]
    kern = functools.partial(_attn_step_kernel, n_keys=n_keys, n_q=n_q)
    smem = pl.BlockSpec(memory_space=pltpu.SMEM)
    return pl.pallas_call(
        kern,
        grid=(n_batch,),
        in_specs=[smem, smem,
                  pl.BlockSpec((n_q, Q_DIM), lambda b: (row_block0 + b, COL_Q // Q_DIM)),
                  pl.BlockSpec((1, n_keys, KV_DIM), lambda b: (b, 0, 0)),
                  pl.BlockSpec((1, n_keys, KV_DIM), lambda b: (b, 0, 0)),
                  pl.BlockSpec((n_q, KEY_TILE), lambda b: (0, 0))],
        out_specs=pl.BlockSpec((n_q, Q_DIM), lambda b: (b, 0)),
        out_shape=jax.ShapeDtypeStruct((n_batch * n_q, Q_DIM), BF16),
        scratch_shapes=[pltpu.VMEM((KEY_TILE, KV_DIM), BF16),
                        pltpu.VMEM((KEY_TILE, KV_DIM), BF16),
                        pltpu.VMEM((N_KV_HEADS, GQA_GROUP * n_q, KEY_TILE), F32)],
        compiler_params=_cparams(("arbitrary",)),
        name="attn_step",
    )(table, sinks, z, k_all, v_all, bkt)


def _t5_bucket(rel):
    nb = N_BUCKETS // 2
    ret = jnp.where(rel > 0, nb, 0)
    n = jnp.abs(rel)
    max_exact = nb // 2
    nf = jnp.maximum(n, 1).astype(jnp.float32)
    large = max_exact + (jnp.log(nf / max_exact) / math.log(MAX_DISTANCE / max_exact)
                         * (nb - max_exact)).astype(jnp.int32)
    large = jnp.minimum(large, nb - 1)
    return ret + jnp.where(n < max_exact, n, large)


def _bucket_map(q_pos, k_pos):
    bkt = _t5_bucket(k_pos[None, :] - q_pos[:, None]).astype(jnp.int32)
    return jnp.pad(bkt, ((0, 0), (0, KEY_TILE - k_pos.shape[0])), constant_values=-1)


ROUTE_LANES = LANES
GATE_TN = 1024
OUT_VMEM_LIMIT = 60 * 1024 * 1024


def _outproj_kernel(hgp_ref, hgs_ref, op_ref, os_ref, ga0_ref, ga1_ref, gb0_ref, gb1_ref, xp_ref, xs_ref,
                    g1_ref, sh2_ref, sc2_ref, ng_ref, wr_ref, wa_ref, wo_ref, wrt_ref, brt_ref,
                    x1_ref, n2_ref, rt_ref, *, n_prompt_tiles):
    tm = x1_ref.shape[0]
    is_prompt = pl.program_id(0) < n_prompt_tiles
    hg = jnp.where(is_prompt, hgp_ref[...], hgs_ref[...])
    o = jnp.where(is_prompt, op_ref[...], os_ref[...])
    x = jnp.where(is_prompt, xp_ref[...], xs_ref[...])
    ya = jnp.dot(hg, wr_ref[...], preferred_element_type=F32)
    yb = jnp.dot(o, wa_ref[...], preferred_element_type=F32)
    halves = []
    for k, (ga_ref, gb_ref) in enumerate(((ga0_ref, gb0_ref), (ga1_ref, gb1_ref))):
        cols = slice(k * GATE_TN, (k + 1) * GATE_TN)
        halves.append((_sigmoid(ga_ref[...]) * ya[:, cols] + _sigmoid(gb_ref[...]) * yb[:, cols]).astype(BF16))
    merged = jnp.concatenate(halves, axis=1)
    mix = jnp.dot(merged, wo_ref[...], preferred_element_type=F32)
    x1 = x + _bcast_rows(g1_ref[...], tm) * mix
    x1_ref[...] = x1
    n2 = _rms_modulate(x1, ng_ref[...], sc2_ref[...], sh2_ref[...])
    _store_slabs(n2_ref, n2)

    lg = jnp.dot(n2.astype(BF16), wrt_ref[...].astype(BF16), preferred_element_type=F32) + brt_ref[...]

    lane = lax.broadcasted_iota(jnp.int32, (tm, ROUTE_LANES), 1)
    lane_f = lane.astype(F32)
    e_lane = lane - N_GROUPS
    lane_group = (e_lane >> 3).astype(F32)

    def first_argmax(vals, vmax):
        return jnp.min(jnp.where(vals == vmax, lane_f, float(ROUTE_LANES)), axis=-1, keepdims=True)

    gl = jnp.where(lane < N_GROUPS, lg, NEG_INF)
    gmax = jnp.max(gl, axis=-1, keepdims=True)
    g_idx = first_argmax(gl, gmax)
    g_w = 1.0 / jnp.sum(jnp.exp(gl - gmax), axis=-1, keepdims=True)
    in_group = jnp.where((e_lane >= 0) & (e_lane < N_EXPERTS), lane_group, -1.0) == g_idx
    el = jnp.where(in_group, lg, NEG_INF)
    v1 = jnp.max(el, axis=-1, keepdims=True)
    i1 = first_argmax(el, v1)
    el2 = jnp.where(lane_f == i1, NEG_INF, el)
    v2 = jnp.max(el2, axis=-1, keepdims=True)
    i2 = first_argmax(el2, v2)
    e21 = jnp.exp(v2 - v1)
    w1 = g_w / (1.0 + e21)
    w2 = g_w * e21 / (1.0 + e21)
    e1 = i1 - float(N_GROUPS)
    e2 = i2 - float(N_GROUPS)
    rt_ref[...] = jnp.where(lane == 0, e1, jnp.where(lane == 1, e2, jnp.where(lane == 2, w1,
                            jnp.where(lane == 3, w2, 0.0))))


def _outproj(hg_p, hg_s, o_p, o_s, z, x_p, x_s, g1, sh2, sc2, norm_g, wr, wa, wo, w_route, b_route):
    tm = ROW_TILE
    t = z.shape[0]
    seg = tm // MOD_ROWS
    n_p = x_p.shape[0] // tm
    row = lambda i: (i, 0)
    fix = lambda i: (0, 0)
    row_p = lambda i: (jnp.minimum(i, n_p - 1), 0)
    row_s = lambda i: (jnp.maximum(i - n_p, 0), 0)
    once = pl.Buffered(1)

    def gate(col):
        return pl.BlockSpec((tm, GATE_TN), lambda i: (i, col // GATE_TN))

    return pl.pallas_call(
        functools.partial(_outproj_kernel, n_prompt_tiles=n_p),
        grid=(t // tm,),
        in_specs=[pl.BlockSpec((tm, D_RNN), row_p), pl.BlockSpec((tm, D_RNN), row_s, pipeline_mode=once),
                  pl.BlockSpec((tm, Q_DIM), row_p), pl.BlockSpec((tm, Q_DIM), row_s, pipeline_mode=once),
                  gate(COL_GA), gate(COL_GA + GATE_TN), gate(COL_GB), gate(COL_GB + GATE_TN),
                  pl.BlockSpec((tm, D_MODEL), row_p), pl.BlockSpec((tm, D_MODEL), row_s, pipeline_mode=once),
                  pl.BlockSpec((seg, D_MODEL), row), pl.BlockSpec((seg, D_MODEL), row),
                  pl.BlockSpec((seg, D_MODEL), row),
                  pl.BlockSpec((1, D_MODEL), fix),
                  pl.BlockSpec((D_RNN, D_MODEL), fix, pipeline_mode=once),
                  pl.BlockSpec((Q_DIM, D_MODEL), fix, pipeline_mode=once),
                  pl.BlockSpec((D_MODEL, D_MODEL), fix, pipeline_mode=once),
                  pl.BlockSpec((D_MODEL, ROUTE_LANES), fix),
                  pl.BlockSpec((1, ROUTE_LANES), fix)],
        out_specs=[pl.BlockSpec((tm, D_MODEL), row), pl.BlockSpec((tm * SLAB, LANES), row),
                   pl.BlockSpec((tm, ROUTE_LANES), row)],
        out_shape=[jax.ShapeDtypeStruct((t, D_MODEL), F32), jax.ShapeDtypeStruct((t * SLAB, LANES), F32),
                   jax.ShapeDtypeStruct((t, ROUTE_LANES), F32)],
        compiler_params=_cparams(("arbitrary",), OUT_VMEM_LIMIT),
        name="outproj",
    )(hg_p, hg_s, o_p, o_s, z, z, z, z, x_p, x_s, g1, sh2, sc2, norm_g.reshape(1, D_MODEL),
      wr, wa, wo, w_route, b_route)


SLAB = D_MODEL // LANES
SLAB_PITCH = 24


def _store_slabs(ref, val):
    rows = val.shape[0]
    for s in range(SLAB):
        ref[pl.ds(s, rows, stride=SLAB), :] = val[:, s * LANES:(s + 1) * LANES]


def _load_slabs(buf, rows):
    return jnp.concatenate([buf[pl.ds(s, rows, stride=SLAB_PITCH), :] for s in range(SLAB)], axis=1)


def _row_gather(src_hbm, idx_ref, base, dst, sem, n_rows):
    for r in range(n_rows):
        row0 = pl.multiple_of(idx_ref[base + r] * SLAB, SLAB)
        pltpu.make_async_copy(src_hbm.at[pl.ds(row0, SLAB), :], dst.at[pl.ds(r * SLAB_PITCH, SLAB), :], sem).start()


def _row_gather_wait(src_hbm, dst, sem, n_rows):
    pltpu.make_async_copy(src_hbm.at[pl.ds(0, n_rows * SLAB), :], dst.at[pl.ds(0, n_rows * SLAB), :], sem).wait()


def _moe_kernel(te_ref, tfirst_ref, tvalid_ref, tnext_ref, src_ref, n2_hbm, wg_hbm, wu_hbm, wd_hbm, y_ref,
                xbuf, xsem, wg_st, wu_st, wd_st, wsem, wg_bf, wu_bf, wd_bf):
    i = pl.program_id(0)
    n_tiles = pl.num_programs(0)
    slot = i % 2
    stages = ((wg_hbm, wg_st, wg_bf), (wu_hbm, wu_st, wu_bf), (wd_hbm, wd_st, wd_bf))

    def weight_copy(k, e):
        hbm, st, _ = stages[k]
        return pltpu.make_async_copy(hbm.at[e], st, wsem.at[k])

    @pl.when(i == 0)
    def _():
        for k in range(len(stages)):
            weight_copy(k, te_ref[0]).start(priority=1)
        _row_gather(n2_hbm, src_ref, 0, xbuf.at[0], xsem.at[0], MOE_TM)

    @pl.when(tvalid_ref[i] == 1)
    def _():
        nxt = jnp.minimum(i + 1, n_tiles - 1)

        @pl.when((i + 1 < n_tiles) & (tvalid_ref[nxt] == 1))
        def _():
            _row_gather(n2_hbm, src_ref, (i + 1) * MOE_TM, xbuf.at[1 - slot], xsem.at[1 - slot], MOE_TM)

        @pl.when(tfirst_ref[i] == 1)
        def _():
            for k, (_, st, bf) in enumerate(stages):
                weight_copy(k, te_ref[i]).wait()
                bf[...] = st[...].astype(BF16)

            @pl.when(tnext_ref[i] >= 0)
            def _():
                for k in range(len(stages)):
                    weight_copy(k, tnext_ref[i]).start(priority=1)

        _row_gather_wait(n2_hbm, xbuf.at[slot], xsem.at[slot], MOE_TM)
        x = _load_slabs(xbuf.at[slot], MOE_TM).astype(BF16)
        hgate = jnp.dot(x, wg_bf[...], preferred_element_type=F32)
        hup = jnp.dot(x, wu_bf[...], preferred_element_type=F32)
        act = (hgate * _sigmoid(hgate) * hup).astype(BF16)
        _store_slabs(y_ref, jnp.dot(act, wd_bf[...], preferred_element_type=F32))

    @pl.when(tvalid_ref[i] == 0)
    def _():
        y_ref[...] = jnp.zeros(y_ref.shape, F32)


def _moe(n2, w_gate, w_up, w_down, tile_expert, tile_first, tile_valid, tile_next, src_tok):
    n_tiles = tile_expert.shape[0]
    hbm = pl.BlockSpec(memory_space=pl.ANY)
    grid_spec = pltpu.PrefetchScalarGridSpec(
        num_scalar_prefetch=5,
        grid=(n_tiles,),
        in_specs=[hbm, hbm, hbm, hbm],
        out_specs=pl.BlockSpec((MOE_TM * SLAB, LANES), lambda i, te, tf, tv, tn, st: (i, 0)),
        scratch_shapes=[pltpu.VMEM((2, MOE_TM * SLAB_PITCH, LANES), F32),
                        pltpu.SemaphoreType.DMA((2,)),
                        pltpu.VMEM((D_MODEL, D_EXPERT), F32),
                        pltpu.VMEM((D_MODEL, D_EXPERT), F32),
                        pltpu.VMEM((D_EXPERT, D_MODEL), F32),
                        pltpu.SemaphoreType.DMA((3,)),
                        pltpu.VMEM((D_MODEL, D_EXPERT), BF16),
                        pltpu.VMEM((D_MODEL, D_EXPERT), BF16),
                        pltpu.VMEM((D_EXPERT, D_MODEL), BF16)],
    )
    return pl.pallas_call(
        _moe_kernel,
        grid_spec=grid_spec,
        out_shape=jax.ShapeDtypeStruct((n_tiles * MOE_TM * SLAB, LANES), F32),
        compiler_params=_cparams(("arbitrary",)),
        name="moe",
    )(tile_expert, tile_first, tile_valid, tile_next, src_tok, n2, w_gate, w_up, w_down)


def _route_plan(e1, e2, n_tok):
    experts = jnp.arange(N_EXPERTS, dtype=jnp.int32)
    flat_e = jnp.concatenate([e1, e2])
    onehot = (flat_e[:, None] == experts[None, :]).astype(jnp.int32)
    csum = jnp.cumsum(onehot, axis=0)
    rank = jnp.sum(csum * onehot, axis=1) - 1
    counts = csum[-1]
    tiles_per = (counts + MOE_TM - 1) // MOE_TM
    tile_end = jnp.cumsum(tiles_per)
    tile_off = tile_end - tiles_per
    slot = jnp.sum(onehot * tile_off[None, :], axis=1) * MOE_TM + rank
    n_tiles = (2 * n_tok) // MOE_TM + N_EXPERTS
    tok = jnp.tile(jnp.arange(n_tok, dtype=jnp.int32), 2)
    src_tok = jnp.zeros((n_tiles * MOE_TM,), jnp.int32).at[slot].set(tok)
    tile_id = jnp.arange(n_tiles, dtype=jnp.int32)
    n_used = tile_end[-1]
    tile_valid = (tile_id < n_used).astype(jnp.int32)
    te = jnp.sum((tile_end[None, :] <= jnp.minimum(tile_id, n_used - 1)[:, None]).astype(jnp.int32), axis=1)
    tile_expert = jnp.minimum(te, N_EXPERTS - 1)
    prev = jnp.concatenate([jnp.full((1,), -1, jnp.int32), tile_expert[:-1]])
    tile_first = (tile_expert != prev).astype(jnp.int32)
    next_tile = tile_end[tile_expert]
    tile_next = jnp.where(next_tile < n_used, tile_expert[jnp.minimum(next_tile, n_tiles - 1)], -1)
    return tile_expert, tile_first, tile_valid, tile_next, src_tok, slot[:n_tok], slot[n_tok:]


def _final_kernel(p1_ref, p2_ref, ys_hbm, x1_ref, g2_ref, rt_ref, fg_ref, yp_ref, ysm_ref, ybuf, sem,
                  *, n_prompt_tiles):
    i = pl.program_id(0)
    n_tiles = pl.num_programs(0)
    slot = i % 2

    def start(tile, s):
        _row_gather(ys_hbm, p1_ref, tile * ROW_TILE, ybuf.at[s, 0], sem.at[s], ROW_TILE)
        _row_gather(ys_hbm, p2_ref, tile * ROW_TILE, ybuf.at[s, 1], sem.at[s], ROW_TILE)

    @pl.when(i == 0)
    def _():
        start(0, 0)

    @pl.when(i + 1 < n_tiles)
    def _():
        start(i + 1, 1 - slot)

    _row_gather_wait(ys_hbm, ybuf.at[slot, 0], sem.at[slot], ROW_TILE)
    _row_gather_wait(ys_hbm, ybuf.at[slot, 1], sem.at[slot], ROW_TILE)
    rt = rt_ref[...]
    moe = (rt[:, 2:3] * _load_slabs(ybuf.at[slot, 0], ROW_TILE)
           + rt[:, 3:4] * _load_slabs(ybuf.at[slot, 1], ROW_TILE))
    x2 = x1_ref[...] + _bcast_rows(g2_ref[...], ROW_TILE) * moe
    y = x2 * lax.rsqrt(jnp.mean(x2 * x2, axis=-1, keepdims=True) + EPS) * fg_ref[...]

    @pl.when(i < n_prompt_tiles)
    def _():
        yp_ref[...] = y

    @pl.when(i >= n_prompt_tiles)
    def _():
        ysm_ref[...] = y


def _final(ys, x1, g2, rt, final_g, p1, p2, n_prompt_rows):
    t = x1.shape[0]
    tm = ROW_TILE
    seg = tm // MOD_ROWS
    n_p = n_prompt_rows // tm
    grid_spec = pltpu.PrefetchScalarGridSpec(
        num_scalar_prefetch=2,
        grid=(t // tm,),
        in_specs=[pl.BlockSpec(memory_space=pl.ANY),
                  pl.BlockSpec((tm, D_MODEL), lambda i, a, b: (i, 0)),
                  pl.BlockSpec((seg, D_MODEL), lambda i, a, b: (i, 0)),
                  pl.BlockSpec((tm, ROUTE_LANES), lambda i, a, b: (i, 0)),
                  pl.BlockSpec((1, D_MODEL), lambda i, a, b: (0, 0))],
        out_specs=[pl.BlockSpec((tm, D_MODEL), lambda i, a, b: (jnp.minimum(i, n_p - 1), 0)),
                   pl.BlockSpec((tm, D_MODEL), lambda i, a, b: (jnp.maximum(i - n_p, 0), 0))],
        scratch_shapes=[pltpu.VMEM((2, 2, tm * SLAB_PITCH, LANES), F32),
                        pltpu.SemaphoreType.DMA((2,))],
    )
    return pl.pallas_call(
        functools.partial(_final_kernel, n_prompt_tiles=n_p),
        grid_spec=grid_spec,
        out_shape=[jax.ShapeDtypeStruct((n_prompt_rows, D_MODEL), F32),
                   jax.ShapeDtypeStruct((t - n_prompt_rows, D_MODEL), F32)],
        compiler_params=_cparams(("arbitrary",)),
        name="final",
    )(p1, p2, ys, x1, g2, rt, final_g.reshape(1, D_MODEL))


def kernel(x_prompt, x_sample, cache_k_win, cache_v_win, state_conv, state_rglru, c_prompt, c_sample, w_ada, b_ada, norm1_g, norm2_g, w_in, b_in, w_conv, b_conv, w_rg_a, b_rg_a, w_rg_x, b_rg_x, lru_lambda, w_rnn_out, w_attn_out, w_out, attn_sinks, w_route_group, b_route_group, w_route_expert, b_route_expert, w_exp_gate, w_exp_up, w_exp_down, rel_bias_table, final_norm_g):
    n_b, seq, _ = x_prompt.shape
    d_b, d_seq, _ = x_sample.shape
    assert w_ada.shape[0] == 1, "single trunk layer"
    assert seq % (ATT_CPB * CHUNK) == 0 and seq % MOD_ROWS == 0 and d_seq == MOD_ROWS
    assert d_seq >= CONV_W - 1 and d_b == N_CHAINS
    t_p, t_s = n_b * seq, d_b * d_seq
    t = t_p + t_s
    assert t_s == ROW_TILE and t_p % ROW_TILE == 0 and t % (IN_NSUB * ROW_TILE) == 0
    cw = cache_k_win.shape[2]
    l = 0

    x_p = x_prompt.reshape(t_p, D_MODEL)
    x_s = x_sample.reshape(t_s, D_MODEL)

    n_c = n_b + d_b
    c_all = jnp.pad(jnp.concatenate([c_prompt, c_sample], axis=0), ((0, -n_c % SUBLANES), (0, 0)))
    mod = _ada(c_all, w_ada[l], b_ada[l])

    def per_segment(m):
        return jnp.concatenate([jnp.repeat(m[:n_b], seq // MOD_ROWS, axis=0),
                                jnp.repeat(m[n_b:n_c], d_seq // MOD_ROWS, axis=0)], axis=0)

    sh1, sc1, g1, sh2, sc2, g2 = [per_segment(m) for m in jnp.split(mod, 6, axis=-1)]

    z = _inproj(x_p, x_s, sc1, sh1, norm1_g[l], w_in[l].astype(BF16), b_in[l])

    w_gate = jnp.concatenate([w_rg_a[l], w_rg_x[l]], axis=-1)
    b_gate = jnp.concatenate([b_rg_a[l], b_rg_x[l]], axis=-1)[:, None, :]
    hg_p, hl_p = _rglru(z, jnp.zeros((n_b, SUBLANES, D_RNN), F32), jnp.zeros((n_b, N_CHAINS, D_RNN), F32),
                        w_conv[l], b_conv[l], w_gate, b_gate, lru_lambda[l],
                        n_blocks=n_b, n_seq=1, seq_len=seq, row_block0=0, link=True)
    state8 = jnp.pad(state_conv[l], ((0, 0), (SUBLANES - (CONV_W - 1), 0), (0, 0)))
    hg_s, hl_s = _rglru(z, state8, state_rglru[l][None], w_conv[l], b_conv[l], w_gate, b_gate, lru_lambda[l],
                        n_blocks=1, n_seq=d_b, seq_len=d_seq, row_block0=t_p // t_s, link=False)

    sinks = attn_sinks[l]
    bkt_p = _bucket_map(WIN_CHUNKS * CHUNK + jnp.arange(CHUNK), jnp.arange(SPAN))
    o_p = _attn_band(z, rel_bias_table, sinks, bkt_p, n_batch=n_b, seq_len=seq)
    k_new = z[t_p:, COL_K:COL_K + KV_DIM].reshape(d_b, d_seq, KV_DIM)
    v_new = z[t_p:, COL_V:COL_V + KV_DIM].reshape(d_b, d_seq, KV_DIM)
    k_all = jnp.concatenate([cache_k_win[l].reshape(d_b, cw, KV_DIM), k_new], axis=1)
    v_all = jnp.concatenate([cache_v_win[l].reshape(d_b, cw, KV_DIM), v_new], axis=1)
    bkt_s = _bucket_map(cw + jnp.arange(d_seq), jnp.arange(cw + d_seq))
    o_s = _attn_step(z, k_all, v_all, rel_bias_table, sinks, bkt_s, n_batch=d_b, n_q=d_seq, row_block0=t_p // d_seq)

    n_route = N_GROUPS + N_EXPERTS
    w_route = jnp.pad(jnp.concatenate([w_route_group[l], w_route_expert[l]], axis=1),
                      ((0, 0), (0, ROUTE_LANES - n_route)))
    b_route = jnp.pad(jnp.concatenate([b_route_group[l], b_route_expert[l]]),
                      (0, ROUTE_LANES - n_route)).reshape(1, ROUTE_LANES)
    x1, n2, rt = _outproj(hg_p, hg_s, o_p, o_s, z, x_p, x_s, g1, sh2, sc2, norm2_g[l],
                          w_rnn_out[l].astype(BF16), w_attn_out[l].astype(BF16), w_out[l].astype(BF16),
                          w_route, b_route)

    e1 = rt[:, 0].astype(jnp.int32)
    e2 = rt[:, 1].astype(jnp.int32)
    tile_expert, tile_first, tile_valid, tile_next, src_tok, p1, p2 = _route_plan(e1, e2, t)
    ys = _moe(n2, w_exp_gate[l], w_exp_up[l], w_exp_down[l], tile_expert, tile_first, tile_valid, tile_next,
              src_tok)
    y_p, y_s = _final(ys, x1, g2, rt, final_norm_g, p1, p2, t_p)

    win = min(WINDOW, seq)

    def tail(col0, width, n_rows):
        return jnp.stack([z[(b + 1) * seq - n_rows:(b + 1) * seq, col0:col0 + width] for b in range(n_b)])

    kp = tail(COL_K, KV_DIM, win).reshape(n_b, win, N_KV_HEADS, HEAD_DIM)
    vp = tail(COL_V, KV_DIM, win).reshape(n_b, win, N_KV_HEADS, HEAD_DIM)
    cp = tail(COL_XR, D_RNN, CONV_W - 1)
    rp = hl_p[:, N_CHAINS - 1, :]
    ks = k_all[:, -cw:].reshape(d_b, cw, N_KV_HEADS, HEAD_DIM)
    vs = v_all[:, -cw:].reshape(d_b, cw, N_KV_HEADS, HEAD_DIM)
    cs = z[t_p:, COL_XR:COL_XR + D_RNN].reshape(d_b, d_seq, D_RNN)[:, -(CONV_W - 1):]
    rs = hl_s[0]
    return (y_p.reshape(n_b, seq, D_MODEL), y_s.reshape(d_b, d_seq, D_MODEL),
            kp[None], vp[None], cp[None], rp[None], ks[None], vs[None], cs[None], rs[None])
```

```python
import functools
import math

import jax
import jax.numpy as jnp
from jax import lax
from jax.experimental import pallas as pl
from jax.experimental.pallas import tpu as pltpu

F32 = jnp.float32
BF16 = jnp.bfloat16

D_MODEL = 2048
D_RNN = 2048
RNN_BLOCK = 128
CONV_W = 4
LRU_C = 8.0
N_HEADS = 16
N_KV_HEADS = 4
HEAD_DIM = 128
GQA_GROUP = N_HEADS // N_KV_HEADS
Q_DIM = N_HEADS * HEAD_DIM
KV_DIM = N_KV_HEADS * HEAD_DIM
CHUNK = 64
WINDOW = 128
WIN_CHUNKS = WINDOW // CHUNK
SPAN = (WIN_CHUNKS + 1) * CHUNK
N_BUCKETS = 32
MAX_DISTANCE = 128
N_GROUPS = 4
E_PER_GROUP = 8
N_EXPERTS = N_GROUPS * E_PER_GROUP
D_EXPERT = 512
EPS = 1e-6
NEG_INF = -1e30
D_IN = 2 * D_RNN + Q_DIM + 2 * KV_DIM + 2 * D_MODEL
COL_XR, COL_GR, COL_Q = 0, D_RNN, 2 * D_RNN
COL_K = COL_Q + Q_DIM
COL_V = COL_K + KV_DIM
COL_GA = COL_V + KV_DIM
COL_GB = COL_GA + D_MODEL

LANES = 128
SUBLANES = 8
MOD_ROWS = 32
KEY_TILE = 256
VMEM_LIMIT = 56 * 1024 * 1024

ROW_TILE = 256
IN_NSUB = 3
IN_TN = 1024
MOE_TM = 128


def _sigmoid(x):
    return 1.0 / (1.0 + jnp.exp(-x))


def _gelu_tanh(x):
    return 0.5 * x * (1.0 + jnp.tanh(math.sqrt(2.0 / math.pi) * (x + 0.044715 * (x * x * x))))


def _bcast_rows(v, rows):
    n, d = v.shape
    return jnp.broadcast_to(v[:, None, :], (n, MOD_ROWS, d)).reshape(rows, d)


def _rms_modulate(x, gain, scale_seg, shift_seg):
    rows = x.shape[0]
    y = x * lax.rsqrt(jnp.mean(x * x, axis=-1, keepdims=True) + EPS) * gain
    return y * (1.0 + _bcast_rows(scale_seg, rows)) + _bcast_rows(shift_seg, rows)


def _cparams(sem, vmem_limit=VMEM_LIMIT):
    return pltpu.CompilerParams(dimension_semantics=sem, vmem_limit_bytes=vmem_limit)


def _ada_kernel(c_ref, w_ref, b_ref, o_ref):
    c = c_ref[...]
    s = (c * _sigmoid(c)).astype(BF16)
    o_ref[...] = jnp.dot(s, w_ref[...].astype(BF16), preferred_element_type=F32) + b_ref[...]


def _ada(c_all, w_ada, b_ada):
    rows = c_all.shape[0]
    n = w_ada.shape[1]
    tn = 1024
    return pl.pallas_call(
        _ada_kernel,
        grid=(n // tn,),
        in_specs=[pl.BlockSpec((rows, D_MODEL), lambda j: (0, 0)),
                  pl.BlockSpec((D_MODEL, tn), lambda j: (0, j)),
                  pl.BlockSpec((1, tn), lambda j: (0, j))],
        out_specs=pl.BlockSpec((rows, tn), lambda j: (0, j)),
        out_shape=jax.ShapeDtypeStruct((rows, n), F32),
        compiler_params=_cparams(("arbitrary",)),
        name="ada",
    )(c_all, w_ada, b_ada.reshape(1, n))


def _inproj_kernel(x0_ref, x1_ref, x2_ref, xs_ref, sc_ref, sh_ref, g_ref, w_ref, b_ref, z_ref, n1_scr,
                   *, n_prompt_tiles):
    i = pl.program_id(0)
    j = pl.program_id(1)

    @pl.when(j == 0)
    def _():
        tail_is_sample = i * IN_NSUB + (IN_NSUB - 1) >= n_prompt_tiles
        subs = (x0_ref[...], x1_ref[...], jnp.where(tail_is_sample, xs_ref[...], x2_ref[...]))
        seg = ROW_TILE // MOD_ROWS
        for r, x in enumerate(subs):
            n1 = _rms_modulate(x, g_ref[...], sc_ref[r * seg:(r + 1) * seg, :], sh_ref[r * seg:(r + 1) * seg, :])
            n1_scr[r * ROW_TILE:(r + 1) * ROW_TILE, :] = n1.astype(BF16)

    z_ref[...] = jnp.dot(n1_scr[...], w_ref[...], preferred_element_type=F32) + b_ref[...]


def _inproj(x_p, x_s, sc1, sh1, norm_g, w_in, b_in):
    n_p = x_p.shape[0] // ROW_TILE
    t = x_p.shape[0] + x_s.shape[0]
    tm = IN_NSUB * ROW_TILE
    assert x_s.shape[0] == ROW_TILE and t % tm == 0
    seg = tm // MOD_ROWS

    def sub(r):
        return pl.BlockSpec((ROW_TILE, D_MODEL), lambda i, j: (jnp.minimum(i * IN_NSUB + r, n_p - 1), 0))

    return pl.pallas_call(
        functools.partial(_inproj_kernel, n_prompt_tiles=n_p),
        grid=(t // tm, D_IN // IN_TN),
        in_specs=[sub(0), sub(1), sub(2),
                  pl.BlockSpec((ROW_TILE, D_MODEL), lambda i, j: (0, 0)),
                  pl.BlockSpec((seg, D_MODEL), lambda i, j: (i, 0)),
                  pl.BlockSpec((seg, D_MODEL), lambda i, j: (i, 0)),
                  pl.BlockSpec((1, D_MODEL), lambda i, j: (0, 0)),
                  pl.BlockSpec((D_MODEL, IN_TN), lambda i, j: (0, j)),
                  pl.BlockSpec((1, IN_TN), lambda i, j: (0, j))],
        out_specs=pl.BlockSpec((tm, IN_TN), lambda i, j: (i, j)),
        out_shape=jax.ShapeDtypeStruct((t, D_IN), F32),
        scratch_shapes=[pltpu.VMEM((tm, D_MODEL), BF16)],
        compiler_params=_cparams(("arbitrary", "arbitrary")),
        name="inproj",
    )(x_p, x_p, x_p, x_s, sc1, sh1, norm_g.reshape(1, D_MODEL), w_in, b_in.reshape(1, D_IN))


RG_CW = 256
RG_SLABS = RG_CW // LANES
N_CHAINS = SUBLANES
CHAIN_PAD = 8


def _rglru_kernel(xr_ref, gr_ref, st_ref, h0_ref, wc_ref, bc_ref, wg_ref, bg_ref, lam_ref,
                  hg_ref, hl_ref, ext_scr, a_scr, u_scr, *, n_seq, seq_len, link):
    rows = n_seq * seq_len
    cl = rows // N_CHAINS
    pitch = cl + CHAIN_PAD

    for s in range(n_seq):
        ext_scr[s, 0:SUBLANES, :] = st_ref[s]
        ext_scr[s, SUBLANES:SUBLANES + seq_len, :] = xr_ref[s * seq_len:(s + 1) * seq_len, :]

    z = -lam_ref[...]
    softplus = jnp.maximum(z, 0.0) + jnp.log1p(jnp.exp(-jnp.abs(z)))
    nsl = -LRU_C * softplus

    for c in range(N_CHAINS):
        s, r0 = divmod(c * cl, seq_len)
        xc = bc_ref[...]
        for j in range(CONV_W):
            xc = xc + wc_ref[j:j + 1, :] * ext_scr[s, pl.ds(SUBLANES - (CONV_W - 1) + j + r0, cl), :]
        for sl in range(RG_SLABS):
            lanes = slice(sl * LANES, (sl + 1) * LANES)
            xb = xc[:, lanes]
            g = jnp.dot(xb.astype(BF16), wg_ref[sl].astype(BF16), preferred_element_type=F32) + bg_ref[sl]
            r = _sigmoid(g[:, :LANES])
            i = _sigmoid(g[:, LANES:])
            th = jnp.tanh(0.5 * (nsl[:, lanes] * r))
            rcp = 1.0 / (1.0 - th)
            a_scr[sl, c * pitch:c * pitch + cl, :] = (1.0 + th) * rcp
            u_scr[sl, c * pitch:c * pitch + cl, :] = (2.0 * jnp.sqrt(-th) * rcp) * (i * xb)

    def step(t, carry):
        hs, ps = carry
        new_h, new_p = [], []
        for sl in range(RG_SLABS):
            a = a_scr[sl, pl.ds(t, N_CHAINS, stride=pitch), :]
            u = u_scr[sl, pl.ds(t, N_CHAINS, stride=pitch), :]
            h = a * hs[sl] + u
            u_scr[sl, pl.ds(t, N_CHAINS, stride=pitch), :] = h
            new_h.append(h)
            if link:
                p = a * ps[sl]
                a_scr[sl, pl.ds(t, N_CHAINS, stride=pitch), :] = p
                new_p.append(p)
            else:
                new_p.append(ps[sl])
        return tuple(new_h), tuple(new_p)

    h_init = tuple(h0_ref[0, :, sl * LANES:(sl + 1) * LANES] for sl in range(RG_SLABS))
    p_init = tuple(jnp.ones((N_CHAINS, LANES), F32) for _ in range(RG_SLABS))
    h_end, p_end = lax.fori_loop(0, cl, step, (h_init, p_init), unroll=8)

    row = lax.broadcasted_iota(jnp.int32, (N_CHAINS, LANES), 0)
    for sl in range(RG_SLABS):
        lanes = slice(sl * LANES, (sl + 1) * LANES)
        if link:
            def shift_down(v):
                return jnp.where(row == 0, 0.0, pltpu.roll(v, 1, axis=0))
            hh = h_end[sl]
            for _ in range(N_CHAINS - 1):
                hh = h_end[sl] + p_end[sl] * shift_down(hh)
            carry_in = shift_down(hh)
        else:
            hh = h_end[sl]
        hl_ref[0, :, lanes] = hh
        for c in range(N_CHAINS):
            h = u_scr[sl, c * pitch:c * pitch + cl, :]
            if link:
                h = h + a_scr[sl, c * pitch:c * pitch + cl, :] * carry_in[c:c + 1, :]
            gg = _gelu_tanh(gr_ref[c * cl:(c + 1) * cl, lanes])
            hg_ref[c * cl:(c + 1) * cl, lanes] = (h * gg).astype(BF16)


def _rglru(z, state8, h0, w_conv, b_conv, w_gate, b_gate, lam, *, n_blocks, n_seq, seq_len, row_block0, link):
    rows = n_seq * seq_len
    cl = rows // N_CHAINS
    ncb = D_RNN // RG_CW
    gr0 = COL_GR // RG_CW
    kern = functools.partial(_rglru_kernel, n_seq=n_seq, seq_len=seq_len, link=link)
    return pl.pallas_call(
        kern,
        grid=(n_blocks, ncb),
        in_specs=[pl.BlockSpec((rows, RG_CW), lambda b, n: (row_block0 + b, n)),
                  pl.BlockSpec((rows, RG_CW), lambda b, n: (row_block0 + b, gr0 + n)),
                  pl.BlockSpec((n_seq, SUBLANES, RG_CW), lambda b, n: (b, 0, n)),
                  pl.BlockSpec((1, N_CHAINS, RG_CW), lambda b, n: (b, 0, n)),
                  pl.BlockSpec((CONV_W, RG_CW), lambda b, n: (0, n)),
                  pl.BlockSpec((1, RG_CW), lambda b, n: (0, n)),
                  pl.BlockSpec((RG_SLABS, RNN_BLOCK, 2 * RNN_BLOCK), lambda b, n: (n, 0, 0)),
                  pl.BlockSpec((RG_SLABS, 1, 2 * RNN_BLOCK), lambda b, n: (n, 0, 0)),
                  pl.BlockSpec((1, RG_CW), lambda b, n: (0, n))],
        out_specs=[pl.BlockSpec((rows, RG_CW), lambda b, n: (b, n)),
                   pl.BlockSpec((1, N_CHAINS, RG_CW), lambda b, n: (b, 0, n))],
        out_shape=[jax.ShapeDtypeStruct((n_blocks * rows, D_RNN), BF16),
                   jax.ShapeDtypeStruct((n_blocks, N_CHAINS, D_RNN), F32)],
        scratch_shapes=[pltpu.VMEM((n_seq, SUBLANES + seq_len, RG_CW), F32),
                        pltpu.VMEM((RG_SLABS, N_CHAINS * (cl + CHAIN_PAD), LANES), F32),
                        pltpu.VMEM((RG_SLABS, N_CHAINS * (cl + CHAIN_PAD), LANES), F32)],
        compiler_params=_cparams(("arbitrary", "arbitrary")),
        name="rglru_link" if link else "rglru_step",
    )(z, z, state8, h0, w_conv, b_conv.reshape(1, D_RNN), w_gate, b_gate, lam.reshape(1, D_RNN))


def _build_bias(tbl_ref, bkt_ref, bias_scr, n_q):
    bkt = bkt_ref[...]
    base = jnp.where(bkt < 0, NEG_INF, 0.0)
    for h in range(N_HEADS):
        acc = base
        for bk in range(N_BUCKETS):
            acc = jnp.where(bkt == bk, tbl_ref[bk, h], acc)
        hk, g = divmod(h, GQA_GROUP)
        bias_scr[hk, g * n_q:(g + 1) * n_q, :] = acc


def _sink_columns(sink_ref, n_q):
    row = lax.broadcasted_iota(jnp.int32, (GQA_GROUP * n_q, 1), 0)
    cols = []
    for hk in range(N_KV_HEADS):
        col = jnp.full((GQA_GROUP * n_q, 1), sink_ref[hk * GQA_GROUP], F32)
        for g in range(1, GQA_GROUP):
            col = jnp.where(row >= g * n_q, sink_ref[hk * GQA_GROUP + g], col)
        cols.append(col)
    return cols


def _attend(q_of, k_of, v_of, bias_scr, sinks, key_ok, store, n_q):
    scores = []
    for hk in range(N_KV_HEADS):
        qg = jnp.concatenate([q_of(hk * GQA_GROUP + g) for g in range(GQA_GROUP)], axis=0)
        s = lax.dot_general(qg, k_of(hk), (((1,), (1,)), ((), ())), preferred_element_type=F32)
        s = s * (HEAD_DIM ** -0.5) + bias_scr[hk]
        if key_ok is not None:
            s = jnp.where(key_ok, s, NEG_INF)
        scores.append(s)
    probs = []
    for hk in range(N_KV_HEADS):
        s, sink = scores[hk], sinks[hk]
        m = jnp.maximum(jnp.max(s, axis=-1, keepdims=True), sink)
        p = jnp.exp(s - m)
        inv = 1.0 / (jnp.sum(p, axis=-1, keepdims=True) + jnp.exp(sink - m))
        probs.append((p * inv).astype(BF16))
    for hk in range(N_KV_HEADS):
        o = jnp.dot(probs[hk], v_of(hk), preferred_element_type=F32)
        for g in range(GQA_GROUP):
            store(hk * GQA_GROUP + g, o[g * n_q:(g + 1) * n_q, :].astype(BF16))


ATT_CPB = 8
ATT_LEAD = WIN_CHUNKS * CHUNK
ATT_TAIL = KEY_TILE - SPAN


def _attn_band_kernel(tbl_ref, sink_ref, q_ref, k_ref, v_ref, bkt_ref, o_ref, kpad, vpad, bias_scr, *, seq_len):
    b = pl.program_id(0)
    cg = pl.program_id(1)

    @pl.when((b == 0) & (cg == 0))
    def _():
        _build_bias(tbl_ref, bkt_ref, bias_scr, CHUNK)

    @pl.when(cg == 0)
    def _():
        for ref, pad in ((k_ref, kpad), (v_ref, vpad)):
            pad[0:ATT_LEAD, :] = jnp.zeros((ATT_LEAD, KV_DIM), BF16)
            pad[ATT_LEAD:ATT_LEAD + seq_len, :] = ref[...].astype(BF16)
            pad[ATT_LEAD + seq_len:, :] = jnp.zeros((ATT_TAIL, KV_DIM), BF16)

    kidx = lax.broadcasted_iota(jnp.int32, (1, KEY_TILE), 1)
    sinks = _sink_columns(sink_ref, CHUNK)

    def chunk(c, carry):
        q0 = pl.multiple_of(c * CHUNK, CHUNK)
        start = pl.multiple_of((cg * ATT_CPB + c) * CHUNK, CHUNK)
        key_ok = start + kidx >= ATT_LEAD

        def head_cols(h):
            return slice(h * HEAD_DIM, (h + 1) * HEAD_DIM)

        def store(h, val):
            o_ref[pl.ds(q0, CHUNK), head_cols(h)] = val

        _attend(lambda h: q_ref[pl.ds(q0, CHUNK), head_cols(h)].astype(BF16),
                lambda hk: kpad[pl.ds(start, KEY_TILE), head_cols(hk)],
                lambda hk: vpad[pl.ds(start, KEY_TILE), head_cols(hk)],
                bias_scr, sinks, key_ok, store, CHUNK)
        return carry

    lax.fori_loop(0, ATT_CPB, chunk, 0)


def _attn_band(z, table, sinks, bkt, *, n_batch, seq_len):
    rows = ATT_CPB * CHUNK
    ng = seq_len // rows
    kern = functools.partial(_attn_band_kernel, seq_len=seq_len)
    smem = pl.BlockSpec(memory_space=pltpu.SMEM)
    pad_rows = ATT_LEAD + seq_len + ATT_TAIL
    return pl.pallas_call(
        kern,
        grid=(n_batch, ng),
        in_specs=[smem, smem,
                  pl.BlockSpec((rows, Q_DIM), lambda b, c: (b * ng + c, COL_Q // Q_DIM)),
                  pl.BlockSpec((seq_len, KV_DIM), lambda b, c: (b, COL_K // KV_DIM)),
                  pl.BlockSpec((seq_len, KV_DIM), lambda b, c: (b, COL_V // KV_DIM)),
                  pl.BlockSpec((CHUNK, KEY_TILE), lambda b, c: (0, 0))],
        out_specs=pl.BlockSpec((rows, Q_DIM), lambda b, c: (b * ng + c, 0)),
        out_shape=jax.ShapeDtypeStruct((n_batch * seq_len, Q_DIM), BF16),
        scratch_shapes=[pltpu.VMEM((pad_rows, KV_DIM), BF16),
                        pltpu.VMEM((pad_rows, KV_DIM), BF16),
                        pltpu.VMEM((N_KV_HEADS, GQA_GROUP * CHUNK, KEY_TILE), F32)],
        compiler_params=_cparams(("arbitrary", "arbitrary")),
        name="attn_band",
    )(table, sinks, z, z, z, bkt)


def _attn_step_kernel(tbl_ref, sink_ref, q_ref, k_ref, v_ref, bkt_ref, o_ref, kbuf, vbuf, bias_scr,
                      *, n_keys, n_q):
    b = pl.program_id(0)

    @pl.when(b == 0)
    def _():
        _build_bias(tbl_ref, bkt_ref, bias_scr, n_q)
        kbuf[n_keys:, :] = jnp.zeros((KEY_TILE - n_keys, KV_DIM), BF16)
        vbuf[n_keys:, :] = jnp.zeros((KEY_TILE - n_keys, KV_DIM), BF16)

    kbuf[0:n_keys, :] = k_ref[0].astype(BF16)
    vbuf[0:n_keys, :] = v_ref[0].astype(BF16)

    def head_cols(h):
        return slice(h * HEAD_DIM, (h + 1) * HEAD_DIM)

    def store(h, val):
        o_ref[:, head_cols(h)] = val

    _attend(lambda h: q_ref[:, head_cols(h)].astype(BF16),
            lambda hk: kbuf[:, head_cols(hk)], lambda hk: vbuf[:, head_cols(hk)],
            bias_scr, _sink_columns(sink_ref, n_q), None, store, n_q)


def _attn_step(z, k_all, v_all, table, sinks, bkt, *, n_batch, n_q, row_block0):
    n_keys = k_all.shape[1]
    kern = functools.partial(_attn_step_kernel, n_keys=n_keys, n_q=n_q)
    smem = pl.BlockSpec(memory_space=pltpu.SMEM)
    return pl.pallas_call(
        kern,
        grid=(n_batch,),
        in_specs=[smem, smem,
                  pl.BlockSpec((n_q, Q_DIM), lambda b: (row_block0 + b, COL_Q // Q_DIM)),
                  pl.BlockSpec((1, n_keys, KV_DIM), lambda b: (b, 0, 0)),
                  pl.BlockSpec((1, n_keys, KV_DIM), lambda b: (b, 0, 0)),
                  pl.BlockSpec((n_q, KEY_TILE), lambda b: (0, 0))],
        out_specs=pl.BlockSpec((n_q, Q_DIM), lambda b: (b, 0)),
        out_shape=jax.ShapeDtypeStruct((n_batch * n_q, Q_DIM), BF16),
        scratch_shapes=[pltpu.VMEM((KEY_TILE, KV_DIM), BF16),
                        pltpu.VMEM((KEY_TILE, KV_DIM), BF16),
                        pltpu.VMEM((N_KV_HEADS, GQA_GROUP * n_q, KEY_TILE), F32)],
        compiler_params=_cparams(("arbitrary",)),
        name="attn_step",
    )(table, sinks, z, k_all, v_all, bkt)


def _t5_bucket(rel):
    nb = N_BUCKETS // 2
    ret = jnp.where(rel > 0, nb, 0)
    n = jnp.abs(rel)
    max_exact = nb // 2
    nf = jnp.maximum(n, 1).astype(jnp.float32)
    large = max_exact + (jnp.log(nf / max_exact) / math.log(MAX_DISTANCE / max_exact)
                         * (nb - max_exact)).astype(jnp.int32)
    large = jnp.minimum(large, nb - 1)
    return ret + jnp.where(n < max_exact, n, large)


def _bucket_map(q_pos, k_pos):
    bkt = _t5_bucket(k_pos[None, :] - q_pos[:, None]).astype(jnp.int32)
    return jnp.pad(bkt, ((0, 0), (0, KEY_TILE - k_pos.shape[0])), constant_values=-1)


ROUTE_LANES = LANES
GATE_TN = 1024
OUT_VMEM_LIMIT = 60 * 1024 * 1024


def _outproj_kernel(hgp_ref, hgs_ref, op_ref, os_ref, ga0_ref, ga1_ref, gb0_ref, gb1_ref, xp_ref, xs_ref,
                    g1_ref, sh2_ref, sc2_ref, ng_ref, wr_ref, wa_ref, wo_ref, wrt_ref, brt_ref,
                    x1_ref, n2_ref, rt_ref, *, n_prompt_tiles):
    tm = x1_ref.shape[0]
    is_prompt = pl.program_id(0) < n_prompt_tiles
    hg = jnp.where(is_prompt, hgp_ref[...], hgs_ref[...])
    o = jnp.where(is_prompt, op_ref[...], os_ref[...])
    x = jnp.where(is_prompt, xp_ref[...], xs_ref[...])
    ya = jnp.dot(hg, wr_ref[...], preferred_element_type=F32)
    yb = jnp.dot(o, wa_ref[...], preferred_element_type=F32)
    halves = []
    for k, (ga_ref, gb_ref) in enumerate(((ga0_ref, gb0_ref), (ga1_ref, gb1_ref))):
        cols = slice(k * GATE_TN, (k + 1) * GATE_TN)
        halves.append((_sigmoid(ga_ref[...]) * ya[:, cols] + _sigmoid(gb_ref[...]) * yb[:, cols]).astype(BF16))
    merged = jnp.concatenate(halves, axis=1)
    mix = jnp.dot(merged, wo_ref[...], preferred_element_type=F32)
    x1 = x + _bcast_rows(g1_ref[...], tm) * mix
    x1_ref[...] = x1
    n2 = _rms_modulate(x1, ng_ref[...], sc2_ref[...], sh2_ref[...])
    n2_ref[...] = n2

    lg = jnp.dot(n2.astype(BF16), wrt_ref[...].astype(BF16), preferred_element_type=F32) + brt_ref[...]

    lane = lax.broadcasted_iota(jnp.int32, (tm, ROUTE_LANES), 1)
    lane_f = lane.astype(F32)
    e_lane = lane - N_GROUPS
    lane_group = (e_lane >> 3).astype(F32)

    def first_argmax(vals, vmax):
        return jnp.min(jnp.where(vals == vmax, lane_f, float(ROUTE_LANES)), axis=-1, keepdims=True)

    gl = jnp.where(lane < N_GROUPS, lg, NEG_INF)
    gmax = jnp.max(gl, axis=-1, keepdims=True)
    g_idx = first_argmax(gl, gmax)
    g_w = 1.0 / jnp.sum(jnp.exp(gl - gmax), axis=-1, keepdims=True)
    in_group = jnp.where((e_lane >= 0) & (e_lane < N_EXPERTS), lane_group, -1.0) == g_idx
    el = jnp.where(in_group, lg, NEG_INF)
    v1 = jnp.max(el, axis=-1, keepdims=True)
    i1 = first_argmax(el, v1)
    el2 = jnp.where(lane_f == i1, NEG_INF, el)
    v2 = jnp.max(el2, axis=-1, keepdims=True)
    i2 = first_argmax(el2, v2)
    e21 = jnp.exp(v2 - v1)
    w1 = g_w / (1.0 + e21)
    w2 = g_w * e21 / (1.0 + e21)
    e1 = i1 - float(N_GROUPS)
    e2 = i2 - float(N_GROUPS)
    rt_ref[...] = jnp.where(lane == 0, e1, jnp.where(lane == 1, e2, jnp.where(lane == 2, w1,
                            jnp.where(lane == 3, w2, 0.0))))


def _outproj(hg_p, hg_s, o_p, o_s, z, x_p, x_s, g1, sh2, sc2, norm_g, wr, wa, wo, w_route, b_route):
    tm = ROW_TILE
    t = z.shape[0]
    seg = tm // MOD_ROWS
    n_p = x_p.shape[0] // tm
    row = lambda i: (i, 0)
    fix = lambda i: (0, 0)
    row_p = lambda i: (jnp.minimum(i, n_p - 1), 0)
    row_s = lambda i: (jnp.maximum(i - n_p, 0), 0)
    once = pl.Buffered(1)

    def gate(col):
        return pl.BlockSpec((tm, GATE_TN), lambda i: (i, col // GATE_TN))

    return pl.pallas_call(
        functools.partial(_outproj_kernel, n_prompt_tiles=n_p),
        grid=(t // tm,),
        in_specs=[pl.BlockSpec((tm, D_RNN), row_p), pl.BlockSpec((tm, D_RNN), row_s, pipeline_mode=once),
                  pl.BlockSpec((tm, Q_DIM), row_p), pl.BlockSpec((tm, Q_DIM), row_s, pipeline_mode=once),
                  gate(COL_GA), gate(COL_GA + GATE_TN), gate(COL_GB), gate(COL_GB + GATE_TN),
                  pl.BlockSpec((tm, D_MODEL), row_p), pl.BlockSpec((tm, D_MODEL), row_s, pipeline_mode=once),
                  pl.BlockSpec((seg, D_MODEL), row), pl.BlockSpec((seg, D_MODEL), row),
                  pl.BlockSpec((seg, D_MODEL), row),
                  pl.BlockSpec((1, D_MODEL), fix),
                  pl.BlockSpec((D_RNN, D_MODEL), fix, pipeline_mode=once),
                  pl.BlockSpec((Q_DIM, D_MODEL), fix, pipeline_mode=once),
                  pl.BlockSpec((D_MODEL, D_MODEL), fix, pipeline_mode=once),
                  pl.BlockSpec((D_MODEL, ROUTE_LANES), fix),
                  pl.BlockSpec((1, ROUTE_LANES), fix)],
        out_specs=[pl.BlockSpec((tm, D_MODEL), row), pl.BlockSpec((tm, D_MODEL), row),
                   pl.BlockSpec((tm, ROUTE_LANES), row)],
        out_shape=[jax.ShapeDtypeStruct((t, D_MODEL), F32), jax.ShapeDtypeStruct((t, D_MODEL), F32),
                   jax.ShapeDtypeStruct((t, ROUTE_LANES), F32)],
        compiler_params=_cparams(("arbitrary",), OUT_VMEM_LIMIT),
        name="outproj",
    )(hg_p, hg_s, o_p, o_s, z, z, z, z, x_p, x_s, g1, sh2, sc2, norm_g.reshape(1, D_MODEL),
      wr, wa, wo, w_route, b_route)


def _row_gather(src_hbm, idx_ref, base, dst, sem, rows, two_queues=False):
    for r in rows:
        tok = idx_ref[base + r]
        cp = pltpu.make_async_copy(src_hbm.at[pl.ds(tok, 1), :], dst.at[pl.ds(r, 1), :], sem)
        cp.start(priority=r % 2 if two_queues else 0)


def _row_gather_wait(src_hbm, dst, sem, n_rows):
    pltpu.make_async_copy(src_hbm.at[pl.ds(0, n_rows), :], dst, sem).wait()


MOE_KC = 256


def _moe_kernel(te_ref, tfirst_ref, tnext_ref, src_ref, n2_hbm, wg_hbm, wu_hbm, wd_hbm, y_ref,
                xbuf, xsem, wg_st, wu_st, wd_st, wsem, wg_bf, wu_bf, wd_bf):
    i = pl.program_id(0)
    n_tiles = pl.num_programs(0)
    slot = i % 2
    stages = ((wg_hbm, wg_st, wg_bf), (wu_hbm, wu_st, wu_bf), (wd_hbm, wd_st, wd_bf))

    def weight_copy(k, e):
        hbm, st, _ = stages[k]
        return pltpu.make_async_copy(hbm.at[e], st, wsem.at[k])

    @pl.when(i == 0)
    def _():
        for k in range(len(stages)):
            weight_copy(k, te_ref[0]).start(priority=1)
        _row_gather(n2_hbm, src_ref, 0, xbuf.at[0], xsem.at[0], range(MOE_TM))

    @pl.when(tfirst_ref[i] == 1)
    def _():
        for k, (_, st, bf) in enumerate(stages):
            weight_copy(k, te_ref[i]).wait()
            bf[...] = st[...].astype(BF16)

        @pl.when(tnext_ref[i] >= 0)
        def _():
            for k in range(len(stages)):
                weight_copy(k, tnext_ref[i]).start(priority=1)

    _row_gather_wait(n2_hbm, xbuf.at[slot], xsem.at[slot], MOE_TM)
    nxt_base = jnp.minimum(i + 1, n_tiles - 1) * MOE_TM
    nk = D_MODEL // MOE_KC
    per_chunk = MOE_TM // nk
    hgate = hup = None
    for kc in range(nk):
        ks = slice(kc * MOE_KC, (kc + 1) * MOE_KC)
        xk = xbuf[slot, :, ks].astype(BF16)
        dg = jnp.dot(xk, wg_bf[ks, :], preferred_element_type=F32)
        du = jnp.dot(xk, wu_bf[ks, :], preferred_element_type=F32)
        hgate = dg if hgate is None else hgate + dg
        hup = du if hup is None else hup + du
        _row_gather(n2_hbm, src_ref, nxt_base, xbuf.at[1 - slot], xsem.at[1 - slot],
                    range(kc * per_chunk, (kc + 1) * per_chunk))
    act = (hgate * _sigmoid(hgate) * hup).astype(BF16)
    y_ref[...] = jnp.dot(act, wd_bf[...], preferred_element_type=F32)

    @pl.when(i == n_tiles - 1)
    def _():
        _row_gather_wait(n2_hbm, xbuf.at[1 - slot], xsem.at[1 - slot], MOE_TM)


def _moe(n2, w_gate, w_up, w_down, tile_expert, tile_first, tile_next, src_tok):
    n_tiles = tile_expert.shape[0]
    hbm = pl.BlockSpec(memory_space=pl.ANY)
    grid_spec = pltpu.PrefetchScalarGridSpec(
        num_scalar_prefetch=4,
        grid=(n_tiles,),
        in_specs=[hbm, hbm, hbm, hbm],
        out_specs=pl.BlockSpec((MOE_TM, D_MODEL), lambda i, te, tf, tn, st: (i, 0)),
        scratch_shapes=[pltpu.VMEM((2, MOE_TM, D_MODEL), F32),
                        pltpu.SemaphoreType.DMA((2,)),
                        pltpu.VMEM((D_MODEL, D_EXPERT), F32),
                        pltpu.VMEM((D_MODEL, D_EXPERT), F32),
                        pltpu.VMEM((D_EXPERT, D_MODEL), F32),
                        pltpu.SemaphoreType.DMA((3,)),
                        pltpu.VMEM((D_MODEL, D_EXPERT), BF16),
                        pltpu.VMEM((D_MODEL, D_EXPERT), BF16),
                        pltpu.VMEM((D_EXPERT, D_MODEL), BF16)],
    )
    return pl.pallas_call(
        _moe_kernel,
        grid_spec=grid_spec,
        out_shape=jax.ShapeDtypeStruct((n_tiles * MOE_TM, D_MODEL), F32),
        compiler_params=_cparams(("arbitrary",)),
        name="moe",
    )(tile_expert, tile_first, tile_next, src_tok, n2, w_gate, w_up, w_down)


def _route_plan(e1, e2, n_tok):
    experts = jnp.arange(N_EXPERTS, dtype=jnp.int32)
    flat_e = jnp.concatenate([e1, e2])
    onehot = (flat_e[:, None] == experts[None, :]).astype(jnp.int32)
    csum = jnp.cumsum(onehot, axis=0)
    rank = jnp.sum(csum * onehot, axis=1) - 1
    counts = csum[-1]
    tiles_per = (counts + MOE_TM - 1) // MOE_TM
    tile_end = jnp.cumsum(tiles_per)
    tile_off = tile_end - tiles_per
    slot = jnp.sum(onehot * tile_off[None, :], axis=1) * MOE_TM + rank
    n_tiles = (2 * n_tok) // MOE_TM + N_EXPERTS
    tok = jnp.tile(jnp.arange(n_tok, dtype=jnp.int32), 2)
    src_tok = jnp.zeros((n_tiles * MOE_TM,), jnp.int32).at[slot].set(tok)
    tile_id = jnp.arange(n_tiles, dtype=jnp.int32)
    n_used = tile_end[-1]
    te = jnp.sum((tile_end[None, :] <= jnp.minimum(tile_id, n_used - 1)[:, None]).astype(jnp.int32), axis=1)
    tile_expert = jnp.minimum(te, N_EXPERTS - 1)
    prev = jnp.concatenate([jnp.full((1,), -1, jnp.int32), tile_expert[:-1]])
    tile_first = (tile_expert != prev).astype(jnp.int32)
    next_tile = tile_end[tile_expert]
    tile_next = jnp.where(next_tile < n_used, tile_expert[jnp.minimum(next_tile, n_tiles - 1)], -1)
    return tile_expert, tile_first, tile_next, src_tok, slot[:n_tok], slot[n_tok:]


def _final_kernel(p1_ref, p2_ref, ys_hbm, x1_ref, g2_ref, rt_ref, fg_ref, yp_ref, ysm_ref, ybuf, sem,
                  *, n_prompt_tiles):
    i = pl.program_id(0)
    n_tiles = pl.num_programs(0)
    slot = i % 2

    def start(tile, s):
        _row_gather(ys_hbm, p1_ref, tile * ROW_TILE, ybuf.at[s, 0], sem.at[s], range(ROW_TILE), two_queues=True)
        _row_gather(ys_hbm, p2_ref, tile * ROW_TILE, ybuf.at[s, 1], sem.at[s], range(ROW_TILE), two_queues=True)

    @pl.when(i == 0)
    def _():
        start(0, 0)

    @pl.when(i + 1 < n_tiles)
    def _():
        start(i + 1, 1 - slot)

    _row_gather_wait(ys_hbm, ybuf.at[slot, 0], sem.at[slot], ROW_TILE)
    _row_gather_wait(ys_hbm, ybuf.at[slot, 1], sem.at[slot], ROW_TILE)
    rt = rt_ref[...]
    moe = rt[:, 2:3] * ybuf[slot, 0] + rt[:, 3:4] * ybuf[slot, 1]
    x2 = x1_ref[...] + _bcast_rows(g2_ref[...], ROW_TILE) * moe
    y = x2 * lax.rsqrt(jnp.mean(x2 * x2, axis=-1, keepdims=True) + EPS) * fg_ref[...]

    @pl.when(i < n_prompt_tiles)
    def _():
        yp_ref[...] = y

    @pl.when(i >= n_prompt_tiles)
    def _():
        ysm_ref[...] = y


def _final(ys, x1, g2, rt, final_g, p1, p2, n_prompt_rows):
    t = x1.shape[0]
    tm = ROW_TILE
    seg = tm // MOD_ROWS
    n_p = n_prompt_rows // tm
    grid_spec = pltpu.PrefetchScalarGridSpec(
        num_scalar_prefetch=2,
        grid=(t // tm,),
        in_specs=[pl.BlockSpec(memory_space=pl.ANY),
                  pl.BlockSpec((tm, D_MODEL), lambda i, a, b: (i, 0)),
                  pl.BlockSpec((seg, D_MODEL), lambda i, a, b: (i, 0)),
                  pl.BlockSpec((tm, ROUTE_LANES), lambda i, a, b: (i, 0)),
                  pl.BlockSpec((1, D_MODEL), lambda i, a, b: (0, 0))],
        out_specs=[pl.BlockSpec((tm, D_MODEL), lambda i, a, b: (jnp.minimum(i, n_p - 1), 0)),
                   pl.BlockSpec((tm, D_MODEL), lambda i, a, b: (jnp.maximum(i - n_p, 0), 0))],
        scratch_shapes=[pltpu.VMEM((2, 2, tm, D_MODEL), F32),
                        pltpu.SemaphoreType.DMA((2,))],
    )
    return pl.pallas_call(
        functools.partial(_final_kernel, n_prompt_tiles=n_p),
        grid_spec=grid_spec,
        out_shape=[jax.ShapeDtypeStruct((n_prompt_rows, D_MODEL), F32),
                   jax.ShapeDtypeStruct((t - n_prompt_rows, D_MODEL), F32)],
        compiler_params=_cparams(("arbitrary",)),
        name="final",
    )(p1, p2, ys, x1, g2, rt, final_g.reshape(1, D_MODEL))


def kernel(x_prompt, x_sample, cache_k_win, cache_v_win, state_conv, state_rglru, c_prompt, c_sample, w_ada, b_ada, norm1_g, norm2_g, w_in, b_in, w_conv, b_conv, w_rg_a, b_rg_a, w_rg_x, b_rg_x, lru_lambda, w_rnn_out, w_attn_out, w_out, attn_sinks, w_route_group, b_route_group, w_route_expert, b_route_expert, w_exp_gate, w_exp_up, w_exp_down, rel_bias_table, final_norm_g):
    n_b, seq, _ = x_prompt.shape
    d_b, d_seq, _ = x_sample.shape
    assert w_ada.shape[0] == 1, "single trunk layer"
    assert seq % (ATT_CPB * CHUNK) == 0 and seq % MOD_ROWS == 0 and d_seq == MOD_ROWS
    assert d_seq >= CONV_W - 1 and d_b == N_CHAINS
    t_p, t_s = n_b * seq, d_b * d_seq
    t = t_p + t_s
    assert t_s == ROW_TILE and t_p % ROW_TILE == 0 and t % (IN_NSUB * ROW_TILE) == 0
    cw = cache_k_win.shape[2]
    l = 0

    x_p = x_prompt.reshape(t_p, D_MODEL)
    x_s = x_sample.reshape(t_s, D_MODEL)

    n_c = n_b + d_b
    c_all = jnp.pad(jnp.concatenate([c_prompt, c_sample], axis=0), ((0, -n_c % SUBLANES), (0, 0)))
    mod = _ada(c_all, w_ada[l], b_ada[l])

    def per_segment(m):
        return jnp.concatenate([jnp.repeat(m[:n_b], seq // MOD_ROWS, axis=0),
                                jnp.repeat(m[n_b:n_c], d_seq // MOD_ROWS, axis=0)], axis=0)

    sh1, sc1, g1, sh2, sc2, g2 = [per_segment(m) for m in jnp.split(mod, 6, axis=-1)]

    z = _inproj(x_p, x_s, sc1, sh1, norm1_g[l], w_in[l].astype(BF16), b_in[l])

    w_gate = jnp.concatenate([w_rg_a[l], w_rg_x[l]], axis=-1)
    b_gate = jnp.concatenate([b_rg_a[l], b_rg_x[l]], axis=-1)[:, None, :]
    hg_p, hl_p = _rglru(z, jnp.zeros((n_b, SUBLANES, D_RNN), F32), jnp.zeros((n_b, N_CHAINS, D_RNN), F32),
                        w_conv[l], b_conv[l], w_gate, b_gate, lru_lambda[l],
                        n_blocks=n_b, n_seq=1, seq_len=seq, row_block0=0, link=True)
    state8 = jnp.pad(state_conv[l], ((0, 0), (SUBLANES - (CONV_W - 1), 0), (0, 0)))
    hg_s, hl_s = _rglru(z, state8, state_rglru[l][None], w_conv[l], b_conv[l], w_gate, b_gate, lru_lambda[l],
                        n_blocks=1, n_seq=d_b, seq_len=d_seq, row_block0=t_p // t_s, link=False)

    sinks = attn_sinks[l]
    bkt_p = _bucket_map(WIN_CHUNKS * CHUNK + jnp.arange(CHUNK), jnp.arange(SPAN))
    o_p = _attn_band(z, rel_bias_table, sinks, bkt_p, n_batch=n_b, seq_len=seq)
    k_new = z[t_p:, COL_K:COL_K + KV_DIM].reshape(d_b, d_seq, KV_DIM)
    v_new = z[t_p:, COL_V:COL_V + KV_DIM].reshape(d_b, d_seq, KV_DIM)
    k_all = jnp.concatenate([cache_k_win[l].reshape(d_b, cw, KV_DIM), k_new], axis=1)
    v_all = jnp.concatenate([cache_v_win[l].reshape(d_b, cw, KV_DIM), v_new], axis=1)
    bkt_s = _bucket_map(cw + jnp.arange(d_seq), jnp.arange(cw + d_seq))
    o_s = _attn_step(z, k_all, v_all, rel_bias_table, sinks, bkt_s, n_batch=d_b, n_q=d_seq, row_block0=t_p // d_seq)

    n_route = N_GROUPS + N_EXPERTS
    w_route = jnp.pad(jnp.concatenate([w_route_group[l], w_route_expert[l]], axis=1),
                      ((0, 0), (0, ROUTE_LANES - n_route)))
    b_route = jnp.pad(jnp.concatenate([b_route_group[l], b_route_expert[l]]),
                      (0, ROUTE_LANES - n_route)).reshape(1, ROUTE_LANES)
    x1, n2, rt = _outproj(hg_p, hg_s, o_p, o_s, z, x_p, x_s, g1, sh2, sc2, norm2_g[l],
                          w_rnn_out[l].astype(BF16), w_attn_out[l].astype(BF16), w_out[l].astype(BF16),
                          w_route, b_route)

    e1 = rt[:, 0].astype(jnp.int32)
    e2 = rt[:, 1].astype(jnp.int32)
    tile_expert, tile_first, tile_next, src_tok, p1, p2 = _route_plan(e1, e2, t)
    ys = _moe(n2, w_exp_gate[l], w_exp_up[l], w_exp_down[l], tile_expert, tile_first, tile_next, src_tok)
    y_p, y_s = _final(ys, x1, g2, rt, final_norm_g, p1, p2, t_p)

    win = min(WINDOW, seq)

    def tail(col0, width, n_rows):
        return jnp.stack([z[(b + 1) * seq - n_rows:(b + 1) * seq, col0:col0 + width] for b in range(n_b)])

    kp = tail(COL_K, KV_DIM, win).reshape(n_b, win, N_KV_HEADS, HEAD_DIM)
    vp = tail(COL_V, KV_DIM, win).reshape(n_b, win, N_KV_HEADS, HEAD_DIM)
    cp = tail(COL_XR, D_RNN, CONV_W - 1)
    rp = hl_p[:, N_CHAINS - 1, :]
    ks = k_all[:, -cw:].reshape(d_b, cw, N_KV_HEADS, HEAD_DIM)
    vs = v_all[:, -cw:].reshape(d_b, cw, N_KV_HEADS, HEAD_DIM)
    cs = z[t_p:, COL_XR:COL_XR + D_RNN].reshape(d_b, d_seq, D_RNN)[:, -(CONV_W - 1):]
    rs = hl_s[0]
    return (y_p.reshape(n_b, seq, D_MODEL), y_s.reshape(d_b, d_seq, D_MODEL),
            kp[None], vp[None], cp[None], rp[None], ks[None], vs[None], cs[None], rs[None])
```

```python
import functools
import math

import jax
import jax.numpy as jnp
from jax import lax
from jax.experimental import pallas as pl
from jax.experimental.pallas import tpu as pltpu

F32 = jnp.float32
BF16 = jnp.bfloat16

D_MODEL = 2048
D_RNN = 2048
RNN_BLOCK = 128
CONV_W = 4
LRU_C = 8.0
N_HEADS = 16
N_KV_HEADS = 4
HEAD_DIM = 128
GQA_GROUP = N_HEADS // N_KV_HEADS
Q_DIM = N_HEADS * HEAD_DIM
KV_DIM = N_KV_HEADS * HEAD_DIM
CHUNK = 64
WINDOW = 128
WIN_CHUNKS = WINDOW // CHUNK
SPAN = (WIN_CHUNKS + 1) * CHUNK
N_BUCKETS = 32
MAX_DISTANCE = 128
N_GROUPS = 4
E_PER_GROUP = 8
N_EXPERTS = N_GROUPS * E_PER_GROUP
D_EXPERT = 512
EPS = 1e-6
NEG_INF = -1e30
D_IN = 2 * D_RNN + Q_DIM + 2 * KV_DIM + 2 * D_MODEL
COL_XR, COL_GR, COL_Q = 0, D_RNN, 2 * D_RNN
COL_K = COL_Q + Q_DIM
COL_V = COL_K + KV_DIM
COL_GA = COL_V + KV_DIM
COL_GB = COL_GA + D_MODEL

LANES = 128
SUBLANES = 8
MOD_ROWS = 32
KEY_TILE = 256
VMEM_LIMIT = 56 * 1024 * 1024

ROW_TILE = 256
IN_TM_MAX = 1408
IN_TN = 1024
MOE_TM = 192


def _sigmoid(x):
    return 0.5 * jnp.tanh(0.5 * x) + 0.5


def _gelu_tanh(x):
    return 0.5 * x * (1.0 + jnp.tanh(math.sqrt(2.0 / math.pi) * (x + 0.044715 * (x * x * x))))


def _bcast_rows(v, rows):
    n, d = v.shape
    return jnp.broadcast_to(v[:, None, :], (n, MOD_ROWS, d)).reshape(rows, d)


def _rms_modulate(x, gain, scale_seg, shift_seg):
    rows = x.shape[0]
    y = x * lax.rsqrt(jnp.mean(x * x, axis=-1, keepdims=True) + EPS) * gain
    return y * (1.0 + _bcast_rows(scale_seg, rows)) + _bcast_rows(shift_seg, rows)


def _cparams(sem, vmem_limit=VMEM_LIMIT):
    return pltpu.CompilerParams(dimension_semantics=sem, vmem_limit_bytes=vmem_limit)


def _ada_kernel(c_ref, w_ref, b_ref, o_ref):
    c = c_ref[...]
    s = (c * _sigmoid(c)).astype(BF16)
    o_ref[...] = jnp.dot(s, w_ref[...].astype(BF16), preferred_element_type=F32) + b_ref[...]


def _ada(c_all, w_ada, b_ada):
    rows = c_all.shape[0]
    n = w_ada.shape[1]
    tn = 1024
    return pl.pallas_call(
        _ada_kernel,
        grid=(n // tn,),
        in_specs=[pl.BlockSpec((rows, D_MODEL), lambda j: (0, 0)),
                  pl.BlockSpec((D_MODEL, tn), lambda j: (0, j)),
                  pl.BlockSpec((1, tn), lambda j: (0, j))],
        out_specs=pl.BlockSpec((rows, tn), lambda j: (0, j)),
        out_shape=jax.ShapeDtypeStruct((rows, n), F32),
        compiler_params=_cparams(("arbitrary",)),
        name="ada",
    )(c_all, w_ada, b_ada.reshape(1, n))


def _norm1_kernel(xp_ref, xs_ref, sc_ref, sh_ref, g_ref, n1_ref, *, n_prompt_tiles):
    x = jnp.where(pl.program_id(0) < n_prompt_tiles, xp_ref[...], xs_ref[...])
    n1_ref[...] = _rms_modulate(x, g_ref[...], sc_ref[...], sh_ref[...]).astype(BF16)


def _norm1(x_p, x_s, sc1, sh1, norm_g):
    tm = ROW_TILE
    n_p = x_p.shape[0] // tm
    t = x_p.shape[0] + x_s.shape[0]
    seg = tm // MOD_ROWS
    return pl.pallas_call(
        functools.partial(_norm1_kernel, n_prompt_tiles=n_p),
        grid=(t // tm,),
        in_specs=[pl.BlockSpec((tm, D_MODEL), lambda i: (jnp.minimum(i, n_p - 1), 0)),
                  pl.BlockSpec((tm, D_MODEL), lambda i: (jnp.maximum(i - n_p, 0), 0)),
                  pl.BlockSpec((seg, D_MODEL), lambda i: (i, 0)),
                  pl.BlockSpec((seg, D_MODEL), lambda i: (i, 0)),
                  pl.BlockSpec((1, D_MODEL), lambda i: (0, 0))],
        out_specs=pl.BlockSpec((tm, D_MODEL), lambda i: (i, 0)),
        out_shape=jax.ShapeDtypeStruct((t, D_MODEL), BF16),
        compiler_params=_cparams(("arbitrary",)),
        name="norm1",
    )(x_p, x_s, sc1, sh1, norm_g.reshape(1, D_MODEL))


def _inproj_kernel(n1_ref, w_ref, b_ref, z_ref, w_bf):
    @pl.when(pl.program_id(1) == 0)
    def _():
        w_bf[...] = w_ref[...].astype(BF16)

    z_ref[...] = jnp.dot(n1_ref[...], w_bf[...], preferred_element_type=F32) + b_ref[...]


def _inproj(n1, w_in, b_in):
    t = n1.shape[0]
    tm = next(m for m in range(IN_TM_MAX, 0, -LANES) if t % m == 0)
    return pl.pallas_call(
        _inproj_kernel,
        grid=(D_IN // IN_TN, t // tm),
        in_specs=[pl.BlockSpec((tm, D_MODEL), lambda j, i: (i, 0)),
                  pl.BlockSpec((D_MODEL, IN_TN), lambda j, i: (0, j)),
                  pl.BlockSpec((1, IN_TN), lambda j, i: (0, j))],
        out_specs=pl.BlockSpec((tm, IN_TN), lambda j, i: (i, j)),
        out_shape=jax.ShapeDtypeStruct((t, D_IN), F32),
        scratch_shapes=[pltpu.VMEM((D_MODEL, IN_TN), BF16)],
        compiler_params=_cparams(("arbitrary", "arbitrary")),
        name="inproj",
    )(n1, w_in, b_in.reshape(1, D_IN))


RG_CW = 256
RG_SLABS = RG_CW // LANES
N_CHAINS = SUBLANES
CHAIN_PAD = 8


def _rglru_kernel(xr_ref, gr_ref, st_ref, h0_ref, wc_ref, bc_ref, wg_ref, bg_ref, lam_ref,
                  hg_ref, hl_ref, ext_scr, a_scr, u_scr, *, n_seq, seq_len, link):
    rows = n_seq * seq_len
    cl = rows // N_CHAINS
    pitch = cl + CHAIN_PAD

    for s in range(n_seq):
        ext_scr[s, 0:SUBLANES, :] = st_ref[s]
        ext_scr[s, SUBLANES:SUBLANES + seq_len, :] = xr_ref[s * seq_len:(s + 1) * seq_len, :]

    z = -lam_ref[...]
    softplus = jnp.maximum(z, 0.0) + jnp.log1p(jnp.exp(-jnp.abs(z)))
    nq = (-0.25 * LRU_C) * softplus

    for c in range(N_CHAINS):
        s, r0 = divmod(c * cl, seq_len)
        xc = bc_ref[...]
        for j in range(CONV_W):
            xc = xc + wc_ref[j:j + 1, :] * ext_scr[s, pl.ds(SUBLANES - (CONV_W - 1) + j + r0, cl), :]
        for sl in range(RG_SLABS):
            lanes = slice(sl * LANES, (sl + 1) * LANES)
            xb = xc[:, lanes]
            g = jnp.dot(xb.astype(BF16), wg_ref[sl].astype(BF16), preferred_element_type=F32) + bg_ref[sl]
            tr = jnp.tanh(0.5 * g[:, :LANES])
            ti = jnp.tanh(0.5 * g[:, LANES:])
            th = jnp.tanh(nq[:, lanes] * (tr + 1.0))
            rcp = 1.0 / (1.0 - th)
            a_scr[sl, c * pitch:c * pitch + cl, :] = (1.0 + th) * rcp
            u_scr[sl, c * pitch:c * pitch + cl, :] = (jnp.sqrt(-th) * rcp) * ((ti + 1.0) * xb)

    def step(t, carry):
        hs, ps = carry
        new_h, new_p = [], []
        for sl in range(RG_SLABS):
            a = a_scr[sl, pl.ds(t, N_CHAINS, stride=pitch), :]
            u = u_scr[sl, pl.ds(t, N_CHAINS, stride=pitch), :]
            h = a * hs[sl] + u
            u_scr[sl, pl.ds(t, N_CHAINS, stride=pitch), :] = h
            new_h.append(h)
            if link:
                p = a * ps[sl]
                a_scr[sl, pl.ds(t, N_CHAINS, stride=pitch), :] = p
                new_p.append(p)
            else:
                new_p.append(ps[sl])
        return tuple(new_h), tuple(new_p)

    h_init = tuple(h0_ref[0, :, sl * LANES:(sl + 1) * LANES] for sl in range(RG_SLABS))
    p_init = tuple(jnp.ones((N_CHAINS, LANES), F32) for _ in range(RG_SLABS))
    h_end, p_end = lax.fori_loop(0, cl, step, (h_init, p_init), unroll=8)

    row = lax.broadcasted_iota(jnp.int32, (N_CHAINS, LANES), 0)
    for sl in range(RG_SLABS):
        lanes = slice(sl * LANES, (sl + 1) * LANES)
        if link:
            def shift_down(v):
                return jnp.where(row == 0, 0.0, pltpu.roll(v, 1, axis=0))
            hh = h_end[sl]
            for _ in range(N_CHAINS - 1):
                hh = h_end[sl] + p_end[sl] * shift_down(hh)
            carry_in = shift_down(hh)
        else:
            hh = h_end[sl]
        hl_ref[0, :, lanes] = hh
        for c in range(N_CHAINS):
            h = u_scr[sl, c * pitch:c * pitch + cl, :]
            if link:
                h = h + a_scr[sl, c * pitch:c * pitch + cl, :] * carry_in[c:c + 1, :]
            gg = _gelu_tanh(gr_ref[c * cl:(c + 1) * cl, lanes])
            hg_ref[c * cl:(c + 1) * cl, lanes] = (h * gg).astype(BF16)


def _rglru(z, state8, h0, w_conv, b_conv, w_gate, b_gate, lam, *, n_blocks, n_seq, seq_len, row_block0, link):
    rows = n_seq * seq_len
    cl = rows // N_CHAINS
    ncb = D_RNN // RG_CW
    gr0 = COL_GR // RG_CW
    kern = functools.partial(_rglru_kernel, n_seq=n_seq, seq_len=seq_len, link=link)
    return pl.pallas_call(
        kern,
        grid=(n_blocks, ncb),
        in_specs=[pl.BlockSpec((rows, RG_CW), lambda b, n: (row_block0 + b, n)),
                  pl.BlockSpec((rows, RG_CW), lambda b, n: (row_block0 + b, gr0 + n)),
                  pl.BlockSpec((n_seq, SUBLANES, RG_CW), lambda b, n: (b, 0, n)),
                  pl.BlockSpec((1, N_CHAINS, RG_CW), lambda b, n: (b, 0, n)),
                  pl.BlockSpec((CONV_W, RG_CW), lambda b, n: (0, n)),
                  pl.BlockSpec((1, RG_CW), lambda b, n: (0, n)),
                  pl.BlockSpec((RG_SLABS, RNN_BLOCK, 2 * RNN_BLOCK), lambda b, n: (n, 0, 0)),
                  pl.BlockSpec((RG_SLABS, 1, 2 * RNN_BLOCK), lambda b, n: (n, 0, 0)),
                  pl.BlockSpec((1, RG_CW), lambda b, n: (0, n))],
        out_specs=[pl.BlockSpec((rows, RG_CW), lambda b, n: (b, n)),
                   pl.BlockSpec((1, N_CHAINS, RG_CW), lambda b, n: (b, 0, n))],
        out_shape=[jax.ShapeDtypeStruct((n_blocks * rows, D_RNN), BF16),
                   jax.ShapeDtypeStruct((n_blocks, N_CHAINS, D_RNN), F32)],
        scratch_shapes=[pltpu.VMEM((n_seq, SUBLANES + seq_len, RG_CW), F32),
                        pltpu.VMEM((RG_SLABS, N_CHAINS * (cl + CHAIN_PAD), LANES), F32),
                        pltpu.VMEM((RG_SLABS, N_CHAINS * (cl + CHAIN_PAD), LANES), F32)],
        compiler_params=_cparams(("arbitrary", "arbitrary")),
        name="rglru_link" if link else "rglru_step",
    )(z, z, state8, h0, w_conv, b_conv.reshape(1, D_RNN), w_gate, b_gate, lam.reshape(1, D_RNN))


def _build_bias(tbl_ref, bkt_ref, bias_scr, n_q):
    bkt = bkt_ref[...]
    base = jnp.where(bkt < 0, NEG_INF, 0.0)
    for h in range(N_HEADS):
        acc = base
        for bk in range(N_BUCKETS):
            acc = jnp.where(bkt == bk, tbl_ref[bk, h], acc)
        hk, g = divmod(h, GQA_GROUP)
        bias_scr[hk, g * n_q:(g + 1) * n_q, :] = acc


def _sink_columns(sink_ref, n_q):
    row = lax.broadcasted_iota(jnp.int32, (GQA_GROUP * n_q, 1), 0)
    cols = []
    for hk in range(N_KV_HEADS):
        col = jnp.full((GQA_GROUP * n_q, 1), sink_ref[hk * GQA_GROUP], F32)
        for g in range(1, GQA_GROUP):
            col = jnp.where(row >= g * n_q, sink_ref[hk * GQA_GROUP + g], col)
        cols.append(col)
    return cols


def _attend(q_of, k_of, v_of, bias_scr, sinks, key_ok, store, n_q):
    scores = []
    for hk in range(N_KV_HEADS):
        qg = jnp.concatenate([q_of(hk * GQA_GROUP + g) for g in range(GQA_GROUP)], axis=0)
        s = lax.dot_general(qg, k_of(hk), (((1,), (1,)), ((), ())), preferred_element_type=F32)
        s = s * (HEAD_DIM ** -0.5) + bias_scr[hk]
        if key_ok is not None:
            s = jnp.where(key_ok, s, NEG_INF)
        scores.append(s)
    probs = []
    for hk in range(N_KV_HEADS):
        s, sink = scores[hk], sinks[hk]
        m = jnp.maximum(jnp.max(s, axis=-1, keepdims=True), sink)
        p = jnp.exp(s - m)
        inv = 1.0 / (jnp.sum(p, axis=-1, keepdims=True) + jnp.exp(sink - m))
        probs.append((p * inv).astype(BF16))
    for hk in range(N_KV_HEADS):
        o = jnp.dot(probs[hk], v_of(hk), preferred_element_type=F32)
        for g in range(GQA_GROUP):
            store(hk * GQA_GROUP + g, o[g * n_q:(g + 1) * n_q, :].astype(BF16))


ATT_CPB = 8
ATT_LEAD = WIN_CHUNKS * CHUNK
ATT_TAIL = KEY_TILE - SPAN


def _attn_band_kernel(tbl_ref, sink_ref, q_ref, k_ref, v_ref, bkt_ref, o_ref, kpad, vpad, bias_scr, *, seq_len):
    b = pl.program_id(0)
    cg = pl.program_id(1)

    @pl.when((b == 0) & (cg == 0))
    def _():
        _build_bias(tbl_ref, bkt_ref, bias_scr, CHUNK)

    @pl.when(cg == 0)
    def _():
        for ref, pad in ((k_ref, kpad), (v_ref, vpad)):
            pad[0:ATT_LEAD, :] = jnp.zeros((ATT_LEAD, KV_DIM), BF16)
            pad[ATT_LEAD:ATT_LEAD + seq_len, :] = ref[...].astype(BF16)
            pad[ATT_LEAD + seq_len:, :] = jnp.zeros((ATT_TAIL, KV_DIM), BF16)

    kidx = lax.broadcasted_iota(jnp.int32, (1, KEY_TILE), 1)
    sinks = _sink_columns(sink_ref, CHUNK)

    def chunk(c, carry):
        q0 = pl.multiple_of(c * CHUNK, CHUNK)
        start = pl.multiple_of((cg * ATT_CPB + c) * CHUNK, CHUNK)
        key_ok = start + kidx >= ATT_LEAD

        def head_cols(h):
            return slice(h * HEAD_DIM, (h + 1) * HEAD_DIM)

        def store(h, val):
            o_ref[pl.ds(q0, CHUNK), head_cols(h)] = val

        _attend(lambda h: q_ref[pl.ds(q0, CHUNK), head_cols(h)].astype(BF16),
                lambda hk: kpad[pl.ds(start, KEY_TILE), head_cols(hk)],
                lambda hk: vpad[pl.ds(start, KEY_TILE), head_cols(hk)],
                bias_scr, sinks, key_ok, store, CHUNK)
        return carry

    lax.fori_loop(0, ATT_CPB, chunk, 0)


def _attn_band(z, table, sinks, bkt, *, n_batch, seq_len):
    rows = ATT_CPB * CHUNK
    ng = seq_len // rows
    kern = functools.partial(_attn_band_kernel, seq_len=seq_len)
    smem = pl.BlockSpec(memory_space=pltpu.SMEM)
    pad_rows = ATT_LEAD + seq_len + ATT_TAIL
    return pl.pallas_call(
        kern,
        grid=(n_batch, ng),
        in_specs=[smem, smem,
                  pl.BlockSpec((rows, Q_DIM), lambda b, c: (b * ng + c, COL_Q // Q_DIM)),
                  pl.BlockSpec((seq_len, KV_DIM), lambda b, c: (b, COL_K // KV_DIM)),
                  pl.BlockSpec((seq_len, KV_DIM), lambda b, c: (b, COL_V // KV_DIM)),
                  pl.BlockSpec((CHUNK, KEY_TILE), lambda b, c: (0, 0))],
        out_specs=pl.BlockSpec((rows, Q_DIM), lambda b, c: (b * ng + c, 0)),
        out_shape=jax.ShapeDtypeStruct((n_batch * seq_len, Q_DIM), BF16),
        scratch_shapes=[pltpu.VMEM((pad_rows, KV_DIM), BF16),
                        pltpu.VMEM((pad_rows, KV_DIM), BF16),
                        pltpu.VMEM((N_KV_HEADS, GQA_GROUP * CHUNK, KEY_TILE), F32)],
        compiler_params=_cparams(("arbitrary", "arbitrary")),
        name="attn_band",
    )(table, sinks, z, z, z, bkt)


def _attn_step_kernel(tbl_ref, sink_ref, q_ref, k_ref, v_ref, bkt_ref, o_ref, kbuf, vbuf, bias_scr,
                      *, n_keys, n_q):
    b = pl.program_id(0)

    @pl.when(b == 0)
    def _():
        _build_bias(tbl_ref, bkt_ref, bias_scr, n_q)
        kbuf[n_keys:, :] = jnp.zeros((KEY_TILE - n_keys, KV_DIM), BF16)
        vbuf[n_keys:, :] = jnp.zeros((KEY_TILE - n_keys, KV_DIM), BF16)

    kbuf[0:n_keys, :] = k_ref[0].astype(BF16)
    vbuf[0:n_keys, :] = v_ref[0].astype(BF16)

    def head_cols(h):
        return slice(h * HEAD_DIM, (h + 1) * HEAD_DIM)

    def store(h, val):
        o_ref[:, head_cols(h)] = val

    _attend(lambda h: q_ref[:, head_cols(h)].astype(BF16),
            lambda hk: kbuf[:, head_cols(hk)], lambda hk: vbuf[:, head_cols(hk)],
            bias_scr, _sink_columns(sink_ref, n_q), None, store, n_q)


def _attn_step(z, k_all, v_all, table, sinks, bkt, *, n_batch, n_q, row_block0):
    n_keys = k_all.shape[1]
    kern = functools.partial(_attn_step_kernel, n_keys=n_keys, n_q=n_q)
    smem = pl.BlockSpec(memory_space=pltpu.SMEM)
    return pl.pallas_call(
        kern,
        grid=(n_batch,),
        in_specs=[smem, smem,
                  pl.BlockSpec((n_q, Q_DIM), lambda b: (row_block0 + b, COL_Q // Q_DIM)),
                  pl.BlockSpec((1, n_keys, KV_DIM), lambda b: (b, 0, 0)),
                  pl.BlockSpec((1, n_keys, KV_DIM), lambda b: (b, 0, 0)),
                  pl.BlockSpec((n_q, KEY_TILE), lambda b: (0, 0))],
        out_specs=pl.BlockSpec((n_q, Q_DIM), lambda b: (b, 0)),
        out_shape=jax.ShapeDtypeStruct((n_batch * n_q, Q_DIM), BF16),
        scratch_shapes=[pltpu.VMEM((KEY_TILE, KV_DIM), BF16),
                        pltpu.VMEM((KEY_TILE, KV_DIM), BF16),
                        pltpu.VMEM((N_KV_HEADS, GQA_GROUP * n_q, KEY_TILE), F32)],
        compiler_params=_cparams(("arbitrary",)),
        name="attn_step",
    )(table, sinks, z, k_all, v_all, bkt)


def _t5_bucket(rel):
    nb = N_BUCKETS // 2
    ret = jnp.where(rel > 0, nb, 0)
    n = jnp.abs(rel)
    max_exact = nb // 2
    nf = jnp.maximum(n, 1).astype(jnp.float32)
    large = max_exact + (jnp.log(nf / max_exact) / math.log(MAX_DISTANCE / max_exact)
                         * (nb - max_exact)).astype(jnp.int32)
    large = jnp.minimum(large, nb - 1)
    return ret + jnp.where(n < max_exact, n, large)


def _bucket_map(q_pos, k_pos):
    bkt = _t5_bucket(k_pos[None, :] - q_pos[:, None]).astype(jnp.int32)
    return jnp.pad(bkt, ((0, 0), (0, KEY_TILE - k_pos.shape[0])), constant_values=-1)


ROUTE_LANES = LANES
GATE_TN = 1024
OUT_VMEM_LIMIT = 60 * 1024 * 1024


def _outproj_kernel(hgp_ref, hgs_ref, op_ref, os_ref, ga0_ref, ga1_ref, gb0_ref, gb1_ref, xp_ref, xs_ref,
                    g1_ref, sh2_ref, sc2_ref, ng_ref, wr_ref, wa_ref, wo_ref, wrt_ref, brt_ref,
                    x1_ref, n2_ref, rt_ref, *, n_prompt_tiles):
    tm = x1_ref.shape[0]
    is_prompt = pl.program_id(0) < n_prompt_tiles
    hg = jnp.where(is_prompt, hgp_ref[...], hgs_ref[...])
    o = jnp.where(is_prompt, op_ref[...], os_ref[...])
    x = jnp.where(is_prompt, xp_ref[...], xs_ref[...])
    ya = jnp.dot(hg, wr_ref[...], preferred_element_type=F32)
    yb = jnp.dot(o, wa_ref[...], preferred_element_type=F32)
    halves = []
    for k, (ga_ref, gb_ref) in enumerate(((ga0_ref, gb0_ref), (ga1_ref, gb1_ref))):
        cols = slice(k * GATE_TN, (k + 1) * GATE_TN)
        halves.append((_sigmoid(ga_ref[...]) * ya[:, cols] + _sigmoid(gb_ref[...]) * yb[:, cols]).astype(BF16))
    merged = jnp.concatenate(halves, axis=1)
    mix = jnp.dot(merged, wo_ref[...], preferred_element_type=F32)
    x1 = x + _bcast_rows(g1_ref[...], tm) * mix
    x1_ref[...] = x1
    n2 = _rms_modulate(x1, ng_ref[...], sc2_ref[...], sh2_ref[...])
    n2_ref[...] = n2

    lg = jnp.dot(n2.astype(BF16), wrt_ref[...].astype(BF16), preferred_element_type=F32) + brt_ref[...]

    lane = lax.broadcasted_iota(jnp.int32, (tm, ROUTE_LANES), 1)
    lane_f = lane.astype(F32)
    e_lane = lane - N_GROUPS
    lane_group = (e_lane >> 3).astype(F32)

    def first_argmax(vals, vmax):
        return jnp.min(jnp.where(vals == vmax, lane_f, float(ROUTE_LANES)), axis=-1, keepdims=True)

    gl = jnp.where(lane < N_GROUPS, lg, NEG_INF)
    gmax = jnp.max(gl, axis=-1, keepdims=True)
    g_idx = first_argmax(gl, gmax)
    g_w = 1.0 / jnp.sum(jnp.exp(gl - gmax), axis=-1, keepdims=True)
    in_group = jnp.where((e_lane >= 0) & (e_lane < N_EXPERTS), lane_group, -1.0) == g_idx
    el = jnp.where(in_group, lg, NEG_INF)
    v1 = jnp.max(el, axis=-1, keepdims=True)
    i1 = first_argmax(el, v1)
    el2 = jnp.where(lane_f == i1, NEG_INF, el)
    v2 = jnp.max(el2, axis=-1, keepdims=True)
    i2 = first_argmax(el2, v2)
    e21 = jnp.exp(v2 - v1)
    w1 = g_w / (1.0 + e21)
    w2 = g_w * e21 / (1.0 + e21)
    e1 = i1 - float(N_GROUPS)
    e2 = i2 - float(N_GROUPS)
    rt_ref[...] = jnp.where(lane == 0, e1, jnp.where(lane == 1, e2, jnp.where(lane == 2, w1,
                            jnp.where(lane == 3, w2, 0.0))))


def _outproj(hg_p, hg_s, o_p, o_s, z, x_p, x_s, g1, sh2, sc2, norm_g, wr, wa, wo, w_route, b_route):
    tm = ROW_TILE
    t = z.shape[0]
    seg = tm // MOD_ROWS
    n_p = x_p.shape[0] // tm
    row = lambda i: (i, 0)
    fix = lambda i: (0, 0)
    row_p = lambda i: (jnp.minimum(i, n_p - 1), 0)
    row_s = lambda i: (jnp.maximum(i - n_p, 0), 0)
    once = pl.Buffered(1)

    def gate(col):
        return pl.BlockSpec((tm, GATE_TN), lambda i: (i, col // GATE_TN))

    return pl.pallas_call(
        functools.partial(_outproj_kernel, n_prompt_tiles=n_p),
        grid=(t // tm,),
        in_specs=[pl.BlockSpec((tm, D_RNN), row_p), pl.BlockSpec((tm, D_RNN), row_s, pipeline_mode=once),
                  pl.BlockSpec((tm, Q_DIM), row_p), pl.BlockSpec((tm, Q_DIM), row_s, pipeline_mode=once),
                  gate(COL_GA), gate(COL_GA + GATE_TN), gate(COL_GB), gate(COL_GB + GATE_TN),
                  pl.BlockSpec((tm, D_MODEL), row_p), pl.BlockSpec((tm, D_MODEL), row_s, pipeline_mode=once),
                  pl.BlockSpec((seg, D_MODEL), row), pl.BlockSpec((seg, D_MODEL), row),
                  pl.BlockSpec((seg, D_MODEL), row),
                  pl.BlockSpec((1, D_MODEL), fix),
                  pl.BlockSpec((D_RNN, D_MODEL), fix, pipeline_mode=once),
                  pl.BlockSpec((Q_DIM, D_MODEL), fix, pipeline_mode=once),
                  pl.BlockSpec((D_MODEL, D_MODEL), fix, pipeline_mode=once),
                  pl.BlockSpec((D_MODEL, ROUTE_LANES), fix),
                  pl.BlockSpec((1, ROUTE_LANES), fix)],
        out_specs=[pl.BlockSpec((tm, D_MODEL), row), pl.BlockSpec((tm, D_MODEL), row),
                   pl.BlockSpec((tm, ROUTE_LANES), row)],
        out_shape=[jax.ShapeDtypeStruct((t, D_MODEL), F32), jax.ShapeDtypeStruct((t, D_MODEL), F32),
                   jax.ShapeDtypeStruct((t, ROUTE_LANES), F32)],
        compiler_params=_cparams(("arbitrary",), OUT_VMEM_LIMIT),
        name="outproj",
    )(hg_p, hg_s, o_p, o_s, z, z, z, z, x_p, x_s, g1, sh2, sc2, norm_g.reshape(1, D_MODEL),
      wr, wa, wo, w_route, b_route)


def _row_gather(src_hbm, idx_ref, base, dst, sem, n_rows):
    for r in range(n_rows):
        tok = idx_ref[base + r]
        pltpu.make_async_copy(src_hbm.at[pl.ds(tok, 1), :], dst.at[pl.ds(r, 1), :], sem).start()


def _row_gather_wait(src_hbm, dst, sem, n_rows):
    pltpu.make_async_copy(src_hbm.at[pl.ds(0, n_rows), :], dst, sem).wait()


def _moe_kernel(te_ref, tfirst_ref, tvalid_ref, tnext_ref, src_ref, n2_hbm, wg_hbm, wu_hbm, wd_hbm, y_ref,
                xbuf, xsem, wg_st, wu_st, wd_st, wsem, wg_bf, wu_bf, wd_bf):
    i = pl.program_id(0)
    n_tiles = pl.num_programs(0)
    slot = i % 2
    stages = ((wg_hbm, wg_st, wg_bf), (wu_hbm, wu_st, wu_bf), (wd_hbm, wd_st, wd_bf))

    def weight_copy(k, e):
        hbm, st, _ = stages[k]
        return pltpu.make_async_copy(hbm.at[e], st, wsem.at[k])

    @pl.when(i == 0)
    def _():
        for k in range(len(stages)):
            weight_copy(k, te_ref[0]).start(priority=1)
        _row_gather(n2_hbm, src_ref, 0, xbuf.at[0], xsem.at[0], MOE_TM)

    @pl.when(tvalid_ref[i] == 1)
    def _():
        nxt = jnp.minimum(i + 1, n_tiles - 1)

        @pl.when((i + 1 < n_tiles) & (tvalid_ref[nxt] == 1))
        def _():
            _row_gather(n2_hbm, src_ref, (i + 1) * MOE_TM, xbuf.at[1 - slot], xsem.at[1 - slot], MOE_TM)

        @pl.when(tfirst_ref[i] == 1)
        def _():
            for k, (_, st, bf) in enumerate(stages):
                weight_copy(k, te_ref[i]).wait()
                bf[...] = st[...].astype(BF16)

            @pl.when(tnext_ref[i] >= 0)
            def _():
                for k in range(len(stages)):
                    weight_copy(k, tnext_ref[i]).start(priority=1)

        _row_gather_wait(n2_hbm, xbuf.at[slot], xsem.at[slot], MOE_TM)
        x = xbuf[slot].astype(BF16)
        hgate = jnp.dot(x, wg_bf[...], preferred_element_type=F32)
        hup = jnp.dot(x, wu_bf[...], preferred_element_type=F32)
        act = (hgate * _sigmoid(hgate) * hup).astype(BF16)
        y_ref[...] = jnp.dot(act, wd_bf[...], preferred_element_type=F32)

    @pl.when(tvalid_ref[i] == 0)
    def _():
        y_ref[...] = jnp.zeros(y_ref.shape, F32)


def _moe(n2, w_gate, w_up, w_down, tile_expert, tile_first, tile_valid, tile_next, src_tok):
    n_tiles = tile_expert.shape[0]
    hbm = pl.BlockSpec(memory_space=pl.ANY)
    grid_spec = pltpu.PrefetchScalarGridSpec(
        num_scalar_prefetch=5,
        grid=(n_tiles,),
        in_specs=[hbm, hbm, hbm, hbm],
        out_specs=pl.BlockSpec((MOE_TM, D_MODEL), lambda i, te, tf, tv, tn, st: (i, 0)),
        scratch_shapes=[pltpu.VMEM((2, MOE_TM, D_MODEL), F32),
                        pltpu.SemaphoreType.DMA((2,)),
                        pltpu.VMEM((D_MODEL, D_EXPERT), F32),
                        pltpu.VMEM((D_MODEL, D_EXPERT), F32),
                        pltpu.VMEM((D_EXPERT, D_MODEL), F32),
                        pltpu.SemaphoreType.DMA((3,)),
                        pltpu.VMEM((D_MODEL, D_EXPERT), BF16),
                        pltpu.VMEM((D_MODEL, D_EXPERT), BF16),
                        pltpu.VMEM((D_EXPERT, D_MODEL), BF16)],
    )
    return pl.pallas_call(
        _moe_kernel,
        grid_spec=grid_spec,
        out_shape=jax.ShapeDtypeStruct((n_tiles * MOE_TM, D_MODEL), F32),
        compiler_params=_cparams(("arbitrary",)),
        name="moe",
    )(tile_expert, tile_first, tile_valid, tile_next, src_tok, n2, w_gate, w_up, w_down)


def _route_plan(e1, e2, n_tok):
    experts = jnp.arange(N_EXPERTS, dtype=jnp.int32)
    flat_e = jnp.concatenate([e1, e2])
    onehot = (flat_e[:, None] == experts[None, :]).astype(jnp.int32)
    csum = jnp.cumsum(onehot, axis=0)
    rank = jnp.sum(csum * onehot, axis=1) - 1
    counts = csum[-1]
    tiles_per = (counts + MOE_TM - 1) // MOE_TM
    tile_end = jnp.cumsum(tiles_per)
    tile_off = tile_end - tiles_per
    slot = jnp.sum(onehot * tile_off[None, :], axis=1) * MOE_TM + rank
    n_tiles = (2 * n_tok) // MOE_TM + N_EXPERTS
    tok = jnp.tile(jnp.arange(n_tok, dtype=jnp.int32), 2)
    src_tok = jnp.zeros((n_tiles * MOE_TM,), jnp.int32).at[slot].set(tok)
    tile_id = jnp.arange(n_tiles, dtype=jnp.int32)
    n_used = tile_end[-1]
    tile_valid = (tile_id < n_used).astype(jnp.int32)
    te = jnp.sum((tile_end[None, :] <= jnp.minimum(tile_id, n_used - 1)[:, None]).astype(jnp.int32), axis=1)
    tile_expert = jnp.minimum(te, N_EXPERTS - 1)
    prev = jnp.concatenate([jnp.full((1,), -1, jnp.int32), tile_expert[:-1]])
    tile_first = (tile_expert != prev).astype(jnp.int32)
    next_tile = tile_end[tile_expert]
    tile_next = jnp.where(next_tile < n_used, tile_expert[jnp.minimum(next_tile, n_tiles - 1)], -1)
    return tile_expert, tile_first, tile_valid, tile_next, src_tok, slot[:n_tok], slot[n_tok:]


def _final_kernel(p1_ref, p2_ref, ys_hbm, x1_ref, g2_ref, rt_ref, fg_ref, yp_ref, ysm_ref, ybuf, sem,
                  *, n_prompt_tiles):
    i = pl.program_id(0)
    n_tiles = pl.num_programs(0)
    slot = i % 2

    def start(tile, s):
        _row_gather(ys_hbm, p1_ref, tile * ROW_TILE, ybuf.at[s, 0], sem.at[s], ROW_TILE)
        _row_gather(ys_hbm, p2_ref, tile * ROW_TILE, ybuf.at[s, 1], sem.at[s], ROW_TILE)

    @pl.when(i == 0)
    def _():
        start(0, 0)

    @pl.when(i + 1 < n_tiles)
    def _():
        start(i + 1, 1 - slot)

    _row_gather_wait(ys_hbm, ybuf.at[slot, 0], sem.at[slot], ROW_TILE)
    _row_gather_wait(ys_hbm, ybuf.at[slot, 1], sem.at[slot], ROW_TILE)
    rt = rt_ref[...]
    moe = rt[:, 2:3] * ybuf[slot, 0] + rt[:, 3:4] * ybuf[slot, 1]
    x2 = x1_ref[...] + _bcast_rows(g2_ref[...], ROW_TILE) * moe
    y = x2 * lax.rsqrt(jnp.mean(x2 * x2, axis=-1, keepdims=True) + EPS) * fg_ref[...]

    @pl.when(i < n_prompt_tiles)
    def _():
        yp_ref[...] = y

    @pl.when(i >= n_prompt_tiles)
    def _():
        ysm_ref[...] = y


def _final(ys, x1, g2, rt, final_g, p1, p2, n_prompt_rows):
    t = x1.shape[0]
    tm = ROW_TILE
    seg = tm // MOD_ROWS
    n_p = n_prompt_rows // tm
    grid_spec = pltpu.PrefetchScalarGridSpec(
        num_scalar_prefetch=2,
        grid=(t // tm,),
        in_specs=[pl.BlockSpec(memory_space=pl.ANY),
                  pl.BlockSpec((tm, D_MODEL), lambda i, a, b: (i, 0)),
                  pl.BlockSpec((seg, D_MODEL), lambda i, a, b: (i, 0)),
                  pl.BlockSpec((tm, ROUTE_LANES), lambda i, a, b: (i, 0)),
                  pl.BlockSpec((1, D_MODEL), lambda i, a, b: (0, 0))],
        out_specs=[pl.BlockSpec((tm, D_MODEL), lambda i, a, b: (jnp.minimum(i, n_p - 1), 0)),
                   pl.BlockSpec((tm, D_MODEL), lambda i, a, b: (jnp.maximum(i - n_p, 0), 0))],
        scratch_shapes=[pltpu.VMEM((2, 2, tm, D_MODEL), F32),
                        pltpu.SemaphoreType.DMA((2,))],
    )
    return pl.pallas_call(
        functools.partial(_final_kernel, n_prompt_tiles=n_p),
        grid_spec=grid_spec,
        out_shape=[jax.ShapeDtypeStruct((n_prompt_rows, D_MODEL), F32),
                   jax.ShapeDtypeStruct((t - n_prompt_rows, D_MODEL), F32)],
        compiler_params=_cparams(("arbitrary",)),
        name="final",
    )(p1, p2, ys, x1, g2, rt, final_g.reshape(1, D_MODEL))


def kernel(x_prompt, x_sample, cache_k_win, cache_v_win, state_conv, state_rglru, c_prompt, c_sample, w_ada, b_ada, norm1_g, norm2_g, w_in, b_in, w_conv, b_conv, w_rg_a, b_rg_a, w_rg_x, b_rg_x, lru_lambda, w_rnn_out, w_attn_out, w_out, attn_sinks, w_route_group, b_route_group, w_route_expert, b_route_expert, w_exp_gate, w_exp_up, w_exp_down, rel_bias_table, final_norm_g):
    n_b, seq, _ = x_prompt.shape
    d_b, d_seq, _ = x_sample.shape
    assert w_ada.shape[0] == 1, "single trunk layer"
    assert seq % (ATT_CPB * CHUNK) == 0 and seq % MOD_ROWS == 0 and d_seq == MOD_ROWS
    assert d_seq >= CONV_W - 1 and d_b == N_CHAINS
    t_p, t_s = n_b * seq, d_b * d_seq
    t = t_p + t_s
    assert t_s == ROW_TILE and t_p % ROW_TILE == 0
    cw = cache_k_win.shape[2]
    l = 0

    x_p = x_prompt.reshape(t_p, D_MODEL)
    x_s = x_sample.reshape(t_s, D_MODEL)

    n_c = n_b + d_b
    c_all = jnp.pad(jnp.concatenate([c_prompt, c_sample], axis=0), ((0, -n_c % SUBLANES), (0, 0)))
    mod = _ada(c_all, w_ada[l], b_ada[l])

    def per_segment(m):
        return jnp.concatenate([jnp.repeat(m[:n_b], seq // MOD_ROWS, axis=0),
                                jnp.repeat(m[n_b:n_c], d_seq // MOD_ROWS, axis=0)], axis=0)

    sh1, sc1, g1, sh2, sc2, g2 = [per_segment(m) for m in jnp.split(mod, 6, axis=-1)]

    z = _inproj(_norm1(x_p, x_s, sc1, sh1, norm1_g[l]), w_in[l], b_in[l])

    w_gate = jnp.concatenate([w_rg_a[l], w_rg_x[l]], axis=-1)
    b_gate = jnp.concatenate([b_rg_a[l], b_rg_x[l]], axis=-1)[:, None, :]
    hg_p, hl_p = _rglru(z, jnp.zeros((n_b, SUBLANES, D_RNN), F32), jnp.zeros((n_b, N_CHAINS, D_RNN), F32),
                        w_conv[l], b_conv[l], w_gate, b_gate, lru_lambda[l],
                        n_blocks=n_b, n_seq=1, seq_len=seq, row_block0=0, link=True)
    state8 = jnp.pad(state_conv[l], ((0, 0), (SUBLANES - (CONV_W - 1), 0), (0, 0)))
    hg_s, hl_s = _rglru(z, state8, state_rglru[l][None], w_conv[l], b_conv[l], w_gate, b_gate, lru_lambda[l],
                        n_blocks=1, n_seq=d_b, seq_len=d_seq, row_block0=t_p // t_s, link=False)

    sinks = attn_sinks[l]
    bkt_p = _bucket_map(WIN_CHUNKS * CHUNK + jnp.arange(CHUNK), jnp.arange(SPAN))
    o_p = _attn_band(z, rel_bias_table, sinks, bkt_p, n_batch=n_b, seq_len=seq)
    k_new = z[t_p:, COL_K:COL_K + KV_DIM].reshape(d_b, d_seq, KV_DIM)
    v_new = z[t_p:, COL_V:COL_V + KV_DIM].reshape(d_b, d_seq, KV_DIM)
    k_all = jnp.concatenate([cache_k_win[l].reshape(d_b, cw, KV_DIM), k_new], axis=1)
    v_all = jnp.concatenate([cache_v_win[l].reshape(d_b, cw, KV_DIM), v_new], axis=1)
    bkt_s = _bucket_map(cw + jnp.arange(d_seq), jnp.arange(cw + d_seq))
    o_s = _attn_step(z, k_all, v_all, rel_bias_table, sinks, bkt_s, n_batch=d_b, n_q=d_seq, row_block0=t_p // d_seq)

    n_route = N_GROUPS + N_EXPERTS
    w_route = jnp.pad(jnp.concatenate([w_route_group[l], w_route_expert[l]], axis=1),
                      ((0, 0), (0, ROUTE_LANES - n_route)))
    b_route = jnp.pad(jnp.concatenate([b_route_group[l], b_route_expert[l]]),
                      (0, ROUTE_LANES - n_route)).reshape(1, ROUTE_LANES)
    x1, n2, rt = _outproj(hg_p, hg_s, o_p, o_s, z, x_p, x_s, g1, sh2, sc2, norm2_g[l],
                          w_rnn_out[l].astype(BF16), w_attn_out[l].astype(BF16), w_out[l].astype(BF16),
                          w_route, b_route)

    e1 = rt[:, 0].astype(jnp.int32)
    e2 = rt[:, 1].astype(jnp.int32)
    tile_expert, tile_first, tile_valid, tile_next, src_tok, p1, p2 = _route_plan(e1, e2, t)
    ys = _moe(n2, w_exp_gate[l], w_exp_up[l], w_exp_down[l], tile_expert, tile_first, tile_valid, tile_next,
              src_tok)
    y_p, y_s = _final(ys, x1, g2, rt, final_norm_g, p1, p2, t_p)

    win = min(WINDOW, seq)

    def tail(col0, width, n_rows):
        return jnp.stack([z[(b + 1) * seq - n_rows:(b + 1) * seq, col0:col0 + width] for b in range(n_b)])

    kp = tail(COL_K, KV_DIM, win).reshape(n_b, win, N_KV_HEADS, HEAD_DIM)
    vp = tail(COL_V, KV_DIM, win).reshape(n_b, win, N_KV_HEADS, HEAD_DIM)
    cp = tail(COL_XR, D_RNN, CONV_W - 1)
    rp = hl_p[:, N_CHAINS - 1, :]
    ks = k_all[:, -cw:].reshape(d_b, cw, N_KV_HEADS, HEAD_DIM)
    vs = v_all[:, -cw:].reshape(d_b, cw, N_KV_HEADS, HEAD_DIM)
    cs = z[t_p:, COL_XR:COL_XR + D_RNN].reshape(d_b, d_seq, D_RNN)[:, -(CONV_W - 1):]
    rs = hl_s[0]
    return (y_p.reshape(n_b, seq, D_MODEL), y_s.reshape(d_b, d_seq, D_MODEL),
            kp[None], vp[None], cp[None], rp[None], ks[None], vs[None], cs[None], rs[None])
```

```python
import functools
import math

import jax
import jax.numpy as jnp
from jax import lax
from jax.experimental import pallas as pl
from jax.experimental.pallas import tpu as pltpu

F32 = jnp.float32
BF16 = jnp.bfloat16

D_MODEL = 2048
D_RNN = 2048
RNN_BLOCK = 128
CONV_W = 4
LRU_C = 8.0
N_HEADS = 16
N_KV_HEADS = 4
HEAD_DIM = 128
GQA_GROUP = N_HEADS // N_KV_HEADS
Q_DIM = N_HEADS * HEAD_DIM
KV_DIM = N_KV_HEADS * HEAD_DIM
CHUNK = 64
WINDOW = 128
WIN_CHUNKS = WINDOW // CHUNK
SPAN = (WIN_CHUNKS + 1) * CHUNK
N_BUCKETS = 32
MAX_DISTANCE = 128
N_GROUPS = 4
E_PER_GROUP = 8
N_EXPERTS = N_GROUPS * E_PER_GROUP
D_EXPERT = 512
EPS = 1e-6
NEG_INF = -1e30
D_IN = 2 * D_RNN + Q_DIM + 2 * KV_DIM + 2 * D_MODEL
COL_XR, COL_GR, COL_Q = 0, D_RNN, 2 * D_RNN
COL_K = COL_Q + Q_DIM
COL_V = COL_K + KV_DIM
COL_GA = COL_V + KV_DIM
COL_GB = COL_GA + D_MODEL

LANES = 128
SUBLANES = 8
MOD_ROWS = 32
KEY_TILE = 256
VMEM_LIMIT = 56 * 1024 * 1024

ROW_TILE = 256
IN_TM_MAX = 1408
IN_TN = 1024
MOE_TM = 128
MOE_RING = 3


def _sigmoid(x):
    return 0.5 * jnp.tanh(0.5 * x) + 0.5


def _gelu_tanh(x):
    return 0.5 * x * (1.0 + jnp.tanh(math.sqrt(2.0 / math.pi) * (x + 0.044715 * (x * x * x))))


def _bcast_rows(v, rows):
    n, d = v.shape
    return jnp.broadcast_to(v[:, None, :], (n, MOD_ROWS, d)).reshape(rows, d)


def _rms_modulate(x, gain, scale_seg, shift_seg):
    rows = x.shape[0]
    y = x * lax.rsqrt(jnp.mean(x * x, axis=-1, keepdims=True) + EPS) * gain
    return y * (1.0 + _bcast_rows(scale_seg, rows)) + _bcast_rows(shift_seg, rows)


def _cparams(sem, vmem_limit=VMEM_LIMIT):
    return pltpu.CompilerParams(dimension_semantics=sem, vmem_limit_bytes=vmem_limit)


def _ada_kernel(c_ref, w_ref, b_ref, o_ref):
    c = c_ref[...]
    s = (c * _sigmoid(c)).astype(BF16)
    o_ref[...] = jnp.dot(s, w_ref[...].astype(BF16), preferred_element_type=F32) + b_ref[...]


def _ada(c_all, w_ada, b_ada):
    rows = c_all.shape[0]
    n = w_ada.shape[1]
    tn = 1024
    return pl.pallas_call(
        _ada_kernel,
        grid=(n // tn,),
        in_specs=[pl.BlockSpec((rows, D_MODEL), lambda j: (0, 0)),
                  pl.BlockSpec((D_MODEL, tn), lambda j: (0, j)),
                  pl.BlockSpec((1, tn), lambda j: (0, j))],
        out_specs=pl.BlockSpec((rows, tn), lambda j: (0, j)),
        out_shape=jax.ShapeDtypeStruct((rows, n), F32),
        compiler_params=_cparams(("arbitrary",)),
        name="ada",
    )(c_all, w_ada, b_ada.reshape(1, n))


def _norm1_kernel(xp_ref, xs_ref, sc_ref, sh_ref, g_ref, n1_ref, *, n_prompt_tiles):
    x = jnp.where(pl.program_id(0) < n_prompt_tiles, xp_ref[...], xs_ref[...])
    n1_ref[...] = _rms_modulate(x, g_ref[...], sc_ref[...], sh_ref[...]).astype(BF16)


def _norm1(x_p, x_s, sc1, sh1, norm_g):
    tm = ROW_TILE
    n_p = x_p.shape[0] // tm
    t = x_p.shape[0] + x_s.shape[0]
    seg = tm // MOD_ROWS
    return pl.pallas_call(
        functools.partial(_norm1_kernel, n_prompt_tiles=n_p),
        grid=(t // tm,),
        in_specs=[pl.BlockSpec((tm, D_MODEL), lambda i: (jnp.minimum(i, n_p - 1), 0)),
                  pl.BlockSpec((tm, D_MODEL), lambda i: (jnp.maximum(i - n_p, 0), 0)),
                  pl.BlockSpec((seg, D_MODEL), lambda i: (i, 0)),
                  pl.BlockSpec((seg, D_MODEL), lambda i: (i, 0)),
                  pl.BlockSpec((1, D_MODEL), lambda i: (0, 0))],
        out_specs=pl.BlockSpec((tm, D_MODEL), lambda i: (i, 0)),
        out_shape=jax.ShapeDtypeStruct((t, D_MODEL), BF16),
        compiler_params=_cparams(("arbitrary",)),
        name="norm1",
    )(x_p, x_s, sc1, sh1, norm_g.reshape(1, D_MODEL))


def _inproj_kernel(n1_ref, w_ref, b_ref, z_ref, w_bf):
    @pl.when(pl.program_id(1) == 0)
    def _():
        w_bf[...] = w_ref[...].astype(BF16)

    z_ref[...] = jnp.dot(n1_ref[...], w_bf[...], preferred_element_type=F32) + b_ref[...]


def _inproj(n1, w_in, b_in):
    t = n1.shape[0]
    tm = next(m for m in range(IN_TM_MAX, 0, -LANES) if t % m == 0)
    return pl.pallas_call(
        _inproj_kernel,
        grid=(D_IN // IN_TN, t // tm),
        in_specs=[pl.BlockSpec((tm, D_MODEL), lambda j, i: (i, 0)),
                  pl.BlockSpec((D_MODEL, IN_TN), lambda j, i: (0, j)),
                  pl.BlockSpec((1, IN_TN), lambda j, i: (0, j))],
        out_specs=pl.BlockSpec((tm, IN_TN), lambda j, i: (i, j)),
        out_shape=jax.ShapeDtypeStruct((t, D_IN), F32),
        scratch_shapes=[pltpu.VMEM((D_MODEL, IN_TN), BF16)],
        compiler_params=_cparams(("arbitrary", "arbitrary")),
        name="inproj",
    )(n1, w_in, b_in.reshape(1, D_IN))


RG_CW = 256
RG_SLABS = RG_CW // LANES
N_CHAINS = SUBLANES
CHAIN_PAD = 8


def _rglru_kernel(xr_ref, gr_ref, st_ref, h0_ref, wc_ref, bc_ref, wg_ref, bg_ref, lam_ref,
                  hg_ref, hl_ref, ext_scr, a_scr, u_scr, *, n_seq, seq_len, link):
    rows = n_seq * seq_len
    cl = rows // N_CHAINS
    pitch = cl + CHAIN_PAD

    for s in range(n_seq):
        ext_scr[s, 0:SUBLANES, :] = st_ref[s]
        ext_scr[s, SUBLANES:SUBLANES + seq_len, :] = xr_ref[s * seq_len:(s + 1) * seq_len, :]

    z = -lam_ref[...]
    softplus = jnp.maximum(z, 0.0) + jnp.log1p(jnp.exp(-jnp.abs(z)))
    nq = (-0.25 * LRU_C) * softplus

    for c in range(N_CHAINS):
        s, r0 = divmod(c * cl, seq_len)
        xc = bc_ref[...]
        for j in range(CONV_W):
            xc = xc + wc_ref[j:j + 1, :] * ext_scr[s, pl.ds(SUBLANES - (CONV_W - 1) + j + r0, cl), :]
        for sl in range(RG_SLABS):
            lanes = slice(sl * LANES, (sl + 1) * LANES)
            xb = xc[:, lanes]
            g = jnp.dot(xb.astype(BF16), wg_ref[sl].astype(BF16), preferred_element_type=F32) + bg_ref[sl]
            tr = jnp.tanh(0.5 * g[:, :LANES])
            ti = jnp.tanh(0.5 * g[:, LANES:])
            th = jnp.tanh(nq[:, lanes] * (tr + 1.0))
            rcp = 1.0 / (1.0 - th)
            a_scr[sl, c * pitch:c * pitch + cl, :] = (1.0 + th) * rcp
            u_scr[sl, c * pitch:c * pitch + cl, :] = (jnp.sqrt(-th) * rcp) * ((ti + 1.0) * xb)

    def step(t, carry):
        hs, ps = carry
        new_h, new_p = [], []
        for sl in range(RG_SLABS):
            a = a_scr[sl, pl.ds(t, N_CHAINS, stride=pitch), :]
            u = u_scr[sl, pl.ds(t, N_CHAINS, stride=pitch), :]
            h = a * hs[sl] + u
            u_scr[sl, pl.ds(t, N_CHAINS, stride=pitch), :] = h
            new_h.append(h)
            if link:
                p = a * ps[sl]
                a_scr[sl, pl.ds(t, N_CHAINS, stride=pitch), :] = p
                new_p.append(p)
            else:
                new_p.append(ps[sl])
        return tuple(new_h), tuple(new_p)

    h_init = tuple(h0_ref[0, :, sl * LANES:(sl + 1) * LANES] for sl in range(RG_SLABS))
    p_init = tuple(jnp.ones((N_CHAINS, LANES), F32) for _ in range(RG_SLABS))
    h_end, p_end = lax.fori_loop(0, cl, step, (h_init, p_init), unroll=8)

    row = lax.broadcasted_iota(jnp.int32, (N_CHAINS, LANES), 0)
    for sl in range(RG_SLABS):
        lanes = slice(sl * LANES, (sl + 1) * LANES)
        if link:
            def shift_down(v):
                return jnp.where(row == 0, 0.0, pltpu.roll(v, 1, axis=0))
            hh = h_end[sl]
            for _ in range(N_CHAINS - 1):
                hh = h_end[sl] + p_end[sl] * shift_down(hh)
            carry_in = shift_down(hh)
        else:
            hh = h_end[sl]
        hl_ref[0, :, lanes] = hh
        for c in range(N_CHAINS):
            h = u_scr[sl, c * pitch:c * pitch + cl, :]
            if link:
                h = h + a_scr[sl, c * pitch:c * pitch + cl, :] * carry_in[c:c + 1, :]
            gg = _gelu_tanh(gr_ref[c * cl:(c + 1) * cl, lanes])
            hg_ref[c * cl:(c + 1) * cl, lanes] = (h * gg).astype(BF16)


def _rglru(z, state8, h0, w_conv, b_conv, w_gate, b_gate, lam, *, n_blocks, n_seq, seq_len, row_block0, link):
    rows = n_seq * seq_len
    cl = rows // N_CHAINS
    ncb = D_RNN // RG_CW
    gr0 = COL_GR // RG_CW
    kern = functools.partial(_rglru_kernel, n_seq=n_seq, seq_len=seq_len, link=link)
    return pl.pallas_call(
        kern,
        grid=(n_blocks, ncb),
        in_specs=[pl.BlockSpec((rows, RG_CW), lambda b, n: (row_block0 + b, n)),
                  pl.BlockSpec((rows, RG_CW), lambda b, n: (row_block0 + b, gr0 + n)),
                  pl.BlockSpec((n_seq, SUBLANES, RG_CW), lambda b, n: (b, 0, n)),
                  pl.BlockSpec((1, N_CHAINS, RG_CW), lambda b, n: (b, 0, n)),
                  pl.BlockSpec((CONV_W, RG_CW), lambda b, n: (0, n)),
                  pl.BlockSpec((1, RG_CW), lambda b, n: (0, n)),
                  pl.BlockSpec((RG_SLABS, RNN_BLOCK, 2 * RNN_BLOCK), lambda b, n: (n, 0, 0)),
                  pl.BlockSpec((RG_SLABS, 1, 2 * RNN_BLOCK), lambda b, n: (n, 0, 0)),
                  pl.BlockSpec((1, RG_CW), lambda b, n: (0, n))],
        out_specs=[pl.BlockSpec((rows, RG_CW), lambda b, n: (b, n)),
                   pl.BlockSpec((1, N_CHAINS, RG_CW), lambda b, n: (b, 0, n))],
        out_shape=[jax.ShapeDtypeStruct((n_blocks * rows, D_RNN), BF16),
                   jax.ShapeDtypeStruct((n_blocks, N_CHAINS, D_RNN), F32)],
        scratch_shapes=[pltpu.VMEM((n_seq, SUBLANES + seq_len, RG_CW), F32),
                        pltpu.VMEM((RG_SLABS, N_CHAINS * (cl + CHAIN_PAD), LANES), F32),
                        pltpu.VMEM((RG_SLABS, N_CHAINS * (cl + CHAIN_PAD), LANES), F32)],
        compiler_params=_cparams(("arbitrary", "arbitrary")),
        name="rglru_link" if link else "rglru_step",
    )(z, z, state8, h0, w_conv, b_conv.reshape(1, D_RNN), w_gate, b_gate, lam.reshape(1, D_RNN))


def _build_bias(tbl_ref, bkt_ref, bias_scr, n_q):
    bkt = bkt_ref[...]
    base = jnp.where(bkt < 0, NEG_INF, 0.0)
    for h in range(N_HEADS):
        acc = base
        for bk in range(N_BUCKETS):
            acc = jnp.where(bkt == bk, tbl_ref[bk, h], acc)
        hk, g = divmod(h, GQA_GROUP)
        bias_scr[hk, g * n_q:(g + 1) * n_q, :] = acc


def _sink_columns(sink_ref, n_q):
    row = lax.broadcasted_iota(jnp.int32, (GQA_GROUP * n_q, 1), 0)
    cols = []
    for hk in range(N_KV_HEADS):
        col = jnp.full((GQA_GROUP * n_q, 1), sink_ref[hk * GQA_GROUP], F32)
        for g in range(1, GQA_GROUP):
            col = jnp.where(row >= g * n_q, sink_ref[hk * GQA_GROUP + g], col)
        cols.append(col)
    return cols


def _attend(q_of, k_of, v_of, bias_scr, sinks, key_ok, store, n_q):
    scores = []
    for hk in range(N_KV_HEADS):
        qg = jnp.concatenate([q_of(hk * GQA_GROUP + g) for g in range(GQA_GROUP)], axis=0)
        s = lax.dot_general(qg, k_of(hk), (((1,), (1,)), ((), ())), preferred_element_type=F32)
        s = s * (HEAD_DIM ** -0.5) + bias_scr[hk]
        if key_ok is not None:
            s = jnp.where(key_ok, s, NEG_INF)
        scores.append(s)
    probs = []
    for hk in range(N_KV_HEADS):
        s, sink = scores[hk], sinks[hk]
        m = jnp.maximum(jnp.max(s, axis=-1, keepdims=True), sink)
        p = jnp.exp(s - m)
        inv = 1.0 / (jnp.sum(p, axis=-1, keepdims=True) + jnp.exp(sink - m))
        probs.append((p * inv).astype(BF16))
    for hk in range(N_KV_HEADS):
        o = jnp.dot(probs[hk], v_of(hk), preferred_element_type=F32)
        for g in range(GQA_GROUP):
            store(hk * GQA_GROUP + g, o[g * n_q:(g + 1) * n_q, :].astype(BF16))


ATT_CPB = 8
ATT_LEAD = WIN_CHUNKS * CHUNK
ATT_TAIL = KEY_TILE - SPAN


def _attn_band_kernel(tbl_ref, sink_ref, q_ref, k_ref, v_ref, bkt_ref, o_ref, kpad, vpad, bias_scr, *, seq_len):
    b = pl.program_id(0)
    cg = pl.program_id(1)

    @pl.when((b == 0) & (cg == 0))
    def _():
        _build_bias(tbl_ref, bkt_ref, bias_scr, CHUNK)

    @pl.when(cg == 0)
    def _():
        for ref, pad in ((k_ref, kpad), (v_ref, vpad)):
            pad[0:ATT_LEAD, :] = jnp.zeros((ATT_LEAD, KV_DIM), BF16)
            pad[ATT_LEAD:ATT_LEAD + seq_len, :] = ref[...].astype(BF16)
            pad[ATT_LEAD + seq_len:, :] = jnp.zeros((ATT_TAIL, KV_DIM), BF16)

    kidx = lax.broadcasted_iota(jnp.int32, (1, KEY_TILE), 1)
    sinks = _sink_columns(sink_ref, CHUNK)

    def chunk(c, carry):
        q0 = pl.multiple_of(c * CHUNK, CHUNK)
        start = pl.multiple_of((cg * ATT_CPB + c) * CHUNK, CHUNK)
        key_ok = start + kidx >= ATT_LEAD

        def head_cols(h):
            return slice(h * HEAD_DIM, (h + 1) * HEAD_DIM)

        def store(h, val):
            o_ref[pl.ds(q0, CHUNK), head_cols(h)] = val

        _attend(lambda h: q_ref[pl.ds(q0, CHUNK), head_cols(h)].astype(BF16),
                lambda hk: kpad[pl.ds(start, KEY_TILE), head_cols(hk)],
                lambda hk: vpad[pl.ds(start, KEY_TILE), head_cols(hk)],
                bias_scr, sinks, key_ok, store, CHUNK)
        return carry

    lax.fori_loop(0, ATT_CPB, chunk, 0)


def _attn_band(z, table, sinks, bkt, *, n_batch, seq_len):
    rows = ATT_CPB * CHUNK
    ng = seq_len // rows
    kern = functools.partial(_attn_band_kernel, seq_len=seq_len)
    smem = pl.BlockSpec(memory_space=pltpu.SMEM)
    pad_rows = ATT_LEAD + seq_len + ATT_TAIL
    return pl.pallas_call(
        kern,
        grid=(n_batch, ng),
        in_specs=[smem, smem,
                  pl.BlockSpec((rows, Q_DIM), lambda b, c: (b * ng + c, COL_Q // Q_DIM)),
                  pl.BlockSpec((seq_len, KV_DIM), lambda b, c: (b, COL_K // KV_DIM)),
                  pl.BlockSpec((seq_len, KV_DIM), lambda b, c: (b, COL_V // KV_DIM)),
                  pl.BlockSpec((CHUNK, KEY_TILE), lambda b, c: (0, 0))],
        out_specs=pl.BlockSpec((rows, Q_DIM), lambda b, c: (b * ng + c, 0)),
        out_shape=jax.ShapeDtypeStruct((n_batch * seq_len, Q_DIM), BF16),
        scratch_shapes=[pltpu.VMEM((pad_rows, KV_DIM), BF16),
                        pltpu.VMEM((pad_rows, KV_DIM), BF16),
                        pltpu.VMEM((N_KV_HEADS, GQA_GROUP * CHUNK, KEY_TILE), F32)],
        compiler_params=_cparams(("arbitrary", "arbitrary")),
        name="attn_band",
    )(table, sinks, z, z, z, bkt)


def _attn_step_kernel(tbl_ref, sink_ref, q_ref, k_ref, v_ref, bkt_ref, o_ref, kbuf, vbuf, bias_scr,
                      *, n_keys, n_q):
    b = pl.program_id(0)

    @pl.when(b == 0)
    def _():
        _build_bias(tbl_ref, bkt_ref, bias_scr, n_q)
        kbuf[n_keys:, :] = jnp.zeros((KEY_TILE - n_keys, KV_DIM), BF16)
        vbuf[n_keys:, :] = jnp.zeros((KEY_TILE - n_keys, KV_DIM), BF16)

    kbuf[0:n_keys, :] = k_ref[0].astype(BF16)
    vbuf[0:n_keys, :] = v_ref[0].astype(BF16)

    def head_cols(h):
        return slice(h * HEAD_DIM, (h + 1) * HEAD_DIM)

    def store(h, val):
        o_ref[:, head_cols(h)] = val

    _attend(lambda h: q_ref[:, head_cols(h)].astype(BF16),
            lambda hk: kbuf[:, head_cols(hk)], lambda hk: vbuf[:, head_cols(hk)],
            bias_scr, _sink_columns(sink_ref, n_q), None, store, n_q)


def _attn_step(z, k_all, v_all, table, sinks, bkt, *, n_batch, n_q, row_block0):
    n_keys = k_all.shape[1]
    kern = functools.partial(_attn_step_kernel, n_keys=n_keys, n_q=n_q)
    smem = pl.BlockSpec(memory_space=pltpu.SMEM)
    return pl.pallas_call(
        kern,
        grid=(n_batch,),
        in_specs=[smem, smem,
                  pl.BlockSpec((n_q, Q_DIM), lambda b: (row_block0 + b, COL_Q // Q_DIM)),
                  pl.BlockSpec((1, n_keys, KV_DIM), lambda b: (b, 0, 0)),
                  pl.BlockSpec((1, n_keys, KV_DIM), lambda b: (b, 0, 0)),
                  pl.BlockSpec((n_q, KEY_TILE), lambda b: (0, 0))],
        out_specs=pl.BlockSpec((n_q, Q_DIM), lambda b: (b, 0)),
        out_shape=jax.ShapeDtypeStruct((n_batch * n_q, Q_DIM), BF16),
        scratch_shapes=[pltpu.VMEM((KEY_TILE, KV_DIM), BF16),
                        pltpu.VMEM((KEY_TILE, KV_DIM), BF16),
                        pltpu.VMEM((N_KV_HEADS, GQA_GROUP * n_q, KEY_TILE), F32)],
        compiler_params=_cparams(("arbitrary",)),
        name="attn_step",
    )(table, sinks, z, k_all, v_all, bkt)


def _t5_bucket(rel):
    nb = N_BUCKETS // 2
    ret = jnp.where(rel > 0, nb, 0)
    n = jnp.abs(rel)
    max_exact = nb // 2
    nf = jnp.maximum(n, 1).astype(jnp.float32)
    large = max_exact + (jnp.log(nf / max_exact) / math.log(MAX_DISTANCE / max_exact)
                         * (nb - max_exact)).astype(jnp.int32)
    large = jnp.minimum(large, nb - 1)
    return ret + jnp.where(n < max_exact, n, large)


def _bucket_map(q_pos, k_pos):
    bkt = _t5_bucket(k_pos[None, :] - q_pos[:, None]).astype(jnp.int32)
    return jnp.pad(bkt, ((0, 0), (0, KEY_TILE - k_pos.shape[0])), constant_values=-1)


ROUTE_LANES = LANES
GATE_TN = 1024
OUT_VMEM_LIMIT = 60 * 1024 * 1024


def _outproj_kernel(hgp_ref, hgs_ref, op_ref, os_ref, ga0_ref, ga1_ref, gb0_ref, gb1_ref, xp_ref, xs_ref,
                    g1_ref, sh2_ref, sc2_ref, ng_ref, wr_ref, wa_ref, wo_ref, wrt_ref, brt_ref,
                    x1_ref, n2_ref, rt_ref, *, n_prompt_tiles):
    tm = x1_ref.shape[0]
    is_prompt = pl.program_id(0) < n_prompt_tiles
    hg = jnp.where(is_prompt, hgp_ref[...], hgs_ref[...])
    o = jnp.where(is_prompt, op_ref[...], os_ref[...])
    x = jnp.where(is_prompt, xp_ref[...], xs_ref[...])
    ya = jnp.dot(hg, wr_ref[...], preferred_element_type=F32)
    yb = jnp.dot(o, wa_ref[...], preferred_element_type=F32)
    halves = []
    for k, (ga_ref, gb_ref) in enumerate(((ga0_ref, gb0_ref), (ga1_ref, gb1_ref))):
        cols = slice(k * GATE_TN, (k + 1) * GATE_TN)
        halves.append((_sigmoid(ga_ref[...]) * ya[:, cols] + _sigmoid(gb_ref[...]) * yb[:, cols]).astype(BF16))
    merged = jnp.concatenate(halves, axis=1)
    mix = jnp.dot(merged, wo_ref[...], preferred_element_type=F32)
    x1 = x + _bcast_rows(g1_ref[...], tm) * mix
    x1_ref[...] = x1
    n2 = _rms_modulate(x1, ng_ref[...], sc2_ref[...], sh2_ref[...])
    n2_ref[...] = n2

    lg = jnp.dot(n2.astype(BF16), wrt_ref[...].astype(BF16), preferred_element_type=F32) + brt_ref[...]

    lane = lax.broadcasted_iota(jnp.int32, (tm, ROUTE_LANES), 1)
    lane_f = lane.astype(F32)
    e_lane = lane - N_GROUPS
    lane_group = (e_lane >> 3).astype(F32)

    def first_argmax(vals, vmax):
        return jnp.min(jnp.where(vals == vmax, lane_f, float(ROUTE_LANES)), axis=-1, keepdims=True)

    gl = jnp.where(lane < N_GROUPS, lg, NEG_INF)
    gmax = jnp.max(gl, axis=-1, keepdims=True)
    g_idx = first_argmax(gl, gmax)
    g_w = 1.0 / jnp.sum(jnp.exp(gl - gmax), axis=-1, keepdims=True)
    in_group = jnp.where((e_lane >= 0) & (e_lane < N_EXPERTS), lane_group, -1.0) == g_idx
    el = jnp.where(in_group, lg, NEG_INF)
    v1 = jnp.max(el, axis=-1, keepdims=True)
    i1 = first_argmax(el, v1)
    el2 = jnp.where(lane_f == i1, NEG_INF, el)
    v2 = jnp.max(el2, axis=-1, keepdims=True)
    i2 = first_argmax(el2, v2)
    e21 = jnp.exp(v2 - v1)
    w1 = g_w / (1.0 + e21)
    w2 = g_w * e21 / (1.0 + e21)
    e1 = i1 - float(N_GROUPS)
    e2 = i2 - float(N_GROUPS)
    rt_ref[...] = jnp.where(lane == 0, e1, jnp.where(lane == 1, e2, jnp.where(lane == 2, w1,
                            jnp.where(lane == 3, w2, 0.0))))


def _outproj(hg_p, hg_s, o_p, o_s, z, x_p, x_s, g1, sh2, sc2, norm_g, wr, wa, wo, w_route, b_route):
    tm = ROW_TILE
    t = z.shape[0]
    seg = tm // MOD_ROWS
    n_p = x_p.shape[0] // tm
    row = lambda i: (i, 0)
    fix = lambda i: (0, 0)
    row_p = lambda i: (jnp.minimum(i, n_p - 1), 0)
    row_s = lambda i: (jnp.maximum(i - n_p, 0), 0)
    once = pl.Buffered(1)

    def gate(col):
        return pl.BlockSpec((tm, GATE_TN), lambda i: (i, col // GATE_TN))

    return pl.pallas_call(
        functools.partial(_outproj_kernel, n_prompt_tiles=n_p),
        grid=(t // tm,),
        in_specs=[pl.BlockSpec((tm, D_RNN), row_p), pl.BlockSpec((tm, D_RNN), row_s, pipeline_mode=once),
                  pl.BlockSpec((tm, Q_DIM), row_p), pl.BlockSpec((tm, Q_DIM), row_s, pipeline_mode=once),
                  gate(COL_GA), gate(COL_GA + GATE_TN), gate(COL_GB), gate(COL_GB + GATE_TN),
                  pl.BlockSpec((tm, D_MODEL), row_p), pl.BlockSpec((tm, D_MODEL), row_s, pipeline_mode=once),
                  pl.BlockSpec((seg, D_MODEL), row), pl.BlockSpec((seg, D_MODEL), row),
                  pl.BlockSpec((seg, D_MODEL), row),
                  pl.BlockSpec((1, D_MODEL), fix),
                  pl.BlockSpec((D_RNN, D_MODEL), fix, pipeline_mode=once),
                  pl.BlockSpec((Q_DIM, D_MODEL), fix, pipeline_mode=once),
                  pl.BlockSpec((D_MODEL, D_MODEL), fix, pipeline_mode=once),
                  pl.BlockSpec((D_MODEL, ROUTE_LANES), fix),
                  pl.BlockSpec((1, ROUTE_LANES), fix)],
        out_specs=[pl.BlockSpec((tm, D_MODEL), row), pl.BlockSpec((tm, D_MODEL), row),
                   pl.BlockSpec((tm, ROUTE_LANES), row)],
        out_shape=[jax.ShapeDtypeStruct((t, D_MODEL), F32), jax.ShapeDtypeStruct((t, D_MODEL), F32),
                   jax.ShapeDtypeStruct((t, ROUTE_LANES), F32)],
        compiler_params=_cparams(("arbitrary",), OUT_VMEM_LIMIT),
        name="outproj",
    )(hg_p, hg_s, o_p, o_s, z, z, z, z, x_p, x_s, g1, sh2, sc2, norm_g.reshape(1, D_MODEL),
      wr, wa, wo, w_route, b_route)


def _row_gather(src_hbm, idx_ref, base, dst, sem, n_rows):
    for r in range(n_rows):
        tok = idx_ref[base + r]
        pltpu.make_async_copy(src_hbm.at[pl.ds(tok, 1), :], dst.at[pl.ds(r, 1), :], sem).start()


def _row_gather_wait(src_hbm, dst, sem, n_rows):
    pltpu.make_async_copy(src_hbm.at[pl.ds(0, n_rows), :], dst, sem).wait()


def _moe_kernel(te_ref, tfirst_ref, tvalid_ref, tnext_ref, src_ref, n2_hbm, wg_hbm, wu_hbm, wd_hbm, y_ref,
                xbuf, xsem, wg_st, wu_st, wd_st, wsem, wg_bf, wu_bf, wd_bf):
    i = pl.program_id(0)
    n_tiles = pl.num_programs(0)
    slot = lax.rem(i, MOE_RING)
    stages = ((wg_hbm, wg_st, wg_bf), (wu_hbm, wu_st, wu_bf), (wd_hbm, wd_st, wd_bf))

    def weight_copy(k, e):
        hbm, st, _ = stages[k]
        return pltpu.make_async_copy(hbm.at[e], st, wsem.at[k])

    def gather_tile(tile):
        @pl.when((tile < n_tiles) & (tvalid_ref[jnp.minimum(tile, n_tiles - 1)] == 1))
        def _():
            s = lax.rem(tile, MOE_RING)
            _row_gather(n2_hbm, src_ref, tile * MOE_TM, xbuf.at[s], xsem.at[s], MOE_TM)

    @pl.when(i == 0)
    def _():
        for k in range(len(stages)):
            weight_copy(k, te_ref[0]).start(priority=1)
        for ahead in range(MOE_RING - 1):
            gather_tile(ahead)

    @pl.when(tvalid_ref[i] == 1)
    def _():
        gather_tile(i + MOE_RING - 1)

        @pl.when(tfirst_ref[i] == 1)
        def _():
            for k, (_, st, bf) in enumerate(stages):
                weight_copy(k, te_ref[i]).wait()
                bf[...] = st[...].astype(BF16)

            @pl.when(tnext_ref[i] >= 0)
            def _():
                for k in range(len(stages)):
                    weight_copy(k, tnext_ref[i]).start(priority=1)

        _row_gather_wait(n2_hbm, xbuf.at[slot], xsem.at[slot], MOE_TM)
        x = xbuf[slot].astype(BF16)
        hgate = jnp.dot(x, wg_bf[...], preferred_element_type=F32)
        hup = jnp.dot(x, wu_bf[...], preferred_element_type=F32)
        act = (hgate * _sigmoid(hgate) * hup).astype(BF16)
        y_ref[...] = jnp.dot(act, wd_bf[...], preferred_element_type=F32)

    @pl.when(tvalid_ref[i] == 0)
    def _():
        y_ref[...] = jnp.zeros(y_ref.shape, F32)


def _moe(n2, w_gate, w_up, w_down, tile_expert, tile_first, tile_valid, tile_next, src_tok):
    n_tiles = tile_expert.shape[0]
    hbm = pl.BlockSpec(memory_space=pl.ANY)
    grid_spec = pltpu.PrefetchScalarGridSpec(
        num_scalar_prefetch=5,
        grid=(n_tiles,),
        in_specs=[hbm, hbm, hbm, hbm],
        out_specs=pl.BlockSpec((MOE_TM, D_MODEL), lambda i, te, tf, tv, tn, st: (i, 0)),
        scratch_shapes=[pltpu.VMEM((MOE_RING, MOE_TM, D_MODEL), F32),
                        pltpu.SemaphoreType.DMA((MOE_RING,)),
                        pltpu.VMEM((D_MODEL, D_EXPERT), F32),
                        pltpu.VMEM((D_MODEL, D_EXPERT), F32),
                        pltpu.VMEM((D_EXPERT, D_MODEL), F32),
                        pltpu.SemaphoreType.DMA((3,)),
                        pltpu.VMEM((D_MODEL, D_EXPERT), BF16),
                        pltpu.VMEM((D_MODEL, D_EXPERT), BF16),
                        pltpu.VMEM((D_EXPERT, D_MODEL), BF16)],
    )
    return pl.pallas_call(
        _moe_kernel,
        grid_spec=grid_spec,
        out_shape=jax.ShapeDtypeStruct((n_tiles * MOE_TM, D_MODEL), F32),
        compiler_params=_cparams(("arbitrary",)),
        name="moe",
    )(tile_expert, tile_first, tile_valid, tile_next, src_tok, n2, w_gate, w_up, w_down)


def _route_plan(e1, e2, n_tok):
    experts = jnp.arange(N_EXPERTS, dtype=jnp.int32)
    flat_e = jnp.concatenate([e1, e2])
    onehot = (flat_e[:, None] == experts[None, :]).astype(jnp.int32)
    csum = jnp.cumsum(onehot, axis=0)
    rank = jnp.sum(csum * onehot, axis=1) - 1
    counts = csum[-1]
    tiles_per = (counts + MOE_TM - 1) // MOE_TM
    tile_end = jnp.cumsum(tiles_per)
    tile_off = tile_end - tiles_per
    slot = jnp.sum(onehot * tile_off[None, :], axis=1) * MOE_TM + rank
    n_tiles = (2 * n_tok) // MOE_TM + N_EXPERTS
    tok = jnp.tile(jnp.arange(n_tok, dtype=jnp.int32), 2)
    src_tok = jnp.zeros((n_tiles * MOE_TM,), jnp.int32).at[slot].set(tok)
    tile_id = jnp.arange(n_tiles, dtype=jnp.int32)
    n_used = tile_end[-1]
    tile_valid = (tile_id < n_used).astype(jnp.int32)
    te = jnp.sum((tile_end[None, :] <= jnp.minimum(tile_id, n_used - 1)[:, None]).astype(jnp.int32), axis=1)
    tile_expert = jnp.minimum(te, N_EXPERTS - 1)
    prev = jnp.concatenate([jnp.full((1,), -1, jnp.int32), tile_expert[:-1]])
    tile_first = (tile_expert != prev).astype(jnp.int32)
    next_tile = tile_end[tile_expert]
    tile_next = jnp.where(next_tile < n_used, tile_expert[jnp.minimum(next_tile, n_tiles - 1)], -1)
    return tile_expert, tile_first, tile_valid, tile_next, src_tok, slot[:n_tok], slot[n_tok:]


def _final_kernel(p1_ref, p2_ref, ys_hbm, x1_ref, g2_ref, rt_ref, fg_ref, yp_ref, ysm_ref, ybuf, sem,
                  *, n_prompt_tiles):
    i = pl.program_id(0)
    n_tiles = pl.num_programs(0)
    slot = i % 2

    def start(tile, s):
        _row_gather(ys_hbm, p1_ref, tile * ROW_TILE, ybuf.at[s, 0], sem.at[s], ROW_TILE)
        _row_gather(ys_hbm, p2_ref, tile * ROW_TILE, ybuf.at[s, 1], sem.at[s], ROW_TILE)

    @pl.when(i == 0)
    def _():
        start(0, 0)

    @pl.when(i + 1 < n_tiles)
    def _():
        start(i + 1, 1 - slot)

    _row_gather_wait(ys_hbm, ybuf.at[slot, 0], sem.at[slot], ROW_TILE)
    _row_gather_wait(ys_hbm, ybuf.at[slot, 1], sem.at[slot], ROW_TILE)
    rt = rt_ref[...]
    moe = rt[:, 2:3] * ybuf[slot, 0] + rt[:, 3:4] * ybuf[slot, 1]
    x2 = x1_ref[...] + _bcast_rows(g2_ref[...], ROW_TILE) * moe
    y = x2 * lax.rsqrt(jnp.mean(x2 * x2, axis=-1, keepdims=True) + EPS) * fg_ref[...]

    @pl.when(i < n_prompt_tiles)
    def _():
        yp_ref[...] = y

    @pl.when(i >= n_prompt_tiles)
    def _():
        ysm_ref[...] = y


def _final(ys, x1, g2, rt, final_g, p1, p2, n_prompt_rows):
    t = x1.shape[0]
    tm = ROW_TILE
    seg = tm // MOD_ROWS
    n_p = n_prompt_rows // tm
    grid_spec = pltpu.PrefetchScalarGridSpec(
        num_scalar_prefetch=2,
        grid=(t // tm,),
        in_specs=[pl.BlockSpec(memory_space=pl.ANY),
                  pl.BlockSpec((tm, D_MODEL), lambda i, a, b: (i, 0)),
                  pl.BlockSpec((seg, D_MODEL), lambda i, a, b: (i, 0)),
                  pl.BlockSpec((tm, ROUTE_LANES), lambda i, a, b: (i, 0)),
                  pl.BlockSpec((1, D_MODEL), lambda i, a, b: (0, 0))],
        out_specs=[pl.BlockSpec((tm, D_MODEL), lambda i, a, b: (jnp.minimum(i, n_p - 1), 0)),
                   pl.BlockSpec((tm, D_MODEL), lambda i, a, b: (jnp.maximum(i - n_p, 0), 0))],
        scratch_shapes=[pltpu.VMEM((2, 2, tm, D_MODEL), F32),
                        pltpu.SemaphoreType.DMA((2,))],
    )
    return pl.pallas_call(
        functools.partial(_final_kernel, n_prompt_tiles=n_p),
        grid_spec=grid_spec,
        out_shape=[jax.ShapeDtypeStruct((n_prompt_rows, D_MODEL), F32),
                   jax.ShapeDtypeStruct((t - n_prompt_rows, D_MODEL), F32)],
        compiler_params=_cparams(("arbitrary",)),
        name="final",
    )(p1, p2, ys, x1, g2, rt, final_g.reshape(1, D_MODEL))


def kernel(x_prompt, x_sample, cache_k_win, cache_v_win, state_conv, state_rglru, c_prompt, c_sample, w_ada, b_ada, norm1_g, norm2_g, w_in, b_in, w_conv, b_conv, w_rg_a, b_rg_a, w_rg_x, b_rg_x, lru_lambda, w_rnn_out, w_attn_out, w_out, attn_sinks, w_route_group, b_route_group, w_route_expert, b_route_expert, w_exp_gate, w_exp_up, w_exp_down, rel_bias_table, final_norm_g):
    n_b, seq, _ = x_prompt.shape
    d_b, d_seq, _ = x_sample.shape
    assert w_ada.shape[0] == 1, "single trunk layer"
    assert seq % (ATT_CPB * CHUNK) == 0 and seq % MOD_ROWS == 0 and d_seq == MOD_ROWS
    assert d_seq >= CONV_W - 1 and d_b == N_CHAINS
    t_p, t_s = n_b * seq, d_b * d_seq
    t = t_p + t_s
    assert t_s == ROW_TILE and t_p % ROW_TILE == 0
    cw = cache_k_win.shape[2]
    l = 0

    x_p = x_prompt.reshape(t_p, D_MODEL)
    x_s = x_sample.reshape(t_s, D_MODEL)

    n_c = n_b + d_b
    c_all = jnp.pad(jnp.concatenate([c_prompt, c_sample], axis=0), ((0, -n_c % SUBLANES), (0, 0)))
    mod = _ada(c_all, w_ada[l], b_ada[l])

    def per_segment(m):
        return jnp.concatenate([jnp.repeat(m[:n_b], seq // MOD_ROWS, axis=0),
                                jnp.repeat(m[n_b:n_c], d_seq // MOD_ROWS, axis=0)], axis=0)

    sh1, sc1, g1, sh2, sc2, g2 = [per_segment(m) for m in jnp.split(mod, 6, axis=-1)]

    z = _inproj(_norm1(x_p, x_s, sc1, sh1, norm1_g[l]), w_in[l], b_in[l])

    w_gate = jnp.concatenate([w_rg_a[l], w_rg_x[l]], axis=-1)
    b_gate = jnp.concatenate([b_rg_a[l], b_rg_x[l]], axis=-1)[:, None, :]
    hg_p, hl_p = _rglru(z, jnp.zeros((n_b, SUBLANES, D_RNN), F32), jnp.zeros((n_b, N_CHAINS, D_RNN), F32),
                        w_conv[l], b_conv[l], w_gate, b_gate, lru_lambda[l],
                        n_blocks=n_b, n_seq=1, seq_len=seq, row_block0=0, link=True)
    state8 = jnp.pad(state_conv[l], ((0, 0), (SUBLANES - (CONV_W - 1), 0), (0, 0)))
    hg_s, hl_s = _rglru(z, state8, state_rglru[l][None], w_conv[l], b_conv[l], w_gate, b_gate, lru_lambda[l],
                        n_blocks=1, n_seq=d_b, seq_len=d_seq, row_block0=t_p // t_s, link=False)

    sinks = attn_sinks[l]
    bkt_p = _bucket_map(WIN_CHUNKS * CHUNK + jnp.arange(CHUNK), jnp.arange(SPAN))
    o_p = _attn_band(z, rel_bias_table, sinks, bkt_p, n_batch=n_b, seq_len=seq)
    k_new = z[t_p:, COL_K:COL_K + KV_DIM].reshape(d_b, d_seq, KV_DIM)
    v_new = z[t_p:, COL_V:COL_V + KV_DIM].reshape(d_b, d_seq, KV_DIM)
    k_all = jnp.concatenate([cache_k_win[l].reshape(d_b, cw, KV_DIM), k_new], axis=1)
    v_all = jnp.concatenate([cache_v_win[l].reshape(d_b, cw, KV_DIM), v_new], axis=1)
    bkt_s = _bucket_map(cw + jnp.arange(d_seq), jnp.arange(cw + d_seq))
    o_s = _attn_step(z, k_all, v_all, rel_bias_table, sinks, bkt_s, n_batch=d_b, n_q=d_seq, row_block0=t_p // d_seq)

    n_route = N_GROUPS + N_EXPERTS
    w_route = jnp.pad(jnp.concatenate([w_route_group[l], w_route_expert[l]], axis=1),
                      ((0, 0), (0, ROUTE_LANES - n_route)))
    b_route = jnp.pad(jnp.concatenate([b_route_group[l], b_route_expert[l]]),
                      (0, ROUTE_LANES - n_route)).reshape(1, ROUTE_LANES)
    x1, n2, rt = _outproj(hg_p, hg_s, o_p, o_s, z, x_p, x_s, g1, sh2, sc2, norm2_g[l],
                          w_rnn_out[l].astype(BF16), w_attn_out[l].astype(BF16), w_out[l].astype(BF16),
                          w_route, b_route)

    e1 = rt[:, 0].astype(jnp.int32)
    e2 = rt[:, 1].astype(jnp.int32)
    tile_expert, tile_first, tile_valid, tile_next, src_tok, p1, p2 = _route_plan(e1, e2, t)
    ys = _moe(n2, w_exp_gate[l], w_exp_up[l], w_exp_down[l], tile_expert, tile_first, tile_valid, tile_next,
              src_tok)
    y_p, y_s = _final(ys, x1, g2, rt, final_norm_g, p1, p2, t_p)

    win = min(WINDOW, seq)

    def tail(col0, width, n_rows):
        return jnp.stack([z[(b + 1) * seq - n_rows:(b + 1) * seq, col0:col0 + width] for b in range(n_b)])

    kp = tail(COL_K, KV_DIM, win).reshape(n_b, win, N_KV_HEADS, HEAD_DIM)
    vp = tail(COL_V, KV_DIM, win).reshape(n_b, win, N_KV_HEADS, HEAD_DIM)
    cp = tail(COL_XR, D_RNN, CONV_W - 1)
    rp = hl_p[:, N_CHAINS - 1, :]
    ks = k_all[:, -cw:].reshape(d_b, cw, N_KV_HEADS, HEAD_DIM)
    vs = v_all[:, -cw:].reshape(d_b, cw, N_KV_HEADS, HEAD_DIM)
    cs = z[t_p:, COL_XR:COL_XR + D_RNN].reshape(d_b, d_seq, D_RNN)[:, -(CONV_W - 1):]
    rs = hl_s[0]
    return (y_p.reshape(n_b, seq, D_MODEL), y_s.reshape(d_b, d_seq, D_MODEL),
            kp[None], vp[None], cp[None], rp[None], ks[None], vs[None], cs[None], rs[None])
```

```python
import functools
import math

import jax
import jax.numpy as jnp
from jax import lax
from jax.experimental import pallas as pl
from jax.experimental.pallas import tpu as pltpu

F32 = jnp.float32
BF16 = jnp.bfloat16

D_MODEL = 2048
D_RNN = 2048
RNN_BLOCK = 128
CONV_W = 4
LRU_C = 8.0
N_HEADS = 16
N_KV_HEADS = 4
HEAD_DIM = 128
GQA_GROUP = N_HEADS // N_KV_HEADS
Q_DIM = N_HEADS * HEAD_DIM
KV_DIM = N_KV_HEADS * HEAD_DIM
CHUNK = 64
WINDOW = 128
WIN_CHUNKS = WINDOW // CHUNK
SPAN = (WIN_CHUNKS + 1) * CHUNK
N_BUCKETS = 32
MAX_DISTANCE = 128
N_GROUPS = 4
E_PER_GROUP = 8
N_EXPERTS = N_GROUPS * E_PER_GROUP
D_EXPERT = 512
EPS = 1e-6
NEG_INF = -1e30
D_IN = 2 * D_RNN + Q_DIM + 2 * KV_DIM + 2 * D_MODEL
COL_XR, COL_GR, COL_Q = 0, D_RNN, 2 * D_RNN
COL_K = COL_Q + Q_DIM
COL_V = COL_K + KV_DIM
COL_GA = COL_V + KV_DIM
COL_GB = COL_GA + D_MODEL

LANES = 128
SUBLANES = 8
MOD_ROWS = 32
KEY_TILE = 256
VMEM_LIMIT = 56 * 1024 * 1024

ROW_TILE = 256
IN_TM_MAX = 1408
IN_TN = 1024
MOE_TM = 128
MOE_RING = 3


def _sigmoid(x):
    return 0.5 * jnp.tanh(0.5 * x) + 0.5


def _gelu_tanh(x):
    return 0.5 * x * (1.0 + jnp.tanh(math.sqrt(2.0 / math.pi) * (x + 0.044715 * (x * x * x))))


def _bcast_rows(v, rows):
    n, d = v.shape
    return jnp.broadcast_to(v[:, None, :], (n, MOD_ROWS, d)).reshape(rows, d)


def _rms_modulate(x, gain, scale_seg, shift_seg):
    rows = x.shape[0]
    y = x * lax.rsqrt(jnp.mean(x * x, axis=-1, keepdims=True) + EPS) * gain
    return y * (1.0 + _bcast_rows(scale_seg, rows)) + _bcast_rows(shift_seg, rows)


def _cparams(sem, vmem_limit=VMEM_LIMIT):
    return pltpu.CompilerParams(dimension_semantics=sem, vmem_limit_bytes=vmem_limit)


def _ada_kernel(c_ref, w_ref, b_ref, o_ref):
    c = c_ref[...]
    s = (c * _sigmoid(c)).astype(BF16)
    o_ref[...] = jnp.dot(s, w_ref[...].astype(BF16), preferred_element_type=F32) + b_ref[...]


def _ada(c_all, w_ada, b_ada):
    rows = c_all.shape[0]
    n = w_ada.shape[1]
    tn = 1024
    return pl.pallas_call(
        _ada_kernel,
        grid=(n // tn,),
        in_specs=[pl.BlockSpec((rows, D_MODEL), lambda j: (0, 0)),
                  pl.BlockSpec((D_MODEL, tn), lambda j: (0, j)),
                  pl.BlockSpec((1, tn), lambda j: (0, j))],
        out_specs=pl.BlockSpec((rows, tn), lambda j: (0, j)),
        out_shape=jax.ShapeDtypeStruct((rows, n), F32),
        compiler_params=_cparams(("arbitrary",)),
        name="ada",
    )(c_all, w_ada, b_ada.reshape(1, n))


def _norm1_kernel(xp_ref, xs_ref, sc_ref, sh_ref, g_ref, n1_ref, *, n_prompt_tiles):
    is_prompt = pl.program_id(0) < n_prompt_tiles
    gain = g_ref[...]

    def segment(s, carry):
        rows = pl.ds(pl.multiple_of(s * MOD_ROWS, MOD_ROWS), MOD_ROWS)
        x = jnp.where(is_prompt, xp_ref[rows, :], xs_ref[rows, :])
        y = x * lax.rsqrt(jnp.mean(x * x, axis=-1, keepdims=True) + EPS) * gain
        n1_ref[rows, :] = (y * (1.0 + sc_ref[pl.ds(s, 1), :]) + sh_ref[pl.ds(s, 1), :]).astype(BF16)
        return carry

    lax.fori_loop(0, n1_ref.shape[0] // MOD_ROWS, segment, 0, unroll=True)


def _norm1(x_p, x_s, sc1, sh1, norm_g):
    tm = ROW_TILE
    n_p = x_p.shape[0] // tm
    t = x_p.shape[0] + x_s.shape[0]
    seg = tm // MOD_ROWS
    return pl.pallas_call(
        functools.partial(_norm1_kernel, n_prompt_tiles=n_p),
        grid=(t // tm,),
        in_specs=[pl.BlockSpec((tm, D_MODEL), lambda i: (jnp.minimum(i, n_p - 1), 0)),
                  pl.BlockSpec((tm, D_MODEL), lambda i: (jnp.maximum(i - n_p, 0), 0)),
                  pl.BlockSpec((seg, D_MODEL), lambda i: (i, 0)),
                  pl.BlockSpec((seg, D_MODEL), lambda i: (i, 0)),
                  pl.BlockSpec((1, D_MODEL), lambda i: (0, 0))],
        out_specs=pl.BlockSpec((tm, D_MODEL), lambda i: (i, 0)),
        out_shape=jax.ShapeDtypeStruct((t, D_MODEL), BF16),
        compiler_params=_cparams(("arbitrary",)),
        name="norm1",
    )(x_p, x_s, sc1, sh1, norm_g.reshape(1, D_MODEL))


def _inproj_kernel(n1_ref, w_ref, b_ref, z_ref, w_bf):
    @pl.when(pl.program_id(1) == 0)
    def _():
        w_bf[...] = w_ref[...].astype(BF16)

    z_ref[...] = jnp.dot(n1_ref[...], w_bf[...], preferred_element_type=F32) + b_ref[...]


def _inproj(n1, w_in, b_in):
    t = n1.shape[0]
    tm = next(m for m in range(IN_TM_MAX, 0, -LANES) if t % m == 0)
    return pl.pallas_call(
        _inproj_kernel,
        grid=(D_IN // IN_TN, t // tm),
        in_specs=[pl.BlockSpec((tm, D_MODEL), lambda j, i: (i, 0)),
                  pl.BlockSpec((D_MODEL, IN_TN), lambda j, i: (0, j)),
                  pl.BlockSpec((1, IN_TN), lambda j, i: (0, j))],
        out_specs=pl.BlockSpec((tm, IN_TN), lambda j, i: (i, j)),
        out_shape=jax.ShapeDtypeStruct((t, D_IN), F32),
        scratch_shapes=[pltpu.VMEM((D_MODEL, IN_TN), BF16)],
        compiler_params=_cparams(("arbitrary", "arbitrary")),
        name="inproj",
    )(n1, w_in, b_in.reshape(1, D_IN))


RG_CW = 256
RG_SLABS = RG_CW // LANES
N_CHAINS = SUBLANES
CHAIN_PAD = 8


def _rglru_kernel(xr_ref, gr_ref, st_ref, h0_ref, wc_ref, bc_ref, wg_ref, bg_ref, lam_ref,
                  hg_ref, hl_ref, ext_scr, a_scr, u_scr, *, n_seq, seq_len, link):
    rows = n_seq * seq_len
    cl = rows // N_CHAINS
    pitch = cl + CHAIN_PAD

    for s in range(n_seq):
        ext_scr[s, 0:SUBLANES, :] = st_ref[s]
        ext_scr[s, SUBLANES:SUBLANES + seq_len, :] = xr_ref[s * seq_len:(s + 1) * seq_len, :]

    z = -lam_ref[...]
    softplus = jnp.maximum(z, 0.0) + jnp.log1p(jnp.exp(-jnp.abs(z)))
    nq = (-0.25 * LRU_C) * softplus

    for c in range(N_CHAINS):
        s, r0 = divmod(c * cl, seq_len)
        xc = bc_ref[...]
        for j in range(CONV_W):
            xc = xc + wc_ref[j:j + 1, :] * ext_scr[s, pl.ds(SUBLANES - (CONV_W - 1) + j + r0, cl), :]
        for sl in range(RG_SLABS):
            lanes = slice(sl * LANES, (sl + 1) * LANES)
            xb = xc[:, lanes]
            g = jnp.dot(xb.astype(BF16), wg_ref[sl].astype(BF16), preferred_element_type=F32) + bg_ref[sl]
            tr = jnp.tanh(0.5 * g[:, :LANES])
            ti = jnp.tanh(0.5 * g[:, LANES:])
            th = jnp.tanh(nq[:, lanes] * (tr + 1.0))
            rcp = 1.0 / (1.0 - th)
            a_scr[sl, c * pitch:c * pitch + cl, :] = (1.0 + th) * rcp
            u_scr[sl, c * pitch:c * pitch + cl, :] = (jnp.sqrt(-th) * rcp) * ((ti + 1.0) * xb)

    def step(t, carry):
        hs, ps = carry
        new_h, new_p = [], []
        for sl in range(RG_SLABS):
            a = a_scr[sl, pl.ds(t, N_CHAINS, stride=pitch), :]
            u = u_scr[sl, pl.ds(t, N_CHAINS, stride=pitch), :]
            h = a * hs[sl] + u
            u_scr[sl, pl.ds(t, N_CHAINS, stride=pitch), :] = h
            new_h.append(h)
            if link:
                p = a * ps[sl]
                a_scr[sl, pl.ds(t, N_CHAINS, stride=pitch), :] = p
                new_p.append(p)
            else:
                new_p.append(ps[sl])
        return tuple(new_h), tuple(new_p)

    h_init = tuple(h0_ref[0, :, sl * LANES:(sl + 1) * LANES] for sl in range(RG_SLABS))
    p_init = tuple(jnp.ones((N_CHAINS, LANES), F32) for _ in range(RG_SLABS))
    h_end, p_end = lax.fori_loop(0, cl, step, (h_init, p_init), unroll=8)

    row = lax.broadcasted_iota(jnp.int32, (N_CHAINS, LANES), 0)
    for sl in range(RG_SLABS):
        lanes = slice(sl * LANES, (sl + 1) * LANES)
        if link:
            def shift_down(v):
                return jnp.where(row == 0, 0.0, pltpu.roll(v, 1, axis=0))
            hh = h_end[sl]
            for _ in range(N_CHAINS - 1):
                hh = h_end[sl] + p_end[sl] * shift_down(hh)
            carry_in = shift_down(hh)
        else:
            hh = h_end[sl]
        hl_ref[0, :, lanes] = hh
        for c in range(N_CHAINS):
            h = u_scr[sl, c * pitch:c * pitch + cl, :]
            if link:
                h = h + a_scr[sl, c * pitch:c * pitch + cl, :] * carry_in[c:c + 1, :]
            gg = _gelu_tanh(gr_ref[c * cl:(c + 1) * cl, lanes])
            hg_ref[c * cl:(c + 1) * cl, lanes] = (h * gg).astype(BF16)


def _rglru(z, state8, h0, w_conv, b_conv, w_gate, b_gate, lam, *, n_blocks, n_seq, seq_len, row_block0, link):
    rows = n_seq * seq_len
    cl = rows // N_CHAINS
    ncb = D_RNN // RG_CW
    gr0 = COL_GR // RG_CW
    kern = functools.partial(_rglru_kernel, n_seq=n_seq, seq_len=seq_len, link=link)
    return pl.pallas_call(
        kern,
        grid=(n_blocks, ncb),
        in_specs=[pl.BlockSpec((rows, RG_CW), lambda b, n: (row_block0 + b, n)),
                  pl.BlockSpec((rows, RG_CW), lambda b, n: (row_block0 + b, gr0 + n)),
                  pl.BlockSpec((n_seq, SUBLANES, RG_CW), lambda b, n: (b, 0, n)),
                  pl.BlockSpec((1, N_CHAINS, RG_CW), lambda b, n: (b, 0, n)),
                  pl.BlockSpec((CONV_W, RG_CW), lambda b, n: (0, n)),
                  pl.BlockSpec((1, RG_CW), lambda b, n: (0, n)),
                  pl.BlockSpec((RG_SLABS, RNN_BLOCK, 2 * RNN_BLOCK), lambda b, n: (n, 0, 0)),
                  pl.BlockSpec((RG_SLABS, 1, 2 * RNN_BLOCK), lambda b, n: (n, 0, 0)),
                  pl.BlockSpec((1, RG_CW), lambda b, n: (0, n))],
        out_specs=[pl.BlockSpec((rows, RG_CW), lambda b, n: (b, n)),
                   pl.BlockSpec((1, N_CHAINS, RG_CW), lambda b, n: (b, 0, n))],
        out_shape=[jax.ShapeDtypeStruct((n_blocks * rows, D_RNN), BF16),
                   jax.ShapeDtypeStruct((n_blocks, N_CHAINS, D_RNN), F32)],
        scratch_shapes=[pltpu.VMEM((n_seq, SUBLANES + seq_len, RG_CW), F32),
                        pltpu.VMEM((RG_SLABS, N_CHAINS * (cl + CHAIN_PAD), LANES), F32),
                        pltpu.VMEM((RG_SLABS, N_CHAINS * (cl + CHAIN_PAD), LANES), F32)],
        compiler_params=_cparams(("arbitrary", "arbitrary")),
        name="rglru_link" if link else "rglru_step",
    )(z, z, state8, h0, w_conv, b_conv.reshape(1, D_RNN), w_gate, b_gate, lam.reshape(1, D_RNN))


def _build_bias(tbl_ref, bkt_ref, bias_scr, n_q):
    bkt = bkt_ref[...]
    base = jnp.where(bkt < 0, NEG_INF, 0.0)
    for h in range(N_HEADS):
        acc = base
        for bk in range(N_BUCKETS):
            acc = jnp.where(bkt == bk, tbl_ref[bk, h], acc)
        hk, g = divmod(h, GQA_GROUP)
        bias_scr[hk, g * n_q:(g + 1) * n_q, :] = acc


def _sink_columns(sink_ref, n_q):
    row = lax.broadcasted_iota(jnp.int32, (GQA_GROUP * n_q, 1), 0)
    cols = []
    for hk in range(N_KV_HEADS):
        col = jnp.full((GQA_GROUP * n_q, 1), sink_ref[hk * GQA_GROUP], F32)
        for g in range(1, GQA_GROUP):
            col = jnp.where(row >= g * n_q, sink_ref[hk * GQA_GROUP + g], col)
        cols.append(col)
    return cols


def _attend(q_of, k_of, v_of, bias_scr, sinks, key_ok, store, n_q):
    scores = []
    for hk in range(N_KV_HEADS):
        qg = jnp.concatenate([q_of(hk * GQA_GROUP + g) for g in range(GQA_GROUP)], axis=0)
        s = lax.dot_general(qg, k_of(hk), (((1,), (1,)), ((), ())), preferred_element_type=F32)
        s = s * (HEAD_DIM ** -0.5) + bias_scr[hk]
        if key_ok is not None:
            s = jnp.where(key_ok, s, NEG_INF)
        scores.append(s)
    probs = []
    for hk in range(N_KV_HEADS):
        s, sink = scores[hk], sinks[hk]
        m = jnp.maximum(jnp.max(s, axis=-1, keepdims=True), sink)
        p = jnp.exp(s - m)
        inv = 1.0 / (jnp.sum(p, axis=-1, keepdims=True) + jnp.exp(sink - m))
        probs.append((p * inv).astype(BF16))
    for hk in range(N_KV_HEADS):
        o = jnp.dot(probs[hk], v_of(hk), preferred_element_type=F32)
        for g in range(GQA_GROUP):
            store(hk * GQA_GROUP + g, o[g * n_q:(g + 1) * n_q, :].astype(BF16))


ATT_CPB = 8
ATT_LEAD = WIN_CHUNKS * CHUNK
ATT_TAIL = KEY_TILE - SPAN


def _attn_band_kernel(tbl_ref, sink_ref, q_ref, k_ref, v_ref, bkt_ref, o_ref, kpad, vpad, bias_scr, *, seq_len):
    b = pl.program_id(0)
    cg = pl.program_id(1)

    @pl.when((b == 0) & (cg == 0))
    def _():
        _build_bias(tbl_ref, bkt_ref, bias_scr, CHUNK)

    @pl.when(cg == 0)
    def _():
        for ref, pad in ((k_ref, kpad), (v_ref, vpad)):
            pad[0:ATT_LEAD, :] = jnp.zeros((ATT_LEAD, KV_DIM), BF16)
            pad[ATT_LEAD:ATT_LEAD + seq_len, :] = ref[...].astype(BF16)
            pad[ATT_LEAD + seq_len:, :] = jnp.zeros((ATT_TAIL, KV_DIM), BF16)

    kidx = lax.broadcasted_iota(jnp.int32, (1, KEY_TILE), 1)
    sinks = _sink_columns(sink_ref, CHUNK)

    def chunk(c, carry):
        q0 = pl.multiple_of(c * CHUNK, CHUNK)
        start = pl.multiple_of((cg * ATT_CPB + c) * CHUNK, CHUNK)
        key_ok = start + kidx >= ATT_LEAD

        def head_cols(h):
            return slice(h * HEAD_DIM, (h + 1) * HEAD_DIM)

        def store(h, val):
            o_ref[pl.ds(q0, CHUNK), head_cols(h)] = val

        _attend(lambda h: q_ref[pl.ds(q0, CHUNK), head_cols(h)].astype(BF16),
                lambda hk: kpad[pl.ds(start, KEY_TILE), head_cols(hk)],
                lambda hk: vpad[pl.ds(start, KEY_TILE), head_cols(hk)],
                bias_scr, sinks, key_ok, store, CHUNK)
        return carry

    lax.fori_loop(0, ATT_CPB, chunk, 0)


def _attn_band(z, table, sinks, bkt, *, n_batch, seq_len):
    rows = ATT_CPB * CHUNK
    ng = seq_len // rows
    kern = functools.partial(_attn_band_kernel, seq_len=seq_len)
    smem = pl.BlockSpec(memory_space=pltpu.SMEM)
    pad_rows = ATT_LEAD + seq_len + ATT_TAIL
    return pl.pallas_call(
        kern,
        grid=(n_batch, ng),
        in_specs=[smem, smem,
                  pl.BlockSpec((rows, Q_DIM), lambda b, c: (b * ng + c, COL_Q // Q_DIM)),
                  pl.BlockSpec((seq_len, KV_DIM), lambda b, c: (b, COL_K // KV_DIM)),
                  pl.BlockSpec((seq_len, KV_DIM), lambda b, c: (b, COL_V // KV_DIM)),
                  pl.BlockSpec((CHUNK, KEY_TILE), lambda b, c: (0, 0))],
        out_specs=pl.BlockSpec((rows, Q_DIM), lambda b, c: (b * ng + c, 0)),
        out_shape=jax.ShapeDtypeStruct((n_batch * seq_len, Q_DIM), BF16),
        scratch_shapes=[pltpu.VMEM((pad_rows, KV_DIM), BF16),
                        pltpu.VMEM((pad_rows, KV_DIM), BF16),
                        pltpu.VMEM((N_KV_HEADS, GQA_GROUP * CHUNK, KEY_TILE), F32)],
        compiler_params=_cparams(("arbitrary", "arbitrary")),
        name="attn_band",
    )(table, sinks, z, z, z, bkt)


def _attn_step_kernel(tbl_ref, sink_ref, q_ref, k_ref, v_ref, bkt_ref, o_ref, kbuf, vbuf, bias_scr,
                      *, n_keys, n_q):
    b = pl.program_id(0)

    @pl.when(b == 0)
    def _():
        _build_bias(tbl_ref, bkt_ref, bias_scr, n_q)
        kbuf[n_keys:, :] = jnp.zeros((KEY_TILE - n_keys, KV_DIM), BF16)
        vbuf[n_keys:, :] = jnp.zeros((KEY_TILE - n_keys, KV_DIM), BF16)

    kbuf[0:n_keys, :] = k_ref[0].astype(BF16)
    vbuf[0:n_keys, :] = v_ref[0].astype(BF16)

    def head_cols(h):
        return slice(h * HEAD_DIM, (h + 1) * HEAD_DIM)

    def store(h, val):
        o_ref[:, head_cols(h)] = val

    _attend(lambda h: q_ref[:, head_cols(h)].astype(BF16),
            lambda hk: kbuf[:, head_cols(hk)], lambda hk: vbuf[:, head_cols(hk)],
            bias_scr, _sink_columns(sink_ref, n_q), None, store, n_q)


def _attn_step(z, k_all, v_all, table, sinks, bkt, *, n_batch, n_q, row_block0):
    n_keys = k_all.shape[1]
    kern = functools.partial(_attn_step_kernel, n_keys=n_keys, n_q=n_q)
    smem = pl.BlockSpec(memory_space=pltpu.SMEM)
    return pl.pallas_call(
        kern,
        grid=(n_batch,),
        in_specs=[smem, smem,
                  pl.BlockSpec((n_q, Q_DIM), lambda b: (row_block0 + b, COL_Q // Q_DIM)),
                  pl.BlockSpec((1, n_keys, KV_DIM), lambda b: (b, 0, 0)),
                  pl.BlockSpec((1, n_keys, KV_DIM), lambda b: (b, 0, 0)),
                  pl.BlockSpec((n_q, KEY_TILE), lambda b: (0, 0))],
        out_specs=pl.BlockSpec((n_q, Q_DIM), lambda b: (b, 0)),
        out_shape=jax.ShapeDtypeStruct((n_batch * n_q, Q_DIM), BF16),
        scratch_shapes=[pltpu.VMEM((KEY_TILE, KV_DIM), BF16),
                        pltpu.VMEM((KEY_TILE, KV_DIM), BF16),
                        pltpu.VMEM((N_KV_HEADS, GQA_GROUP * n_q, KEY_TILE), F32)],
        compiler_params=_cparams(("arbitrary",)),
        name="attn_step",
    )(table, sinks, z, k_all, v_all, bkt)


def _t5_bucket(rel):
    nb = N_BUCKETS // 2
    ret = jnp.where(rel > 0, nb, 0)
    n = jnp.abs(rel)
    max_exact = nb // 2
    nf = jnp.maximum(n, 1).astype(jnp.float32)
    large = max_exact + (jnp.log(nf / max_exact) / math.log(MAX_DISTANCE / max_exact)
                         * (nb - max_exact)).astype(jnp.int32)
    large = jnp.minimum(large, nb - 1)
    return ret + jnp.where(n < max_exact, n, large)


def _bucket_map(q_pos, k_pos):
    bkt = _t5_bucket(k_pos[None, :] - q_pos[:, None]).astype(jnp.int32)
    return jnp.pad(bkt, ((0, 0), (0, KEY_TILE - k_pos.shape[0])), constant_values=-1)


ROUTE_LANES = LANES
GATE_TN = 1024
OUT_VMEM_LIMIT = 60 * 1024 * 1024


def _outproj_kernel(hgp_ref, hgs_ref, op_ref, os_ref, ga0_ref, ga1_ref, gb0_ref, gb1_ref, xp_ref, xs_ref,
                    g1_ref, sh2_ref, sc2_ref, ng_ref, wr_ref, wa_ref, wo_ref, wrt_ref, brt_ref,
                    x1_ref, n2_ref, rt_ref, *, n_prompt_tiles):
    tm = x1_ref.shape[0]
    is_prompt = pl.program_id(0) < n_prompt_tiles
    hg = jnp.where(is_prompt, hgp_ref[...], hgs_ref[...])
    o = jnp.where(is_prompt, op_ref[...], os_ref[...])
    x = jnp.where(is_prompt, xp_ref[...], xs_ref[...])
    ya = jnp.dot(hg, wr_ref[...], preferred_element_type=F32)
    yb = jnp.dot(o, wa_ref[...], preferred_element_type=F32)
    halves = []
    for k, (ga_ref, gb_ref) in enumerate(((ga0_ref, gb0_ref), (ga1_ref, gb1_ref))):
        cols = slice(k * GATE_TN, (k + 1) * GATE_TN)
        halves.append((_sigmoid(ga_ref[...]) * ya[:, cols] + _sigmoid(gb_ref[...]) * yb[:, cols]).astype(BF16))
    merged = jnp.concatenate(halves, axis=1)
    mix = jnp.dot(merged, wo_ref[...], preferred_element_type=F32)
    x1 = x + _bcast_rows(g1_ref[...], tm) * mix
    x1_ref[...] = x1
    n2 = _rms_modulate(x1, ng_ref[...], sc2_ref[...], sh2_ref[...])
    n2_ref[...] = n2

    lg = jnp.dot(n2.astype(BF16), wrt_ref[...].astype(BF16), preferred_element_type=F32) + brt_ref[...]

    lane = lax.broadcasted_iota(jnp.int32, (tm, ROUTE_LANES), 1)
    lane_f = lane.astype(F32)
    e_lane = lane - N_GROUPS
    lane_group = (e_lane >> 3).astype(F32)

    def first_argmax(vals, vmax):
        return jnp.min(jnp.where(vals == vmax, lane_f, float(ROUTE_LANES)), axis=-1, keepdims=True)

    gl = jnp.where(lane < N_GROUPS, lg, NEG_INF)
    gmax = jnp.max(gl, axis=-1, keepdims=True)
    g_idx = first_argmax(gl, gmax)
    g_w = 1.0 / jnp.sum(jnp.exp(gl - gmax), axis=-1, keepdims=True)
    in_group = jnp.where((e_lane >= 0) & (e_lane < N_EXPERTS), lane_group, -1.0) == g_idx
    el = jnp.where(in_group, lg, NEG_INF)
    v1 = jnp.max(el, axis=-1, keepdims=True)
    i1 = first_argmax(el, v1)
    el2 = jnp.where(lane_f == i1, NEG_INF, el)
    v2 = jnp.max(el2, axis=-1, keepdims=True)
    i2 = first_argmax(el2, v2)
    e21 = jnp.exp(v2 - v1)
    w1 = g_w / (1.0 + e21)
    w2 = g_w * e21 / (1.0 + e21)
    e1 = i1 - float(N_GROUPS)
    e2 = i2 - float(N_GROUPS)
    rt_ref[...] = jnp.where(lane == 0, e1, jnp.where(lane == 1, e2, jnp.where(lane == 2, w1,
                            jnp.where(lane == 3, w2, 0.0))))


def _outproj(hg_p, hg_s, o_p, o_s, z, x_p, x_s, g1, sh2, sc2, norm_g, wr, wa, wo, w_route, b_route):
    tm = ROW_TILE
    t = z.shape[0]
    seg = tm // MOD_ROWS
    n_p = x_p.shape[0] // tm
    row = lambda i: (i, 0)
    fix = lambda i: (0, 0)
    row_p = lambda i: (jnp.minimum(i, n_p - 1), 0)
    row_s = lambda i: (jnp.maximum(i - n_p, 0), 0)
    once = pl.Buffered(1)

    def gate(col):
        return pl.BlockSpec((tm, GATE_TN), lambda i: (i, col // GATE_TN))

    return pl.pallas_call(
        functools.partial(_outproj_kernel, n_prompt_tiles=n_p),
        grid=(t // tm,),
        in_specs=[pl.BlockSpec((tm, D_RNN), row_p), pl.BlockSpec((tm, D_RNN), row_s, pipeline_mode=once),
                  pl.BlockSpec((tm, Q_DIM), row_p), pl.BlockSpec((tm, Q_DIM), row_s, pipeline_mode=once),
                  gate(COL_GA), gate(COL_GA + GATE_TN), gate(COL_GB), gate(COL_GB + GATE_TN),
                  pl.BlockSpec((tm, D_MODEL), row_p), pl.BlockSpec((tm, D_MODEL), row_s, pipeline_mode=once),
                  pl.BlockSpec((seg, D_MODEL), row), pl.BlockSpec((seg, D_MODEL), row),
                  pl.BlockSpec((seg, D_MODEL), row),
                  pl.BlockSpec((1, D_MODEL), fix),
                  pl.BlockSpec((D_RNN, D_MODEL), fix, pipeline_mode=once),
                  pl.BlockSpec((Q_DIM, D_MODEL), fix, pipeline_mode=once),
                  pl.BlockSpec((D_MODEL, D_MODEL), fix, pipeline_mode=once),
                  pl.BlockSpec((D_MODEL, ROUTE_LANES), fix),
                  pl.BlockSpec((1, ROUTE_LANES), fix)],
        out_specs=[pl.BlockSpec((tm, D_MODEL), row), pl.BlockSpec((tm, D_MODEL), row),
                   pl.BlockSpec((tm, ROUTE_LANES), row)],
        out_shape=[jax.ShapeDtypeStruct((t, D_MODEL), F32), jax.ShapeDtypeStruct((t, D_MODEL), F32),
                   jax.ShapeDtypeStruct((t, ROUTE_LANES), F32)],
        compiler_params=_cparams(("arbitrary",), OUT_VMEM_LIMIT),
        name="outproj",
    )(hg_p, hg_s, o_p, o_s, z, z, z, z, x_p, x_s, g1, sh2, sc2, norm_g.reshape(1, D_MODEL),
      wr, wa, wo, w_route, b_route)


def _row_gather(src_hbm, idx_ref, base, dst, sem, n_rows):
    for r in range(n_rows):
        tok = idx_ref[base + r]
        pltpu.make_async_copy(src_hbm.at[pl.ds(tok, 1), :], dst.at[pl.ds(r, 1), :], sem).start()


def _row_gather_wait(src_hbm, dst, sem, n_rows):
    pltpu.make_async_copy(src_hbm.at[pl.ds(0, n_rows), :], dst, sem).wait()


def _moe_kernel(te_ref, tfirst_ref, tvalid_ref, tnext_ref, tbase_ref, src_ref, n2_hbm, wg_hbm, wu_hbm, wd_hbm, y_ref,
                xbuf, xsem, wg_st, wu_st, wd_st, wsem, wg_bf, wu_bf, wd_bf):
    i = pl.program_id(0)
    n_tiles = pl.num_programs(0)
    slot = lax.rem(i, MOE_RING)
    stages = ((wg_hbm, wg_st, wg_bf), (wu_hbm, wu_st, wu_bf), (wd_hbm, wd_st, wd_bf))

    def weight_copy(k, e):
        hbm, st, _ = stages[k]
        return pltpu.make_async_copy(hbm.at[e], st, wsem.at[k])

    def gather_tile(tile):
        tc = jnp.minimum(tile, n_tiles - 1)

        @pl.when((tile < n_tiles) & (tvalid_ref[tc] == 1))
        def _():
            s = lax.rem(tile, MOE_RING)
            _row_gather(n2_hbm, src_ref, tbase_ref[tc], xbuf.at[s], xsem.at[s], MOE_TM)

    @pl.when(i == 0)
    def _():
        for k in range(len(stages)):
            weight_copy(k, te_ref[0]).start(priority=1)
        for ahead in range(MOE_RING - 1):
            gather_tile(ahead)

    @pl.when(tvalid_ref[i] == 1)
    def _():
        gather_tile(i + MOE_RING - 1)

        @pl.when(tfirst_ref[i] == 1)
        def _():
            for k, (_, st, bf) in enumerate(stages):
                weight_copy(k, te_ref[i]).wait()
                bf[...] = st[...].astype(BF16)

            @pl.when(tnext_ref[i] >= 0)
            def _():
                for k in range(len(stages)):
                    weight_copy(k, tnext_ref[i]).start(priority=1)

        _row_gather_wait(n2_hbm, xbuf.at[slot], xsem.at[slot], MOE_TM)
        x = xbuf[slot].astype(BF16)
        hgate = jnp.dot(x, wg_bf[...], preferred_element_type=F32)
        hup = jnp.dot(x, wu_bf[...], preferred_element_type=F32)
        act = (hgate * _sigmoid(hgate) * hup).astype(BF16)
        y_ref[...] = jnp.dot(act, wd_bf[...], preferred_element_type=F32)

    @pl.when(tvalid_ref[i] == 0)
    def _():
        y_ref[...] = jnp.zeros(y_ref.shape, F32)


def _moe(n2, w_gate, w_up, w_down, tile_expert, tile_first, tile_valid, tile_next, tile_base, src_tok):
    n_tiles = tile_expert.shape[0]
    hbm = pl.BlockSpec(memory_space=pl.ANY)
    grid_spec = pltpu.PrefetchScalarGridSpec(
        num_scalar_prefetch=6,
        grid=(n_tiles,),
        in_specs=[hbm, hbm, hbm, hbm],
        out_specs=pl.BlockSpec((MOE_TM, D_MODEL), lambda i, te, tf, tv, tn, tb, st: (i, 0)),
        scratch_shapes=[pltpu.VMEM((MOE_RING, MOE_TM, D_MODEL), F32),
                        pltpu.SemaphoreType.DMA((MOE_RING,)),
                        pltpu.VMEM((D_MODEL, D_EXPERT), F32),
                        pltpu.VMEM((D_MODEL, D_EXPERT), F32),
                        pltpu.VMEM((D_EXPERT, D_MODEL), F32),
                        pltpu.SemaphoreType.DMA((3,)),
                        pltpu.VMEM((D_MODEL, D_EXPERT), BF16),
                        pltpu.VMEM((D_MODEL, D_EXPERT), BF16),
                        pltpu.VMEM((D_EXPERT, D_MODEL), BF16)],
    )
    return pl.pallas_call(
        _moe_kernel,
        grid_spec=grid_spec,
        out_shape=jax.ShapeDtypeStruct((n_tiles * MOE_TM, D_MODEL), F32),
        compiler_params=_cparams(("arbitrary",)),
        name="moe",
    )(tile_expert, tile_first, tile_valid, tile_next, tile_base, src_tok, n2, w_gate, w_up, w_down)


def _route_plan(e1, e2, n_tok):
    experts = jnp.arange(N_EXPERTS, dtype=jnp.int32)
    flat_e = jnp.concatenate([e1, e2])
    onehot = (flat_e[:, None] == experts[None, :]).astype(jnp.int32)
    csum = jnp.cumsum(onehot, axis=0)
    rank = jnp.sum(csum * onehot, axis=1) - 1
    counts = csum[-1]
    tiles_per = (counts + MOE_TM - 1) // MOE_TM
    tile_end = jnp.cumsum(tiles_per)
    tile_off = tile_end - tiles_per
    slot = jnp.sum(onehot * tile_off[None, :], axis=1) * MOE_TM + rank
    n_tiles = (2 * n_tok) // MOE_TM + N_EXPERTS
    tok = jnp.tile(jnp.arange(n_tok, dtype=jnp.int32), 2)
    _, src_tok = lax.sort((slot, tok), num_keys=1)
    src_tok = jnp.concatenate([src_tok, jnp.zeros((MOE_TM,), jnp.int32)])
    count_off = jnp.cumsum(counts) - counts
    tile_id = jnp.arange(n_tiles, dtype=jnp.int32)
    n_used = tile_end[-1]
    tile_valid = (tile_id < n_used).astype(jnp.int32)
    te = jnp.sum((tile_end[None, :] <= jnp.minimum(tile_id, n_used - 1)[:, None]).astype(jnp.int32), axis=1)
    tile_expert = jnp.minimum(te, N_EXPERTS - 1)
    prev = jnp.concatenate([jnp.full((1,), -1, jnp.int32), tile_expert[:-1]])
    tile_first = (tile_expert != prev).astype(jnp.int32)
    next_tile = tile_end[tile_expert]
    tile_next = jnp.where(next_tile < n_used, tile_expert[jnp.minimum(next_tile, n_tiles - 1)], -1)
    tile_base = jnp.where(tile_valid == 1, count_off[tile_expert] + (tile_id - tile_off[tile_expert]) * MOE_TM, 0)
    return tile_expert, tile_first, tile_valid, tile_next, tile_base, src_tok, slot[:n_tok], slot[n_tok:]


def _final_kernel(p1_ref, p2_ref, ys_hbm, x1_ref, g2_ref, rt_ref, fg_ref, yp_ref, ysm_ref, ybuf, sem,
                  *, n_prompt_tiles):
    i = pl.program_id(0)
    n_tiles = pl.num_programs(0)
    slot = i % 2

    def start(tile, s):
        _row_gather(ys_hbm, p1_ref, tile * ROW_TILE, ybuf.at[s, 0], sem.at[s], ROW_TILE)
        _row_gather(ys_hbm, p2_ref, tile * ROW_TILE, ybuf.at[s, 1], sem.at[s], ROW_TILE)

    @pl.when(i == 0)
    def _():
        start(0, 0)

    @pl.when(i + 1 < n_tiles)
    def _():
        start(i + 1, 1 - slot)

    _row_gather_wait(ys_hbm, ybuf.at[slot, 0], sem.at[slot], ROW_TILE)
    _row_gather_wait(ys_hbm, ybuf.at[slot, 1], sem.at[slot], ROW_TILE)
    rt = rt_ref[...]
    moe = rt[:, 2:3] * ybuf[slot, 0] + rt[:, 3:4] * ybuf[slot, 1]
    x2 = x1_ref[...] + _bcast_rows(g2_ref[...], ROW_TILE) * moe
    y = x2 * lax.rsqrt(jnp.mean(x2 * x2, axis=-1, keepdims=True) + EPS) * fg_ref[...]

    @pl.when(i < n_prompt_tiles)
    def _():
        yp_ref[...] = y

    @pl.when(i >= n_prompt_tiles)
    def _():
        ysm_ref[...] = y


def _final(ys, x1, g2, rt, final_g, p1, p2, n_prompt_rows):
    t = x1.shape[0]
    tm = ROW_TILE
    seg = tm // MOD_ROWS
    n_p = n_prompt_rows // tm
    grid_spec = pltpu.PrefetchScalarGridSpec(
        num_scalar_prefetch=2,
        grid=(t // tm,),
        in_specs=[pl.BlockSpec(memory_space=pl.ANY),
                  pl.BlockSpec((tm, D_MODEL), lambda i, a, b: (i, 0)),
                  pl.BlockSpec((seg, D_MODEL), lambda i, a, b: (i, 0)),
                  pl.BlockSpec((tm, ROUTE_LANES), lambda i, a, b: (i, 0)),
                  pl.BlockSpec((1, D_MODEL), lambda i, a, b: (0, 0))],
        out_specs=[pl.BlockSpec((tm, D_MODEL), lambda i, a, b: (jnp.minimum(i, n_p - 1), 0)),
                   pl.BlockSpec((tm, D_MODEL), lambda i, a, b: (jnp.maximum(i - n_p, 0), 0))],
        scratch_shapes=[pltpu.VMEM((2, 2, tm, D_MODEL), F32),
                        pltpu.SemaphoreType.DMA((2,))],
    )
    return pl.pallas_call(
        functools.partial(_final_kernel, n_prompt_tiles=n_p),
        grid_spec=grid_spec,
        out_shape=[jax.ShapeDtypeStruct((n_prompt_rows, D_MODEL), F32),
                   jax.ShapeDtypeStruct((t - n_prompt_rows, D_MODEL), F32)],
        compiler_params=_cparams(("arbitrary",)),
        name="final",
    )(p1, p2, ys, x1, g2, rt, final_g.reshape(1, D_MODEL))


def kernel(x_prompt, x_sample, cache_k_win, cache_v_win, state_conv, state_rglru, c_prompt, c_sample, w_ada, b_ada, norm1_g, norm2_g, w_in, b_in, w_conv, b_conv, w_rg_a, b_rg_a, w_rg_x, b_rg_x, lru_lambda, w_rnn_out, w_attn_out, w_out, attn_sinks, w_route_group, b_route_group, w_route_expert, b_route_expert, w_exp_gate, w_exp_up, w_exp_down, rel_bias_table, final_norm_g):
    n_b, seq, _ = x_prompt.shape
    d_b, d_seq, _ = x_sample.shape
    assert w_ada.shape[0] == 1, "single trunk layer"
    assert seq % (ATT_CPB * CHUNK) == 0 and seq % MOD_ROWS == 0 and d_seq == MOD_ROWS
    assert d_seq >= CONV_W - 1 and d_b == N_CHAINS
    t_p, t_s = n_b * seq, d_b * d_seq
    t = t_p + t_s
    assert t_s == ROW_TILE and t_p % ROW_TILE == 0
    cw = cache_k_win.shape[2]
    l = 0

    x_p = x_prompt.reshape(t_p, D_MODEL)
    x_s = x_sample.reshape(t_s, D_MODEL)

    n_c = n_b + d_b
    c_all = jnp.pad(jnp.concatenate([c_prompt, c_sample], axis=0), ((0, -n_c % SUBLANES), (0, 0)))
    mod = _ada(c_all, w_ada[l], b_ada[l])

    def per_segment(m):
        return jnp.concatenate([jnp.repeat(m[:n_b], seq // MOD_ROWS, axis=0),
                                jnp.repeat(m[n_b:n_c], d_seq // MOD_ROWS, axis=0)], axis=0)

    sh1, sc1, g1, sh2, sc2, g2 = [per_segment(m) for m in jnp.split(mod, 6, axis=-1)]

    z = _inproj(_norm1(x_p, x_s, sc1, sh1, norm1_g[l]), w_in[l], b_in[l])

    w_gate = jnp.concatenate([w_rg_a[l], w_rg_x[l]], axis=-1)
    b_gate = jnp.concatenate([b_rg_a[l], b_rg_x[l]], axis=-1)[:, None, :]
    hg_p, hl_p = _rglru(z, jnp.zeros((n_b, SUBLANES, D_RNN), F32), jnp.zeros((n_b, N_CHAINS, D_RNN), F32),
                        w_conv[l], b_conv[l], w_gate, b_gate, lru_lambda[l],
                        n_blocks=n_b, n_seq=1, seq_len=seq, row_block0=0, link=True)
    state8 = jnp.pad(state_conv[l], ((0, 0), (SUBLANES - (CONV_W - 1), 0), (0, 0)))
    hg_s, hl_s = _rglru(z, state8, state_rglru[l][None], w_conv[l], b_conv[l], w_gate, b_gate, lru_lambda[l],
                        n_blocks=1, n_seq=d_b, seq_len=d_seq, row_block0=t_p // t_s, link=False)

    sinks = attn_sinks[l]
    bkt_p = _bucket_map(WIN_CHUNKS * CHUNK + jnp.arange(CHUNK), jnp.arange(SPAN))
    o_p = _attn_band(z, rel_bias_table, sinks, bkt_p, n_batch=n_b, seq_len=seq)
    k_new = z[t_p:, COL_K:COL_K + KV_DIM].reshape(d_b, d_seq, KV_DIM)
    v_new = z[t_p:, COL_V:COL_V + KV_DIM].reshape(d_b, d_seq, KV_DIM)
    k_all = jnp.concatenate([cache_k_win[l].reshape(d_b, cw, KV_DIM), k_new], axis=1)
    v_all = jnp.concatenate([cache_v_win[l].reshape(d_b, cw, KV_DIM), v_new], axis=1)
    bkt_s = _bucket_map(cw + jnp.arange(d_seq), jnp.arange(cw + d_seq))
    o_s = _attn_step(z, k_all, v_all, rel_bias_table, sinks, bkt_s, n_batch=d_b, n_q=d_seq, row_block0=t_p // d_seq)

    n_route = N_GROUPS + N_EXPERTS
    w_route = jnp.pad(jnp.concatenate([w_route_group[l], w_route_expert[l]], axis=1),
                      ((0, 0), (0, ROUTE_LANES - n_route)))
    b_route = jnp.pad(jnp.concatenate([b_route_group[l], b_route_expert[l]]),
                      (0, ROUTE_LANES - n_route)).reshape(1, ROUTE_LANES)
    x1, n2, rt = _outproj(hg_p, hg_s, o_p, o_s, z, x_p, x_s, g1, sh2, sc2, norm2_g[l],
                          w_rnn_out[l].astype(BF16), w_attn_out[l].astype(BF16), w_out[l].astype(BF16),
                          w_route, b_route)

    e1 = rt[:, 0].astype(jnp.int32)
    e2 = rt[:, 1].astype(jnp.int32)
    tile_expert, tile_first, tile_valid, tile_next, tile_base, src_tok, p1, p2 = _route_plan(e1, e2, t)
    ys = _moe(n2, w_exp_gate[l], w_exp_up[l], w_exp_down[l], tile_expert, tile_first, tile_valid, tile_next,
              tile_base, src_tok)
    y_p, y_s = _final(ys, x1, g2, rt, final_norm_g, p1, p2, t_p)

    win = min(WINDOW, seq)

    def tail(col0, width, n_rows):
        return jnp.stack([z[(b + 1) * seq - n_rows:(b + 1) * seq, col0:col0 + width] for b in range(n_b)])

    kp = tail(COL_K, KV_DIM, win).reshape(n_b, win, N_KV_HEADS, HEAD_DIM)
    vp = tail(COL_V, KV_DIM, win).reshape(n_b, win, N_KV_HEADS, HEAD_DIM)
    cp = tail(COL_XR, D_RNN, CONV_W - 1)
    rp = hl_p[:, N_CHAINS - 1, :]
    ks = k_all[:, -cw:].reshape(d_b, cw, N_KV_HEADS, HEAD_DIM)
    vs = v_all[:, -cw:].reshape(d_b, cw, N_KV_HEADS, HEAD_DIM)
    cs = z[t_p:, COL_XR:COL_XR + D_RNN].reshape(d_b, d_seq, D_RNN)[:, -(CONV_W - 1):]
    rs = hl_s[0]
    return (y_p.reshape(n_b, seq, D_MODEL), y_s.reshape(d_b, d_seq, D_MODEL),
            kp[None], vp[None], cp[None], rp[None], ks[None], vs[None], cs[None], rs[None])
```

```python
import functools
import math

import jax
import jax.numpy as jnp
from jax import lax
from jax.experimental import pallas as pl
from jax.experimental.pallas import tpu as pltpu

F32 = jnp.float32
BF16 = jnp.bfloat16

D_MODEL = 2048
D_RNN = 2048
RNN_BLOCK = 128
CONV_W = 4
LRU_C = 8.0
N_HEADS = 16
N_KV_HEADS = 4
HEAD_DIM = 128
GQA_GROUP = N_HEADS // N_KV_HEADS
Q_DIM = N_HEADS * HEAD_DIM
KV_DIM = N_KV_HEADS * HEAD_DIM
CHUNK = 64
WINDOW = 128
WIN_CHUNKS = WINDOW // CHUNK
SPAN = (WIN_CHUNKS + 1) * CHUNK
N_BUCKETS = 32
MAX_DISTANCE = 128
N_GROUPS = 4
E_PER_GROUP = 8
N_EXPERTS = N_GROUPS * E_PER_GROUP
D_EXPERT = 512
EPS = 1e-6
NEG_INF = -1e30
D_IN = 2 * D_RNN + Q_DIM + 2 * KV_DIM + 2 * D_MODEL
COL_XR, COL_GR, COL_Q = 0, D_RNN, 2 * D_RNN
COL_K = COL_Q + Q_DIM
COL_V = COL_K + KV_DIM
COL_GA = COL_V + KV_DIM
COL_GB = COL_GA + D_MODEL

LANES = 128
SUBLANES = 8
MOD_ROWS = 32
KEY_TILE = 256
VMEM_LIMIT = 56 * 1024 * 1024

ROW_TILE = 256
IN_TM_MAX = 1408
IN_TN = 1024
MOE_TM = 192
MOE_RING = 3


def _sigmoid(x):
    return 0.5 * jnp.tanh(0.5 * x) + 0.5


def _gelu_tanh(x):
    return 0.5 * x * (1.0 + jnp.tanh(math.sqrt(2.0 / math.pi) * (x + 0.044715 * (x * x * x))))


def _bcast_rows(v, rows):
    n, d = v.shape
    return jnp.broadcast_to(v[:, None, :], (n, MOD_ROWS, d)).reshape(rows, d)


def _rms_modulate(x, gain, scale_seg, shift_seg):
    rows = x.shape[0]
    y = x * lax.rsqrt(jnp.mean(x * x, axis=-1, keepdims=True) + EPS) * gain
    return y * (1.0 + _bcast_rows(scale_seg, rows)) + _bcast_rows(shift_seg, rows)


def _cparams(sem, vmem_limit=VMEM_LIMIT):
    return pltpu.CompilerParams(dimension_semantics=sem, vmem_limit_bytes=vmem_limit)


def _ada_kernel(c_ref, w_ref, b_ref, o_ref):
    c = c_ref[...]
    s = (c * _sigmoid(c)).astype(BF16)
    o_ref[...] = jnp.dot(s, w_ref[...].astype(BF16), preferred_element_type=F32) + b_ref[...]


def _ada(c_all, w_ada, b_ada):
    rows = c_all.shape[0]
    n = w_ada.shape[1]
    tn = 1024
    return pl.pallas_call(
        _ada_kernel,
        grid=(n // tn,),
        in_specs=[pl.BlockSpec((rows, D_MODEL), lambda j: (0, 0)),
                  pl.BlockSpec((D_MODEL, tn), lambda j: (0, j)),
                  pl.BlockSpec((1, tn), lambda j: (0, j))],
        out_specs=pl.BlockSpec((rows, tn), lambda j: (0, j)),
        out_shape=jax.ShapeDtypeStruct((rows, n), F32),
        compiler_params=_cparams(("arbitrary",)),
        name="ada",
    )(c_all, w_ada, b_ada.reshape(1, n))


def _norm1_kernel(xp_ref, xs_ref, sc_ref, sh_ref, g_ref, n1_ref, *, n_prompt_tiles):
    is_prompt = pl.program_id(0) < n_prompt_tiles
    gain = g_ref[...]

    def segment(s, carry):
        rows = pl.ds(pl.multiple_of(s * MOD_ROWS, MOD_ROWS), MOD_ROWS)
        x = jnp.where(is_prompt, xp_ref[rows, :], xs_ref[rows, :])
        y = x * lax.rsqrt(jnp.mean(x * x, axis=-1, keepdims=True) + EPS) * gain
        n1_ref[rows, :] = (y * (1.0 + sc_ref[pl.ds(s, 1), :]) + sh_ref[pl.ds(s, 1), :]).astype(BF16)
        return carry

    lax.fori_loop(0, n1_ref.shape[0] // MOD_ROWS, segment, 0, unroll=True)


def _norm1(x_p, x_s, sc1, sh1, norm_g):
    tm = ROW_TILE
    n_p = x_p.shape[0] // tm
    t = x_p.shape[0] + x_s.shape[0]
    seg = tm // MOD_ROWS
    return pl.pallas_call(
        functools.partial(_norm1_kernel, n_prompt_tiles=n_p),
        grid=(t // tm,),
        in_specs=[pl.BlockSpec((tm, D_MODEL), lambda i: (jnp.minimum(i, n_p - 1), 0)),
                  pl.BlockSpec((tm, D_MODEL), lambda i: (jnp.maximum(i - n_p, 0), 0)),
                  pl.BlockSpec((seg, D_MODEL), lambda i: (i, 0)),
                  pl.BlockSpec((seg, D_MODEL), lambda i: (i, 0)),
                  pl.BlockSpec((1, D_MODEL), lambda i: (0, 0))],
        out_specs=pl.BlockSpec((tm, D_MODEL), lambda i: (i, 0)),
        out_shape=jax.ShapeDtypeStruct((t, D_MODEL), BF16),
        compiler_params=_cparams(("arbitrary",)),
        name="norm1",
    )(x_p, x_s, sc1, sh1, norm_g.reshape(1, D_MODEL))


def _inproj_kernel(n1_ref, w_ref, b_ref, z_ref, w_bf):
    @pl.when(pl.program_id(1) == 0)
    def _():
        w_bf[...] = w_ref[...].astype(BF16)

    z_ref[...] = jnp.dot(n1_ref[...], w_bf[...], preferred_element_type=F32) + b_ref[...]


def _inproj(n1, w_in, b_in):
    t = n1.shape[0]
    tm = next(m for m in range(IN_TM_MAX, 0, -LANES) if t % m == 0)
    return pl.pallas_call(
        _inproj_kernel,
        grid=(D_IN // IN_TN, t // tm),
        in_specs=[pl.BlockSpec((tm, D_MODEL), lambda j, i: (i, 0)),
                  pl.BlockSpec((D_MODEL, IN_TN), lambda j, i: (0, j)),
                  pl.BlockSpec((1, IN_TN), lambda j, i: (0, j))],
        out_specs=pl.BlockSpec((tm, IN_TN), lambda j, i: (i, j)),
        out_shape=jax.ShapeDtypeStruct((t, D_IN), F32),
        scratch_shapes=[pltpu.VMEM((D_MODEL, IN_TN), BF16)],
        compiler_params=_cparams(("arbitrary", "arbitrary")),
        name="inproj",
    )(n1, w_in, b_in.reshape(1, D_IN))


RG_CW = 256
RG_SLABS = RG_CW // LANES
N_CHAINS = SUBLANES
CHAIN_PAD = 8


def _rglru_kernel(xr_ref, gr_ref, st_ref, h0_ref, wc_ref, bc_ref, wg_ref, bg_ref, lam_ref,
                  hg_ref, hl_ref, ext_scr, a_scr, u_scr, *, n_seq, seq_len, link):
    rows = n_seq * seq_len
    cl = rows // N_CHAINS
    pitch = cl + CHAIN_PAD

    for s in range(n_seq):
        ext_scr[s, 0:SUBLANES, :] = st_ref[s]
        ext_scr[s, SUBLANES:SUBLANES + seq_len, :] = xr_ref[s * seq_len:(s + 1) * seq_len, :]

    z = -lam_ref[...]
    softplus = jnp.maximum(z, 0.0) + jnp.log1p(jnp.exp(-jnp.abs(z)))
    nq = (-0.25 * LRU_C) * softplus

    for c in range(N_CHAINS):
        s, r0 = divmod(c * cl, seq_len)
        xc = bc_ref[...]
        for j in range(CONV_W):
            xc = xc + wc_ref[j:j + 1, :] * ext_scr[s, pl.ds(SUBLANES - (CONV_W - 1) + j + r0, cl), :]
        for sl in range(RG_SLABS):
            lanes = slice(sl * LANES, (sl + 1) * LANES)
            xb = xc[:, lanes]
            g = jnp.dot(xb.astype(BF16), wg_ref[sl].astype(BF16), preferred_element_type=F32) + bg_ref[sl]
            tr = jnp.tanh(0.5 * g[:, :LANES])
            ti = jnp.tanh(0.5 * g[:, LANES:])
            th = jnp.tanh(nq[:, lanes] * (tr + 1.0))
            rcp = 1.0 / (1.0 - th)
            a_scr[sl, c * pitch:c * pitch + cl, :] = (1.0 + th) * rcp
            u_scr[sl, c * pitch:c * pitch + cl, :] = (jnp.sqrt(-th) * rcp) * ((ti + 1.0) * xb)

    def step(t, carry):
        hs, ps = carry
        new_h, new_p = [], []
        for sl in range(RG_SLABS):
            a = a_scr[sl, pl.ds(t, N_CHAINS, stride=pitch), :]
            u = u_scr[sl, pl.ds(t, N_CHAINS, stride=pitch), :]
            h = a * hs[sl] + u
            u_scr[sl, pl.ds(t, N_CHAINS, stride=pitch), :] = h
            new_h.append(h)
            if link:
                p = a * ps[sl]
                a_scr[sl, pl.ds(t, N_CHAINS, stride=pitch), :] = p
                new_p.append(p)
            else:
                new_p.append(ps[sl])
        return tuple(new_h), tuple(new_p)

    h_init = tuple(h0_ref[0, :, sl * LANES:(sl + 1) * LANES] for sl in range(RG_SLABS))
    p_init = tuple(jnp.ones((N_CHAINS, LANES), F32) for _ in range(RG_SLABS))
    h_end, p_end = lax.fori_loop(0, cl, step, (h_init, p_init), unroll=8)

    row = lax.broadcasted_iota(jnp.int32, (N_CHAINS, LANES), 0)
    for sl in range(RG_SLABS):
        lanes = slice(sl * LANES, (sl + 1) * LANES)
        if link:
            def shift_down(v):
                return jnp.where(row == 0, 0.0, pltpu.roll(v, 1, axis=0))
            hh = h_end[sl]
            for _ in range(N_CHAINS - 1):
                hh = h_end[sl] + p_end[sl] * shift_down(hh)
            carry_in = shift_down(hh)
        else:
            hh = h_end[sl]
        hl_ref[0, :, lanes] = hh
        for c in range(N_CHAINS):
            h = u_scr[sl, c * pitch:c * pitch + cl, :]
            if link:
                h = h + a_scr[sl, c * pitch:c * pitch + cl, :] * carry_in[c:c + 1, :]
            gg = _gelu_tanh(gr_ref[c * cl:(c + 1) * cl, lanes])
            hg_ref[c * cl:(c + 1) * cl, lanes] = (h * gg).astype(BF16)


def _rglru(z, state8, h0, w_conv, b_conv, w_gate, b_gate, lam, *, n_blocks, n_seq, seq_len, row_block0, link):
    rows = n_seq * seq_len
    cl = rows // N_CHAINS
    ncb = D_RNN // RG_CW
    gr0 = COL_GR // RG_CW
    kern = functools.partial(_rglru_kernel, n_seq=n_seq, seq_len=seq_len, link=link)
    return pl.pallas_call(
        kern,
        grid=(n_blocks, ncb),
        in_specs=[pl.BlockSpec((rows, RG_CW), lambda b, n: (row_block0 + b, n)),
                  pl.BlockSpec((rows, RG_CW), lambda b, n: (row_block0 + b, gr0 + n)),
                  pl.BlockSpec((n_seq, SUBLANES, RG_CW), lambda b, n: (b, 0, n)),
                  pl.BlockSpec((1, N_CHAINS, RG_CW), lambda b, n: (b, 0, n)),
                  pl.BlockSpec((CONV_W, RG_CW), lambda b, n: (0, n)),
                  pl.BlockSpec((1, RG_CW), lambda b, n: (0, n)),
                  pl.BlockSpec((RG_SLABS, RNN_BLOCK, 2 * RNN_BLOCK), lambda b, n: (n, 0, 0)),
                  pl.BlockSpec((RG_SLABS, 1, 2 * RNN_BLOCK), lambda b, n: (n, 0, 0)),
                  pl.BlockSpec((1, RG_CW), lambda b, n: (0, n))],
        out_specs=[pl.BlockSpec((rows, RG_CW), lambda b, n: (b, n)),
                   pl.BlockSpec((1, N_CHAINS, RG_CW), lambda b, n: (b, 0, n))],
        out_shape=[jax.ShapeDtypeStruct((n_blocks * rows, D_RNN), BF16),
                   jax.ShapeDtypeStruct((n_blocks, N_CHAINS, D_RNN), F32)],
        scratch_shapes=[pltpu.VMEM((n_seq, SUBLANES + seq_len, RG_CW), F32),
                        pltpu.VMEM((RG_SLABS, N_CHAINS * (cl + CHAIN_PAD), LANES), F32),
                        pltpu.VMEM((RG_SLABS, N_CHAINS * (cl + CHAIN_PAD), LANES), F32)],
        compiler_params=_cparams(("arbitrary", "arbitrary")),
        name="rglru_link" if link else "rglru_step",
    )(z, z, state8, h0, w_conv, b_conv.reshape(1, D_RNN), w_gate, b_gate, lam.reshape(1, D_RNN))


def _build_bias(tbl_ref, bkt_ref, bias_scr, n_q):
    bkt = bkt_ref[...]
    base = jnp.where(bkt < 0, NEG_INF, 0.0)
    for h in range(N_HEADS):
        acc = base
        for bk in range(N_BUCKETS):
            acc = jnp.where(bkt == bk, tbl_ref[bk, h], acc)
        hk, g = divmod(h, GQA_GROUP)
        bias_scr[hk, g * n_q:(g + 1) * n_q, :] = acc


def _sink_columns(sink_ref, n_q):
    row = lax.broadcasted_iota(jnp.int32, (GQA_GROUP * n_q, 1), 0)
    cols = []
    for hk in range(N_KV_HEADS):
        col = jnp.full((GQA_GROUP * n_q, 1), sink_ref[hk * GQA_GROUP], F32)
        for g in range(1, GQA_GROUP):
            col = jnp.where(row >= g * n_q, sink_ref[hk * GQA_GROUP + g], col)
        cols.append(col)
    return cols


def _attend(q_of, k_of, v_of, bias_scr, sinks, key_ok, store, n_q):
    scores = []
    for hk in range(N_KV_HEADS):
        qg = jnp.concatenate([q_of(hk * GQA_GROUP + g) for g in range(GQA_GROUP)], axis=0)
        s = lax.dot_general(qg, k_of(hk), (((1,), (1,)), ((), ())), preferred_element_type=F32)
        s = s * (HEAD_DIM ** -0.5) + bias_scr[hk]
        if key_ok is not None:
            s = jnp.where(key_ok, s, NEG_INF)
        scores.append(s)
    probs = []
    for hk in range(N_KV_HEADS):
        s, sink = scores[hk], sinks[hk]
        m = jnp.maximum(jnp.max(s, axis=-1, keepdims=True), sink)
        p = jnp.exp(s - m)
        inv = 1.0 / (jnp.sum(p, axis=-1, keepdims=True) + jnp.exp(sink - m))
        probs.append((p * inv).astype(BF16))
    for hk in range(N_KV_HEADS):
        o = jnp.dot(probs[hk], v_of(hk), preferred_element_type=F32)
        for g in range(GQA_GROUP):
            store(hk * GQA_GROUP + g, o[g * n_q:(g + 1) * n_q, :].astype(BF16))


ATT_CPB = 8
ATT_LEAD = WIN_CHUNKS * CHUNK
ATT_TAIL = KEY_TILE - SPAN


def _attn_band_kernel(tbl_ref, sink_ref, q_ref, k_ref, v_ref, bkt_ref, o_ref, kpad, vpad, bias_scr, *, seq_len):
    b = pl.program_id(0)
    cg = pl.program_id(1)

    @pl.when((b == 0) & (cg == 0))
    def _():
        _build_bias(tbl_ref, bkt_ref, bias_scr, CHUNK)

    @pl.when(cg == 0)
    def _():
        for ref, pad in ((k_ref, kpad), (v_ref, vpad)):
            pad[0:ATT_LEAD, :] = jnp.zeros((ATT_LEAD, KV_DIM), BF16)
            pad[ATT_LEAD:ATT_LEAD + seq_len, :] = ref[...].astype(BF16)
            pad[ATT_LEAD + seq_len:, :] = jnp.zeros((ATT_TAIL, KV_DIM), BF16)

    kidx = lax.broadcasted_iota(jnp.int32, (1, KEY_TILE), 1)
    sinks = _sink_columns(sink_ref, CHUNK)

    def chunk(c, carry):
        q0 = pl.multiple_of(c * CHUNK, CHUNK)
        start = pl.multiple_of((cg * ATT_CPB + c) * CHUNK, CHUNK)
        key_ok = start + kidx >= ATT_LEAD

        def head_cols(h):
            return slice(h * HEAD_DIM, (h + 1) * HEAD_DIM)

        def store(h, val):
            o_ref[pl.ds(q0, CHUNK), head_cols(h)] = val

        _attend(lambda h: q_ref[pl.ds(q0, CHUNK), head_cols(h)].astype(BF16),
                lambda hk: kpad[pl.ds(start, KEY_TILE), head_cols(hk)],
                lambda hk: vpad[pl.ds(start, KEY_TILE), head_cols(hk)],
                bias_scr, sinks, key_ok, store, CHUNK)
        return carry

    lax.fori_loop(0, ATT_CPB, chunk, 0)


def _attn_band(z, table, sinks, bkt, *, n_batch, seq_len):
    rows = ATT_CPB * CHUNK
    ng = seq_len // rows
    kern = functools.partial(_attn_band_kernel, seq_len=seq_len)
    smem = pl.BlockSpec(memory_space=pltpu.SMEM)
    pad_rows = ATT_LEAD + seq_len + ATT_TAIL
    return pl.pallas_call(
        kern,
        grid=(n_batch, ng),
        in_specs=[smem, smem,
                  pl.BlockSpec((rows, Q_DIM), lambda b, c: (b * ng + c, COL_Q // Q_DIM)),
                  pl.BlockSpec((seq_len, KV_DIM), lambda b, c: (b, COL_K // KV_DIM)),
                  pl.BlockSpec((seq_len, KV_DIM), lambda b, c: (b, COL_V // KV_DIM)),
                  pl.BlockSpec((CHUNK, KEY_TILE), lambda b, c: (0, 0))],
        out_specs=pl.BlockSpec((rows, Q_DIM), lambda b, c: (b * ng + c, 0)),
        out_shape=jax.ShapeDtypeStruct((n_batch * seq_len, Q_DIM), BF16),
        scratch_shapes=[pltpu.VMEM((pad_rows, KV_DIM), BF16),
                        pltpu.VMEM((pad_rows, KV_DIM), BF16),
                        pltpu.VMEM((N_KV_HEADS, GQA_GROUP * CHUNK, KEY_TILE), F32)],
        compiler_params=_cparams(("arbitrary", "arbitrary")),
        name="attn_band",
    )(table, sinks, z, z, z, bkt)


def _attn_step_kernel(tbl_ref, sink_ref, q_ref, k_ref, v_ref, bkt_ref, o_ref, kbuf, vbuf, bias_scr,
                      *, n_keys, n_q):
    b = pl.program_id(0)

    @pl.when(b == 0)
    def _():
        _build_bias(tbl_ref, bkt_ref, bias_scr, n_q)
        kbuf[n_keys:, :] = jnp.zeros((KEY_TILE - n_keys, KV_DIM), BF16)
        vbuf[n_keys:, :] = jnp.zeros((KEY_TILE - n_keys, KV_DIM), BF16)

    kbuf[0:n_keys, :] = k_ref[0].astype(BF16)
    vbuf[0:n_keys, :] = v_ref[0].astype(BF16)

    def head_cols(h):
        return slice(h * HEAD_DIM, (h + 1) * HEAD_DIM)

    def store(h, val):
        o_ref[:, head_cols(h)] = val

    _attend(lambda h: q_ref[:, head_cols(h)].astype(BF16),
            lambda hk: kbuf[:, head_cols(hk)], lambda hk: vbuf[:, head_cols(hk)],
            bias_scr, _sink_columns(sink_ref, n_q), None, store, n_q)


def _attn_step(z, k_all, v_all, table, sinks, bkt, *, n_batch, n_q, row_block0):
    n_keys = k_all.shape[1]
    kern = functools.partial(_attn_step_kernel, n_keys=n_keys, n_q=n_q)
    smem = pl.BlockSpec(memory_space=pltpu.SMEM)
    return pl.pallas_call(
        kern,
        grid=(n_batch,),
        in_specs=[smem, smem,
                  pl.BlockSpec((n_q, Q_DIM), lambda b: (row_block0 + b, COL_Q // Q_DIM)),
                  pl.BlockSpec((1, n_keys, KV_DIM), lambda b: (b, 0, 0)),
                  pl.BlockSpec((1, n_keys, KV_DIM), lambda b: (b, 0, 0)),
                  pl.BlockSpec((n_q, KEY_TILE), lambda b: (0, 0))],
        out_specs=pl.BlockSpec((n_q, Q_DIM), lambda b: (b, 0)),
        out_shape=jax.ShapeDtypeStruct((n_batch * n_q, Q_DIM), BF16),
        scratch_shapes=[pltpu.VMEM((KEY_TILE, KV_DIM), BF16),
                        pltpu.VMEM((KEY_TILE, KV_DIM), BF16),
                        pltpu.VMEM((N_KV_HEADS, GQA_GROUP * n_q, KEY_TILE), F32)],
        compiler_params=_cparams(("arbitrary",)),
        name="attn_step",
    )(table, sinks, z, k_all, v_all, bkt)


def _t5_bucket(rel):
    nb = N_BUCKETS // 2
    ret = jnp.where(rel > 0, nb, 0)
    n = jnp.abs(rel)
    max_exact = nb // 2
    nf = jnp.maximum(n, 1).astype(jnp.float32)
    large = max_exact + (jnp.log(nf / max_exact) / math.log(MAX_DISTANCE / max_exact)
                         * (nb - max_exact)).astype(jnp.int32)
    large = jnp.minimum(large, nb - 1)
    return ret + jnp.where(n < max_exact, n, large)


def _bucket_map(q_pos, k_pos):
    bkt = _t5_bucket(k_pos[None, :] - q_pos[:, None]).astype(jnp.int32)
    return jnp.pad(bkt, ((0, 0), (0, KEY_TILE - k_pos.shape[0])), constant_values=-1)


ROUTE_LANES = LANES
GATE_TN = 1024
OUT_VMEM_LIMIT = 60 * 1024 * 1024


def _outproj_kernel(hgp_ref, hgs_ref, op_ref, os_ref, ga0_ref, ga1_ref, gb0_ref, gb1_ref, xp_ref, xs_ref,
                    g1_ref, sh2_ref, sc2_ref, ng_ref, wr_ref, wa_ref, wo_ref, wrt_ref, brt_ref,
                    x1_ref, n2_ref, rt_ref, *, n_prompt_tiles):
    tm = x1_ref.shape[0]
    is_prompt = pl.program_id(0) < n_prompt_tiles
    hg = jnp.where(is_prompt, hgp_ref[...], hgs_ref[...])
    o = jnp.where(is_prompt, op_ref[...], os_ref[...])
    x = jnp.where(is_prompt, xp_ref[...], xs_ref[...])
    ya = jnp.dot(hg, wr_ref[...], preferred_element_type=F32)
    yb = jnp.dot(o, wa_ref[...], preferred_element_type=F32)
    halves = []
    for k, (ga_ref, gb_ref) in enumerate(((ga0_ref, gb0_ref), (ga1_ref, gb1_ref))):
        cols = slice(k * GATE_TN, (k + 1) * GATE_TN)
        halves.append((_sigmoid(ga_ref[...]) * ya[:, cols] + _sigmoid(gb_ref[...]) * yb[:, cols]).astype(BF16))
    merged = jnp.concatenate(halves, axis=1)
    mix = jnp.dot(merged, wo_ref[...], preferred_element_type=F32)
    x1 = x + _bcast_rows(g1_ref[...], tm) * mix
    x1_ref[...] = x1
    n2 = _rms_modulate(x1, ng_ref[...], sc2_ref[...], sh2_ref[...])
    n2_ref[...] = n2

    lg = jnp.dot(n2.astype(BF16), wrt_ref[...].astype(BF16), preferred_element_type=F32) + brt_ref[...]

    lane = lax.broadcasted_iota(jnp.int32, (tm, ROUTE_LANES), 1)
    lane_f = lane.astype(F32)
    e_lane = lane - N_GROUPS
    lane_group = (e_lane >> 3).astype(F32)

    def first_argmax(vals, vmax):
        return jnp.min(jnp.where(vals == vmax, lane_f, float(ROUTE_LANES)), axis=-1, keepdims=True)

    gl = jnp.where(lane < N_GROUPS, lg, NEG_INF)
    gmax = jnp.max(gl, axis=-1, keepdims=True)
    g_idx = first_argmax(gl, gmax)
    g_w = 1.0 / jnp.sum(jnp.exp(gl - gmax), axis=-1, keepdims=True)
    in_group = jnp.where((e_lane >= 0) & (e_lane < N_EXPERTS), lane_group, -1.0) == g_idx
    el = jnp.where(in_group, lg, NEG_INF)
    v1 = jnp.max(el, axis=-1, keepdims=True)
    i1 = first_argmax(el, v1)
    el2 = jnp.where(lane_f == i1, NEG_INF, el)
    v2 = jnp.max(el2, axis=-1, keepdims=True)
    i2 = first_argmax(el2, v2)
    e21 = jnp.exp(v2 - v1)
    w1 = g_w / (1.0 + e21)
    w2 = g_w * e21 / (1.0 + e21)
    e1 = i1 - float(N_GROUPS)
    e2 = i2 - float(N_GROUPS)
    rt_ref[...] = jnp.where(lane == 0, e1, jnp.where(lane == 1, e2, jnp.where(lane == 2, w1,
                            jnp.where(lane == 3, w2, 0.0))))


def _outproj(hg_p, hg_s, o_p, o_s, z, x_p, x_s, g1, sh2, sc2, norm_g, wr, wa, wo, w_route, b_route):
    tm = ROW_TILE
    t = z.shape[0]
    seg = tm // MOD_ROWS
    n_p = x_p.shape[0] // tm
    row = lambda i: (i, 0)
    fix = lambda i: (0, 0)
    row_p = lambda i: (jnp.minimum(i, n_p - 1), 0)
    row_s = lambda i: (jnp.maximum(i - n_p, 0), 0)
    once = pl.Buffered(1)

    def gate(col):
        return pl.BlockSpec((tm, GATE_TN), lambda i: (i, col // GATE_TN))

    return pl.pallas_call(
        functools.partial(_outproj_kernel, n_prompt_tiles=n_p),
        grid=(t // tm,),
        in_specs=[pl.BlockSpec((tm, D_RNN), row_p), pl.BlockSpec((tm, D_RNN), row_s, pipeline_mode=once),
                  pl.BlockSpec((tm, Q_DIM), row_p), pl.BlockSpec((tm, Q_DIM), row_s, pipeline_mode=once),
                  gate(COL_GA), gate(COL_GA + GATE_TN), gate(COL_GB), gate(COL_GB + GATE_TN),
                  pl.BlockSpec((tm, D_MODEL), row_p), pl.BlockSpec((tm, D_MODEL), row_s, pipeline_mode=once),
                  pl.BlockSpec((seg, D_MODEL), row), pl.BlockSpec((seg, D_MODEL), row),
                  pl.BlockSpec((seg, D_MODEL), row),
                  pl.BlockSpec((1, D_MODEL), fix),
                  pl.BlockSpec((D_RNN, D_MODEL), fix, pipeline_mode=once),
                  pl.BlockSpec((Q_DIM, D_MODEL), fix, pipeline_mode=once),
                  pl.BlockSpec((D_MODEL, D_MODEL), fix, pipeline_mode=once),
                  pl.BlockSpec((D_MODEL, ROUTE_LANES), fix),
                  pl.BlockSpec((1, ROUTE_LANES), fix)],
        out_specs=[pl.BlockSpec((tm, D_MODEL), row), pl.BlockSpec((tm, D_MODEL), row),
                   pl.BlockSpec((tm, ROUTE_LANES), row)],
        out_shape=[jax.ShapeDtypeStruct((t, D_MODEL), F32), jax.ShapeDtypeStruct((t, D_MODEL), F32),
                   jax.ShapeDtypeStruct((t, ROUTE_LANES), F32)],
        compiler_params=_cparams(("arbitrary",), OUT_VMEM_LIMIT),
        name="outproj",
    )(hg_p, hg_s, o_p, o_s, z, z, z, z, x_p, x_s, g1, sh2, sc2, norm_g.reshape(1, D_MODEL),
      wr, wa, wo, w_route, b_route)


def _row_gather(src_hbm, idx_ref, base, dst, sem, n_rows):
    for r in range(n_rows):
        tok = idx_ref[base + r]
        pltpu.make_async_copy(src_hbm.at[pl.ds(tok, 1), :], dst.at[pl.ds(r, 1), :], sem).start()


def _row_gather_wait(src_hbm, dst, sem, n_rows):
    pltpu.make_async_copy(src_hbm.at[pl.ds(0, n_rows), :], dst, sem).wait()


def _moe_kernel(te_ref, tfirst_ref, tvalid_ref, tnext_ref, tbase_ref, src_ref, n2_hbm, wg_hbm, wu_hbm, wd_hbm, y_ref,
                xbuf, xsem, wg_st, wu_st, wd_st, wsem, wg_bf, wu_bf, wd_bf):
    i = pl.program_id(0)
    n_tiles = pl.num_programs(0)
    slot = lax.rem(i, MOE_RING)
    stages = ((wg_hbm, wg_st, wg_bf), (wu_hbm, wu_st, wu_bf), (wd_hbm, wd_st, wd_bf))

    def weight_copy(k, e):
        hbm, st, _ = stages[k]
        return pltpu.make_async_copy(hbm.at[e], st, wsem.at[k])

    def gather_tile(tile):
        tc = jnp.minimum(tile, n_tiles - 1)

        @pl.when((tile < n_tiles) & (tvalid_ref[tc] == 1))
        def _():
            s = lax.rem(tile, MOE_RING)
            _row_gather(n2_hbm, src_ref, tbase_ref[tc], xbuf.at[s], xsem.at[s], MOE_TM)

    @pl.when(i == 0)
    def _():
        for k in range(len(stages)):
            weight_copy(k, te_ref[0]).start(priority=1)
        for ahead in range(MOE_RING - 1):
            gather_tile(ahead)

    @pl.when(tvalid_ref[i] == 1)
    def _():
        gather_tile(i + MOE_RING - 1)

        @pl.when(tfirst_ref[i] == 1)
        def _():
            for k, (_, st, bf) in enumerate(stages):
                weight_copy(k, te_ref[i]).wait()
                bf[...] = st[...].astype(BF16)

            @pl.when(tnext_ref[i] >= 0)
            def _():
                for k in range(len(stages)):
                    weight_copy(k, tnext_ref[i]).start(priority=1)

        _row_gather_wait(n2_hbm, xbuf.at[slot], xsem.at[slot], MOE_TM)
        x = xbuf[slot].astype(BF16)
        hgate = jnp.dot(x, wg_bf[...], preferred_element_type=F32)
        hup = jnp.dot(x, wu_bf[...], preferred_element_type=F32)
        act = (hgate * _sigmoid(hgate) * hup).astype(BF16)
        y_ref[...] = jnp.dot(act, wd_bf[...], preferred_element_type=F32)

    @pl.when(tvalid_ref[i] == 0)
    def _():
        y_ref[...] = jnp.zeros(y_ref.shape, F32)


def _moe(n2, w_gate, w_up, w_down, tile_expert, tile_first, tile_valid, tile_next, tile_base, src_tok):
    n_tiles = tile_expert.shape[0]
    hbm = pl.BlockSpec(memory_space=pl.ANY)
    grid_spec = pltpu.PrefetchScalarGridSpec(
        num_scalar_prefetch=6,
        grid=(n_tiles,),
        in_specs=[hbm, hbm, hbm, hbm],
        out_specs=pl.BlockSpec((MOE_TM, D_MODEL), lambda i, te, tf, tv, tn, tb, st: (i, 0)),
        scratch_shapes=[pltpu.VMEM((MOE_RING, MOE_TM, D_MODEL), F32),
                        pltpu.SemaphoreType.DMA((MOE_RING,)),
                        pltpu.VMEM((D_MODEL, D_EXPERT), F32),
                        pltpu.VMEM((D_MODEL, D_EXPERT), F32),
                        pltpu.VMEM((D_EXPERT, D_MODEL), F32),
                        pltpu.SemaphoreType.DMA((3,)),
                        pltpu.VMEM((D_MODEL, D_EXPERT), BF16),
                        pltpu.VMEM((D_MODEL, D_EXPERT), BF16),
                        pltpu.VMEM((D_EXPERT, D_MODEL), BF16)],
    )
    return pl.pallas_call(
        _moe_kernel,
        grid_spec=grid_spec,
        out_shape=jax.ShapeDtypeStruct((n_tiles * MOE_TM, D_MODEL), F32),
        compiler_params=_cparams(("arbitrary",)),
        name="moe",
    )(tile_expert, tile_first, tile_valid, tile_next, tile_base, src_tok, n2, w_gate, w_up, w_down)


def _route_plan(e1, e2, n_tok):
    experts = jnp.arange(N_EXPERTS, dtype=jnp.int32)
    flat_e = jnp.concatenate([e1, e2])
    onehot = (flat_e[:, None] == experts[None, :]).astype(jnp.int32)
    csum = jnp.cumsum(onehot, axis=0)
    rank = jnp.sum(csum * onehot, axis=1) - 1
    counts = csum[-1]
    tiles_per = (counts + MOE_TM - 1) // MOE_TM
    tile_end = jnp.cumsum(tiles_per)
    tile_off = tile_end - tiles_per
    slot = jnp.sum(onehot * tile_off[None, :], axis=1) * MOE_TM + rank
    n_tiles = (2 * n_tok) // MOE_TM + N_EXPERTS
    tok = jnp.tile(jnp.arange(n_tok, dtype=jnp.int32), 2)
    _, src_tok = lax.sort((slot, tok), num_keys=1)
    src_tok = jnp.concatenate([src_tok, jnp.zeros((MOE_TM,), jnp.int32)])
    count_off = jnp.cumsum(counts) - counts
    tile_id = jnp.arange(n_tiles, dtype=jnp.int32)
    n_used = tile_end[-1]
    tile_valid = (tile_id < n_used).astype(jnp.int32)
    te = jnp.sum((tile_end[None, :] <= jnp.minimum(tile_id, n_used - 1)[:, None]).astype(jnp.int32), axis=1)
    tile_expert = jnp.minimum(te, N_EXPERTS - 1)
    prev = jnp.concatenate([jnp.full((1,), -1, jnp.int32), tile_expert[:-1]])
    tile_first = (tile_expert != prev).astype(jnp.int32)
    oh_te = (tile_expert[:, None] == experts[None, :]).astype(jnp.int32)

    def of_expert(table):
        return jnp.sum(oh_te * table[None, :], axis=1)

    next_tile = of_expert(tile_end)
    oh_next = (tile_id[None, :] == jnp.minimum(next_tile, n_tiles - 1)[:, None]).astype(jnp.int32)
    tile_next = jnp.where(next_tile < n_used, jnp.sum(oh_next * tile_expert[None, :], axis=1), -1)
    tile_base = jnp.where(tile_valid == 1, of_expert(count_off) + (tile_id - of_expert(tile_off)) * MOE_TM, 0)
    return tile_expert, tile_first, tile_valid, tile_next, tile_base, src_tok, slot[:n_tok], slot[n_tok:]


def _final_kernel(p1_ref, p2_ref, ys_hbm, x1_ref, g2_ref, rt_ref, fg_ref, yp_ref, ysm_ref, ybuf, sem,
                  *, n_prompt_tiles):
    i = pl.program_id(0)
    n_tiles = pl.num_programs(0)
    slot = i % 2

    def start(tile, s):
        _row_gather(ys_hbm, p1_ref, tile * ROW_TILE, ybuf.at[s, 0], sem.at[s], ROW_TILE)
        _row_gather(ys_hbm, p2_ref, tile * ROW_TILE, ybuf.at[s, 1], sem.at[s], ROW_TILE)

    @pl.when(i == 0)
    def _():
        start(0, 0)

    @pl.when(i + 1 < n_tiles)
    def _():
        start(i + 1, 1 - slot)

    _row_gather_wait(ys_hbm, ybuf.at[slot, 0], sem.at[slot], ROW_TILE)
    _row_gather_wait(ys_hbm, ybuf.at[slot, 1], sem.at[slot], ROW_TILE)
    rt = rt_ref[...]
    moe = rt[:, 2:3] * ybuf[slot, 0] + rt[:, 3:4] * ybuf[slot, 1]
    x2 = x1_ref[...] + _bcast_rows(g2_ref[...], ROW_TILE) * moe
    y = x2 * lax.rsqrt(jnp.mean(x2 * x2, axis=-1, keepdims=True) + EPS) * fg_ref[...]

    @pl.when(i < n_prompt_tiles)
    def _():
        yp_ref[...] = y

    @pl.when(i >= n_prompt_tiles)
    def _():
        ysm_ref[...] = y


def _final(ys, x1, g2, rt, final_g, p1, p2, n_prompt_rows):
    t = x1.shape[0]
    tm = ROW_TILE
    seg = tm // MOD_ROWS
    n_p = n_prompt_rows // tm
    grid_spec = pltpu.PrefetchScalarGridSpec(
        num_scalar_prefetch=2,
        grid=(t // tm,),
        in_specs=[pl.BlockSpec(memory_space=pl.ANY),
                  pl.BlockSpec((tm, D_MODEL), lambda i, a, b: (i, 0)),
                  pl.BlockSpec((seg, D_MODEL), lambda i, a, b: (i, 0)),
                  pl.BlockSpec((tm, ROUTE_LANES), lambda i, a, b: (i, 0)),
                  pl.BlockSpec((1, D_MODEL), lambda i, a, b: (0, 0))],
        out_specs=[pl.BlockSpec((tm, D_MODEL), lambda i, a, b: (jnp.minimum(i, n_p - 1), 0)),
                   pl.BlockSpec((tm, D_MODEL), lambda i, a, b: (jnp.maximum(i - n_p, 0), 0))],
        scratch_shapes=[pltpu.VMEM((2, 2, tm, D_MODEL), F32),
                        pltpu.SemaphoreType.DMA((2,))],
    )
    return pl.pallas_call(
        functools.partial(_final_kernel, n_prompt_tiles=n_p),
        grid_spec=grid_spec,
        out_shape=[jax.ShapeDtypeStruct((n_prompt_rows, D_MODEL), F32),
                   jax.ShapeDtypeStruct((t - n_prompt_rows, D_MODEL), F32)],
        compiler_params=_cparams(("arbitrary",)),
        name="final",
    )(p1, p2, ys, x1, g2, rt, final_g.reshape(1, D_MODEL))


def kernel(x_prompt, x_sample, cache_k_win, cache_v_win, state_conv, state_rglru, c_prompt, c_sample, w_ada, b_ada, norm1_g, norm2_g, w_in, b_in, w_conv, b_conv, w_rg_a, b_rg_a, w_rg_x, b_rg_x, lru_lambda, w_rnn_out, w_attn_out, w_out, attn_sinks, w_route_group, b_route_group, w_route_expert, b_route_expert, w_exp_gate, w_exp_up, w_exp_down, rel_bias_table, final_norm_g):
    n_b, seq, _ = x_prompt.shape
    d_b, d_seq, _ = x_sample.shape
    assert w_ada.shape[0] == 1, "single trunk layer"
    assert seq % (ATT_CPB * CHUNK) == 0 and seq % MOD_ROWS == 0 and d_seq == MOD_ROWS
    assert d_seq >= CONV_W - 1 and d_b == N_CHAINS
    t_p, t_s = n_b * seq, d_b * d_seq
    t = t_p + t_s
    assert t_s == ROW_TILE and t_p % ROW_TILE == 0
    cw = cache_k_win.shape[2]
    l = 0

    x_p = x_prompt.reshape(t_p, D_MODEL)
    x_s = x_sample.reshape(t_s, D_MODEL)

    n_c = n_b + d_b
    c_all = jnp.pad(jnp.concatenate([c_prompt, c_sample], axis=0), ((0, -n_c % SUBLANES), (0, 0)))
    mod = _ada(c_all, w_ada[l], b_ada[l])

    def per_segment(m):
        return jnp.concatenate([jnp.repeat(m[:n_b], seq // MOD_ROWS, axis=0),
                                jnp.repeat(m[n_b:n_c], d_seq // MOD_ROWS, axis=0)], axis=0)

    sh1, sc1, g1, sh2, sc2, g2 = [per_segment(m) for m in jnp.split(mod, 6, axis=-1)]

    z = _inproj(_norm1(x_p, x_s, sc1, sh1, norm1_g[l]), w_in[l], b_in[l])

    w_gate = jnp.concatenate([w_rg_a[l], w_rg_x[l]], axis=-1)
    b_gate = jnp.concatenate([b_rg_a[l], b_rg_x[l]], axis=-1)[:, None, :]
    hg_p, hl_p = _rglru(z, jnp.zeros((n_b, SUBLANES, D_RNN), F32), jnp.zeros((n_b, N_CHAINS, D_RNN), F32),
                        w_conv[l], b_conv[l], w_gate, b_gate, lru_lambda[l],
                        n_blocks=n_b, n_seq=1, seq_len=seq, row_block0=0, link=True)
    state8 = jnp.pad(state_conv[l], ((0, 0), (SUBLANES - (CONV_W - 1), 0), (0, 0)))
    hg_s, hl_s = _rglru(z, state8, state_rglru[l][None], w_conv[l], b_conv[l], w_gate, b_gate, lru_lambda[l],
                        n_blocks=1, n_seq=d_b, seq_len=d_seq, row_block0=t_p // t_s, link=False)

    sinks = attn_sinks[l]
    bkt_p = _bucket_map(WIN_CHUNKS * CHUNK + jnp.arange(CHUNK), jnp.arange(SPAN))
    o_p = _attn_band(z, rel_bias_table, sinks, bkt_p, n_batch=n_b, seq_len=seq)
    k_new = z[t_p:, COL_K:COL_K + KV_DIM].reshape(d_b, d_seq, KV_DIM)
    v_new = z[t_p:, COL_V:COL_V + KV_DIM].reshape(d_b, d_seq, KV_DIM)
    k_all = jnp.concatenate([cache_k_win[l].reshape(d_b, cw, KV_DIM), k_new], axis=1)
    v_all = jnp.concatenate([cache_v_win[l].reshape(d_b, cw, KV_DIM), v_new], axis=1)
    bkt_s = _bucket_map(cw + jnp.arange(d_seq), jnp.arange(cw + d_seq))
    o_s = _attn_step(z, k_all, v_all, rel_bias_table, sinks, bkt_s, n_batch=d_b, n_q=d_seq, row_block0=t_p // d_seq)

    n_route = N_GROUPS + N_EXPERTS
    w_route = jnp.pad(jnp.concatenate([w_route_group[l], w_route_expert[l]], axis=1),
                      ((0, 0), (0, ROUTE_LANES - n_route)))
    b_route = jnp.pad(jnp.concatenate([b_route_group[l], b_route_expert[l]]),
                      (0, ROUTE_LANES - n_route)).reshape(1, ROUTE_LANES)
    x1, n2, rt = _outproj(hg_p, hg_s, o_p, o_s, z, x_p, x_s, g1, sh2, sc2, norm2_g[l],
                          w_rnn_out[l].astype(BF16), w_attn_out[l].astype(BF16), w_out[l].astype(BF16),
                          w_route, b_route)

    e1 = rt[:, 0].astype(jnp.int32)
    e2 = rt[:, 1].astype(jnp.int32)
    tile_expert, tile_first, tile_valid, tile_next, tile_base, src_tok, p1, p2 = _route_plan(e1, e2, t)
    ys = _moe(n2, w_exp_gate[l], w_exp_up[l], w_exp_down[l], tile_expert, tile_first, tile_valid, tile_next,
              tile_base, src_tok)
    y_p, y_s = _final(ys, x1, g2, rt, final_norm_g, p1, p2, t_p)

    win = min(WINDOW, seq)

    def tail(col0, width, n_rows):
        return jnp.stack([z[(b + 1) * seq - n_rows:(b + 1) * seq, col0:col0 + width] for b in range(n_b)])

    kp = tail(COL_K, KV_DIM, win).reshape(n_b, win, N_KV_HEADS, HEAD_DIM)
    vp = tail(COL_V, KV_DIM, win).reshape(n_b, win, N_KV_HEADS, HEAD_DIM)
    cp = tail(COL_XR, D_RNN, CONV_W - 1)
    rp = hl_p[:, N_CHAINS - 1, :]
    ks = k_all[:, -cw:].reshape(d_b, cw, N_KV_HEADS, HEAD_DIM)
    vs = v_all[:, -cw:].reshape(d_b, cw, N_KV_HEADS, HEAD_DIM)
    cs = z[t_p:, COL_XR:COL_XR + D_RNN].reshape(d_b, d_seq, D_RNN)[:, -(CONV_W - 1):]
    rs = hl_s[0]
    return (y_p.reshape(n_b, seq, D_MODEL), y_s.reshape(d_b, d_seq, D_MODEL),
            kp[None], vp[None], cp[None], rp[None], ks[None], vs[None], cs[None], rs[None])
```

```python
import functools
import math

import jax
import jax.numpy as jnp
from jax import lax
from jax.experimental import pallas as pl
from jax.experimental.pallas import tpu as pltpu

F32 = jnp.float32
BF16 = jnp.bfloat16

D_MODEL = 2048
D_RNN = 2048
RNN_BLOCK = 128
CONV_W = 4
LRU_C = 8.0
N_HEADS = 16
N_KV_HEADS = 4
HEAD_DIM = 128
GQA_GROUP = N_HEADS // N_KV_HEADS
Q_DIM = N_HEADS * HEAD_DIM
KV_DIM = N_KV_HEADS * HEAD_DIM
CHUNK = 64
WINDOW = 128
WIN_CHUNKS = WINDOW // CHUNK
SPAN = (WIN_CHUNKS + 1) * CHUNK
N_BUCKETS = 32
MAX_DISTANCE = 128
N_GROUPS = 4
E_PER_GROUP = 8
N_EXPERTS = N_GROUPS * E_PER_GROUP
D_EXPERT = 512
EPS = 1e-6
NEG_INF = -1e30
D_IN = 2 * D_RNN + Q_DIM + 2 * KV_DIM + 2 * D_MODEL
COL_XR, COL_GR, COL_Q = 0, D_RNN, 2 * D_RNN
COL_K = COL_Q + Q_DIM
COL_V = COL_K + KV_DIM
COL_GA = COL_V + KV_DIM
COL_GB = COL_GA + D_MODEL

LANES = 128
SUBLANES = 8
MOD_ROWS = 32
KEY_TILE = 256
VMEM_LIMIT = 56 * 1024 * 1024

ROW_TILE = 256
IN_TM_MAX = 1408
IN_TN = 1024
MOE_TM = 192
MOE_RING = 3


def _sigmoid(x):
    return 0.5 * jnp.tanh(0.5 * x) + 0.5


def _gelu_tanh(x):
    return 0.5 * x * (1.0 + jnp.tanh(math.sqrt(2.0 / math.pi) * (x + 0.044715 * (x * x * x))))


def _bcast_rows(v, rows):
    n, d = v.shape
    return jnp.broadcast_to(v[:, None, :], (n, MOD_ROWS, d)).reshape(rows, d)


def _rms_modulate(x, gain, scale_seg, shift_seg):
    rows = x.shape[0]
    y = x * lax.rsqrt(jnp.mean(x * x, axis=-1, keepdims=True) + EPS) * gain
    return y * (1.0 + _bcast_rows(scale_seg, rows)) + _bcast_rows(shift_seg, rows)


def _cparams(sem, vmem_limit=VMEM_LIMIT):
    return pltpu.CompilerParams(dimension_semantics=sem, vmem_limit_bytes=vmem_limit)


def _ada_kernel(c_ref, w_ref, b_ref, o_ref):
    c = c_ref[...]
    s = (c * _sigmoid(c)).astype(BF16)
    o_ref[...] = jnp.dot(s, w_ref[...].astype(BF16), preferred_element_type=F32) + b_ref[...]


def _ada(c_all, w_ada, b_ada):
    rows = c_all.shape[0]
    n = w_ada.shape[1]
    tn = 1024
    return pl.pallas_call(
        _ada_kernel,
        grid=(n // tn,),
        in_specs=[pl.BlockSpec((rows, D_MODEL), lambda j: (0, 0)),
                  pl.BlockSpec((D_MODEL, tn), lambda j: (0, j)),
                  pl.BlockSpec((1, tn), lambda j: (0, j))],
        out_specs=pl.BlockSpec((rows, tn), lambda j: (0, j)),
        out_shape=jax.ShapeDtypeStruct((rows, n), F32),
        compiler_params=_cparams(("arbitrary",)),
        name="ada",
    )(c_all, w_ada, b_ada.reshape(1, n))


NORM_NSUB = 3


def _norm1_kernel(x0_ref, x1_ref, x2_ref, xs_ref, sc_ref, sh_ref, g_ref, n1_ref, *, n_prompt_tiles):
    tail_is_sample = pl.program_id(0) * NORM_NSUB + (NORM_NSUB - 1) >= n_prompt_tiles
    gain = g_ref[...]
    segs = ROW_TILE // MOD_ROWS
    for r, x_ref in enumerate((x0_ref, x1_ref, x2_ref)):
        for s in range(segs):
            rows = slice(s * MOD_ROWS, (s + 1) * MOD_ROWS)
            x = x_ref[rows, :]
            if r == NORM_NSUB - 1:
                x = jnp.where(tail_is_sample, xs_ref[rows, :], x)
            y = x * lax.rsqrt(jnp.mean(x * x, axis=-1, keepdims=True) + EPS) * gain
            k = r * segs + s
            out_rows = slice(k * MOD_ROWS, (k + 1) * MOD_ROWS)
            n1_ref[out_rows, :] = (y * (1.0 + sc_ref[k:k + 1, :]) + sh_ref[k:k + 1, :]).astype(BF16)


def _norm1(x_p, x_s, sc1, sh1, norm_g):
    n_p = x_p.shape[0] // ROW_TILE
    t = x_p.shape[0] + x_s.shape[0]
    tm = NORM_NSUB * ROW_TILE
    assert x_s.shape[0] == ROW_TILE and t % tm == 0
    seg = tm // MOD_ROWS

    def sub(r):
        return pl.BlockSpec((ROW_TILE, D_MODEL), lambda i: (jnp.minimum(i * NORM_NSUB + r, n_p - 1), 0))

    return pl.pallas_call(
        functools.partial(_norm1_kernel, n_prompt_tiles=n_p),
        grid=(t // tm,),
        in_specs=[sub(0), sub(1), sub(2),
                  pl.BlockSpec((ROW_TILE, D_MODEL), lambda i: (0, 0)),
                  pl.BlockSpec((seg, D_MODEL), lambda i: (i, 0)),
                  pl.BlockSpec((seg, D_MODEL), lambda i: (i, 0)),
                  pl.BlockSpec((1, D_MODEL), lambda i: (0, 0))],
        out_specs=pl.BlockSpec((tm, D_MODEL), lambda i: (i, 0)),
        out_shape=jax.ShapeDtypeStruct((t, D_MODEL), BF16),
        compiler_params=_cparams(("arbitrary",)),
        name="norm1",
    )(x_p, x_p, x_p, x_s, sc1, sh1, norm_g.reshape(1, D_MODEL))


def _inproj_kernel(n1_ref, w_ref, b_ref, z_ref, w_bf):
    @pl.when(pl.program_id(1) == 0)
    def _():
        w_bf[...] = w_ref[...].astype(BF16)

    z_ref[...] = jnp.dot(n1_ref[...], w_bf[...], preferred_element_type=F32) + b_ref[...]


def _inproj(n1, w_in, b_in):
    t = n1.shape[0]
    tm = next(m for m in range(IN_TM_MAX, 0, -LANES) if t % m == 0)
    return pl.pallas_call(
        _inproj_kernel,
        grid=(D_IN // IN_TN, t // tm),
        in_specs=[pl.BlockSpec((tm, D_MODEL), lambda j, i: (i, 0)),
                  pl.BlockSpec((D_MODEL, IN_TN), lambda j, i: (0, j)),
                  pl.BlockSpec((1, IN_TN), lambda j, i: (0, j))],
        out_specs=pl.BlockSpec((tm, IN_TN), lambda j, i: (i, j)),
        out_shape=jax.ShapeDtypeStruct((t, D_IN), F32),
        scratch_shapes=[pltpu.VMEM((D_MODEL, IN_TN), BF16)],
        compiler_params=_cparams(("arbitrary", "arbitrary")),
        name="inproj",
    )(n1, w_in, b_in.reshape(1, D_IN))


RG_CW = 256
RG_SLABS = RG_CW // LANES
N_CHAINS = SUBLANES
CHAIN_PAD = 8


def _rglru_kernel(xr_ref, gr_ref, st_ref, h0_ref, wc_ref, bc_ref, wg_ref, bg_ref, lam_ref,
                  hg_ref, hl_ref, ext_scr, a_scr, u_scr, *, n_seq, seq_len, link):
    rows = n_seq * seq_len
    cl = rows // N_CHAINS
    pitch = cl + CHAIN_PAD

    for s in range(n_seq):
        ext_scr[s, 0:SUBLANES, :] = st_ref[s]
        ext_scr[s, SUBLANES:SUBLANES + seq_len, :] = xr_ref[s * seq_len:(s + 1) * seq_len, :]

    z = -lam_ref[...]
    softplus = jnp.maximum(z, 0.0) + jnp.log1p(jnp.exp(-jnp.abs(z)))
    nq = (-0.25 * LRU_C) * softplus

    for c in range(N_CHAINS):
        s, r0 = divmod(c * cl, seq_len)
        xc = bc_ref[...]
        for j in range(CONV_W):
            xc = xc + wc_ref[j:j + 1, :] * ext_scr[s, pl.ds(SUBLANES - (CONV_W - 1) + j + r0, cl), :]
        for sl in range(RG_SLABS):
            lanes = slice(sl * LANES, (sl + 1) * LANES)
            xb = xc[:, lanes]
            g = jnp.dot(xb.astype(BF16), wg_ref[sl].astype(BF16), preferred_element_type=F32) + bg_ref[sl]
            tr = jnp.tanh(0.5 * g[:, :LANES])
            ti = jnp.tanh(0.5 * g[:, LANES:])
            th = jnp.tanh(nq[:, lanes] * (tr + 1.0))
            rcp = 1.0 / (1.0 - th)
            a_scr[sl, c * pitch:c * pitch + cl, :] = (1.0 + th) * rcp
            u_scr[sl, c * pitch:c * pitch + cl, :] = (jnp.sqrt(-th) * rcp) * ((ti + 1.0) * xb)

    def step(t, carry):
        hs, ps = carry
        new_h, new_p = [], []
        for sl in range(RG_SLABS):
            a = a_scr[sl, pl.ds(t, N_CHAINS, stride=pitch), :]
            u = u_scr[sl, pl.ds(t, N_CHAINS, stride=pitch), :]
            h = a * hs[sl] + u
            u_scr[sl, pl.ds(t, N_CHAINS, stride=pitch), :] = h
            new_h.append(h)
            if link:
                p = a * ps[sl]
                a_scr[sl, pl.ds(t, N_CHAINS, stride=pitch), :] = p
                new_p.append(p)
            else:
                new_p.append(ps[sl])
        return tuple(new_h), tuple(new_p)

    h_init = tuple(h0_ref[0, :, sl * LANES:(sl + 1) * LANES] for sl in range(RG_SLABS))
    p_init = tuple(jnp.ones((N_CHAINS, LANES), F32) for _ in range(RG_SLABS))
    h_end, p_end = lax.fori_loop(0, cl, step, (h_init, p_init), unroll=8)

    row = lax.broadcasted_iota(jnp.int32, (N_CHAINS, LANES), 0)
    for sl in range(RG_SLABS):
        lanes = slice(sl * LANES, (sl + 1) * LANES)
        if link:
            def shift_down(v):
                return jnp.where(row == 0, 0.0, pltpu.roll(v, 1, axis=0))
            hh = h_end[sl]
            for _ in range(N_CHAINS - 1):
                hh = h_end[sl] + p_end[sl] * shift_down(hh)
            carry_in = shift_down(hh)
        else:
            hh = h_end[sl]
        hl_ref[0, :, lanes] = hh
        for c in range(N_CHAINS):
            h = u_scr[sl, c * pitch:c * pitch + cl, :]
            if link:
                h = h + a_scr[sl, c * pitch:c * pitch + cl, :] * carry_in[c:c + 1, :]
            gg = _gelu_tanh(gr_ref[c * cl:(c + 1) * cl, lanes])
            hg_ref[c * cl:(c + 1) * cl, lanes] = (h * gg).astype(BF16)


def _rglru(z, state8, h0, w_conv, b_conv, w_gate, b_gate, lam, *, n_blocks, n_seq, seq_len, row_block0, link):
    rows = n_seq * seq_len
    cl = rows // N_CHAINS
    ncb = D_RNN // RG_CW
    gr0 = COL_GR // RG_CW
    kern = functools.partial(_rglru_kernel, n_seq=n_seq, seq_len=seq_len, link=link)
    return pl.pallas_call(
        kern,
        grid=(n_blocks, ncb),
        in_specs=[pl.BlockSpec((rows, RG_CW), lambda b, n: (row_block0 + b, n)),
                  pl.BlockSpec((rows, RG_CW), lambda b, n: (row_block0 + b, gr0 + n)),
                  pl.BlockSpec((n_seq, SUBLANES, RG_CW), lambda b, n: (b, 0, n)),
                  pl.BlockSpec((1, N_CHAINS, RG_CW), lambda b, n: (b, 0, n)),
                  pl.BlockSpec((CONV_W, RG_CW), lambda b, n: (0, n)),
                  pl.BlockSpec((1, RG_CW), lambda b, n: (0, n)),
                  pl.BlockSpec((RG_SLABS, RNN_BLOCK, 2 * RNN_BLOCK), lambda b, n: (n, 0, 0)),
                  pl.BlockSpec((RG_SLABS, 1, 2 * RNN_BLOCK), lambda b, n: (n, 0, 0)),
                  pl.BlockSpec((1, RG_CW), lambda b, n: (0, n))],
        out_specs=[pl.BlockSpec((rows, RG_CW), lambda b, n: (b, n)),
                   pl.BlockSpec((1, N_CHAINS, RG_CW), lambda b, n: (b, 0, n))],
        out_shape=[jax.ShapeDtypeStruct((n_blocks * rows, D_RNN), BF16),
                   jax.ShapeDtypeStruct((n_blocks, N_CHAINS, D_RNN), F32)],
        scratch_shapes=[pltpu.VMEM((n_seq, SUBLANES + seq_len, RG_CW), F32),
                        pltpu.VMEM((RG_SLABS, N_CHAINS * (cl + CHAIN_PAD), LANES), F32),
                        pltpu.VMEM((RG_SLABS, N_CHAINS * (cl + CHAIN_PAD), LANES), F32)],
        compiler_params=_cparams(("arbitrary", "arbitrary")),
        name="rglru_link" if link else "rglru_step",
    )(z, z, state8, h0, w_conv, b_conv.reshape(1, D_RNN), w_gate, b_gate, lam.reshape(1, D_RNN))


def _build_bias(tbl_ref, bkt_ref, bias_scr, n_q):
    bkt = bkt_ref[...]
    base = jnp.where(bkt < 0, NEG_INF, 0.0)
    for h in range(N_HEADS):
        acc = base
        for bk in range(N_BUCKETS):
            acc = jnp.where(bkt == bk, tbl_ref[bk, h], acc)
        hk, g = divmod(h, GQA_GROUP)
        bias_scr[hk, g * n_q:(g + 1) * n_q, :] = acc


def _sink_columns(sink_ref, n_q):
    row = lax.broadcasted_iota(jnp.int32, (GQA_GROUP * n_q, 1), 0)
    cols = []
    for hk in range(N_KV_HEADS):
        col = jnp.full((GQA_GROUP * n_q, 1), sink_ref[hk * GQA_GROUP], F32)
        for g in range(1, GQA_GROUP):
            col = jnp.where(row >= g * n_q, sink_ref[hk * GQA_GROUP + g], col)
        cols.append(col)
    return cols


def _attend(q_of, k_of, v_of, bias_scr, sinks, key_ok, store, n_q):
    scores = []
    for hk in range(N_KV_HEADS):
        qg = jnp.concatenate([q_of(hk * GQA_GROUP + g) for g in range(GQA_GROUP)], axis=0)
        s = lax.dot_general(qg, k_of(hk), (((1,), (1,)), ((), ())), preferred_element_type=F32)
        s = s * (HEAD_DIM ** -0.5) + bias_scr[hk]
        if key_ok is not None:
            s = jnp.where(key_ok, s, NEG_INF)
        scores.append(s)
    probs = []
    for hk in range(N_KV_HEADS):
        s, sink = scores[hk], sinks[hk]
        m = jnp.maximum(jnp.max(s, axis=-1, keepdims=True), sink)
        p = jnp.exp(s - m)
        inv = 1.0 / (jnp.sum(p, axis=-1, keepdims=True) + jnp.exp(sink - m))
        probs.append((p * inv).astype(BF16))
    for hk in range(N_KV_HEADS):
        o = jnp.dot(probs[hk], v_of(hk), preferred_element_type=F32)
        for g in range(GQA_GROUP):
            store(hk * GQA_GROUP + g, o[g * n_q:(g + 1) * n_q, :].astype(BF16))


ATT_CPB = 8
ATT_LEAD = WIN_CHUNKS * CHUNK
ATT_TAIL = KEY_TILE - SPAN


def _attn_band_kernel(tbl_ref, sink_ref, q_ref, k_ref, v_ref, bkt_ref, o_ref, kpad, vpad, bias_scr, *, seq_len):
    b = pl.program_id(0)
    cg = pl.program_id(1)

    @pl.when((b == 0) & (cg == 0))
    def _():
        _build_bias(tbl_ref, bkt_ref, bias_scr, CHUNK)

    @pl.when(cg == 0)
    def _():
        for ref, pad in ((k_ref, kpad), (v_ref, vpad)):
            pad[0:ATT_LEAD, :] = jnp.zeros((ATT_LEAD, KV_DIM), BF16)
            pad[ATT_LEAD:ATT_LEAD + seq_len, :] = ref[...].astype(BF16)
            pad[ATT_LEAD + seq_len:, :] = jnp.zeros((ATT_TAIL, KV_DIM), BF16)

    kidx = lax.broadcasted_iota(jnp.int32, (1, KEY_TILE), 1)
    sinks = _sink_columns(sink_ref, CHUNK)

    def chunk(c, carry):
        q0 = pl.multiple_of(c * CHUNK, CHUNK)
        start = pl.multiple_of((cg * ATT_CPB + c) * CHUNK, CHUNK)
        key_ok = start + kidx >= ATT_LEAD

        def head_cols(h):
            return slice(h * HEAD_DIM, (h + 1) * HEAD_DIM)

        def store(h, val):
            o_ref[pl.ds(q0, CHUNK), head_cols(h)] = val

        _attend(lambda h: q_ref[pl.ds(q0, CHUNK), head_cols(h)].astype(BF16),
                lambda hk: kpad[pl.ds(start, KEY_TILE), head_cols(hk)],
                lambda hk: vpad[pl.ds(start, KEY_TILE), head_cols(hk)],
                bias_scr, sinks, key_ok, store, CHUNK)
        return carry

    lax.fori_loop(0, ATT_CPB, chunk, 0)


def _attn_band(z, table, sinks, bkt, *, n_batch, seq_len):
    rows = ATT_CPB * CHUNK
    ng = seq_len // rows
    kern = functools.partial(_attn_band_kernel, seq_len=seq_len)
    smem = pl.BlockSpec(memory_space=pltpu.SMEM)
    pad_rows = ATT_LEAD + seq_len + ATT_TAIL
    return pl.pallas_call(
        kern,
        grid=(n_batch, ng),
        in_specs=[smem, smem,
                  pl.BlockSpec((rows, Q_DIM), lambda b, c: (b * ng + c, COL_Q // Q_DIM)),
                  pl.BlockSpec((seq_len, KV_DIM), lambda b, c: (b, COL_K // KV_DIM)),
                  pl.BlockSpec((seq_len, KV_DIM), lambda b, c: (b, COL_V // KV_DIM)),
                  pl.BlockSpec((CHUNK, KEY_TILE), lambda b, c: (0, 0))],
        out_specs=pl.BlockSpec((rows, Q_DIM), lambda b, c: (b * ng + c, 0)),
        out_shape=jax.ShapeDtypeStruct((n_batch * seq_len, Q_DIM), BF16),
        scratch_shapes=[pltpu.VMEM((pad_rows, KV_DIM), BF16),
                        pltpu.VMEM((pad_rows, KV_DIM), BF16),
                        pltpu.VMEM((N_KV_HEADS, GQA_GROUP * CHUNK, KEY_TILE), F32)],
        compiler_params=_cparams(("arbitrary", "arbitrary")),
        name="attn_band",
    )(table, sinks, z, z, z, bkt)


def _attn_step_kernel(tbl_ref, sink_ref, q_ref, k_ref, v_ref, bkt_ref, o_ref, kbuf, vbuf, bias_scr,
                      *, n_keys, n_q):
    b = pl.program_id(0)

    @pl.when(b == 0)
    def _():
        _build_bias(tbl_ref, bkt_ref, bias_scr, n_q)
        kbuf[n_keys:, :] = jnp.zeros((KEY_TILE - n_keys, KV_DIM), BF16)
        vbuf[n_keys:, :] = jnp.zeros((KEY_TILE - n_keys, KV_DIM), BF16)

    kbuf[0:n_keys, :] = k_ref[0].astype(BF16)
    vbuf[0:n_keys, :] = v_ref[0].astype(BF16)

    def head_cols(h):
        return slice(h * HEAD_DIM, (h + 1) * HEAD_DIM)

    def store(h, val):
        o_ref[:, head_cols(h)] = val

    _attend(lambda h: q_ref[:, head_cols(h)].astype(BF16),
            lambda hk: kbuf[:, head_cols(hk)], lambda hk: vbuf[:, head_cols(hk)],
            bias_scr, _sink_columns(sink_ref, n_q), None, store, n_q)


def _attn_step(z, k_all, v_all, table, sinks, bkt, *, n_batch, n_q, row_block0):
    n_keys = k_all.shape[1]
    kern = functools.partial(_attn_step_kernel, n_keys=n_keys, n_q=n_q)
    smem = pl.BlockSpec(memory_space=pltpu.SMEM)
    return pl.pallas_call(
        kern,
        grid=(n_batch,),
        in_specs=[smem, smem,
                  pl.BlockSpec((n_q, Q_DIM), lambda b: (row_block0 + b, COL_Q // Q_DIM)),
                  pl.BlockSpec((1, n_keys, KV_DIM), lambda b: (b, 0, 0)),
                  pl.BlockSpec((1, n_keys, KV_DIM), lambda b: (b, 0, 0)),
                  pl.BlockSpec((n_q, KEY_TILE), lambda b: (0, 0))],
        out_specs=pl.BlockSpec((n_q, Q_DIM), lambda b: (b, 0)),
        out_shape=jax.ShapeDtypeStruct((n_batch * n_q, Q_DIM), BF16),
        scratch_shapes=[pltpu.VMEM((KEY_TILE, KV_DIM), BF16),
                        pltpu.VMEM((KEY_TILE, KV_DIM), BF16),
                        pltpu.VMEM((N_KV_HEADS, GQA_GROUP * n_q, KEY_TILE), F32)],
        compiler_params=_cparams(("arbitrary",)),
        name="attn_step",
    )(table, sinks, z, k_all, v_all, bkt)


def _t5_bucket(rel):
    nb = N_BUCKETS // 2
    ret = jnp.where(rel > 0, nb, 0)
    n = jnp.abs(rel)
    max_exact = nb // 2
    nf = jnp.maximum(n, 1).astype(jnp.float32)
    large = max_exact + (jnp.log(nf / max_exact) / math.log(MAX_DISTANCE / max_exact)
                         * (nb - max_exact)).astype(jnp.int32)
    large = jnp.minimum(large, nb - 1)
    return ret + jnp.where(n < max_exact, n, large)


def _bucket_map(q_pos, k_pos):
    bkt = _t5_bucket(k_pos[None, :] - q_pos[:, None]).astype(jnp.int32)
    return jnp.pad(bkt, ((0, 0), (0, KEY_TILE - k_pos.shape[0])), constant_values=-1)


ROUTE_LANES = LANES
GATE_TN = 1024
OUT_VMEM_LIMIT = 60 * 1024 * 1024


def _outproj_kernel(hgp_ref, hgs_ref, op_ref, os_ref, ga0_ref, ga1_ref, gb0_ref, gb1_ref, xp_ref, xs_ref,
                    g1_ref, sh2_ref, sc2_ref, ng_ref, wr_ref, wa_ref, wo_ref, wrt_ref, brt_ref,
                    x1_ref, n2_ref, rt_ref, *, n_prompt_tiles):
    tm = x1_ref.shape[0]
    is_prompt = pl.program_id(0) < n_prompt_tiles
    hg = jnp.where(is_prompt, hgp_ref[...], hgs_ref[...])
    o = jnp.where(is_prompt, op_ref[...], os_ref[...])
    x = jnp.where(is_prompt, xp_ref[...], xs_ref[...])
    ya = jnp.dot(hg, wr_ref[...], preferred_element_type=F32)
    yb = jnp.dot(o, wa_ref[...], preferred_element_type=F32)
    halves = []
    for k, (ga_ref, gb_ref) in enumerate(((ga0_ref, gb0_ref), (ga1_ref, gb1_ref))):
        cols = slice(k * GATE_TN, (k + 1) * GATE_TN)
        halves.append((_sigmoid(ga_ref[...]) * ya[:, cols] + _sigmoid(gb_ref[...]) * yb[:, cols]).astype(BF16))
    merged = jnp.concatenate(halves, axis=1)
    mix = jnp.dot(merged, wo_ref[...], preferred_element_type=F32)
    x1 = x + _bcast_rows(g1_ref[...], tm) * mix
    x1_ref[...] = x1
    n2 = _rms_modulate(x1, ng_ref[...], sc2_ref[...], sh2_ref[...])
    n2_ref[...] = n2

    lg = jnp.dot(n2.astype(BF16), wrt_ref[...].astype(BF16), preferred_element_type=F32) + brt_ref[...]

    lane = lax.broadcasted_iota(jnp.int32, (tm, ROUTE_LANES), 1)
    lane_f = lane.astype(F32)
    e_lane = lane - N_GROUPS
    lane_group = (e_lane >> 3).astype(F32)

    def first_argmax(vals, vmax):
        return jnp.min(jnp.where(vals == vmax, lane_f, float(ROUTE_LANES)), axis=-1, keepdims=True)

    gl = jnp.where(lane < N_GROUPS, lg, NEG_INF)
    gmax = jnp.max(gl, axis=-1, keepdims=True)
    g_idx = first_argmax(gl, gmax)
    g_w = 1.0 / jnp.sum(jnp.exp(gl - gmax), axis=-1, keepdims=True)
    in_group = jnp.where((e_lane >= 0) & (e_lane < N_EXPERTS), lane_group, -1.0) == g_idx
    el = jnp.where(in_group, lg, NEG_INF)
    v1 = jnp.max(el, axis=-1, keepdims=True)
    i1 = first_argmax(el, v1)
    el2 = jnp.where(lane_f == i1, NEG_INF, el)
    v2 = jnp.max(el2, axis=-1, keepdims=True)
    i2 = first_argmax(el2, v2)
    e21 = jnp.exp(v2 - v1)
    w1 = g_w / (1.0 + e21)
    w2 = g_w * e21 / (1.0 + e21)
    e1 = i1 - float(N_GROUPS)
    e2 = i2 - float(N_GROUPS)
    rt_ref[...] = jnp.where(lane == 0, e1, jnp.where(lane == 1, e2, jnp.where(lane == 2, w1,
                            jnp.where(lane == 3, w2, 0.0))))


def _outproj(hg_p, hg_s, o_p, o_s, z, x_p, x_s, g1, sh2, sc2, norm_g, wr, wa, wo, w_route, b_route):
    tm = ROW_TILE
    t = z.shape[0]
    seg = tm // MOD_ROWS
    n_p = x_p.shape[0] // tm
    row = lambda i: (i, 0)
    fix = lambda i: (0, 0)
    row_p = lambda i: (jnp.minimum(i, n_p - 1), 0)
    row_s = lambda i: (jnp.maximum(i - n_p, 0), 0)
    once = pl.Buffered(1)

    def gate(col):
        return pl.BlockSpec((tm, GATE_TN), lambda i: (i, col // GATE_TN))

    return pl.pallas_call(
        functools.partial(_outproj_kernel, n_prompt_tiles=n_p),
        grid=(t // tm,),
        in_specs=[pl.BlockSpec((tm, D_RNN), row_p), pl.BlockSpec((tm, D_RNN), row_s, pipeline_mode=once),
                  pl.BlockSpec((tm, Q_DIM), row_p), pl.BlockSpec((tm, Q_DIM), row_s, pipeline_mode=once),
                  gate(COL_GA), gate(COL_GA + GATE_TN), gate(COL_GB), gate(COL_GB + GATE_TN),
                  pl.BlockSpec((tm, D_MODEL), row_p), pl.BlockSpec((tm, D_MODEL), row_s, pipeline_mode=once),
                  pl.BlockSpec((seg, D_MODEL), row), pl.BlockSpec((seg, D_MODEL), row),
                  pl.BlockSpec((seg, D_MODEL), row),
                  pl.BlockSpec((1, D_MODEL), fix),
                  pl.BlockSpec((D_RNN, D_MODEL), fix, pipeline_mode=once),
                  pl.BlockSpec((Q_DIM, D_MODEL), fix, pipeline_mode=once),
                  pl.BlockSpec((D_MODEL, D_MODEL), fix, pipeline_mode=once),
                  pl.BlockSpec((D_MODEL, ROUTE_LANES), fix),
                  pl.BlockSpec((1, ROUTE_LANES), fix)],
        out_specs=[pl.BlockSpec((tm, D_MODEL), row), pl.BlockSpec((tm, D_MODEL), row),
                   pl.BlockSpec((tm, ROUTE_LANES), row)],
        out_shape=[jax.ShapeDtypeStruct((t, D_MODEL), F32), jax.ShapeDtypeStruct((t, D_MODEL), F32),
                   jax.ShapeDtypeStruct((t, ROUTE_LANES), F32)],
        compiler_params=_cparams(("arbitrary",), OUT_VMEM_LIMIT),
        name="outproj",
    )(hg_p, hg_s, o_p, o_s, z, z, z, z, x_p, x_s, g1, sh2, sc2, norm_g.reshape(1, D_MODEL),
      wr, wa, wo, w_route, b_route)


def _row_gather(src_hbm, idx_ref, base, dst, sem, n_rows):
    for r in range(n_rows):
        tok = idx_ref[base + r]
        pltpu.make_async_copy(src_hbm.at[pl.ds(tok, 1), :], dst.at[pl.ds(r, 1), :], sem).start()


def _row_gather_wait(src_hbm, dst, sem, n_rows):
    pltpu.make_async_copy(src_hbm.at[pl.ds(0, n_rows), :], dst, sem).wait()


def _moe_kernel(te_ref, tfirst_ref, tvalid_ref, tnext_ref, tbase_ref, src_ref, n2_hbm, wg_hbm, wu_hbm, wd_hbm, y_ref,
                xbuf, xsem, wg_st, wu_st, wd_st, wsem, wg_bf, wu_bf, wd_bf):
    i = pl.program_id(0)
    n_tiles = pl.num_programs(0)
    slot = lax.rem(i, MOE_RING)
    stages = ((wg_hbm, wg_st, wg_bf), (wu_hbm, wu_st, wu_bf), (wd_hbm, wd_st, wd_bf))

    def weight_copy(k, e):
        hbm, st, _ = stages[k]
        return pltpu.make_async_copy(hbm.at[e], st, wsem.at[k])

    def gather_tile(tile):
        tc = jnp.minimum(tile, n_tiles - 1)

        @pl.when((tile < n_tiles) & (tvalid_ref[tc] == 1))
        def _():
            s = lax.rem(tile, MOE_RING)
            _row_gather(n2_hbm, src_ref, tbase_ref[tc], xbuf.at[s], xsem.at[s], MOE_TM)

    @pl.when(i == 0)
    def _():
        for k in range(len(stages)):
            weight_copy(k, te_ref[0]).start(priority=1)
        for ahead in range(MOE_RING - 1):
            gather_tile(ahead)

    @pl.when(tvalid_ref[i] == 1)
    def _():
        gather_tile(i + MOE_RING - 1)

        @pl.when(tfirst_ref[i] == 1)
        def _():
            for k, (_, st, bf) in enumerate(stages):
                weight_copy(k, te_ref[i]).wait()
                bf[...] = st[...].astype(BF16)

            @pl.when(tnext_ref[i] >= 0)
            def _():
                for k in range(len(stages)):
                    weight_copy(k, tnext_ref[i]).start(priority=1)

        _row_gather_wait(n2_hbm, xbuf.at[slot], xsem.at[slot], MOE_TM)
        x = xbuf[slot].astype(BF16)
        hgate = jnp.dot(x, wg_bf[...], preferred_element_type=F32)
        hup = jnp.dot(x, wu_bf[...], preferred_element_type=F32)
        act = (hgate * _sigmoid(hgate) * hup).astype(BF16)
        y_ref[...] = jnp.dot(act, wd_bf[...], preferred_element_type=F32)

    @pl.when(tvalid_ref[i] == 0)
    def _():
        y_ref[...] = jnp.zeros(y_ref.shape, F32)


def _moe(n2, w_gate, w_up, w_down, tile_expert, tile_first, tile_valid, tile_next, tile_base, src_tok):
    n_tiles = tile_expert.shape[0]
    hbm = pl.BlockSpec(memory_space=pl.ANY)
    grid_spec = pltpu.PrefetchScalarGridSpec(
        num_scalar_prefetch=6,
        grid=(n_tiles,),
        in_specs=[hbm, hbm, hbm, hbm],
        out_specs=pl.BlockSpec((MOE_TM, D_MODEL), lambda i, te, tf, tv, tn, tb, st: (i, 0)),
        scratch_shapes=[pltpu.VMEM((MOE_RING, MOE_TM, D_MODEL), F32),
                        pltpu.SemaphoreType.DMA((MOE_RING,)),
                        pltpu.VMEM((D_MODEL, D_EXPERT), F32),
                        pltpu.VMEM((D_MODEL, D_EXPERT), F32),
                        pltpu.VMEM((D_EXPERT, D_MODEL), F32),
                        pltpu.SemaphoreType.DMA((3,)),
                        pltpu.VMEM((D_MODEL, D_EXPERT), BF16),
                        pltpu.VMEM((D_MODEL, D_EXPERT), BF16),
                        pltpu.VMEM((D_EXPERT, D_MODEL), BF16)],
    )
    return pl.pallas_call(
        _moe_kernel,
        grid_spec=grid_spec,
        out_shape=jax.ShapeDtypeStruct((n_tiles * MOE_TM, D_MODEL), F32),
        compiler_params=_cparams(("arbitrary",)),
        name="moe",
    )(tile_expert, tile_first, tile_valid, tile_next, tile_base, src_tok, n2, w_gate, w_up, w_down)


def _route_plan(e1, e2, n_tok):
    experts = jnp.arange(N_EXPERTS, dtype=jnp.int32)
    flat_e = jnp.concatenate([e1, e2])
    onehot = (flat_e[:, None] == experts[None, :]).astype(jnp.int32)
    blk = onehot.astype(F32).reshape(-1, LANES, N_EXPERTS)
    tri = (jnp.arange(LANES)[:, None] >= jnp.arange(LANES)[None, :]).astype(F32)
    within = jnp.einsum("ij,bje->bie", tri, blk)
    before = jnp.cumsum(within[:, -1, :], axis=0) - within[:, -1, :]
    csum = (within + before[:, None, :]).reshape(-1, N_EXPERTS).astype(jnp.int32)
    rank = jnp.sum(csum * onehot, axis=1) - 1
    counts = csum[-1]
    tiles_per = (counts + MOE_TM - 1) // MOE_TM
    tile_end = jnp.cumsum(tiles_per)
    tile_off = tile_end - tiles_per
    slot = jnp.sum(onehot * tile_off[None, :], axis=1) * MOE_TM + rank
    n_tiles = (2 * n_tok) // MOE_TM + N_EXPERTS
    tok = jnp.tile(jnp.arange(n_tok, dtype=jnp.int32), 2)
    _, src_tok = lax.sort((slot, tok), num_keys=1)
    src_tok = jnp.concatenate([src_tok, jnp.zeros((MOE_TM,), jnp.int32)])
    count_off = jnp.cumsum(counts) - counts
    tile_id = jnp.arange(n_tiles, dtype=jnp.int32)
    n_used = tile_end[-1]
    tile_valid = (tile_id < n_used).astype(jnp.int32)
    te = jnp.sum((tile_end[None, :] <= jnp.minimum(tile_id, n_used - 1)[:, None]).astype(jnp.int32), axis=1)
    tile_expert = jnp.minimum(te, N_EXPERTS - 1)
    prev = jnp.concatenate([jnp.full((1,), -1, jnp.int32), tile_expert[:-1]])
    tile_first = (tile_expert != prev).astype(jnp.int32)
    oh_te = (tile_expert[:, None] == experts[None, :]).astype(jnp.int32)

    def of_expert(table):
        return jnp.sum(oh_te * table[None, :], axis=1)

    next_tile = of_expert(tile_end)
    oh_next = (tile_id[None, :] == jnp.minimum(next_tile, n_tiles - 1)[:, None]).astype(jnp.int32)
    tile_next = jnp.where(next_tile < n_used, jnp.sum(oh_next * tile_expert[None, :], axis=1), -1)
    tile_base = jnp.where(tile_valid == 1, of_expert(count_off) + (tile_id - of_expert(tile_off)) * MOE_TM, 0)
    return tile_expert, tile_first, tile_valid, tile_next, tile_base, src_tok, slot[:n_tok], slot[n_tok:]


def _final_kernel(p1_ref, p2_ref, ys_hbm, x1_ref, g2_ref, rt_ref, fg_ref, yp_ref, ysm_ref, ybuf, sem,
                  *, n_prompt_tiles):
    i = pl.program_id(0)
    n_tiles = pl.num_programs(0)
    slot = i % 2

    def start(tile, s):
        _row_gather(ys_hbm, p1_ref, tile * ROW_TILE, ybuf.at[s, 0], sem.at[s], ROW_TILE)
        _row_gather(ys_hbm, p2_ref, tile * ROW_TILE, ybuf.at[s, 1], sem.at[s], ROW_TILE)

    @pl.when(i == 0)
    def _():
        start(0, 0)

    @pl.when(i + 1 < n_tiles)
    def _():
        start(i + 1, 1 - slot)

    _row_gather_wait(ys_hbm, ybuf.at[slot, 0], sem.at[slot], ROW_TILE)
    _row_gather_wait(ys_hbm, ybuf.at[slot, 1], sem.at[slot], ROW_TILE)
    rt = rt_ref[...]
    moe = rt[:, 2:3] * ybuf[slot, 0] + rt[:, 3:4] * ybuf[slot, 1]
    x2 = x1_ref[...] + _bcast_rows(g2_ref[...], ROW_TILE) * moe
    y = x2 * lax.rsqrt(jnp.mean(x2 * x2, axis=-1, keepdims=True) + EPS) * fg_ref[...]

    @pl.when(i < n_prompt_tiles)
    def _():
        yp_ref[...] = y

    @pl.when(i >= n_prompt_tiles)
    def _():
        ysm_ref[...] = y


def _final(ys, x1, g2, rt, final_g, p1, p2, n_prompt_rows):
    t = x1.shape[0]
    tm = ROW_TILE
    seg = tm // MOD_ROWS
    n_p = n_prompt_rows // tm
    grid_spec = pltpu.PrefetchScalarGridSpec(
        num_scalar_prefetch=2,
        grid=(t // tm,),
        in_specs=[pl.BlockSpec(memory_space=pl.ANY),
                  pl.BlockSpec((tm, D_MODEL), lambda i, a, b: (i, 0)),
                  pl.BlockSpec((seg, D_MODEL), lambda i, a, b: (i, 0)),
                  pl.BlockSpec((tm, ROUTE_LANES), lambda i, a, b: (i, 0)),
                  pl.BlockSpec((1, D_MODEL), lambda i, a, b: (0, 0))],
        out_specs=[pl.BlockSpec((tm, D_MODEL), lambda i, a, b: (jnp.minimum(i, n_p - 1), 0)),
                   pl.BlockSpec((tm, D_MODEL), lambda i, a, b: (jnp.maximum(i - n_p, 0), 0))],
        scratch_shapes=[pltpu.VMEM((2, 2, tm, D_MODEL), F32),
                        pltpu.SemaphoreType.DMA((2,))],
    )
    return pl.pallas_call(
        functools.partial(_final_kernel, n_prompt_tiles=n_p),
        grid_spec=grid_spec,
        out_shape=[jax.ShapeDtypeStruct((n_prompt_rows, D_MODEL), F32),
                   jax.ShapeDtypeStruct((t - n_prompt_rows, D_MODEL), F32)],
        compiler_params=_cparams(("arbitrary",)),
        name="final",
    )(p1, p2, ys, x1, g2, rt, final_g.reshape(1, D_MODEL))


def kernel(x_prompt, x_sample, cache_k_win, cache_v_win, state_conv, state_rglru, c_prompt, c_sample, w_ada, b_ada, norm1_g, norm2_g, w_in, b_in, w_conv, b_conv, w_rg_a, b_rg_a, w_rg_x, b_rg_x, lru_lambda, w_rnn_out, w_attn_out, w_out, attn_sinks, w_route_group, b_route_group, w_route_expert, b_route_expert, w_exp_gate, w_exp_up, w_exp_down, rel_bias_table, final_norm_g):
    n_b, seq, _ = x_prompt.shape
    d_b, d_seq, _ = x_sample.shape
    assert w_ada.shape[0] == 1, "single trunk layer"
    assert seq % (ATT_CPB * CHUNK) == 0 and seq % MOD_ROWS == 0 and d_seq == MOD_ROWS
    assert d_seq >= CONV_W - 1 and d_b == N_CHAINS
    t_p, t_s = n_b * seq, d_b * d_seq
    t = t_p + t_s
    assert t_s == ROW_TILE and t_p % ROW_TILE == 0
    cw = cache_k_win.shape[2]
    l = 0

    x_p = x_prompt.reshape(t_p, D_MODEL)
    x_s = x_sample.reshape(t_s, D_MODEL)

    n_c = n_b + d_b
    c_all = jnp.pad(jnp.concatenate([c_prompt, c_sample], axis=0), ((0, -n_c % SUBLANES), (0, 0)))
    mod = _ada(c_all, w_ada[l], b_ada[l])

    def per_segment(m):
        return jnp.concatenate([jnp.repeat(m[:n_b], seq // MOD_ROWS, axis=0),
                                jnp.repeat(m[n_b:n_c], d_seq // MOD_ROWS, axis=0)], axis=0)

    sh1, sc1, g1, sh2, sc2, g2 = [per_segment(m) for m in jnp.split(mod, 6, axis=-1)]

    z = _inproj(_norm1(x_p, x_s, sc1, sh1, norm1_g[l]), w_in[l], b_in[l])

    w_gate = jnp.concatenate([w_rg_a[l], w_rg_x[l]], axis=-1)
    b_gate = jnp.concatenate([b_rg_a[l], b_rg_x[l]], axis=-1)[:, None, :]
    hg_p, hl_p = _rglru(z, jnp.zeros((n_b, SUBLANES, D_RNN), F32), jnp.zeros((n_b, N_CHAINS, D_RNN), F32),
                        w_conv[l], b_conv[l], w_gate, b_gate, lru_lambda[l],
                        n_blocks=n_b, n_seq=1, seq_len=seq, row_block0=0, link=True)
    state8 = jnp.pad(state_conv[l], ((0, 0), (SUBLANES - (CONV_W - 1), 0), (0, 0)))
    hg_s, hl_s = _rglru(z, state8, state_rglru[l][None], w_conv[l], b_conv[l], w_gate, b_gate, lru_lambda[l],
                        n_blocks=1, n_seq=d_b, seq_len=d_seq, row_block0=t_p // t_s, link=False)

    sinks = attn_sinks[l]
    bkt_p = _bucket_map(WIN_CHUNKS * CHUNK + jnp.arange(CHUNK), jnp.arange(SPAN))
    o_p = _attn_band(z, rel_bias_table, sinks, bkt_p, n_batch=n_b, seq_len=seq)
    k_new = z[t_p:, COL_K:COL_K + KV_DIM].reshape(d_b, d_seq, KV_DIM)
    v_new = z[t_p:, COL_V:COL_V + KV_DIM].reshape(d_b, d_seq, KV_DIM)
    k_all = jnp.concatenate([cache_k_win[l].reshape(d_b, cw, KV_DIM), k_new], axis=1)
    v_all = jnp.concatenate([cache_v_win[l].reshape(d_b, cw, KV_DIM), v_new], axis=1)
    bkt_s = _bucket_map(cw + jnp.arange(d_seq), jnp.arange(cw + d_seq))
    o_s = _attn_step(z, k_all, v_all, rel_bias_table, sinks, bkt_s, n_batch=d_b, n_q=d_seq, row_block0=t_p // d_seq)

    n_route = N_GROUPS + N_EXPERTS
    w_route = jnp.pad(jnp.concatenate([w_route_group[l], w_route_expert[l]], axis=1),
                      ((0, 0), (0, ROUTE_LANES - n_route)))
    b_route = jnp.pad(jnp.concatenate([b_route_group[l], b_route_expert[l]]),
                      (0, ROUTE_LANES - n_route)).reshape(1, ROUTE_LANES)
    x1, n2, rt = _outproj(hg_p, hg_s, o_p, o_s, z, x_p, x_s, g1, sh2, sc2, norm2_g[l],
                          w_rnn_out[l].astype(BF16), w_attn_out[l].astype(BF16), w_out[l].astype(BF16),
                          w_route, b_route)

    e1 = rt[:, 0].astype(jnp.int32)
    e2 = rt[:, 1].astype(jnp.int32)
    tile_expert, tile_first, tile_valid, tile_next, tile_base, src_tok, p1, p2 = _route_plan(e1, e2, t)
    ys = _moe(n2, w_exp_gate[l], w_exp_up[l], w_exp_down[l], tile_expert, tile_first, tile_valid, tile_next,
              tile_base, src_tok)
    y_p, y_s = _final(ys, x1, g2, rt, final_norm_g, p1, p2, t_p)

    win = min(WINDOW, seq)

    def tail(col0, width, n_rows):
        return jnp.stack([z[(b + 1) * seq - n_rows:(b + 1) * seq, col0:col0 + width] for b in range(n_b)])

    kp = tail(COL_K, KV_DIM, win).reshape(n_b, win, N_KV_HEADS, HEAD_DIM)
    vp = tail(COL_V, KV_DIM, win).reshape(n_b, win, N_KV_HEADS, HEAD_DIM)
    cp = tail(COL_XR, D_RNN, CONV_W - 1)
    rp = hl_p[:, N_CHAINS - 1, :]
    ks = k_all[:, -cw:].reshape(d_b, cw, N_KV_HEADS, HEAD_DIM)
    vs = v_all[:, -cw:].reshape(d_b, cw, N_KV_HEADS, HEAD_DIM)
    cs = z[t_p:, COL_XR:COL_XR + D_RNN].reshape(d_b, d_seq, D_RNN)[:, -(CONV_W - 1):]
    rs = hl_s[0]
    return (y_p.reshape(n_b, seq, D_MODEL), y_s.reshape(d_b, d_seq, D_MODEL),
            kp[None], vp[None], cp[None], rp[None], ks[None], vs[None], cs[None], rs[None])
```

```python
import functools
import math

import jax
import jax.numpy as jnp
from jax import lax
from jax.experimental import pallas as pl
from jax.experimental.pallas import tpu as pltpu

F32 = jnp.float32
BF16 = jnp.bfloat16

D_MODEL = 2048
D_RNN = 2048
RNN_BLOCK = 128
CONV_W = 4
LRU_C = 8.0
N_HEADS = 16
N_KV_HEADS = 4
HEAD_DIM = 128
GQA_GROUP = N_HEADS // N_KV_HEADS
Q_DIM = N_HEADS * HEAD_DIM
KV_DIM = N_KV_HEADS * HEAD_DIM
CHUNK = 64
WINDOW = 128
WIN_CHUNKS = WINDOW // CHUNK
SPAN = (WIN_CHUNKS + 1) * CHUNK
N_BUCKETS = 32
MAX_DISTANCE = 128
N_GROUPS = 4
E_PER_GROUP = 8
N_EXPERTS = N_GROUPS * E_PER_GROUP
D_EXPERT = 512
EPS = 1e-6
NEG_INF = -1e30
D_IN = 2 * D_RNN + Q_DIM + 2 * KV_DIM + 2 * D_MODEL
COL_XR, COL_GR, COL_Q = 0, D_RNN, 2 * D_RNN
COL_K = COL_Q + Q_DIM
COL_V = COL_K + KV_DIM
COL_GA = COL_V + KV_DIM
COL_GB = COL_GA + D_MODEL

LANES = 128
SUBLANES = 8
MOD_ROWS = 32
KEY_TILE = 256
VMEM_LIMIT = 56 * 1024 * 1024
VMEM_LIMIT_BIG = 60 * 1024 * 1024
ADA_TN = 1024

ROW_TILE = 256
IN_TM_CHOICES = (1408, 768, 256)
IN_TN = 1024
MOE_TM = 192
MOE_RING = 3


def _sigmoid(x):
    return 0.5 * jnp.tanh(0.5 * x) + 0.5


def _gelu_tanh(x):
    return 0.5 * x * (1.0 + jnp.tanh(math.sqrt(2.0 / math.pi) * (x + 0.044715 * (x * x * x))))


def _bcast_rows(v, rows):
    n, d = v.shape
    return jnp.broadcast_to(v[:, None, :], (n, MOD_ROWS, d)).reshape(rows, d)


def _rms_modulate(x, gain, scale_seg, shift_seg):
    rows = x.shape[0]
    y = x * lax.rsqrt(jnp.mean(x * x, axis=-1, keepdims=True) + EPS) * gain
    return y * (1.0 + _bcast_rows(scale_seg, rows)) + _bcast_rows(shift_seg, rows)


def _cparams(sem, vmem_limit=VMEM_LIMIT):
    return pltpu.CompilerParams(dimension_semantics=sem, vmem_limit_bytes=vmem_limit)


def _ada_kernel(c_ref, w_ref, b_ref, o_ref):
    c = c_ref[...]
    s = (c * _sigmoid(c)).astype(BF16)
    o_ref[...] = jnp.dot(s, w_ref[...].astype(BF16), preferred_element_type=F32) + b_ref[...]


def _ada(c_all, w_ada, b_ada):
    rows = c_all.shape[0]
    n = w_ada.shape[1]
    tn = ADA_TN
    return pl.pallas_call(
        _ada_kernel,
        grid=(n // tn,),
        in_specs=[pl.BlockSpec((rows, D_MODEL), lambda j: (0, 0)),
                  pl.BlockSpec((D_MODEL, tn), lambda j: (0, j)),
                  pl.BlockSpec((1, tn), lambda j: (0, j))],
        out_specs=pl.BlockSpec((rows, tn), lambda j: (0, j)),
        out_shape=jax.ShapeDtypeStruct((rows, n), F32),
        compiler_params=_cparams(("arbitrary",)),
        name="ada",
    )(c_all, w_ada, b_ada.reshape(1, n))


NORM_NSUB = 3


def _norm1_kernel(x0_ref, x1_ref, x2_ref, xs_ref, sc_ref, sh_ref, g_ref, n1_ref, *, n_prompt_tiles):
    tail_is_sample = pl.program_id(0) * NORM_NSUB + (NORM_NSUB - 1) >= n_prompt_tiles
    gain = g_ref[...]
    segs = ROW_TILE // MOD_ROWS
    for r, x_ref in enumerate((x0_ref, x1_ref, x2_ref)):
        for s in range(segs):
            rows = slice(s * MOD_ROWS, (s + 1) * MOD_ROWS)
            x = x_ref[rows, :]
            if r == NORM_NSUB - 1:
                x = jnp.where(tail_is_sample, xs_ref[rows, :], x)
            y = x * lax.rsqrt(jnp.mean(x * x, axis=-1, keepdims=True) + EPS) * gain
            k = r * segs + s
            out_rows = slice(k * MOD_ROWS, (k + 1) * MOD_ROWS)
            n1_ref[out_rows, :] = (y * (1.0 + sc_ref[k:k + 1, :]) + sh_ref[k:k + 1, :]).astype(BF16)


def _norm1(x_p, x_s, sc1, sh1, norm_g):
    n_p = x_p.shape[0] // ROW_TILE
    t = x_p.shape[0] + x_s.shape[0]
    tm = NORM_NSUB * ROW_TILE
    assert x_s.shape[0] == ROW_TILE and t % tm == 0
    seg = tm // MOD_ROWS

    def sub(r):
        return pl.BlockSpec((ROW_TILE, D_MODEL), lambda i: (jnp.minimum(i * NORM_NSUB + r, n_p - 1), 0))

    return pl.pallas_call(
        functools.partial(_norm1_kernel, n_prompt_tiles=n_p),
        grid=(t // tm,),
        in_specs=[sub(0), sub(1), sub(2),
                  pl.BlockSpec((ROW_TILE, D_MODEL), lambda i: (0, 0)),
                  pl.BlockSpec((seg, D_MODEL), lambda i: (i, 0)),
                  pl.BlockSpec((seg, D_MODEL), lambda i: (i, 0)),
                  pl.BlockSpec((1, D_MODEL), lambda i: (0, 0))],
        out_specs=pl.BlockSpec((tm, D_MODEL), lambda i: (i, 0)),
        out_shape=jax.ShapeDtypeStruct((t, D_MODEL), BF16),
        compiler_params=_cparams(("arbitrary",)),
        name="norm1",
    )(x_p, x_p, x_p, x_s, sc1, sh1, norm_g.reshape(1, D_MODEL))


def _inproj_kernel(n1_ref, w_ref, b_ref, z_ref, w_bf):
    @pl.when(pl.program_id(1) == 0)
    def _():
        w_bf[...] = w_ref[...].astype(BF16)

    z_ref[...] = jnp.dot(n1_ref[...], w_bf[...], preferred_element_type=F32) + b_ref[...]


def _inproj(n1, w_in, b_in):
    t = n1.shape[0]
    tm = next(m for m in IN_TM_CHOICES if t % m == 0)
    return pl.pallas_call(
        _inproj_kernel,
        grid=(D_IN // IN_TN, t // tm),
        in_specs=[pl.BlockSpec((tm, D_MODEL), lambda j, i: (i, 0)),
                  pl.BlockSpec((D_MODEL, IN_TN), lambda j, i: (0, j)),
                  pl.BlockSpec((1, IN_TN), lambda j, i: (0, j))],
        out_specs=pl.BlockSpec((tm, IN_TN), lambda j, i: (i, j)),
        out_shape=jax.ShapeDtypeStruct((t, D_IN), F32),
        scratch_shapes=[pltpu.VMEM((D_MODEL, IN_TN), BF16)],
        compiler_params=_cparams(("arbitrary", "arbitrary")),
        name="inproj",
    )(n1, w_in, b_in.reshape(1, D_IN))


RG_CW = 256
RG_SLABS = RG_CW // LANES
N_CHAINS = SUBLANES
CHAIN_PAD = 8


def _rglru_kernel(xr_ref, gr_ref, st_ref, h0_ref, wc_ref, bc_ref, wg_ref, bg_ref, lam_ref,
                  hg_ref, hl_ref, ext_scr, a_scr, u_scr, *, n_seq, seq_len, link):
    rows = n_seq * seq_len
    cl = rows // N_CHAINS
    pitch = cl + CHAIN_PAD

    for s in range(n_seq):
        ext_scr[s, 0:SUBLANES, :] = st_ref[s]
        ext_scr[s, SUBLANES:SUBLANES + seq_len, :] = xr_ref[s * seq_len:(s + 1) * seq_len, :]

    z = -lam_ref[...]
    softplus = jnp.maximum(z, 0.0) + jnp.log1p(jnp.exp(-jnp.abs(z)))
    nq = (-0.25 * LRU_C) * softplus

    for c in range(N_CHAINS):
        s, r0 = divmod(c * cl, seq_len)
        xc = bc_ref[...]
        for j in range(CONV_W):
            xc = xc + wc_ref[j:j + 1, :] * ext_scr[s, pl.ds(SUBLANES - (CONV_W - 1) + j + r0, cl), :]
        for sl in range(RG_SLABS):
            lanes = slice(sl * LANES, (sl + 1) * LANES)
            xb = xc[:, lanes]
            g = jnp.dot(xb.astype(BF16), wg_ref[sl].astype(BF16), preferred_element_type=F32) + bg_ref[sl]
            tr = jnp.tanh(0.5 * g[:, :LANES])
            ti = jnp.tanh(0.5 * g[:, LANES:])
            th = jnp.tanh(nq[:, lanes] * (tr + 1.0))
            rcp = 1.0 / (1.0 - th)
            a_scr[sl, c * pitch:c * pitch + cl, :] = (1.0 + th) * rcp
            u_scr[sl, c * pitch:c * pitch + cl, :] = (jnp.sqrt(-th) * rcp) * ((ti + 1.0) * xb)

    def step(t, carry):
        hs, ps = carry
        new_h, new_p = [], []
        for sl in range(RG_SLABS):
            a = a_scr[sl, pl.ds(t, N_CHAINS, stride=pitch), :]
            u = u_scr[sl, pl.ds(t, N_CHAINS, stride=pitch), :]
            h = a * hs[sl] + u
            u_scr[sl, pl.ds(t, N_CHAINS, stride=pitch), :] = h
            new_h.append(h)
            if link:
                p = a * ps[sl]
                a_scr[sl, pl.ds(t, N_CHAINS, stride=pitch), :] = p
                new_p.append(p)
            else:
                new_p.append(ps[sl])
        return tuple(new_h), tuple(new_p)

    h_init = tuple(h0_ref[0, :, sl * LANES:(sl + 1) * LANES] for sl in range(RG_SLABS))
    p_init = tuple(jnp.ones((N_CHAINS, LANES), F32) for _ in range(RG_SLABS))
    h_end, p_end = lax.fori_loop(0, cl, step, (h_init, p_init), unroll=8)

    row = lax.broadcasted_iota(jnp.int32, (N_CHAINS, LANES), 0)
    for sl in range(RG_SLABS):
        lanes = slice(sl * LANES, (sl + 1) * LANES)
        if link:
            def shift_down(v):
                return jnp.where(row == 0, 0.0, pltpu.roll(v, 1, axis=0))
            hh = h_end[sl]
            for _ in range(N_CHAINS - 1):
                hh = h_end[sl] + p_end[sl] * shift_down(hh)
            carry_in = shift_down(hh)
        else:
            hh = h_end[sl]
        hl_ref[0, :, lanes] = hh
        for c in range(N_CHAINS):
            h = u_scr[sl, c * pitch:c * pitch + cl, :]
            if link:
                h = h + a_scr[sl, c * pitch:c * pitch + cl, :] * carry_in[c:c + 1, :]
            gg = _gelu_tanh(gr_ref[c * cl:(c + 1) * cl, lanes])
            hg_ref[c * cl:(c + 1) * cl, lanes] = (h * gg).astype(BF16)


def _rglru(z, state8, h0, w_conv, b_conv, w_gate, b_gate, lam, *, n_blocks, n_seq, seq_len, row_block0, link):
    rows = n_seq * seq_len
    cl = rows // N_CHAINS
    ncb = D_RNN // RG_CW
    gr0 = COL_GR // RG_CW
    kern = functools.partial(_rglru_kernel, n_seq=n_seq, seq_len=seq_len, link=link)
    return pl.pallas_call(
        kern,
        grid=(n_blocks, ncb),
        in_specs=[pl.BlockSpec((rows, RG_CW), lambda b, n: (row_block0 + b, n)),
                  pl.BlockSpec((rows, RG_CW), lambda b, n: (row_block0 + b, gr0 + n)),
                  pl.BlockSpec((n_seq, SUBLANES, RG_CW), lambda b, n: (b, 0, n)),
                  pl.BlockSpec((1, N_CHAINS, RG_CW), lambda b, n: (b, 0, n)),
                  pl.BlockSpec((CONV_W, RG_CW), lambda b, n: (0, n)),
                  pl.BlockSpec((1, RG_CW), lambda b, n: (0, n)),
                  pl.BlockSpec((RG_SLABS, RNN_BLOCK, 2 * RNN_BLOCK), lambda b, n: (n, 0, 0)),
                  pl.BlockSpec((RG_SLABS, 1, 2 * RNN_BLOCK), lambda b, n: (n, 0, 0)),
                  pl.BlockSpec((1, RG_CW), lambda b, n: (0, n))],
        out_specs=[pl.BlockSpec((rows, RG_CW), lambda b, n: (b, n)),
                   pl.BlockSpec((1, N_CHAINS, RG_CW), lambda b, n: (b, 0, n))],
        out_shape=[jax.ShapeDtypeStruct((n_blocks * rows, D_RNN), BF16),
                   jax.ShapeDtypeStruct((n_blocks, N_CHAINS, D_RNN), F32)],
        scratch_shapes=[pltpu.VMEM((n_seq, SUBLANES + seq_len, RG_CW), F32),
                        pltpu.VMEM((RG_SLABS, N_CHAINS * (cl + CHAIN_PAD), LANES), F32),
                        pltpu.VMEM((RG_SLABS, N_CHAINS * (cl + CHAIN_PAD), LANES), F32)],
        compiler_params=_cparams(("arbitrary", "arbitrary")),
        name="rglru_link" if link else "rglru_step",
    )(z, z, state8, h0, w_conv, b_conv.reshape(1, D_RNN), w_gate, b_gate, lam.reshape(1, D_RNN))


def _build_bias(tbl_ref, bkt_ref, bias_scr, n_q):
    bkt = bkt_ref[...]
    base = jnp.where(bkt < 0, NEG_INF, 0.0)
    for h in range(N_HEADS):
        acc = base
        for bk in range(N_BUCKETS):
            acc = jnp.where(bkt == bk, tbl_ref[bk, h], acc)
        hk, g = divmod(h, GQA_GROUP)
        bias_scr[hk, g * n_q:(g + 1) * n_q, :] = acc


def _sink_columns(sink_ref, n_q):
    row = lax.broadcasted_iota(jnp.int32, (GQA_GROUP * n_q, 1), 0)
    cols = []
    for hk in range(N_KV_HEADS):
        col = jnp.full((GQA_GROUP * n_q, 1), sink_ref[hk * GQA_GROUP], F32)
        for g in range(1, GQA_GROUP):
            col = jnp.where(row >= g * n_q, sink_ref[hk * GQA_GROUP + g], col)
        cols.append(col)
    return cols


def _attend(q_of, k_of, v_of, bias_scr, sinks, key_ok, store, n_q):
    scores = []
    for hk in range(N_KV_HEADS):
        qg = jnp.concatenate([q_of(hk * GQA_GROUP + g) for g in range(GQA_GROUP)], axis=0)
        s = lax.dot_general(qg, k_of(hk), (((1,), (1,)), ((), ())), preferred_element_type=F32)
        s = s * (HEAD_DIM ** -0.5) + bias_scr[hk]
        if key_ok is not None:
            s = jnp.where(key_ok, s, NEG_INF)
        scores.append(s)
    probs = []
    for hk in range(N_KV_HEADS):
        s, sink = scores[hk], sinks[hk]
        m = jnp.maximum(jnp.max(s, axis=-1, keepdims=True), sink)
        p = jnp.exp(s - m)
        inv = 1.0 / (jnp.sum(p, axis=-1, keepdims=True) + jnp.exp(sink - m))
        probs.append((p * inv).astype(BF16))
    for hk in range(N_KV_HEADS):
        o = jnp.dot(probs[hk], v_of(hk), preferred_element_type=F32)
        for g in range(GQA_GROUP):
            store(hk * GQA_GROUP + g, o[g * n_q:(g + 1) * n_q, :].astype(BF16))


ATT_CPB = 8
ATT_LEAD = WIN_CHUNKS * CHUNK
ATT_TAIL = KEY_TILE - SPAN


def _attn_band_kernel(tbl_ref, sink_ref, q_ref, k_ref, v_ref, bkt_ref, o_ref, kpad, vpad, bias_scr, *, seq_len):
    b = pl.program_id(0)
    cg = pl.program_id(1)

    @pl.when((b == 0) & (cg == 0))
    def _():
        _build_bias(tbl_ref, bkt_ref, bias_scr, CHUNK)

    @pl.when(cg == 0)
    def _():
        for ref, pad in ((k_ref, kpad), (v_ref, vpad)):
            pad[0:ATT_LEAD, :] = jnp.zeros((ATT_LEAD, KV_DIM), BF16)
            pad[ATT_LEAD:ATT_LEAD + seq_len, :] = ref[...].astype(BF16)
            pad[ATT_LEAD + seq_len:, :] = jnp.zeros((ATT_TAIL, KV_DIM), BF16)

    kidx = lax.broadcasted_iota(jnp.int32, (1, KEY_TILE), 1)
    sinks = _sink_columns(sink_ref, CHUNK)

    def chunk(c, carry):
        q0 = pl.multiple_of(c * CHUNK, CHUNK)
        start = pl.multiple_of((cg * ATT_CPB + c) * CHUNK, CHUNK)
        key_ok = start + kidx >= ATT_LEAD

        def head_cols(h):
            return slice(h * HEAD_DIM, (h + 1) * HEAD_DIM)

        def store(h, val):
            o_ref[pl.ds(q0, CHUNK), head_cols(h)] = val

        _attend(lambda h: q_ref[pl.ds(q0, CHUNK), head_cols(h)].astype(BF16),
                lambda hk: kpad[pl.ds(start, KEY_TILE), head_cols(hk)],
                lambda hk: vpad[pl.ds(start, KEY_TILE), head_cols(hk)],
                bias_scr, sinks, key_ok, store, CHUNK)
        return carry

    lax.fori_loop(0, ATT_CPB, chunk, 0)


def _attn_band(z, table, sinks, bkt, *, n_batch, seq_len):
    rows = ATT_CPB * CHUNK
    ng = seq_len // rows
    kern = functools.partial(_attn_band_kernel, seq_len=seq_len)
    smem = pl.BlockSpec(memory_space=pltpu.SMEM)
    pad_rows = ATT_LEAD + seq_len + ATT_TAIL
    return pl.pallas_call(
        kern,
        grid=(n_batch, ng),
        in_specs=[smem, smem,
                  pl.BlockSpec((rows, Q_DIM), lambda b, c: (b * ng + c, COL_Q // Q_DIM)),
                  pl.BlockSpec((seq_len, KV_DIM), lambda b, c: (b, COL_K // KV_DIM)),
                  pl.BlockSpec((seq_len, KV_DIM), lambda b, c: (b, COL_V // KV_DIM)),
                  pl.BlockSpec((CHUNK, KEY_TILE), lambda b, c: (0, 0))],
        out_specs=pl.BlockSpec((rows, Q_DIM), lambda b, c: (b * ng + c, 0)),
        out_shape=jax.ShapeDtypeStruct((n_batch * seq_len, Q_DIM), BF16),
        scratch_shapes=[pltpu.VMEM((pad_rows, KV_DIM), BF16),
                        pltpu.VMEM((pad_rows, KV_DIM), BF16),
                        pltpu.VMEM((N_KV_HEADS, GQA_GROUP * CHUNK, KEY_TILE), F32)],
        compiler_params=_cparams(("arbitrary", "arbitrary")),
        name="attn_band",
    )(table, sinks, z, z, z, bkt)


def _attn_step_kernel(tbl_ref, sink_ref, q_ref, k_ref, v_ref, bkt_ref, o_ref, kbuf, vbuf, bias_scr,
                      *, n_keys, n_q):
    b = pl.program_id(0)

    @pl.when(b == 0)
    def _():
        _build_bias(tbl_ref, bkt_ref, bias_scr, n_q)
        kbuf[n_keys:, :] = jnp.zeros((KEY_TILE - n_keys, KV_DIM), BF16)
        vbuf[n_keys:, :] = jnp.zeros((KEY_TILE - n_keys, KV_DIM), BF16)

    kbuf[0:n_keys, :] = k_ref[0].astype(BF16)
    vbuf[0:n_keys, :] = v_ref[0].astype(BF16)

    def head_cols(h):
        return slice(h * HEAD_DIM, (h + 1) * HEAD_DIM)

    def store(h, val):
        o_ref[:, head_cols(h)] = val

    _attend(lambda h: q_ref[:, head_cols(h)].astype(BF16),
            lambda hk: kbuf[:, head_cols(hk)], lambda hk: vbuf[:, head_cols(hk)],
            bias_scr, _sink_columns(sink_ref, n_q), None, store, n_q)


def _attn_step(z, k_all, v_all, table, sinks, bkt, *, n_batch, n_q, row_block0):
    n_keys = k_all.shape[1]
    kern = functools.partial(_attn_step_kernel, n_keys=n_keys, n_q=n_q)
    smem = pl.BlockSpec(memory_space=pltpu.SMEM)
    return pl.pallas_call(
        kern,
        grid=(n_batch,),
        in_specs=[smem, smem,
                  pl.BlockSpec((n_q, Q_DIM), lambda b: (row_block0 + b, COL_Q // Q_DIM)),
                  pl.BlockSpec((1, n_keys, KV_DIM), lambda b: (b, 0, 0)),
                  pl.BlockSpec((1, n_keys, KV_DIM), lambda b: (b, 0, 0)),
                  pl.BlockSpec((n_q, KEY_TILE), lambda b: (0, 0))],
        out_specs=pl.BlockSpec((n_q, Q_DIM), lambda b: (b, 0)),
        out_shape=jax.ShapeDtypeStruct((n_batch * n_q, Q_DIM), BF16),
        scratch_shapes=[pltpu.VMEM((KEY_TILE, KV_DIM), BF16),
                        pltpu.VMEM((KEY_TILE, KV_DIM), BF16),
                        pltpu.VMEM((N_KV_HEADS, GQA_GROUP * n_q, KEY_TILE), F32)],
        compiler_params=_cparams(("arbitrary",)),
        name="attn_step",
    )(table, sinks, z, k_all, v_all, bkt)


def _t5_bucket(rel):
    nb = N_BUCKETS // 2
    ret = jnp.where(rel > 0, nb, 0)
    n = jnp.abs(rel)
    max_exact = nb // 2
    nf = jnp.maximum(n, 1).astype(jnp.float32)
    large = max_exact + (jnp.log(nf / max_exact) / math.log(MAX_DISTANCE / max_exact)
                         * (nb - max_exact)).astype(jnp.int32)
    large = jnp.minimum(large, nb - 1)
    return ret + jnp.where(n < max_exact, n, large)


def _bucket_map(q_pos, k_pos):
    bkt = _t5_bucket(k_pos[None, :] - q_pos[:, None]).astype(jnp.int32)
    return jnp.pad(bkt, ((0, 0), (0, KEY_TILE - k_pos.shape[0])), constant_values=-1)


ROUTE_LANES = LANES
GATE_TN = 1024


def _outproj_kernel(hgp_ref, hgs_ref, op_ref, os_ref, ga0_ref, ga1_ref, gb0_ref, gb1_ref, xp_ref, xs_ref,
                    g1_ref, sh2_ref, sc2_ref, ng_ref, wr_ref, wa_ref, wo_ref, wrt_ref, brt_ref,
                    x1_ref, n2_ref, rt_ref, *, n_prompt_tiles):
    tm = x1_ref.shape[0]
    is_prompt = pl.program_id(0) < n_prompt_tiles
    hg = jnp.where(is_prompt, hgp_ref[...], hgs_ref[...])
    o = jnp.where(is_prompt, op_ref[...], os_ref[...])
    x = jnp.where(is_prompt, xp_ref[...], xs_ref[...])
    ya = jnp.dot(hg, wr_ref[...], preferred_element_type=F32)
    yb = jnp.dot(o, wa_ref[...], preferred_element_type=F32)
    halves = []
    for k, (ga_ref, gb_ref) in enumerate(((ga0_ref, gb0_ref), (ga1_ref, gb1_ref))):
        cols = slice(k * GATE_TN, (k + 1) * GATE_TN)
        halves.append((_sigmoid(ga_ref[...]) * ya[:, cols] + _sigmoid(gb_ref[...]) * yb[:, cols]).astype(BF16))
    merged = jnp.concatenate(halves, axis=1)
    mix = jnp.dot(merged, wo_ref[...], preferred_element_type=F32)
    x1 = x + _bcast_rows(g1_ref[...], tm) * mix
    x1_ref[...] = x1
    n2 = _rms_modulate(x1, ng_ref[...], sc2_ref[...], sh2_ref[...])
    n2_ref[...] = n2

    lg = jnp.dot(n2.astype(BF16), wrt_ref[...].astype(BF16), preferred_element_type=F32) + brt_ref[...]

    lane = lax.broadcasted_iota(jnp.int32, (tm, ROUTE_LANES), 1)
    lane_f = lane.astype(F32)
    e_lane = lane - N_GROUPS
    lane_group = (e_lane >> 3).astype(F32)

    def first_argmax(vals, vmax):
        return jnp.min(jnp.where(vals == vmax, lane_f, float(ROUTE_LANES)), axis=-1, keepdims=True)

    gl = jnp.where(lane < N_GROUPS, lg, NEG_INF)
    gmax = jnp.max(gl, axis=-1, keepdims=True)
    g_idx = first_argmax(gl, gmax)
    g_w = 1.0 / jnp.sum(jnp.exp(gl - gmax), axis=-1, keepdims=True)
    in_group = jnp.where((e_lane >= 0) & (e_lane < N_EXPERTS), lane_group, -1.0) == g_idx
    el = jnp.where(in_group, lg, NEG_INF)
    v1 = jnp.max(el, axis=-1, keepdims=True)
    i1 = first_argmax(el, v1)
    el2 = jnp.where(lane_f == i1, NEG_INF, el)
    v2 = jnp.max(el2, axis=-1, keepdims=True)
    i2 = first_argmax(el2, v2)
    e21 = jnp.exp(v2 - v1)
    w1 = g_w / (1.0 + e21)
    w2 = g_w * e21 / (1.0 + e21)
    e1 = i1 - float(N_GROUPS)
    e2 = i2 - float(N_GROUPS)
    rt_ref[...] = jnp.where(lane == 0, e1, jnp.where(lane == 1, e2, jnp.where(lane == 2, w1,
                            jnp.where(lane == 3, w2, 0.0))))


def _outproj(hg_p, hg_s, o_p, o_s, z, x_p, x_s, g1, sh2, sc2, norm_g, wr, wa, wo, w_route, b_route):
    tm = ROW_TILE
    t = z.shape[0]
    seg = tm // MOD_ROWS
    n_p = x_p.shape[0] // tm
    row = lambda i: (i, 0)
    fix = lambda i: (0, 0)
    row_p = lambda i: (jnp.minimum(i, n_p - 1), 0)
    row_s = lambda i: (jnp.maximum(i - n_p, 0), 0)
    once = pl.Buffered(1)

    def gate(col):
        return pl.BlockSpec((tm, GATE_TN), lambda i: (i, col // GATE_TN))

    return pl.pallas_call(
        functools.partial(_outproj_kernel, n_prompt_tiles=n_p),
        grid=(t // tm,),
        in_specs=[pl.BlockSpec((tm, D_RNN), row_p), pl.BlockSpec((tm, D_RNN), row_s, pipeline_mode=once),
                  pl.BlockSpec((tm, Q_DIM), row_p), pl.BlockSpec((tm, Q_DIM), row_s, pipeline_mode=once),
                  gate(COL_GA), gate(COL_GA + GATE_TN), gate(COL_GB), gate(COL_GB + GATE_TN),
                  pl.BlockSpec((tm, D_MODEL), row_p), pl.BlockSpec((tm, D_MODEL), row_s, pipeline_mode=once),
                  pl.BlockSpec((seg, D_MODEL), row), pl.BlockSpec((seg, D_MODEL), row),
                  pl.BlockSpec((seg, D_MODEL), row),
                  pl.BlockSpec((1, D_MODEL), fix),
                  pl.BlockSpec((D_RNN, D_MODEL), fix, pipeline_mode=once),
                  pl.BlockSpec((Q_DIM, D_MODEL), fix, pipeline_mode=once),
                  pl.BlockSpec((D_MODEL, D_MODEL), fix, pipeline_mode=once),
                  pl.BlockSpec((D_MODEL, ROUTE_LANES), fix),
                  pl.BlockSpec((1, ROUTE_LANES), fix)],
        out_specs=[pl.BlockSpec((tm, D_MODEL), row), pl.BlockSpec((tm, D_MODEL), row),
                   pl.BlockSpec((tm, ROUTE_LANES), row)],
        out_shape=[jax.ShapeDtypeStruct((t, D_MODEL), F32), jax.ShapeDtypeStruct((t, D_MODEL), F32),
                   jax.ShapeDtypeStruct((t, ROUTE_LANES), F32)],
        compiler_params=_cparams(("arbitrary",), VMEM_LIMIT_BIG),
        name="outproj",
    )(hg_p, hg_s, o_p, o_s, z, z, z, z, x_p, x_s, g1, sh2, sc2, norm_g.reshape(1, D_MODEL),
      wr, wa, wo, w_route, b_route)


def _row_gather(src_hbm, idx_ref, base, dst, sem, n_rows):
    for r in range(n_rows):
        tok = idx_ref[base + r]
        pltpu.make_async_copy(src_hbm.at[pl.ds(tok, 1), :], dst.at[pl.ds(r, 1), :], sem).start()


def _row_gather_wait(src_hbm, dst, sem, n_rows):
    pltpu.make_async_copy(src_hbm.at[pl.ds(0, n_rows), :], dst, sem).wait()


def _moe_kernel(te_ref, tfirst_ref, tvalid_ref, tnext_ref, tbase_ref, src_ref, n2_hbm, wg_hbm, wu_hbm, wd_hbm, y_ref,
                xbuf, xsem, wg_st, wu_st, wd_st, wsem, wg_bf, wu_bf, wd_bf):
    i = pl.program_id(0)
    n_tiles = pl.num_programs(0)
    slot = lax.rem(i, MOE_RING)
    stages = ((wg_hbm, wg_st, wg_bf), (wu_hbm, wu_st, wu_bf), (wd_hbm, wd_st, wd_bf))

    def weight_copy(k, e):
        hbm, st, _ = stages[k]
        return pltpu.make_async_copy(hbm.at[e], st, wsem.at[k])

    def gather_tile(tile):
        tc = jnp.minimum(tile, n_tiles - 1)

        @pl.when((tile < n_tiles) & (tvalid_ref[tc] == 1))
        def _():
            s = lax.rem(tile, MOE_RING)
            _row_gather(n2_hbm, src_ref, tbase_ref[tc], xbuf.at[s], xsem.at[s], MOE_TM)

    @pl.when(i == 0)
    def _():
        for k in range(len(stages)):
            weight_copy(k, te_ref[0]).start(priority=1)
        for ahead in range(MOE_RING - 1):
            gather_tile(ahead)

    @pl.when(tvalid_ref[i] == 1)
    def _():
        gather_tile(i + MOE_RING - 1)

        @pl.when(tfirst_ref[i] == 1)
        def _():
            for k, (_, st, bf) in enumerate(stages):
                weight_copy(k, te_ref[i]).wait()
                bf[...] = st[...].astype(BF16)

            @pl.when(tnext_ref[i] >= 0)
            def _():
                for k in range(len(stages)):
                    weight_copy(k, tnext_ref[i]).start(priority=1)

        _row_gather_wait(n2_hbm, xbuf.at[slot], xsem.at[slot], MOE_TM)
        x = xbuf[slot].astype(BF16)
        hgate = jnp.dot(x, wg_bf[...], preferred_element_type=F32)
        hup = jnp.dot(x, wu_bf[...], preferred_element_type=F32)
        act = (hgate * _sigmoid(hgate) * hup).astype(BF16)
        y_ref[...] = jnp.dot(act, wd_bf[...], preferred_element_type=F32)

    @pl.when(tvalid_ref[i] == 0)
    def _():
        y_ref[...] = jnp.zeros(y_ref.shape, F32)


def _moe(n2, w_gate, w_up, w_down, tile_expert, tile_first, tile_valid, tile_next, tile_base, src_tok):
    n_tiles = tile_expert.shape[0]
    hbm = pl.BlockSpec(memory_space=pl.ANY)
    grid_spec = pltpu.PrefetchScalarGridSpec(
        num_scalar_prefetch=6,
        grid=(n_tiles,),
        in_specs=[hbm, hbm, hbm, hbm],
        out_specs=pl.BlockSpec((MOE_TM, D_MODEL), lambda i, te, tf, tv, tn, tb, st: (i, 0)),
        scratch_shapes=[pltpu.VMEM((MOE_RING, MOE_TM, D_MODEL), F32),
                        pltpu.SemaphoreType.DMA((MOE_RING,)),
                        pltpu.VMEM((D_MODEL, D_EXPERT), F32),
                        pltpu.VMEM((D_MODEL, D_EXPERT), F32),
                        pltpu.VMEM((D_EXPERT, D_MODEL), F32),
                        pltpu.SemaphoreType.DMA((3,)),
                        pltpu.VMEM((D_MODEL, D_EXPERT), BF16),
                        pltpu.VMEM((D_MODEL, D_EXPERT), BF16),
                        pltpu.VMEM((D_EXPERT, D_MODEL), BF16)],
    )
    return pl.pallas_call(
        _moe_kernel,
        grid_spec=grid_spec,
        out_shape=jax.ShapeDtypeStruct((n_tiles * MOE_TM, D_MODEL), F32),
        compiler_params=_cparams(("arbitrary",)),
        name="moe",
    )(tile_expert, tile_first, tile_valid, tile_next, tile_base, src_tok, n2, w_gate, w_up, w_down)


def _route_plan(e1, e2, n_tok):
    experts = jnp.arange(N_EXPERTS, dtype=jnp.int32)
    flat_e = jnp.concatenate([e1, e2])
    onehot = (flat_e[:, None] == experts[None, :]).astype(jnp.int32)
    blk = onehot.astype(F32).reshape(-1, LANES, N_EXPERTS)
    tri = (jnp.arange(LANES)[:, None] >= jnp.arange(LANES)[None, :]).astype(F32)
    within = jnp.einsum("ij,bje->bie", tri, blk)
    before = jnp.cumsum(within[:, -1, :], axis=0) - within[:, -1, :]
    csum = (within + before[:, None, :]).reshape(-1, N_EXPERTS).astype(jnp.int32)
    rank = jnp.sum(csum * onehot, axis=1) - 1
    counts = csum[-1]
    tiles_per = (counts + MOE_TM - 1) // MOE_TM
    tile_end = jnp.cumsum(tiles_per)
    tile_off = tile_end - tiles_per
    slot = jnp.sum(onehot * tile_off[None, :], axis=1) * MOE_TM + rank
    n_tiles = (2 * n_tok) // MOE_TM + N_EXPERTS
    tok = jnp.tile(jnp.arange(n_tok, dtype=jnp.int32), 2)
    _, src_tok = lax.sort((slot, tok), num_keys=1)
    src_tok = jnp.concatenate([src_tok, jnp.zeros((MOE_TM,), jnp.int32)])
    count_off = jnp.cumsum(counts) - counts
    tile_id = jnp.arange(n_tiles, dtype=jnp.int32)
    n_used = tile_end[-1]
    tile_valid = (tile_id < n_used).astype(jnp.int32)
    te = jnp.sum((tile_end[None, :] <= jnp.minimum(tile_id, n_used - 1)[:, None]).astype(jnp.int32), axis=1)
    tile_expert = jnp.minimum(te, N_EXPERTS - 1)
    prev = jnp.concatenate([jnp.full((1,), -1, jnp.int32), tile_expert[:-1]])
    tile_first = (tile_expert != prev).astype(jnp.int32)
    oh_te = (tile_expert[:, None] == experts[None, :]).astype(jnp.int32)

    def of_expert(table):
        return jnp.sum(oh_te * table[None, :], axis=1)

    next_tile = of_expert(tile_end)
    oh_next = (tile_id[None, :] == jnp.minimum(next_tile, n_tiles - 1)[:, None]).astype(jnp.int32)
    tile_next = jnp.where(next_tile < n_used, jnp.sum(oh_next * tile_expert[None, :], axis=1), -1)
    tile_base = jnp.where(tile_valid == 1, of_expert(count_off) + (tile_id - of_expert(tile_off)) * MOE_TM, 0)
    return tile_expert, tile_first, tile_valid, tile_next, tile_base, src_tok, slot[:n_tok], slot[n_tok:]


def _final_kernel(p1_ref, p2_ref, ys_hbm, x1_ref, g2_ref, rt_ref, fg_ref, yp_ref, ysm_ref, ybuf, sem,
                  *, n_prompt_tiles):
    i = pl.program_id(0)
    n_tiles = pl.num_programs(0)
    slot = i % 2

    def start(tile, s):
        _row_gather(ys_hbm, p1_ref, tile * ROW_TILE, ybuf.at[s, 0], sem.at[s], ROW_TILE)
        _row_gather(ys_hbm, p2_ref, tile * ROW_TILE, ybuf.at[s, 1], sem.at[s], ROW_TILE)

    @pl.when(i == 0)
    def _():
        start(0, 0)

    @pl.when(i + 1 < n_tiles)
    def _():
        start(i + 1, 1 - slot)

    _row_gather_wait(ys_hbm, ybuf.at[slot, 0], sem.at[slot], ROW_TILE)
    _row_gather_wait(ys_hbm, ybuf.at[slot, 1], sem.at[slot], ROW_TILE)
    rt = rt_ref[...]
    moe = rt[:, 2:3] * ybuf[slot, 0] + rt[:, 3:4] * ybuf[slot, 1]
    x2 = x1_ref[...] + _bcast_rows(g2_ref[...], ROW_TILE) * moe
    y = x2 * lax.rsqrt(jnp.mean(x2 * x2, axis=-1, keepdims=True) + EPS) * fg_ref[...]

    @pl.when(i < n_prompt_tiles)
    def _():
        yp_ref[...] = y

    @pl.when(i >= n_prompt_tiles)
    def _():
        ysm_ref[...] = y


def _final(ys, x1, g2, rt, final_g, p1, p2, n_prompt_rows):
    t = x1.shape[0]
    tm = ROW_TILE
    seg = tm // MOD_ROWS
    n_p = n_prompt_rows // tm
    grid_spec = pltpu.PrefetchScalarGridSpec(
        num_scalar_prefetch=2,
        grid=(t // tm,),
        in_specs=[pl.BlockSpec(memory_space=pl.ANY),
                  pl.BlockSpec((tm, D_MODEL), lambda i, a, b: (i, 0)),
                  pl.BlockSpec((seg, D_MODEL), lambda i, a, b: (i, 0)),
                  pl.BlockSpec((tm, ROUTE_LANES), lambda i, a, b: (i, 0)),
                  pl.BlockSpec((1, D_MODEL), lambda i, a, b: (0, 0))],
        out_specs=[pl.BlockSpec((tm, D_MODEL), lambda i, a, b: (jnp.minimum(i, n_p - 1), 0)),
                   pl.BlockSpec((tm, D_MODEL), lambda i, a, b: (jnp.maximum(i - n_p, 0), 0))],
        scratch_shapes=[pltpu.VMEM((2, 2, tm, D_MODEL), F32),
                        pltpu.SemaphoreType.DMA((2,))],
    )
    return pl.pallas_call(
        functools.partial(_final_kernel, n_prompt_tiles=n_p),
        grid_spec=grid_spec,
        out_shape=[jax.ShapeDtypeStruct((n_prompt_rows, D_MODEL), F32),
                   jax.ShapeDtypeStruct((t - n_prompt_rows, D_MODEL), F32)],
        compiler_params=_cparams(("arbitrary",)),
        name="final",
    )(p1, p2, ys, x1, g2, rt, final_g.reshape(1, D_MODEL))


def kernel(x_prompt, x_sample, cache_k_win, cache_v_win, state_conv, state_rglru, c_prompt, c_sample, w_ada, b_ada, norm1_g, norm2_g, w_in, b_in, w_conv, b_conv, w_rg_a, b_rg_a, w_rg_x, b_rg_x, lru_lambda, w_rnn_out, w_attn_out, w_out, attn_sinks, w_route_group, b_route_group, w_route_expert, b_route_expert, w_exp_gate, w_exp_up, w_exp_down, rel_bias_table, final_norm_g):
    n_b, seq, _ = x_prompt.shape
    d_b, d_seq, _ = x_sample.shape
    assert w_ada.shape[0] == 1, "single trunk layer"
    assert seq % (ATT_CPB * CHUNK) == 0 and seq % MOD_ROWS == 0 and d_seq == MOD_ROWS
    assert d_seq >= CONV_W - 1 and d_b == N_CHAINS
    t_p, t_s = n_b * seq, d_b * d_seq
    t = t_p + t_s
    assert t_s == ROW_TILE and t_p % ROW_TILE == 0
    cw = cache_k_win.shape[2]
    l = 0

    x_p = x_prompt.reshape(t_p, D_MODEL)
    x_s = x_sample.reshape(t_s, D_MODEL)

    n_c = n_b + d_b
    c_all = jnp.pad(jnp.concatenate([c_prompt, c_sample], axis=0), ((0, -n_c % SUBLANES), (0, 0)))
    mod = _ada(c_all, w_ada[l], b_ada[l])

    def per_segment(m):
        return jnp.concatenate([jnp.repeat(m[:n_b], seq // MOD_ROWS, axis=0),
                                jnp.repeat(m[n_b:n_c], d_seq // MOD_ROWS, axis=0)], axis=0)

    sh1, sc1, g1, sh2, sc2, g2 = [per_segment(m) for m in jnp.split(mod, 6, axis=-1)]

    z = _inproj(_norm1(x_p, x_s, sc1, sh1, norm1_g[l]), w_in[l], b_in[l])

    w_gate = jnp.concatenate([w_rg_a[l], w_rg_x[l]], axis=-1)
    b_gate = jnp.concatenate([b_rg_a[l], b_rg_x[l]], axis=-1)[:, None, :]
    hg_p, hl_p = _rglru(z, jnp.zeros((n_b, SUBLANES, D_RNN), F32), jnp.zeros((n_b, N_CHAINS, D_RNN), F32),
                        w_conv[l], b_conv[l], w_gate, b_gate, lru_lambda[l],
                        n_blocks=n_b, n_seq=1, seq_len=seq, row_block0=0, link=True)
    state8 = jnp.pad(state_conv[l], ((0, 0), (SUBLANES - (CONV_W - 1), 0), (0, 0)))
    hg_s, hl_s = _rglru(z, state8, state_rglru[l][None], w_conv[l], b_conv[l], w_gate, b_gate, lru_lambda[l],
                        n_blocks=1, n_seq=d_b, seq_len=d_seq, row_block0=t_p // t_s, link=False)

    sinks = attn_sinks[l]
    bkt_p = _bucket_map(WIN_CHUNKS * CHUNK + jnp.arange(CHUNK), jnp.arange(SPAN))
    o_p = _attn_band(z, rel_bias_table, sinks, bkt_p, n_batch=n_b, seq_len=seq)
    k_new = z[t_p:, COL_K:COL_K + KV_DIM].reshape(d_b, d_seq, KV_DIM)
    v_new = z[t_p:, COL_V:COL_V + KV_DIM].reshape(d_b, d_seq, KV_DIM)
    k_all = jnp.concatenate([cache_k_win[l].reshape(d_b, cw, KV_DIM), k_new], axis=1)
    v_all = jnp.concatenate([cache_v_win[l].reshape(d_b, cw, KV_DIM), v_new], axis=1)
    bkt_s = _bucket_map(cw + jnp.arange(d_seq), jnp.arange(cw + d_seq))
    o_s = _attn_step(z, k_all, v_all, rel_bias_table, sinks, bkt_s, n_batch=d_b, n_q=d_seq, row_block0=t_p // d_seq)

    n_route = N_GROUPS + N_EXPERTS
    w_route = jnp.pad(jnp.concatenate([w_route_group[l], w_route_expert[l]], axis=1),
                      ((0, 0), (0, ROUTE_LANES - n_route)))
    b_route = jnp.pad(jnp.concatenate([b_route_group[l], b_route_expert[l]]),
                      (0, ROUTE_LANES - n_route)).reshape(1, ROUTE_LANES)
    x1, n2, rt = _outproj(hg_p, hg_s, o_p, o_s, z, x_p, x_s, g1, sh2, sc2, norm2_g[l],
                          w_rnn_out[l].astype(BF16), w_attn_out[l].astype(BF16), w_out[l].astype(BF16),
                          w_route, b_route)

    e1 = rt[:, 0].astype(jnp.int32)
    e2 = rt[:, 1].astype(jnp.int32)
    tile_expert, tile_first, tile_valid, tile_next, tile_base, src_tok, p1, p2 = _route_plan(e1, e2, t)
    ys = _moe(n2, w_exp_gate[l], w_exp_up[l], w_exp_down[l], tile_expert, tile_first, tile_valid, tile_next,
              tile_base, src_tok)
    y_p, y_s = _final(ys, x1, g2, rt, final_norm_g, p1, p2, t_p)

    win = min(WINDOW, seq)

    def tail(col0, width, n_rows):
        return jnp.stack([z[(b + 1) * seq - n_rows:(b + 1) * seq, col0:col0 + width] for b in range(n_b)])

    kp = tail(COL_K, KV_DIM, win).reshape(n_b, win, N_KV_HEADS, HEAD_DIM)
    vp = tail(COL_V, KV_DIM, win).reshape(n_b, win, N_KV_HEADS, HEAD_DIM)
    cp = tail(COL_XR, D_RNN, CONV_W - 1)
    rp = hl_p[:, N_CHAINS - 1, :]
    ks = k_all[:, -cw:].reshape(d_b, cw, N_KV_HEADS, HEAD_DIM)
    vs = v_all[:, -cw:].reshape(d_b, cw, N_KV_HEADS, HEAD_DIM)
    cs = z[t_p:, COL_XR:COL_XR + D_RNN].reshape(d_b, d_seq, D_RNN)[:, -(CONV_W - 1):]
    rs = hl_s[0]
    return (y_p.reshape(n_b, seq, D_MODEL), y_s.reshape(d_b, d_seq, D_MODEL),
            kp[None], vp[None], cp[None], rp[None], ks[None], vs[None], cs[None], rs[None])
```

```python
import functools
import math

import jax
import jax.numpy as jnp
from jax import lax
from jax.experimental import pallas as pl
from jax.experimental.pallas import tpu as pltpu

F32 = jnp.float32
BF16 = jnp.bfloat16

D_MODEL = 2048
D_RNN = 2048
RNN_BLOCK = 128
CONV_W = 4
LRU_C = 8.0
N_HEADS = 16
N_KV_HEADS = 4
HEAD_DIM = 128
GQA_GROUP = N_HEADS // N_KV_HEADS
Q_DIM = N_HEADS * HEAD_DIM
KV_DIM = N_KV_HEADS * HEAD_DIM
CHUNK = 64
WINDOW = 128
WIN_CHUNKS = WINDOW // CHUNK
SPAN = (WIN_CHUNKS + 1) * CHUNK
N_BUCKETS = 32
MAX_DISTANCE = 128
N_GROUPS = 4
E_PER_GROUP = 8
N_EXPERTS = N_GROUPS * E_PER_GROUP
D_EXPERT = 512
EPS = 1e-6
NEG_INF = -1e30
D_IN = 2 * D_RNN + Q_DIM + 2 * KV_DIM + 2 * D_MODEL
COL_XR, COL_GR, COL_Q = 0, D_RNN, 2 * D_RNN
COL_K = COL_Q + Q_DIM
COL_V = COL_K + KV_DIM
COL_GA = COL_V + KV_DIM
COL_GB = COL_GA + D_MODEL

LANES = 128
SUBLANES = 8
MOD_ROWS = 32
KEY_TILE = 256
VMEM_LIMIT = 56 * 1024 * 1024
VMEM_LIMIT_BIG = 60 * 1024 * 1024
ADA_TN = 1024

ROW_TILE = 256
IN_TM_CHOICES = (1408, 768, 256)
IN_TN = 1024
MOE_TM = 192
MOE_RING = 4


def _sigmoid(x):
    return 0.5 * jnp.tanh(0.5 * x) + 0.5


def _gelu_tanh(x):
    return 0.5 * x * (1.0 + jnp.tanh(math.sqrt(2.0 / math.pi) * (x + 0.044715 * (x * x * x))))


def _bcast_rows(v, rows):
    n, d = v.shape
    return jnp.broadcast_to(v[:, None, :], (n, MOD_ROWS, d)).reshape(rows, d)


def _rms_modulate(x, gain, scale_seg, shift_seg):
    rows = x.shape[0]
    y = x * lax.rsqrt(jnp.mean(x * x, axis=-1, keepdims=True) + EPS) * gain
    return y * (1.0 + _bcast_rows(scale_seg, rows)) + _bcast_rows(shift_seg, rows)


def _cparams(sem, vmem_limit=VMEM_LIMIT):
    return pltpu.CompilerParams(dimension_semantics=sem, vmem_limit_bytes=vmem_limit)


def _ada_kernel(c_ref, w_ref, b_ref, o_ref):
    c = c_ref[...]
    s = (c * _sigmoid(c)).astype(BF16)
    o_ref[...] = jnp.dot(s, w_ref[...].astype(BF16), preferred_element_type=F32) + b_ref[...]


def _ada(c_all, w_ada, b_ada):
    rows = c_all.shape[0]
    n = w_ada.shape[1]
    tn = ADA_TN
    return pl.pallas_call(
        _ada_kernel,
        grid=(n // tn,),
        in_specs=[pl.BlockSpec((rows, D_MODEL), lambda j: (0, 0)),
                  pl.BlockSpec((D_MODEL, tn), lambda j: (0, j)),
                  pl.BlockSpec((1, tn), lambda j: (0, j))],
        out_specs=pl.BlockSpec((rows, tn), lambda j: (0, j)),
        out_shape=jax.ShapeDtypeStruct((rows, n), F32),
        compiler_params=_cparams(("arbitrary",)),
        name="ada",
    )(c_all, w_ada, b_ada.reshape(1, n))


NORM_NSUB = 3


def _norm1_kernel(x0_ref, x1_ref, x2_ref, xs_ref, sc_ref, sh_ref, g_ref, n1_ref, *, n_prompt_tiles):
    tail_is_sample = pl.program_id(0) * NORM_NSUB + (NORM_NSUB - 1) >= n_prompt_tiles
    gain = g_ref[...]
    segs = ROW_TILE // MOD_ROWS
    for r, x_ref in enumerate((x0_ref, x1_ref, x2_ref)):
        for s in range(segs):
            rows = slice(s * MOD_ROWS, (s + 1) * MOD_ROWS)
            x = x_ref[rows, :]
            if r == NORM_NSUB - 1:
                x = jnp.where(tail_is_sample, xs_ref[rows, :], x)
            y = x * lax.rsqrt(jnp.mean(x * x, axis=-1, keepdims=True) + EPS) * gain
            k = r * segs + s
            out_rows = slice(k * MOD_ROWS, (k + 1) * MOD_ROWS)
            n1_ref[out_rows, :] = (y * (1.0 + sc_ref[k:k + 1, :]) + sh_ref[k:k + 1, :]).astype(BF16)


def _norm1(x_p, x_s, sc1, sh1, norm_g):
    n_p = x_p.shape[0] // ROW_TILE
    t = x_p.shape[0] + x_s.shape[0]
    tm = NORM_NSUB * ROW_TILE
    assert x_s.shape[0] == ROW_TILE and t % tm == 0
    seg = tm // MOD_ROWS

    def sub(r):
        return pl.BlockSpec((ROW_TILE, D_MODEL), lambda i: (jnp.minimum(i * NORM_NSUB + r, n_p - 1), 0))

    return pl.pallas_call(
        functools.partial(_norm1_kernel, n_prompt_tiles=n_p),
        grid=(t // tm,),
        in_specs=[sub(0), sub(1), sub(2),
                  pl.BlockSpec((ROW_TILE, D_MODEL), lambda i: (0, 0)),
                  pl.BlockSpec((seg, D_MODEL), lambda i: (i, 0)),
                  pl.BlockSpec((seg, D_MODEL), lambda i: (i, 0)),
                  pl.BlockSpec((1, D_MODEL), lambda i: (0, 0))],
        out_specs=pl.BlockSpec((tm, D_MODEL), lambda i: (i, 0)),
        out_shape=jax.ShapeDtypeStruct((t, D_MODEL), BF16),
        compiler_params=_cparams(("arbitrary",)),
        name="norm1",
    )(x_p, x_p, x_p, x_s, sc1, sh1, norm_g.reshape(1, D_MODEL))


def _inproj_kernel(n1_ref, w_ref, b_ref, z_ref, w_bf):
    @pl.when(pl.program_id(1) == 0)
    def _():
        w_bf[...] = w_ref[...].astype(BF16)

    z_ref[...] = jnp.dot(n1_ref[...], w_bf[...], preferred_element_type=F32) + b_ref[...]


def _inproj(n1, w_in, b_in):
    t = n1.shape[0]
    tm = next(m for m in IN_TM_CHOICES if t % m == 0)
    return pl.pallas_call(
        _inproj_kernel,
        grid=(D_IN // IN_TN, t // tm),
        in_specs=[pl.BlockSpec((tm, D_MODEL), lambda j, i: (i, 0)),
                  pl.BlockSpec((D_MODEL, IN_TN), lambda j, i: (0, j)),
                  pl.BlockSpec((1, IN_TN), lambda j, i: (0, j))],
        out_specs=pl.BlockSpec((tm, IN_TN), lambda j, i: (i, j)),
        out_shape=jax.ShapeDtypeStruct((t, D_IN), F32),
        scratch_shapes=[pltpu.VMEM((D_MODEL, IN_TN), BF16)],
        compiler_params=_cparams(("arbitrary", "arbitrary")),
        name="inproj",
    )(n1, w_in, b_in.reshape(1, D_IN))


RG_CW = 256
RG_SLABS = RG_CW // LANES
N_CHAINS = SUBLANES
CHAIN_PAD = 8


def _rglru_kernel(xr_ref, gr_ref, st_ref, h0_ref, wc_ref, bc_ref, wg_ref, bg_ref, lam_ref,
                  hg_ref, hl_ref, ext_scr, a_scr, u_scr, *, n_seq, seq_len, link):
    rows = n_seq * seq_len
    cl = rows // N_CHAINS
    pitch = cl + CHAIN_PAD

    for s in range(n_seq):
        ext_scr[s, 0:SUBLANES, :] = st_ref[s]
        ext_scr[s, SUBLANES:SUBLANES + seq_len, :] = xr_ref[s * seq_len:(s + 1) * seq_len, :]

    z = -lam_ref[...]
    softplus = jnp.maximum(z, 0.0) + jnp.log1p(jnp.exp(-jnp.abs(z)))
    nq = (-0.25 * LRU_C) * softplus

    for c in range(N_CHAINS):
        s, r0 = divmod(c * cl, seq_len)
        xc = bc_ref[...]
        for j in range(CONV_W):
            xc = xc + wc_ref[j:j + 1, :] * ext_scr[s, pl.ds(SUBLANES - (CONV_W - 1) + j + r0, cl), :]
        for sl in range(RG_SLABS):
            lanes = slice(sl * LANES, (sl + 1) * LANES)
            xb = xc[:, lanes]
            g = jnp.dot(xb.astype(BF16), wg_ref[sl].astype(BF16), preferred_element_type=F32) + bg_ref[sl]
            tr = jnp.tanh(0.5 * g[:, :LANES])
            ti = jnp.tanh(0.5 * g[:, LANES:])
            th = jnp.tanh(nq[:, lanes] * (tr + 1.0))
            rcp = 1.0 / (1.0 - th)
            a_scr[sl, c * pitch:c * pitch + cl, :] = (1.0 + th) * rcp
            u_scr[sl, c * pitch:c * pitch + cl, :] = (jnp.sqrt(-th) * rcp) * ((ti + 1.0) * xb)

    def step(t, carry):
        hs, ps = carry
        new_h, new_p = [], []
        for sl in range(RG_SLABS):
            a = a_scr[sl, pl.ds(t, N_CHAINS, stride=pitch), :]
            u = u_scr[sl, pl.ds(t, N_CHAINS, stride=pitch), :]
            h = a * hs[sl] + u
            u_scr[sl, pl.ds(t, N_CHAINS, stride=pitch), :] = h
            new_h.append(h)
            if link:
                p = a * ps[sl]
                a_scr[sl, pl.ds(t, N_CHAINS, stride=pitch), :] = p
                new_p.append(p)
            else:
                new_p.append(ps[sl])
        return tuple(new_h), tuple(new_p)

    h_init = tuple(h0_ref[0, :, sl * LANES:(sl + 1) * LANES] for sl in range(RG_SLABS))
    p_init = tuple(jnp.ones((N_CHAINS, LANES), F32) for _ in range(RG_SLABS))
    h_end, p_end = lax.fori_loop(0, cl, step, (h_init, p_init), unroll=8)

    row = lax.broadcasted_iota(jnp.int32, (N_CHAINS, LANES), 0)
    for sl in range(RG_SLABS):
        lanes = slice(sl * LANES, (sl + 1) * LANES)
        if link:
            def shift_down(v):
                return jnp.where(row == 0, 0.0, pltpu.roll(v, 1, axis=0))
            hh = h_end[sl]
            for _ in range(N_CHAINS - 1):
                hh = h_end[sl] + p_end[sl] * shift_down(hh)
            carry_in = shift_down(hh)
        else:
            hh = h_end[sl]
        hl_ref[0, :, lanes] = hh
        for c in range(N_CHAINS):
            h = u_scr[sl, c * pitch:c * pitch + cl, :]
            if link:
                h = h + a_scr[sl, c * pitch:c * pitch + cl, :] * carry_in[c:c + 1, :]
            gg = _gelu_tanh(gr_ref[c * cl:(c + 1) * cl, lanes])
            hg_ref[c * cl:(c + 1) * cl, lanes] = (h * gg).astype(BF16)


def _rglru(z, state8, h0, w_conv, b_conv, w_gate, b_gate, lam, *, n_blocks, n_seq, seq_len, row_block0, link):
    rows = n_seq * seq_len
    cl = rows // N_CHAINS
    ncb = D_RNN // RG_CW
    gr0 = COL_GR // RG_CW
    kern = functools.partial(_rglru_kernel, n_seq=n_seq, seq_len=seq_len, link=link)
    return pl.pallas_call(
        kern,
        grid=(n_blocks, ncb),
        in_specs=[pl.BlockSpec((rows, RG_CW), lambda b, n: (row_block0 + b, n)),
                  pl.BlockSpec((rows, RG_CW), lambda b, n: (row_block0 + b, gr0 + n)),
                  pl.BlockSpec((n_seq, SUBLANES, RG_CW), lambda b, n: (b, 0, n)),
                  pl.BlockSpec((1, N_CHAINS, RG_CW), lambda b, n: (b, 0, n)),
                  pl.BlockSpec((CONV_W, RG_CW), lambda b, n: (0, n)),
                  pl.BlockSpec((1, RG_CW), lambda b, n: (0, n)),
                  pl.BlockSpec((RG_SLABS, RNN_BLOCK, 2 * RNN_BLOCK), lambda b, n: (n, 0, 0)),
                  pl.BlockSpec((RG_SLABS, 1, 2 * RNN_BLOCK), lambda b, n: (n, 0, 0)),
                  pl.BlockSpec((1, RG_CW), lambda b, n: (0, n))],
        out_specs=[pl.BlockSpec((rows, RG_CW), lambda b, n: (b, n)),
                   pl.BlockSpec((1, N_CHAINS, RG_CW), lambda b, n: (b, 0, n))],
        out_shape=[jax.ShapeDtypeStruct((n_blocks * rows, D_RNN), BF16),
                   jax.ShapeDtypeStruct((n_blocks, N_CHAINS, D_RNN), F32)],
        scratch_shapes=[pltpu.VMEM((n_seq, SUBLANES + seq_len, RG_CW), F32),
                        pltpu.VMEM((RG_SLABS, N_CHAINS * (cl + CHAIN_PAD), LANES), F32),
                        pltpu.VMEM((RG_SLABS, N_CHAINS * (cl + CHAIN_PAD), LANES), F32)],
        compiler_params=_cparams(("arbitrary", "arbitrary")),
        name="rglru_link" if link else "rglru_step",
    )(z, z, state8, h0, w_conv, b_conv.reshape(1, D_RNN), w_gate, b_gate, lam.reshape(1, D_RNN))


def _build_bias(tbl_ref, bkt_ref, bias_scr, n_q):
    bkt = bkt_ref[...]
    base = jnp.where(bkt < 0, NEG_INF, 0.0)
    for h in range(N_HEADS):
        acc = base
        for bk in range(N_BUCKETS):
            acc = jnp.where(bkt == bk, tbl_ref[bk, h], acc)
        hk, g = divmod(h, GQA_GROUP)
        bias_scr[hk, g * n_q:(g + 1) * n_q, :] = acc


def _sink_columns(sink_ref, n_q):
    row = lax.broadcasted_iota(jnp.int32, (GQA_GROUP * n_q, 1), 0)
    cols = []
    for hk in range(N_KV_HEADS):
        col = jnp.full((GQA_GROUP * n_q, 1), sink_ref[hk * GQA_GROUP], F32)
        for g in range(1, GQA_GROUP):
            col = jnp.where(row >= g * n_q, sink_ref[hk * GQA_GROUP + g], col)
        cols.append(col)
    return cols


def _attend(q_of, k_of, v_of, bias_scr, sinks, key_ok, store, n_q):
    scores = []
    for hk in range(N_KV_HEADS):
        qg = jnp.concatenate([q_of(hk * GQA_GROUP + g) for g in range(GQA_GROUP)], axis=0)
        s = lax.dot_general(qg, k_of(hk), (((1,), (1,)), ((), ())), preferred_element_type=F32)
        s = s * (HEAD_DIM ** -0.5) + bias_scr[hk]
        if key_ok is not None:
            s = jnp.where(key_ok, s, NEG_INF)
        scores.append(s)
    probs = []
    for hk in range(N_KV_HEADS):
        s, sink = scores[hk], sinks[hk]
        m = jnp.maximum(jnp.max(s, axis=-1, keepdims=True), sink)
        p = jnp.exp(s - m)
        inv = 1.0 / (jnp.sum(p, axis=-1, keepdims=True) + jnp.exp(sink - m))
        probs.append((p * inv).astype(BF16))
    for hk in range(N_KV_HEADS):
        o = jnp.dot(probs[hk], v_of(hk), preferred_element_type=F32)
        for g in range(GQA_GROUP):
            store(hk * GQA_GROUP + g, o[g * n_q:(g + 1) * n_q, :].astype(BF16))


ATT_CPB = 16
ATT_LEAD = WIN_CHUNKS * CHUNK
ATT_TAIL = KEY_TILE - SPAN


def _attn_band_kernel(tbl_ref, sink_ref, q_ref, k_ref, v_ref, bkt_ref, o_ref, kpad, vpad, bias_scr, *, seq_len):
    b = pl.program_id(0)
    cg = pl.program_id(1)

    @pl.when((b == 0) & (cg == 0))
    def _():
        _build_bias(tbl_ref, bkt_ref, bias_scr, CHUNK)

    @pl.when(cg == 0)
    def _():
        for ref, pad in ((k_ref, kpad), (v_ref, vpad)):
            pad[0:ATT_LEAD, :] = jnp.zeros((ATT_LEAD, KV_DIM), BF16)
            pad[ATT_LEAD:ATT_LEAD + seq_len, :] = ref[...].astype(BF16)
            pad[ATT_LEAD + seq_len:, :] = jnp.zeros((ATT_TAIL, KV_DIM), BF16)

    kidx = lax.broadcasted_iota(jnp.int32, (1, KEY_TILE), 1)
    sinks = _sink_columns(sink_ref, CHUNK)

    def chunk(c, carry):
        q0 = pl.multiple_of(c * CHUNK, CHUNK)
        start = pl.multiple_of((cg * ATT_CPB + c) * CHUNK, CHUNK)
        key_ok = start + kidx >= ATT_LEAD

        def head_cols(h):
            return slice(h * HEAD_DIM, (h + 1) * HEAD_DIM)

        def store(h, val):
            o_ref[pl.ds(q0, CHUNK), head_cols(h)] = val

        _attend(lambda h: q_ref[pl.ds(q0, CHUNK), head_cols(h)].astype(BF16),
                lambda hk: kpad[pl.ds(start, KEY_TILE), head_cols(hk)],
                lambda hk: vpad[pl.ds(start, KEY_TILE), head_cols(hk)],
                bias_scr, sinks, key_ok, store, CHUNK)
        return carry

    lax.fori_loop(0, ATT_CPB, chunk, 0)


def _attn_band(z, table, sinks, bkt, *, n_batch, seq_len):
    rows = ATT_CPB * CHUNK
    ng = seq_len // rows
    kern = functools.partial(_attn_band_kernel, seq_len=seq_len)
    smem = pl.BlockSpec(memory_space=pltpu.SMEM)
    pad_rows = ATT_LEAD + seq_len + ATT_TAIL
    return pl.pallas_call(
        kern,
        grid=(n_batch, ng),
        in_specs=[smem, smem,
                  pl.BlockSpec((rows, Q_DIM), lambda b, c: (b * ng + c, COL_Q // Q_DIM)),
                  pl.BlockSpec((seq_len, KV_DIM), lambda b, c: (b, COL_K // KV_DIM)),
                  pl.BlockSpec((seq_len, KV_DIM), lambda b, c: (b, COL_V // KV_DIM)),
                  pl.BlockSpec((CHUNK, KEY_TILE), lambda b, c: (0, 0))],
        out_specs=pl.BlockSpec((rows, Q_DIM), lambda b, c: (b * ng + c, 0)),
        out_shape=jax.ShapeDtypeStruct((n_batch * seq_len, Q_DIM), BF16),
        scratch_shapes=[pltpu.VMEM((pad_rows, KV_DIM), BF16),
                        pltpu.VMEM((pad_rows, KV_DIM), BF16),
                        pltpu.VMEM((N_KV_HEADS, GQA_GROUP * CHUNK, KEY_TILE), F32)],
        compiler_params=_cparams(("arbitrary", "arbitrary")),
        name="attn_band",
    )(table, sinks, z, z, z, bkt)


def _attn_step_kernel(tbl_ref, sink_ref, q_ref, k_ref, v_ref, bkt_ref, o_ref, kbuf, vbuf, bias_scr,
                      *, n_keys, n_q):
    b = pl.program_id(0)

    @pl.when(b == 0)
    def _():
        _build_bias(tbl_ref, bkt_ref, bias_scr, n_q)
        kbuf[n_keys:, :] = jnp.zeros((KEY_TILE - n_keys, KV_DIM), BF16)
        vbuf[n_keys:, :] = jnp.zeros((KEY_TILE - n_keys, KV_DIM), BF16)

    kbuf[0:n_keys, :] = k_ref[0].astype(BF16)
    vbuf[0:n_keys, :] = v_ref[0].astype(BF16)

    def head_cols(h):
        return slice(h * HEAD_DIM, (h + 1) * HEAD_DIM)

    def store(h, val):
        o_ref[:, head_cols(h)] = val

    _attend(lambda h: q_ref[:, head_cols(h)].astype(BF16),
            lambda hk: kbuf[:, head_cols(hk)], lambda hk: vbuf[:, head_cols(hk)],
            bias_scr, _sink_columns(sink_ref, n_q), None, store, n_q)


def _attn_step(z, k_all, v_all, table, sinks, bkt, *, n_batch, n_q, row_block0):
    n_keys = k_all.shape[1]
    kern = functools.partial(_attn_step_kernel, n_keys=n_keys, n_q=n_q)
    smem = pl.BlockSpec(memory_space=pltpu.SMEM)
    return pl.pallas_call(
        kern,
        grid=(n_batch,),
        in_specs=[smem, smem,
                  pl.BlockSpec((n_q, Q_DIM), lambda b: (row_block0 + b, COL_Q // Q_DIM)),
                  pl.BlockSpec((1, n_keys, KV_DIM), lambda b: (b, 0, 0)),
                  pl.BlockSpec((1, n_keys, KV_DIM), lambda b: (b, 0, 0)),
                  pl.BlockSpec((n_q, KEY_TILE), lambda b: (0, 0))],
        out_specs=pl.BlockSpec((n_q, Q_DIM), lambda b: (b, 0)),
        out_shape=jax.ShapeDtypeStruct((n_batch * n_q, Q_DIM), BF16),
        scratch_shapes=[pltpu.VMEM((KEY_TILE, KV_DIM), BF16),
                        pltpu.VMEM((KEY_TILE, KV_DIM), BF16),
                        pltpu.VMEM((N_KV_HEADS, GQA_GROUP * n_q, KEY_TILE), F32)],
        compiler_params=_cparams(("arbitrary",)),
        name="attn_step",
    )(table, sinks, z, k_all, v_all, bkt)


def _t5_bucket(rel):
    nb = N_BUCKETS // 2
    ret = jnp.where(rel > 0, nb, 0)
    n = jnp.abs(rel)
    max_exact = nb // 2
    nf = jnp.maximum(n, 1).astype(jnp.float32)
    large = max_exact + (jnp.log(nf / max_exact) / math.log(MAX_DISTANCE / max_exact)
                         * (nb - max_exact)).astype(jnp.int32)
    large = jnp.minimum(large, nb - 1)
    return ret + jnp.where(n < max_exact, n, large)


def _bucket_map(q_pos, k_pos):
    bkt = _t5_bucket(k_pos[None, :] - q_pos[:, None]).astype(jnp.int32)
    return jnp.pad(bkt, ((0, 0), (0, KEY_TILE - k_pos.shape[0])), constant_values=-1)


ROUTE_LANES = LANES
GATE_TN = 1024


def _outproj_kernel(hgp_ref, hgs_ref, op_ref, os_ref, ga0_ref, ga1_ref, gb0_ref, gb1_ref, xp_ref, xs_ref,
                    g1_ref, sh2_ref, sc2_ref, ng_ref, wr_ref, wa_ref, wo_ref, wrt_ref, brt_ref,
                    x1_ref, n2_ref, rt_ref, *, n_prompt_tiles):
    tm = x1_ref.shape[0]
    is_prompt = pl.program_id(0) < n_prompt_tiles
    hg = jnp.where(is_prompt, hgp_ref[...], hgs_ref[...])
    o = jnp.where(is_prompt, op_ref[...], os_ref[...])
    x = jnp.where(is_prompt, xp_ref[...], xs_ref[...])
    ya = jnp.dot(hg, wr_ref[...], preferred_element_type=F32)
    yb = jnp.dot(o, wa_ref[...], preferred_element_type=F32)
    halves = []
    for k, (ga_ref, gb_ref) in enumerate(((ga0_ref, gb0_ref), (ga1_ref, gb1_ref))):
        cols = slice(k * GATE_TN, (k + 1) * GATE_TN)
        halves.append((_sigmoid(ga_ref[...]) * ya[:, cols] + _sigmoid(gb_ref[...]) * yb[:, cols]).astype(BF16))
    merged = jnp.concatenate(halves, axis=1)
    mix = jnp.dot(merged, wo_ref[...], preferred_element_type=F32)
    x1 = x + _bcast_rows(g1_ref[...], tm) * mix
    x1_ref[...] = x1
    n2 = _rms_modulate(x1, ng_ref[...], sc2_ref[...], sh2_ref[...])
    n2_ref[...] = n2

    lg = jnp.dot(n2.astype(BF16), wrt_ref[...].astype(BF16), preferred_element_type=F32) + brt_ref[...]

    lane = lax.broadcasted_iota(jnp.int32, (tm, ROUTE_LANES), 1)
    lane_f = lane.astype(F32)
    e_lane = lane - N_GROUPS
    lane_group = (e_lane >> 3).astype(F32)

    def first_argmax(vals, vmax):
        return jnp.min(jnp.where(vals == vmax, lane_f, float(ROUTE_LANES)), axis=-1, keepdims=True)

    gl = jnp.where(lane < N_GROUPS, lg, NEG_INF)
    gmax = jnp.max(gl, axis=-1, keepdims=True)
    g_idx = first_argmax(gl, gmax)
    g_w = 1.0 / jnp.sum(jnp.exp(gl - gmax), axis=-1, keepdims=True)
    in_group = jnp.where((e_lane >= 0) & (e_lane < N_EXPERTS), lane_group, -1.0) == g_idx
    el = jnp.where(in_group, lg, NEG_INF)
    v1 = jnp.max(el, axis=-1, keepdims=True)
    i1 = first_argmax(el, v1)
    el2 = jnp.where(lane_f == i1, NEG_INF, el)
    v2 = jnp.max(el2, axis=-1, keepdims=True)
    i2 = first_argmax(el2, v2)
    e21 = jnp.exp(v2 - v1)
    w1 = g_w / (1.0 + e21)
    w2 = g_w * e21 / (1.0 + e21)
    e1 = i1 - float(N_GROUPS)
    e2 = i2 - float(N_GROUPS)
    rt_ref[...] = jnp.where(lane == 0, e1, jnp.where(lane == 1, e2, jnp.where(lane == 2, w1,
                            jnp.where(lane == 3, w2, 0.0))))


def _outproj(hg_p, hg_s, o_p, o_s, z, x_p, x_s, g1, sh2, sc2, norm_g, wr, wa, wo, w_route, b_route):
    tm = ROW_TILE
    t = z.shape[0]
    seg = tm // MOD_ROWS
    n_p = x_p.shape[0] // tm
    row = lambda i: (i, 0)
    fix = lambda i: (0, 0)
    row_p = lambda i: (jnp.minimum(i, n_p - 1), 0)
    row_s = lambda i: (jnp.maximum(i - n_p, 0), 0)
    once = pl.Buffered(1)

    def gate(col):
        return pl.BlockSpec((tm, GATE_TN), lambda i: (i, col // GATE_TN))

    return pl.pallas_call(
        functools.partial(_outproj_kernel, n_prompt_tiles=n_p),
        grid=(t // tm,),
        in_specs=[pl.BlockSpec((tm, D_RNN), row_p), pl.BlockSpec((tm, D_RNN), row_s, pipeline_mode=once),
                  pl.BlockSpec((tm, Q_DIM), row_p), pl.BlockSpec((tm, Q_DIM), row_s, pipeline_mode=once),
                  gate(COL_GA), gate(COL_GA + GATE_TN), gate(COL_GB), gate(COL_GB + GATE_TN),
                  pl.BlockSpec((tm, D_MODEL), row_p), pl.BlockSpec((tm, D_MODEL), row_s, pipeline_mode=once),
                  pl.BlockSpec((seg, D_MODEL), row), pl.BlockSpec((seg, D_MODEL), row),
                  pl.BlockSpec((seg, D_MODEL), row),
                  pl.BlockSpec((1, D_MODEL), fix),
                  pl.BlockSpec((D_RNN, D_MODEL), fix, pipeline_mode=once),
                  pl.BlockSpec((Q_DIM, D_MODEL), fix, pipeline_mode=once),
                  pl.BlockSpec((D_MODEL, D_MODEL), fix, pipeline_mode=once),
                  pl.BlockSpec((D_MODEL, ROUTE_LANES), fix),
                  pl.BlockSpec((1, ROUTE_LANES), fix)],
        out_specs=[pl.BlockSpec((tm, D_MODEL), row), pl.BlockSpec((tm, D_MODEL), row),
                   pl.BlockSpec((tm, ROUTE_LANES), row)],
        out_shape=[jax.ShapeDtypeStruct((t, D_MODEL), F32), jax.ShapeDtypeStruct((t, D_MODEL), F32),
                   jax.ShapeDtypeStruct((t, ROUTE_LANES), F32)],
        compiler_params=_cparams(("arbitrary",), VMEM_LIMIT_BIG),
        name="outproj",
    )(hg_p, hg_s, o_p, o_s, z, z, z, z, x_p, x_s, g1, sh2, sc2, norm_g.reshape(1, D_MODEL),
      wr, wa, wo, w_route, b_route)


def _row_gather(src_hbm, idx_ref, base, dst, sem, n_rows):
    for r in range(n_rows):
        tok = idx_ref[base + r]
        pltpu.make_async_copy(src_hbm.at[pl.ds(tok, 1), :], dst.at[pl.ds(r, 1), :], sem).start()


def _row_gather_wait(src_hbm, dst, sem, n_rows):
    pltpu.make_async_copy(src_hbm.at[pl.ds(0, n_rows), :], dst, sem).wait()


def _moe_kernel(te_ref, tfirst_ref, tvalid_ref, tnext_ref, tbase_ref, src_ref, n2_hbm, wg_hbm, wu_hbm, wd_hbm, y_ref,
                xbuf, xsem, wg_st, wu_st, wd_st, wsem, wg_bf, wu_bf, wd_bf):
    i = pl.program_id(0)
    n_tiles = pl.num_programs(0)
    slot = lax.rem(i, MOE_RING)
    stages = ((wg_hbm, wg_st, wg_bf), (wu_hbm, wu_st, wu_bf), (wd_hbm, wd_st, wd_bf))

    def weight_copy(k, e):
        hbm, st, _ = stages[k]
        return pltpu.make_async_copy(hbm.at[e], st, wsem.at[k])

    def gather_tile(tile):
        tc = jnp.minimum(tile, n_tiles - 1)

        @pl.when((tile < n_tiles) & (tvalid_ref[tc] == 1))
        def _():
            s = lax.rem(tile, MOE_RING)
            _row_gather(n2_hbm, src_ref, tbase_ref[tc], xbuf.at[s], xsem.at[s], MOE_TM)

    @pl.when(i == 0)
    def _():
        for k in range(len(stages)):
            weight_copy(k, te_ref[0]).start(priority=1)
        for ahead in range(MOE_RING - 1):
            gather_tile(ahead)

    @pl.when(tvalid_ref[i] == 1)
    def _():
        gather_tile(i + MOE_RING - 1)

        @pl.when(tfirst_ref[i] == 1)
        def _():
            for k, (_, st, bf) in enumerate(stages):
                weight_copy(k, te_ref[i]).wait()
                bf[...] = st[...].astype(BF16)

            @pl.when(tnext_ref[i] >= 0)
            def _():
                for k in range(len(stages)):
                    weight_copy(k, tnext_ref[i]).start(priority=1)

        _row_gather_wait(n2_hbm, xbuf.at[slot], xsem.at[slot], MOE_TM)
        x = xbuf[slot].astype(BF16)
        hgate = jnp.dot(x, wg_bf[...], preferred_element_type=F32)
        hup = jnp.dot(x, wu_bf[...], preferred_element_type=F32)
        act = (hgate * _sigmoid(hgate) * hup).astype(BF16)
        y_ref[...] = jnp.dot(act, wd_bf[...], preferred_element_type=F32)

    @pl.when(tvalid_ref[i] == 0)
    def _():
        y_ref[...] = jnp.zeros(y_ref.shape, F32)


def _moe(n2, w_gate, w_up, w_down, tile_expert, tile_first, tile_valid, tile_next, tile_base, src_tok):
    n_tiles = tile_expert.shape[0]
    hbm = pl.BlockSpec(memory_space=pl.ANY)
    grid_spec = pltpu.PrefetchScalarGridSpec(
        num_scalar_prefetch=6,
        grid=(n_tiles,),
        in_specs=[hbm, hbm, hbm, hbm],
        out_specs=pl.BlockSpec((MOE_TM, D_MODEL), lambda i, te, tf, tv, tn, tb, st: (i, 0)),
        scratch_shapes=[pltpu.VMEM((MOE_RING, MOE_TM, D_MODEL), F32),
                        pltpu.SemaphoreType.DMA((MOE_RING,)),
                        pltpu.VMEM((D_MODEL, D_EXPERT), F32),
                        pltpu.VMEM((D_MODEL, D_EXPERT), F32),
                        pltpu.VMEM((D_EXPERT, D_MODEL), F32),
                        pltpu.SemaphoreType.DMA((3,)),
                        pltpu.VMEM((D_MODEL, D_EXPERT), BF16),
                        pltpu.VMEM((D_MODEL, D_EXPERT), BF16),
                        pltpu.VMEM((D_EXPERT, D_MODEL), BF16)],
    )
    return pl.pallas_call(
        _moe_kernel,
        grid_spec=grid_spec,
        out_shape=jax.ShapeDtypeStruct((n_tiles * MOE_TM, D_MODEL), F32),
        compiler_params=_cparams(("arbitrary",)),
        name="moe",
    )(tile_expert, tile_first, tile_valid, tile_next, tile_base, src_tok, n2, w_gate, w_up, w_down)


def _route_plan(e1, e2, n_tok):
    experts = jnp.arange(N_EXPERTS, dtype=jnp.int32)
    flat_e = jnp.concatenate([e1, e2])
    onehot = (flat_e[:, None] == experts[None, :]).astype(jnp.int32)
    blk = onehot.astype(F32).reshape(-1, LANES, N_EXPERTS)
    tri = (jnp.arange(LANES)[:, None] >= jnp.arange(LANES)[None, :]).astype(F32)
    within = jnp.einsum("ij,bje->bie", tri, blk)
    before = jnp.cumsum(within[:, -1, :], axis=0) - within[:, -1, :]
    csum = (within + before[:, None, :]).reshape(-1, N_EXPERTS).astype(jnp.int32)
    rank = jnp.sum(csum * onehot, axis=1) - 1
    counts = csum[-1]
    tiles_per = (counts + MOE_TM - 1) // MOE_TM
    tile_end = jnp.cumsum(tiles_per)
    tile_off = tile_end - tiles_per
    slot = jnp.sum(onehot * tile_off[None, :], axis=1) * MOE_TM + rank
    n_tiles = (2 * n_tok) // MOE_TM + N_EXPERTS
    tok = jnp.tile(jnp.arange(n_tok, dtype=jnp.int32), 2)
    _, src_tok = lax.sort((slot, tok), num_keys=1)
    src_tok = jnp.concatenate([src_tok, jnp.zeros((MOE_TM,), jnp.int32)])
    count_off = jnp.cumsum(counts) - counts
    tile_id = jnp.arange(n_tiles, dtype=jnp.int32)
    n_used = tile_end[-1]
    tile_valid = (tile_id < n_used).astype(jnp.int32)
    te = jnp.sum((tile_end[None, :] <= jnp.minimum(tile_id, n_used - 1)[:, None]).astype(jnp.int32), axis=1)
    tile_expert = jnp.minimum(te, N_EXPERTS - 1)
    prev = jnp.concatenate([jnp.full((1,), -1, jnp.int32), tile_expert[:-1]])
    tile_first = (tile_expert != prev).astype(jnp.int32)
    oh_te = (tile_expert[:, None] == experts[None, :]).astype(jnp.int32)

    def of_expert(table):
        return jnp.sum(oh_te * table[None, :], axis=1)

    next_tile = of_expert(tile_end)
    oh_next = (tile_id[None, :] == jnp.minimum(next_tile, n_tiles - 1)[:, None]).astype(jnp.int32)
    tile_next = jnp.where(next_tile < n_used, jnp.sum(oh_next * tile_expert[None, :], axis=1), -1)
    tile_base = jnp.where(tile_valid == 1, of_expert(count_off) + (tile_id - of_expert(tile_off)) * MOE_TM, 0)
    return tile_expert, tile_first, tile_valid, tile_next, tile_base, src_tok, slot[:n_tok], slot[n_tok:]


def _final_kernel(p1_ref, p2_ref, ys_hbm, x1_ref, g2_ref, rt_ref, fg_ref, yp_ref, ysm_ref, ybuf, sem,
                  *, n_prompt_tiles):
    i = pl.program_id(0)
    n_tiles = pl.num_programs(0)
    slot = i % 2

    def start(tile, s):
        _row_gather(ys_hbm, p1_ref, tile * ROW_TILE, ybuf.at[s, 0], sem.at[s], ROW_TILE)
        _row_gather(ys_hbm, p2_ref, tile * ROW_TILE, ybuf.at[s, 1], sem.at[s], ROW_TILE)

    @pl.when(i == 0)
    def _():
        start(0, 0)

    @pl.when(i + 1 < n_tiles)
    def _():
        start(i + 1, 1 - slot)

    _row_gather_wait(ys_hbm, ybuf.at[slot, 0], sem.at[slot], ROW_TILE)
    _row_gather_wait(ys_hbm, ybuf.at[slot, 1], sem.at[slot], ROW_TILE)
    rt = rt_ref[...]
    moe = rt[:, 2:3] * ybuf[slot, 0] + rt[:, 3:4] * ybuf[slot, 1]
    x2 = x1_ref[...] + _bcast_rows(g2_ref[...], ROW_TILE) * moe
    y = x2 * lax.rsqrt(jnp.mean(x2 * x2, axis=-1, keepdims=True) + EPS) * fg_ref[...]

    @pl.when(i < n_prompt_tiles)
    def _():
        yp_ref[...] = y

    @pl.when(i >= n_prompt_tiles)
    def _():
        ysm_ref[...] = y


def _final(ys, x1, g2, rt, final_g, p1, p2, n_prompt_rows):
    t = x1.shape[0]
    tm = ROW_TILE
    seg = tm // MOD_ROWS
    n_p = n_prompt_rows // tm
    grid_spec = pltpu.PrefetchScalarGridSpec(
        num_scalar_prefetch=2,
        grid=(t // tm,),
        in_specs=[pl.BlockSpec(memory_space=pl.ANY),
                  pl.BlockSpec((tm, D_MODEL), lambda i, a, b: (i, 0)),
                  pl.BlockSpec((seg, D_MODEL), lambda i, a, b: (i, 0)),
                  pl.BlockSpec((tm, ROUTE_LANES), lambda i, a, b: (i, 0)),
                  pl.BlockSpec((1, D_MODEL), lambda i, a, b: (0, 0))],
        out_specs=[pl.BlockSpec((tm, D_MODEL), lambda i, a, b: (jnp.minimum(i, n_p - 1), 0)),
                   pl.BlockSpec((tm, D_MODEL), lambda i, a, b: (jnp.maximum(i - n_p, 0), 0))],
        scratch_shapes=[pltpu.VMEM((2, 2, tm, D_MODEL), F32),
                        pltpu.SemaphoreType.DMA((2,))],
    )
    return pl.pallas_call(
        functools.partial(_final_kernel, n_prompt_tiles=n_p),
        grid_spec=grid_spec,
        out_shape=[jax.ShapeDtypeStruct((n_prompt_rows, D_MODEL), F32),
                   jax.ShapeDtypeStruct((t - n_prompt_rows, D_MODEL), F32)],
        compiler_params=_cparams(("arbitrary",)),
        name="final",
    )(p1, p2, ys, x1, g2, rt, final_g.reshape(1, D_MODEL))


def kernel(x_prompt, x_sample, cache_k_win, cache_v_win, state_conv, state_rglru, c_prompt, c_sample, w_ada, b_ada, norm1_g, norm2_g, w_in, b_in, w_conv, b_conv, w_rg_a, b_rg_a, w_rg_x, b_rg_x, lru_lambda, w_rnn_out, w_attn_out, w_out, attn_sinks, w_route_group, b_route_group, w_route_expert, b_route_expert, w_exp_gate, w_exp_up, w_exp_down, rel_bias_table, final_norm_g):
    n_b, seq, _ = x_prompt.shape
    d_b, d_seq, _ = x_sample.shape
    assert w_ada.shape[0] == 1, "single trunk layer"
    assert seq % (ATT_CPB * CHUNK) == 0 and seq % MOD_ROWS == 0 and d_seq == MOD_ROWS
    assert d_seq >= CONV_W - 1 and d_b == N_CHAINS
    t_p, t_s = n_b * seq, d_b * d_seq
    t = t_p + t_s
    assert t_s == ROW_TILE and t_p % ROW_TILE == 0
    cw = cache_k_win.shape[2]
    l = 0

    x_p = x_prompt.reshape(t_p, D_MODEL)
    x_s = x_sample.reshape(t_s, D_MODEL)

    n_c = n_b + d_b
    c_all = jnp.pad(jnp.concatenate([c_prompt, c_sample], axis=0), ((0, -n_c % SUBLANES), (0, 0)))
    mod = _ada(c_all, w_ada[l], b_ada[l])

    def per_segment(m):
        return jnp.concatenate([jnp.repeat(m[:n_b], seq // MOD_ROWS, axis=0),
                                jnp.repeat(m[n_b:n_c], d_seq // MOD_ROWS, axis=0)], axis=0)

    sh1, sc1, g1, sh2, sc2, g2 = [per_segment(m) for m in jnp.split(mod, 6, axis=-1)]

    z = _inproj(_norm1(x_p, x_s, sc1, sh1, norm1_g[l]), w_in[l], b_in[l])

    w_gate = jnp.concatenate([w_rg_a[l], w_rg_x[l]], axis=-1)
    b_gate = jnp.concatenate([b_rg_a[l], b_rg_x[l]], axis=-1)[:, None, :]
    hg_p, hl_p = _rglru(z, jnp.zeros((n_b, SUBLANES, D_RNN), F32), jnp.zeros((n_b, N_CHAINS, D_RNN), F32),
                        w_conv[l], b_conv[l], w_gate, b_gate, lru_lambda[l],
                        n_blocks=n_b, n_seq=1, seq_len=seq, row_block0=0, link=True)
    state8 = jnp.pad(state_conv[l], ((0, 0), (SUBLANES - (CONV_W - 1), 0), (0, 0)))
    hg_s, hl_s = _rglru(z, state8, state_rglru[l][None], w_conv[l], b_conv[l], w_gate, b_gate, lru_lambda[l],
                        n_blocks=1, n_seq=d_b, seq_len=d_seq, row_block0=t_p // t_s, link=False)

    sinks = attn_sinks[l]
    bkt_p = _bucket_map(WIN_CHUNKS * CHUNK + jnp.arange(CHUNK), jnp.arange(SPAN))
    o_p = _attn_band(z, rel_bias_table, sinks, bkt_p, n_batch=n_b, seq_len=seq)
    k_new = z[t_p:, COL_K:COL_K + KV_DIM].reshape(d_b, d_seq, KV_DIM)
    v_new = z[t_p:, COL_V:COL_V + KV_DIM].reshape(d_b, d_seq, KV_DIM)
    k_all = jnp.concatenate([cache_k_win[l].reshape(d_b, cw, KV_DIM), k_new], axis=1)
    v_all = jnp.concatenate([cache_v_win[l].reshape(d_b, cw, KV_DIM), v_new], axis=1)
    bkt_s = _bucket_map(cw + jnp.arange(d_seq), jnp.arange(cw + d_seq))
    o_s = _attn_step(z, k_all, v_all, rel_bias_table, sinks, bkt_s, n_batch=d_b, n_q=d_seq, row_block0=t_p // d_seq)

    n_route = N_GROUPS + N_EXPERTS
    w_route = jnp.pad(jnp.concatenate([w_route_group[l], w_route_expert[l]], axis=1),
                      ((0, 0), (0, ROUTE_LANES - n_route)))
    b_route = jnp.pad(jnp.concatenate([b_route_group[l], b_route_expert[l]]),
                      (0, ROUTE_LANES - n_route)).reshape(1, ROUTE_LANES)
    x1, n2, rt = _outproj(hg_p, hg_s, o_p, o_s, z, x_p, x_s, g1, sh2, sc2, norm2_g[l],
                          w_rnn_out[l].astype(BF16), w_attn_out[l].astype(BF16), w_out[l].astype(BF16),
                          w_route, b_route)

    e1 = rt[:, 0].astype(jnp.int32)
    e2 = rt[:, 1].astype(jnp.int32)
    tile_expert, tile_first, tile_valid, tile_next, tile_base, src_tok, p1, p2 = _route_plan(e1, e2, t)
    ys = _moe(n2, w_exp_gate[l], w_exp_up[l], w_exp_down[l], tile_expert, tile_first, tile_valid, tile_next,
              tile_base, src_tok)
    y_p, y_s = _final(ys, x1, g2, rt, final_norm_g, p1, p2, t_p)

    win = min(WINDOW, seq)

    def tail(col0, width, n_rows):
        return jnp.stack([z[(b + 1) * seq - n_rows:(b + 1) * seq, col0:col0 + width] for b in range(n_b)])

    kp = tail(COL_K, KV_DIM, win).reshape(n_b, win, N_KV_HEADS, HEAD_DIM)
    vp = tail(COL_V, KV_DIM, win).reshape(n_b, win, N_KV_HEADS, HEAD_DIM)
    cp = tail(COL_XR, D_RNN, CONV_W - 1)
    rp = hl_p[:, N_CHAINS - 1, :]
    ks = k_all[:, -cw:].reshape(d_b, cw, N_KV_HEADS, HEAD_DIM)
    vs = v_all[:, -cw:].reshape(d_b, cw, N_KV_HEADS, HEAD_DIM)
    cs = z[t_p:, COL_XR:COL_XR + D_RNN].reshape(d_b, d_seq, D_RNN)[:, -(CONV_W - 1):]
    rs = hl_s[0]
    return (y_p.reshape(n_b, seq, D_MODEL), y_s.reshape(d_b, d_seq, D_MODEL),
            kp[None], vp[None], cp[None], rp[None], ks[None], vs[None], cs[None], rs[None])
```

```python
import functools
import math

import jax
import jax.numpy as jnp
from jax import lax
from jax.experimental import pallas as pl
from jax.experimental.pallas import tpu as pltpu

F32 = jnp.float32
BF16 = jnp.bfloat16

D_MODEL = 2048
D_RNN = 2048
RNN_BLOCK = 128
CONV_W = 4
LRU_C = 8.0
N_HEADS = 16
N_KV_HEADS = 4
HEAD_DIM = 128
GQA_GROUP = N_HEADS // N_KV_HEADS
Q_DIM = N_HEADS * HEAD_DIM
KV_DIM = N_KV_HEADS * HEAD_DIM
CHUNK = 64
WINDOW = 128
WIN_CHUNKS = WINDOW // CHUNK
SPAN = (WIN_CHUNKS + 1) * CHUNK
N_BUCKETS = 32
MAX_DISTANCE = 128
N_GROUPS = 4
E_PER_GROUP = 8
N_EXPERTS = N_GROUPS * E_PER_GROUP
D_EXPERT = 512
EPS = 1e-6
NEG_INF = -1e30
D_IN = 2 * D_RNN + Q_DIM + 2 * KV_DIM + 2 * D_MODEL
COL_XR, COL_GR, COL_Q = 0, D_RNN, 2 * D_RNN
COL_K = COL_Q + Q_DIM
COL_V = COL_K + KV_DIM
COL_GA = COL_V + KV_DIM
COL_GB = COL_GA + D_MODEL

LANES = 128
SUBLANES = 8
MOD_ROWS = 32
KEY_TILE = 256
VMEM_LIMIT = 56 * 1024 * 1024
VMEM_LIMIT_BIG = 60 * 1024 * 1024
ADA_TN = 1024

ROW_TILE = 256
IN_TM_CHOICES = (1408, 768, 256)
IN_TN = 1024
MOE_TM = 192
MOE_RING = 3


def _sigmoid(x):
    return 0.5 * jnp.tanh(0.5 * x) + 0.5


def _gelu_tanh(x):
    return 0.5 * x * (1.0 + jnp.tanh(math.sqrt(2.0 / math.pi) * (x + 0.044715 * (x * x * x))))


def _bcast_rows(v, rows):
    n, d = v.shape
    return jnp.broadcast_to(v[:, None, :], (n, MOD_ROWS, d)).reshape(rows, d)


def _rms_modulate(x, gain, scale_seg, shift_seg):
    rows = x.shape[0]
    y = x * lax.rsqrt(jnp.mean(x * x, axis=-1, keepdims=True) + EPS) * gain
    return y * (1.0 + _bcast_rows(scale_seg, rows)) + _bcast_rows(shift_seg, rows)


def _cparams(sem, vmem_limit=VMEM_LIMIT):
    return pltpu.CompilerParams(dimension_semantics=sem, vmem_limit_bytes=vmem_limit)


def _ada_kernel(c_ref, w_ref, b_ref, o_ref):
    c = c_ref[...]
    s = (c * _sigmoid(c)).astype(BF16)
    o_ref[...] = jnp.dot(s, w_ref[...].astype(BF16), preferred_element_type=F32) + b_ref[...]


def _ada(c_all, w_ada, b_ada):
    rows = c_all.shape[0]
    n = w_ada.shape[1]
    tn = ADA_TN
    return pl.pallas_call(
        _ada_kernel,
        grid=(n // tn,),
        in_specs=[pl.BlockSpec((rows, D_MODEL), lambda j: (0, 0)),
                  pl.BlockSpec((D_MODEL, tn), lambda j: (0, j)),
                  pl.BlockSpec((1, tn), lambda j: (0, j))],
        out_specs=pl.BlockSpec((rows, tn), lambda j: (0, j)),
        out_shape=jax.ShapeDtypeStruct((rows, n), F32),
        compiler_params=_cparams(("arbitrary",)),
        name="ada",
    )(c_all, w_ada, b_ada.reshape(1, n))


NORM_NSUB = 3


def _norm1_kernel(x0_ref, x1_ref, x2_ref, xs_ref, sc_ref, sh_ref, g_ref, n1_ref, *, n_prompt_tiles):
    tail_is_sample = pl.program_id(0) * NORM_NSUB + (NORM_NSUB - 1) >= n_prompt_tiles
    gain = g_ref[...]
    segs = ROW_TILE // MOD_ROWS
    for r, x_ref in enumerate((x0_ref, x1_ref, x2_ref)):
        for s in range(segs):
            rows = slice(s * MOD_ROWS, (s + 1) * MOD_ROWS)
            x = x_ref[rows, :]
            if r == NORM_NSUB - 1:
                x = jnp.where(tail_is_sample, xs_ref[rows, :], x)
            y = x * lax.rsqrt(jnp.mean(x * x, axis=-1, keepdims=True) + EPS) * gain
            k = r * segs + s
            out_rows = slice(k * MOD_ROWS, (k + 1) * MOD_ROWS)
            n1_ref[out_rows, :] = (y * (1.0 + sc_ref[k:k + 1, :]) + sh_ref[k:k + 1, :]).astype(BF16)


def _norm1(x_p, x_s, sc1, sh1, norm_g):
    n_p = x_p.shape[0] // ROW_TILE
    t = x_p.shape[0] + x_s.shape[0]
    tm = NORM_NSUB * ROW_TILE
    assert x_s.shape[0] == ROW_TILE and t % tm == 0
    seg = tm // MOD_ROWS

    def sub(r):
        return pl.BlockSpec((ROW_TILE, D_MODEL), lambda i: (jnp.minimum(i * NORM_NSUB + r, n_p - 1), 0))

    return pl.pallas_call(
        functools.partial(_norm1_kernel, n_prompt_tiles=n_p),
        grid=(t // tm,),
        in_specs=[sub(0), sub(1), sub(2),
                  pl.BlockSpec((ROW_TILE, D_MODEL), lambda i: (0, 0)),
                  pl.BlockSpec((seg, D_MODEL), lambda i: (i, 0)),
                  pl.BlockSpec((seg, D_MODEL), lambda i: (i, 0)),
                  pl.BlockSpec((1, D_MODEL), lambda i: (0, 0))],
        out_specs=pl.BlockSpec((tm, D_MODEL), lambda i: (i, 0)),
        out_shape=jax.ShapeDtypeStruct((t, D_MODEL), BF16),
        compiler_params=_cparams(("arbitrary",)),
        name="norm1",
    )(x_p, x_p, x_p, x_s, sc1, sh1, norm_g.reshape(1, D_MODEL))


def _inproj_kernel(n1_ref, w_ref, b_ref, z_ref, w_bf):
    @pl.when(pl.program_id(1) == 0)
    def _():
        w_bf[...] = w_ref[...].astype(BF16)

    z_ref[...] = jnp.dot(n1_ref[...], w_bf[...], preferred_element_type=F32) + b_ref[...]


def _inproj(n1, w_in, b_in):
    t = n1.shape[0]
    tm = next(m for m in IN_TM_CHOICES if t % m == 0)
    return pl.pallas_call(
        _inproj_kernel,
        grid=(D_IN // IN_TN, t // tm),
        in_specs=[pl.BlockSpec((tm, D_MODEL), lambda j, i: (i, 0)),
                  pl.BlockSpec((D_MODEL, IN_TN), lambda j, i: (0, j)),
                  pl.BlockSpec((1, IN_TN), lambda j, i: (0, j))],
        out_specs=pl.BlockSpec((tm, IN_TN), lambda j, i: (i, j)),
        out_shape=jax.ShapeDtypeStruct((t, D_IN), F32),
        scratch_shapes=[pltpu.VMEM((D_MODEL, IN_TN), BF16)],
        compiler_params=_cparams(("arbitrary", "arbitrary")),
        name="inproj",
    )(n1, w_in, b_in.reshape(1, D_IN))


RG_CW = 256
RG_SLABS = RG_CW // LANES
N_CHAINS = SUBLANES
CHAIN_PAD = 8


def _rglru_kernel(xr_ref, gr_ref, st_ref, h0_ref, wc_ref, bc_ref, wg_ref, bg_ref, lam_ref,
                  hg_ref, hl_ref, ext_scr, a_scr, u_scr, *, n_seq, seq_len, link):
    rows = n_seq * seq_len
    cl = rows // N_CHAINS
    pitch = cl + CHAIN_PAD

    for s in range(n_seq):
        ext_scr[s, 0:SUBLANES, :] = st_ref[s]
        ext_scr[s, SUBLANES:SUBLANES + seq_len, :] = xr_ref[s * seq_len:(s + 1) * seq_len, :]

    z = -lam_ref[...]
    softplus = jnp.maximum(z, 0.0) + jnp.log1p(jnp.exp(-jnp.abs(z)))
    nq = (-0.25 * LRU_C) * softplus

    for c in range(N_CHAINS):
        s, r0 = divmod(c * cl, seq_len)
        xc = bc_ref[...]
        for j in range(CONV_W):
            xc = xc + wc_ref[j:j + 1, :] * ext_scr[s, pl.ds(SUBLANES - (CONV_W - 1) + j + r0, cl), :]
        for sl in range(RG_SLABS):
            lanes = slice(sl * LANES, (sl + 1) * LANES)
            xb = xc[:, lanes]
            g = jnp.dot(xb.astype(BF16), wg_ref[sl].astype(BF16), preferred_element_type=F32) + bg_ref[sl]
            tr = jnp.tanh(0.5 * g[:, :LANES])
            ti = jnp.tanh(0.5 * g[:, LANES:])
            th = jnp.tanh(nq[:, lanes] * (tr + 1.0))
            rcp = 1.0 / (1.0 - th)
            a_scr[sl, c * pitch:c * pitch + cl, :] = (1.0 + th) * rcp
            u_scr[sl, c * pitch:c * pitch + cl, :] = (jnp.sqrt(-th) * rcp) * ((ti + 1.0) * xb)

    def step(t, carry):
        hs, ps = carry
        new_h, new_p = [], []
        for sl in range(RG_SLABS):
            a = a_scr[sl, pl.ds(t, N_CHAINS, stride=pitch), :]
            u = u_scr[sl, pl.ds(t, N_CHAINS, stride=pitch), :]
            h = a * hs[sl] + u
            u_scr[sl, pl.ds(t, N_CHAINS, stride=pitch), :] = h
            new_h.append(h)
            if link:
                p = a * ps[sl]
                a_scr[sl, pl.ds(t, N_CHAINS, stride=pitch), :] = p
                new_p.append(p)
            else:
                new_p.append(ps[sl])
        return tuple(new_h), tuple(new_p)

    h_init = tuple(h0_ref[0, :, sl * LANES:(sl + 1) * LANES] for sl in range(RG_SLABS))
    p_init = tuple(jnp.ones((N_CHAINS, LANES), F32) for _ in range(RG_SLABS))
    h_end, p_end = lax.fori_loop(0, cl, step, (h_init, p_init), unroll=8)

    row = lax.broadcasted_iota(jnp.int32, (N_CHAINS, LANES), 0)
    for sl in range(RG_SLABS):
        lanes = slice(sl * LANES, (sl + 1) * LANES)
        if link:
            def shift_down(v):
                return jnp.where(row == 0, 0.0, pltpu.roll(v, 1, axis=0))
            hh = h_end[sl]
            for _ in range(N_CHAINS - 1):
                hh = h_end[sl] + p_end[sl] * shift_down(hh)
            carry_in = shift_down(hh)
        else:
            hh = h_end[sl]
        hl_ref[0, :, lanes] = hh
        for c in range(N_CHAINS):
            h = u_scr[sl, c * pitch:c * pitch + cl, :]
            if link:
                h = h + a_scr[sl, c * pitch:c * pitch + cl, :] * carry_in[c:c + 1, :]
            gg = _gelu_tanh(gr_ref[c * cl:(c + 1) * cl, lanes])
            hg_ref[c * cl:(c + 1) * cl, lanes] = (h * gg).astype(BF16)


def _rglru(z, state8, h0, w_conv, b_conv, w_gate, b_gate, lam, *, n_blocks, n_seq, seq_len, row_block0, link):
    rows = n_seq * seq_len
    cl = rows // N_CHAINS
    ncb = D_RNN // RG_CW
    gr0 = COL_GR // RG_CW
    kern = functools.partial(_rglru_kernel, n_seq=n_seq, seq_len=seq_len, link=link)
    return pl.pallas_call(
        kern,
        grid=(n_blocks, ncb),
        in_specs=[pl.BlockSpec((rows, RG_CW), lambda b, n: (row_block0 + b, n)),
                  pl.BlockSpec((rows, RG_CW), lambda b, n: (row_block0 + b, gr0 + n)),
                  pl.BlockSpec((n_seq, SUBLANES, RG_CW), lambda b, n: (b, 0, n)),
                  pl.BlockSpec((1, N_CHAINS, RG_CW), lambda b, n: (b, 0, n)),
                  pl.BlockSpec((CONV_W, RG_CW), lambda b, n: (0, n)),
                  pl.BlockSpec((1, RG_CW), lambda b, n: (0, n)),
                  pl.BlockSpec((RG_SLABS, RNN_BLOCK, 2 * RNN_BLOCK), lambda b, n: (n, 0, 0)),
                  pl.BlockSpec((RG_SLABS, 1, 2 * RNN_BLOCK), lambda b, n: (n, 0, 0)),
                  pl.BlockSpec((1, RG_CW), lambda b, n: (0, n))],
        out_specs=[pl.BlockSpec((rows, RG_CW), lambda b, n: (b, n)),
                   pl.BlockSpec((1, N_CHAINS, RG_CW), lambda b, n: (b, 0, n))],
        out_shape=[jax.ShapeDtypeStruct((n_blocks * rows, D_RNN), BF16),
                   jax.ShapeDtypeStruct((n_blocks, N_CHAINS, D_RNN), F32)],
        scratch_shapes=[pltpu.VMEM((n_seq, SUBLANES + seq_len, RG_CW), F32),
                        pltpu.VMEM((RG_SLABS, N_CHAINS * (cl + CHAIN_PAD), LANES), F32),
                        pltpu.VMEM((RG_SLABS, N_CHAINS * (cl + CHAIN_PAD), LANES), F32)],
        compiler_params=_cparams(("arbitrary", "arbitrary")),
        name="rglru_link" if link else "rglru_step",
    )(z, z, state8, h0, w_conv, b_conv.reshape(1, D_RNN), w_gate, b_gate, lam.reshape(1, D_RNN))


LOG2E = math.log2(math.e)


def _build_bias(tbl_ref, bkt_ref, bias_scr, n_q):
    bkt = bkt_ref[...]
    base = jnp.where(bkt < 0, NEG_INF, 0.0)
    for h in range(N_HEADS):
        acc = base
        for bk in range(N_BUCKETS):
            acc = jnp.where(bkt == bk, tbl_ref[bk, h] * LOG2E, acc)
        hk, g = divmod(h, GQA_GROUP)
        bias_scr[hk, g * n_q:(g + 1) * n_q, :] = acc


def _sink_columns(sink_ref, n_q):
    row = lax.broadcasted_iota(jnp.int32, (GQA_GROUP * n_q, 1), 0)
    cols = []
    for hk in range(N_KV_HEADS):
        col = jnp.full((GQA_GROUP * n_q, 1), sink_ref[hk * GQA_GROUP] * LOG2E, F32)
        for g in range(1, GQA_GROUP):
            col = jnp.where(row >= g * n_q, sink_ref[hk * GQA_GROUP + g] * LOG2E, col)
        cols.append(col)
    return cols


def _attend(q_of, k_of, v_of, bias_scr, sinks, key_ok, store, n_q):
    scores = []
    for hk in range(N_KV_HEADS):
        qg = jnp.concatenate([q_of(hk * GQA_GROUP + g) for g in range(GQA_GROUP)], axis=0)
        s = lax.dot_general(qg, k_of(hk), (((1,), (1,)), ((), ())), preferred_element_type=F32)
        s = s * (HEAD_DIM ** -0.5 * LOG2E) + bias_scr[hk]
        if key_ok is not None:
            s = jnp.where(key_ok, s, NEG_INF)
        scores.append(s)
    probs = []
    for hk in range(N_KV_HEADS):
        s, sink = scores[hk], sinks[hk]
        m = jnp.maximum(jnp.max(s, axis=-1, keepdims=True), sink)
        p = jnp.exp2(s - m)
        inv = 1.0 / (jnp.sum(p, axis=-1, keepdims=True) + jnp.exp2(sink - m))
        probs.append((p * inv).astype(BF16))
    for hk in range(N_KV_HEADS):
        o = jnp.dot(probs[hk], v_of(hk), preferred_element_type=F32)
        for g in range(GQA_GROUP):
            store(hk * GQA_GROUP + g, o[g * n_q:(g + 1) * n_q, :].astype(BF16))


ATT_CPB = 8
ATT_LEAD = WIN_CHUNKS * CHUNK
ATT_TAIL = KEY_TILE - SPAN


def _attn_band_kernel(tbl_ref, sink_ref, q_ref, k_ref, v_ref, bkt_ref, o_ref, kpad, vpad, bias_scr, *, seq_len):
    b = pl.program_id(0)
    cg = pl.program_id(1)

    @pl.when((b == 0) & (cg == 0))
    def _():
        _build_bias(tbl_ref, bkt_ref, bias_scr, CHUNK)

    @pl.when(cg == 0)
    def _():
        for ref, pad in ((k_ref, kpad), (v_ref, vpad)):
            pad[0:ATT_LEAD, :] = jnp.zeros((ATT_LEAD, KV_DIM), BF16)
            pad[ATT_LEAD:ATT_LEAD + seq_len, :] = ref[...].astype(BF16)
            pad[ATT_LEAD + seq_len:, :] = jnp.zeros((ATT_TAIL, KV_DIM), BF16)

    kidx = lax.broadcasted_iota(jnp.int32, (1, KEY_TILE), 1)
    sinks = _sink_columns(sink_ref, CHUNK)

    def chunk(c, carry):
        q0 = pl.multiple_of(c * CHUNK, CHUNK)
        start = pl.multiple_of((cg * ATT_CPB + c) * CHUNK, CHUNK)
        key_ok = start + kidx >= ATT_LEAD

        def head_cols(h):
            return slice(h * HEAD_DIM, (h + 1) * HEAD_DIM)

        def store(h, val):
            o_ref[pl.ds(q0, CHUNK), head_cols(h)] = val

        def run(mask):
            _attend(lambda h: q_ref[pl.ds(q0, CHUNK), head_cols(h)].astype(BF16),
                    lambda hk: kpad[pl.ds(start, KEY_TILE), head_cols(hk)],
                    lambda hk: vpad[pl.ds(start, KEY_TILE), head_cols(hk)],
                    bias_scr, sinks, mask, store, CHUNK)

        pl.when(start < ATT_LEAD)(lambda: run(key_ok))
        pl.when(start >= ATT_LEAD)(lambda: run(None))
        return carry

    lax.fori_loop(0, ATT_CPB, chunk, 0)


def _attn_band(z, table, sinks, bkt, *, n_batch, seq_len):
    rows = ATT_CPB * CHUNK
    ng = seq_len // rows
    kern = functools.partial(_attn_band_kernel, seq_len=seq_len)
    smem = pl.BlockSpec(memory_space=pltpu.SMEM)
    pad_rows = ATT_LEAD + seq_len + ATT_TAIL
    return pl.pallas_call(
        kern,
        grid=(n_batch, ng),
        in_specs=[smem, smem,
                  pl.BlockSpec((rows, Q_DIM), lambda b, c: (b * ng + c, COL_Q // Q_DIM)),
                  pl.BlockSpec((seq_len, KV_DIM), lambda b, c: (b, COL_K // KV_DIM)),
                  pl.BlockSpec((seq_len, KV_DIM), lambda b, c: (b, COL_V // KV_DIM)),
                  pl.BlockSpec((CHUNK, KEY_TILE), lambda b, c: (0, 0))],
        out_specs=pl.BlockSpec((rows, Q_DIM), lambda b, c: (b * ng + c, 0)),
        out_shape=jax.ShapeDtypeStruct((n_batch * seq_len, Q_DIM), BF16),
        scratch_shapes=[pltpu.VMEM((pad_rows, KV_DIM), BF16),
                        pltpu.VMEM((pad_rows, KV_DIM), BF16),
                        pltpu.VMEM((N_KV_HEADS, GQA_GROUP * CHUNK, KEY_TILE), F32)],
        compiler_params=_cparams(("arbitrary", "arbitrary")),
        name="attn_band",
    )(table, sinks, z, z, z, bkt)


def _attn_step_kernel(tbl_ref, sink_ref, q_ref, k_ref, v_ref, bkt_ref, o_ref, kbuf, vbuf, bias_scr,
                      *, n_keys, n_q):
    b = pl.program_id(0)

    @pl.when(b == 0)
    def _():
        _build_bias(tbl_ref, bkt_ref, bias_scr, n_q)
        kbuf[n_keys:, :] = jnp.zeros((KEY_TILE - n_keys, KV_DIM), BF16)
        vbuf[n_keys:, :] = jnp.zeros((KEY_TILE - n_keys, KV_DIM), BF16)

    kbuf[0:n_keys, :] = k_ref[0].astype(BF16)
    vbuf[0:n_keys, :] = v_ref[0].astype(BF16)

    def head_cols(h):
        return slice(h * HEAD_DIM, (h + 1) * HEAD_DIM)

    def store(h, val):
        o_ref[:, head_cols(h)] = val

    _attend(lambda h: q_ref[:, head_cols(h)].astype(BF16),
            lambda hk: kbuf[:, head_cols(hk)], lambda hk: vbuf[:, head_cols(hk)],
            bias_scr, _sink_columns(sink_ref, n_q), None, store, n_q)


def _attn_step(z, k_all, v_all, table, sinks, bkt, *, n_batch, n_q, row_block0):
    n_keys = k_all.shape[1]
    kern = functools.partial(_attn_step_kernel, n_keys=n_keys, n_q=n_q)
    smem = pl.BlockSpec(memory_space=pltpu.SMEM)
    return pl.pallas_call(
        kern,
        grid=(n_batch,),
        in_specs=[smem, smem,
                  pl.BlockSpec((n_q, Q_DIM), lambda b: (row_block0 + b, COL_Q // Q_DIM)),
                  pl.BlockSpec((1, n_keys, KV_DIM), lambda b: (b, 0, 0)),
                  pl.BlockSpec((1, n_keys, KV_DIM), lambda b: (b, 0, 0)),
                  pl.BlockSpec((n_q, KEY_TILE), lambda b: (0, 0))],
        out_specs=pl.BlockSpec((n_q, Q_DIM), lambda b: (b, 0)),
        out_shape=jax.ShapeDtypeStruct((n_batch * n_q, Q_DIM), BF16),
        scratch_shapes=[pltpu.VMEM((KEY_TILE, KV_DIM), BF16),
                        pltpu.VMEM((KEY_TILE, KV_DIM), BF16),
                        pltpu.VMEM((N_KV_HEADS, GQA_GROUP * n_q, KEY_TILE), F32)],
        compiler_params=_cparams(("arbitrary",)),
        name="attn_step",
    )(table, sinks, z, k_all, v_all, bkt)


def _t5_bucket(rel):
    nb = N_BUCKETS // 2
    ret = jnp.where(rel > 0, nb, 0)
    n = jnp.abs(rel)
    max_exact = nb // 2
    nf = jnp.maximum(n, 1).astype(jnp.float32)
    large = max_exact + (jnp.log(nf / max_exact) / math.log(MAX_DISTANCE / max_exact)
                         * (nb - max_exact)).astype(jnp.int32)
    large = jnp.minimum(large, nb - 1)
    return ret + jnp.where(n < max_exact, n, large)


def _bucket_map(q_pos, k_pos):
    bkt = _t5_bucket(k_pos[None, :] - q_pos[:, None]).astype(jnp.int32)
    return jnp.pad(bkt, ((0, 0), (0, KEY_TILE - k_pos.shape[0])), constant_values=-1)


ROUTE_LANES = LANES
GATE_TN = 1024


def _outproj_kernel(hgp_ref, hgs_ref, op_ref, os_ref, ga0_ref, ga1_ref, gb0_ref, gb1_ref, xp_ref, xs_ref,
                    g1_ref, sh2_ref, sc2_ref, ng_ref, wr_ref, wa_ref, wo_ref, wrt_ref, brt_ref,
                    x1_ref, n2_ref, rt_ref, *, n_prompt_tiles):
    tm = x1_ref.shape[0]
    is_prompt = pl.program_id(0) < n_prompt_tiles
    hg = jnp.where(is_prompt, hgp_ref[...], hgs_ref[...])
    o = jnp.where(is_prompt, op_ref[...], os_ref[...])
    x = jnp.where(is_prompt, xp_ref[...], xs_ref[...])
    ya = jnp.dot(hg, wr_ref[...], preferred_element_type=F32)
    yb = jnp.dot(o, wa_ref[...], preferred_element_type=F32)
    halves = []
    for k, (ga_ref, gb_ref) in enumerate(((ga0_ref, gb0_ref), (ga1_ref, gb1_ref))):
        cols = slice(k * GATE_TN, (k + 1) * GATE_TN)
        halves.append((_sigmoid(ga_ref[...]) * ya[:, cols] + _sigmoid(gb_ref[...]) * yb[:, cols]).astype(BF16))
    merged = jnp.concatenate(halves, axis=1)
    mix = jnp.dot(merged, wo_ref[...], preferred_element_type=F32)
    x1 = x + _bcast_rows(g1_ref[...], tm) * mix
    x1_ref[...] = x1
    n2 = _rms_modulate(x1, ng_ref[...], sc2_ref[...], sh2_ref[...])
    n2_ref[...] = n2

    lg = jnp.dot(n2.astype(BF16), wrt_ref[...].astype(BF16), preferred_element_type=F32) + brt_ref[...]

    lane = lax.broadcasted_iota(jnp.int32, (tm, ROUTE_LANES), 1)
    lane_f = lane.astype(F32)
    e_lane = lane - N_GROUPS
    lane_group = (e_lane >> 3).astype(F32)

    def first_argmax(vals, vmax):
        return jnp.min(jnp.where(vals == vmax, lane_f, float(ROUTE_LANES)), axis=-1, keepdims=True)

    gl = jnp.where(lane < N_GROUPS, lg, NEG_INF)
    gmax = jnp.max(gl, axis=-1, keepdims=True)
    g_idx = first_argmax(gl, gmax)
    g_w = 1.0 / jnp.sum(jnp.exp(gl - gmax), axis=-1, keepdims=True)
    in_group = jnp.where((e_lane >= 0) & (e_lane < N_EXPERTS), lane_group, -1.0) == g_idx
    el = jnp.where(in_group, lg, NEG_INF)
    v1 = jnp.max(el, axis=-1, keepdims=True)
    i1 = first_argmax(el, v1)
    el2 = jnp.where(lane_f == i1, NEG_INF, el)
    v2 = jnp.max(el2, axis=-1, keepdims=True)
    i2 = first_argmax(el2, v2)
    e21 = jnp.exp(v2 - v1)
    w1 = g_w / (1.0 + e21)
    w2 = g_w * e21 / (1.0 + e21)
    e1 = i1 - float(N_GROUPS)
    e2 = i2 - float(N_GROUPS)
    rt_ref[...] = jnp.where(lane == 0, e1, jnp.where(lane == 1, e2, jnp.where(lane == 2, w1,
                            jnp.where(lane == 3, w2, 0.0))))


def _outproj(hg_p, hg_s, o_p, o_s, z, x_p, x_s, g1, sh2, sc2, norm_g, wr, wa, wo, w_route, b_route):
    tm = ROW_TILE
    t = z.shape[0]
    seg = tm // MOD_ROWS
    n_p = x_p.shape[0] // tm
    row = lambda i: (i, 0)
    fix = lambda i: (0, 0)
    row_p = lambda i: (jnp.minimum(i, n_p - 1), 0)
    row_s = lambda i: (jnp.maximum(i - n_p, 0), 0)
    once = pl.Buffered(1)

    def gate(col):
        return pl.BlockSpec((tm, GATE_TN), lambda i: (i, col // GATE_TN))

    return pl.pallas_call(
        functools.partial(_outproj_kernel, n_prompt_tiles=n_p),
        grid=(t // tm,),
        in_specs=[pl.BlockSpec((tm, D_RNN), row_p), pl.BlockSpec((tm, D_RNN), row_s, pipeline_mode=once),
                  pl.BlockSpec((tm, Q_DIM), row_p), pl.BlockSpec((tm, Q_DIM), row_s, pipeline_mode=once),
                  gate(COL_GA), gate(COL_GA + GATE_TN), gate(COL_GB), gate(COL_GB + GATE_TN),
                  pl.BlockSpec((tm, D_MODEL), row_p), pl.BlockSpec((tm, D_MODEL), row_s, pipeline_mode=once),
                  pl.BlockSpec((seg, D_MODEL), row), pl.BlockSpec((seg, D_MODEL), row),
                  pl.BlockSpec((seg, D_MODEL), row),
                  pl.BlockSpec((1, D_MODEL), fix),
                  pl.BlockSpec((D_RNN, D_MODEL), fix, pipeline_mode=once),
                  pl.BlockSpec((Q_DIM, D_MODEL), fix, pipeline_mode=once),
                  pl.BlockSpec((D_MODEL, D_MODEL), fix, pipeline_mode=once),
                  pl.BlockSpec((D_MODEL, ROUTE_LANES), fix),
                  pl.BlockSpec((1, ROUTE_LANES), fix)],
        out_specs=[pl.BlockSpec((tm, D_MODEL), row), pl.BlockSpec((tm, D_MODEL), row),
                   pl.BlockSpec((tm, ROUTE_LANES), row)],
        out_shape=[jax.ShapeDtypeStruct((t, D_MODEL), F32), jax.ShapeDtypeStruct((t, D_MODEL), F32),
                   jax.ShapeDtypeStruct((t, ROUTE_LANES), F32)],
        compiler_params=_cparams(("arbitrary",), VMEM_LIMIT_BIG),
        name="outproj",
    )(hg_p, hg_s, o_p, o_s, z, z, z, z, x_p, x_s, g1, sh2, sc2, norm_g.reshape(1, D_MODEL),
      wr, wa, wo, w_route, b_route)


def _row_gather(src_hbm, idx_ref, base, dst, sem, n_rows):
    for r in range(n_rows):
        tok = idx_ref[base + r]
        pltpu.make_async_copy(src_hbm.at[pl.ds(tok, 1), :], dst.at[pl.ds(r, 1), :], sem).start()


def _row_gather_wait(src_hbm, dst, sem, n_rows):
    pltpu.make_async_copy(src_hbm.at[pl.ds(0, n_rows), :], dst, sem).wait()


def _moe_kernel(te_ref, tfirst_ref, tvalid_ref, tnext_ref, tbase_ref, src_ref, n2_hbm, wg_hbm, wu_hbm, wd_hbm, y_ref,
                xbuf, xsem, wg_st, wu_st, wd_st, wsem, wg_bf, wu_bf, wd_bf):
    i = pl.program_id(0)
    n_tiles = pl.num_programs(0)
    slot = lax.rem(i, MOE_RING)
    stages = ((wg_hbm, wg_st, wg_bf), (wu_hbm, wu_st, wu_bf), (wd_hbm, wd_st, wd_bf))

    def weight_copy(k, e):
        hbm, st, _ = stages[k]
        return pltpu.make_async_copy(hbm.at[e], st, wsem.at[k])

    def gather_tile(tile):
        tc = jnp.minimum(tile, n_tiles - 1)

        @pl.when((tile < n_tiles) & (tvalid_ref[tc] == 1))
        def _():
            s = lax.rem(tile, MOE_RING)
            _row_gather(n2_hbm, src_ref, tbase_ref[tc], xbuf.at[s], xsem.at[s], MOE_TM)

    @pl.when(i == 0)
    def _():
        for k in range(len(stages)):
            weight_copy(k, te_ref[0]).start(priority=1)
        for ahead in range(MOE_RING - 1):
            gather_tile(ahead)

    @pl.when(tvalid_ref[i] == 1)
    def _():
        gather_tile(i + MOE_RING - 1)

        @pl.when(tfirst_ref[i] == 1)
        def _():
            for k, (_, st, bf) in enumerate(stages):
                weight_copy(k, te_ref[i]).wait()
                bf[...] = st[...].astype(BF16)

            @pl.when(tnext_ref[i] >= 0)
            def _():
                for k in range(len(stages)):
                    weight_copy(k, tnext_ref[i]).start(priority=1)

        _row_gather_wait(n2_hbm, xbuf.at[slot], xsem.at[slot], MOE_TM)
        x = xbuf[slot].astype(BF16)
        hgate = jnp.dot(x, wg_bf[...], preferred_element_type=F32)
        hup = jnp.dot(x, wu_bf[...], preferred_element_type=F32)
        act = (hgate * _sigmoid(hgate) * hup).astype(BF16)
        y_ref[...] = jnp.dot(act, wd_bf[...], preferred_element_type=F32)

    @pl.when(tvalid_ref[i] == 0)
    def _():
        y_ref[...] = jnp.zeros(y_ref.shape, F32)


def _moe(n2, w_gate, w_up, w_down, tile_expert, tile_first, tile_valid, tile_next, tile_base, src_tok):
    n_tiles = tile_expert.shape[0]
    hbm = pl.BlockSpec(memory_space=pl.ANY)
    grid_spec = pltpu.PrefetchScalarGridSpec(
        num_scalar_prefetch=6,
        grid=(n_tiles,),
        in_specs=[hbm, hbm, hbm, hbm],
        out_specs=pl.BlockSpec((MOE_TM, D_MODEL), lambda i, te, tf, tv, tn, tb, st: (i, 0)),
        scratch_shapes=[pltpu.VMEM((MOE_RING, MOE_TM, D_MODEL), F32),
                        pltpu.SemaphoreType.DMA((MOE_RING,)),
                        pltpu.VMEM((D_MODEL, D_EXPERT), F32),
                        pltpu.VMEM((D_MODEL, D_EXPERT), F32),
                        pltpu.VMEM((D_EXPERT, D_MODEL), F32),
                        pltpu.SemaphoreType.DMA((3,)),
                        pltpu.VMEM((D_MODEL, D_EXPERT), BF16),
                        pltpu.VMEM((D_MODEL, D_EXPERT), BF16),
                        pltpu.VMEM((D_EXPERT, D_MODEL), BF16)],
    )
    return pl.pallas_call(
        _moe_kernel,
        grid_spec=grid_spec,
        out_shape=jax.ShapeDtypeStruct((n_tiles * MOE_TM, D_MODEL), F32),
        compiler_params=_cparams(("arbitrary",)),
        name="moe",
    )(tile_expert, tile_first, tile_valid, tile_next, tile_base, src_tok, n2, w_gate, w_up, w_down)


def _route_plan(e1, e2, n_tok):
    experts = jnp.arange(N_EXPERTS, dtype=jnp.int32)
    flat_e = jnp.concatenate([e1, e2])
    onehot = (flat_e[:, None] == experts[None, :]).astype(jnp.int32)
    blk = onehot.astype(F32).reshape(-1, LANES, N_EXPERTS)
    tri = (jnp.arange(LANES)[:, None] >= jnp.arange(LANES)[None, :]).astype(F32)
    within = jnp.einsum("ij,bje->bie", tri, blk)
    before = jnp.cumsum(within[:, -1, :], axis=0) - within[:, -1, :]
    csum = (within + before[:, None, :]).reshape(-1, N_EXPERTS).astype(jnp.int32)
    rank = jnp.sum(csum * onehot, axis=1) - 1
    counts = csum[-1]
    tiles_per = (counts + MOE_TM - 1) // MOE_TM
    tile_end = jnp.cumsum(tiles_per)
    tile_off = tile_end - tiles_per
    slot = jnp.sum(onehot * tile_off[None, :], axis=1) * MOE_TM + rank
    n_tiles = (2 * n_tok) // MOE_TM + N_EXPERTS
    tok = jnp.tile(jnp.arange(n_tok, dtype=jnp.int32), 2)
    _, src_tok = lax.sort((slot, tok), num_keys=1)
    src_tok = jnp.concatenate([src_tok, jnp.zeros((MOE_TM,), jnp.int32)])
    count_off = jnp.cumsum(counts) - counts
    tile_id = jnp.arange(n_tiles, dtype=jnp.int32)
    n_used = tile_end[-1]
    tile_valid = (tile_id < n_used).astype(jnp.int32)
    te = jnp.sum((tile_end[None, :] <= jnp.minimum(tile_id, n_used - 1)[:, None]).astype(jnp.int32), axis=1)
    tile_expert = jnp.minimum(te, N_EXPERTS - 1)
    prev = jnp.concatenate([jnp.full((1,), -1, jnp.int32), tile_expert[:-1]])
    tile_first = (tile_expert != prev).astype(jnp.int32)
    oh_te = (tile_expert[:, None] == experts[None, :]).astype(jnp.int32)

    def of_expert(table):
        return jnp.sum(oh_te * table[None, :], axis=1)

    next_tile = of_expert(tile_end)
    oh_next = (tile_id[None, :] == jnp.minimum(next_tile, n_tiles - 1)[:, None]).astype(jnp.int32)
    tile_next = jnp.where(next_tile < n_used, jnp.sum(oh_next * tile_expert[None, :], axis=1), -1)
    tile_base = jnp.where(tile_valid == 1, of_expert(count_off) + (tile_id - of_expert(tile_off)) * MOE_TM, 0)
    return tile_expert, tile_first, tile_valid, tile_next, tile_base, src_tok, slot[:n_tok], slot[n_tok:]


def _final_kernel(p1_ref, p2_ref, ys_hbm, x1_ref, g2_ref, rt_ref, fg_ref, yp_ref, ysm_ref, ybuf, sem,
                  *, n_prompt_tiles):
    i = pl.program_id(0)
    n_tiles = pl.num_programs(0)
    slot = i % 2

    def start(tile, s):
        _row_gather(ys_hbm, p1_ref, tile * ROW_TILE, ybuf.at[s, 0], sem.at[s], ROW_TILE)
        _row_gather(ys_hbm, p2_ref, tile * ROW_TILE, ybuf.at[s, 1], sem.at[s], ROW_TILE)

    @pl.when(i == 0)
    def _():
        start(0, 0)

    @pl.when(i + 1 < n_tiles)
    def _():
        start(i + 1, 1 - slot)

    _row_gather_wait(ys_hbm, ybuf.at[slot, 0], sem.at[slot], ROW_TILE)
    _row_gather_wait(ys_hbm, ybuf.at[slot, 1], sem.at[slot], ROW_TILE)
    rt = rt_ref[...]
    moe = rt[:, 2:3] * ybuf[slot, 0] + rt[:, 3:4] * ybuf[slot, 1]
    x2 = x1_ref[...] + _bcast_rows(g2_ref[...], ROW_TILE) * moe
    y = x2 * lax.rsqrt(jnp.mean(x2 * x2, axis=-1, keepdims=True) + EPS) * fg_ref[...]

    @pl.when(i < n_prompt_tiles)
    def _():
        yp_ref[...] = y

    @pl.when(i >= n_prompt_tiles)
    def _():
        ysm_ref[...] = y


def _final(ys, x1, g2, rt, final_g, p1, p2, n_prompt_rows):
    t = x1.shape[0]
    tm = ROW_TILE
    seg = tm // MOD_ROWS
    n_p = n_prompt_rows // tm
    grid_spec = pltpu.PrefetchScalarGridSpec(
        num_scalar_prefetch=2,
        grid=(t // tm,),
        in_specs=[pl.BlockSpec(memory_space=pl.ANY),
                  pl.BlockSpec((tm, D_MODEL), lambda i, a, b: (i, 0)),
                  pl.BlockSpec((seg, D_MODEL), lambda i, a, b: (i, 0)),
                  pl.BlockSpec((tm, ROUTE_LANES), lambda i, a, b: (i, 0)),
                  pl.BlockSpec((1, D_MODEL), lambda i, a, b: (0, 0))],
        out_specs=[pl.BlockSpec((tm, D_MODEL), lambda i, a, b: (jnp.minimum(i, n_p - 1), 0)),
                   pl.BlockSpec((tm, D_MODEL), lambda i, a, b: (jnp.maximum(i - n_p, 0), 0))],
        scratch_shapes=[pltpu.VMEM((2, 2, tm, D_MODEL), F32),
                        pltpu.SemaphoreType.DMA((2,))],
    )
    return pl.pallas_call(
        functools.partial(_final_kernel, n_prompt_tiles=n_p),
        grid_spec=grid_spec,
        out_shape=[jax.ShapeDtypeStruct((n_prompt_rows, D_MODEL), F32),
                   jax.ShapeDtypeStruct((t - n_prompt_rows, D_MODEL), F32)],
        compiler_params=_cparams(("arbitrary",)),
        name="final",
    )(p1, p2, ys, x1, g2, rt, final_g.reshape(1, D_MODEL))


def kernel(x_prompt, x_sample, cache_k_win, cache_v_win, state_conv, state_rglru, c_prompt, c_sample, w_ada, b_ada, norm1_g, norm2_g, w_in, b_in, w_conv, b_conv, w_rg_a, b_rg_a, w_rg_x, b_rg_x, lru_lambda, w_rnn_out, w_attn_out, w_out, attn_sinks, w_route_group, b_route_group, w_route_expert, b_route_expert, w_exp_gate, w_exp_up, w_exp_down, rel_bias_table, final_norm_g):
    n_b, seq, _ = x_prompt.shape
    d_b, d_seq, _ = x_sample.shape
    assert w_ada.shape[0] == 1, "single trunk layer"
    assert seq % (ATT_CPB * CHUNK) == 0 and seq % MOD_ROWS == 0 and d_seq == MOD_ROWS
    assert d_seq >= CONV_W - 1 and d_b == N_CHAINS
    t_p, t_s = n_b * seq, d_b * d_seq
    t = t_p + t_s
    assert t_s == ROW_TILE and t_p % ROW_TILE == 0
    cw = cache_k_win.shape[2]
    l = 0

    x_p = x_prompt.reshape(t_p, D_MODEL)
    x_s = x_sample.reshape(t_s, D_MODEL)

    n_c = n_b + d_b
    c_all = jnp.pad(jnp.concatenate([c_prompt, c_sample], axis=0), ((0, -n_c % SUBLANES), (0, 0)))
    mod = _ada(c_all, w_ada[l], b_ada[l])

    def per_segment(m):
        return jnp.concatenate([jnp.repeat(m[:n_b], seq // MOD_ROWS, axis=0),
                                jnp.repeat(m[n_b:n_c], d_seq // MOD_ROWS, axis=0)], axis=0)

    sh1, sc1, g1, sh2, sc2, g2 = [per_segment(m) for m in jnp.split(mod, 6, axis=-1)]

    z = _inproj(_norm1(x_p, x_s, sc1, sh1, norm1_g[l]), w_in[l], b_in[l])

    w_gate = jnp.concatenate([w_rg_a[l], w_rg_x[l]], axis=-1)
    b_gate = jnp.concatenate([b_rg_a[l], b_rg_x[l]], axis=-1)[:, None, :]
    hg_p, hl_p = _rglru(z, jnp.zeros((n_b, SUBLANES, D_RNN), F32), jnp.zeros((n_b, N_CHAINS, D_RNN), F32),
                        w_conv[l], b_conv[l], w_gate, b_gate, lru_lambda[l],
                        n_blocks=n_b, n_seq=1, seq_len=seq, row_block0=0, link=True)
    state8 = jnp.pad(state_conv[l], ((0, 0), (SUBLANES - (CONV_W - 1), 0), (0, 0)))
    hg_s, hl_s = _rglru(z, state8, state_rglru[l][None], w_conv[l], b_conv[l], w_gate, b_gate, lru_lambda[l],
                        n_blocks=1, n_seq=d_b, seq_len=d_seq, row_block0=t_p // t_s, link=False)

    sinks = attn_sinks[l]
    bkt_p = _bucket_map(WIN_CHUNKS * CHUNK + jnp.arange(CHUNK), jnp.arange(SPAN))
    o_p = _attn_band(z, rel_bias_table, sinks, bkt_p, n_batch=n_b, seq_len=seq)
    k_new = z[t_p:, COL_K:COL_K + KV_DIM].reshape(d_b, d_seq, KV_DIM)
    v_new = z[t_p:, COL_V:COL_V + KV_DIM].reshape(d_b, d_seq, KV_DIM)
    k_all = jnp.concatenate([cache_k_win[l].reshape(d_b, cw, KV_DIM), k_new], axis=1)
    v_all = jnp.concatenate([cache_v_win[l].reshape(d_b, cw, KV_DIM), v_new], axis=1)
    bkt_s = _bucket_map(cw + jnp.arange(d_seq), jnp.arange(cw + d_seq))
    o_s = _attn_step(z, k_all, v_all, rel_bias_table, sinks, bkt_s, n_batch=d_b, n_q=d_seq, row_block0=t_p // d_seq)

    n_route = N_GROUPS + N_EXPERTS
    w_route = jnp.pad(jnp.concatenate([w_route_group[l], w_route_expert[l]], axis=1),
                      ((0, 0), (0, ROUTE_LANES - n_route)))
    b_route = jnp.pad(jnp.concatenate([b_route_group[l], b_route_expert[l]]),
                      (0, ROUTE_LANES - n_route)).reshape(1, ROUTE_LANES)
    x1, n2, rt = _outproj(hg_p, hg_s, o_p, o_s, z, x_p, x_s, g1, sh2, sc2, norm2_g[l],
                          w_rnn_out[l].astype(BF16), w_attn_out[l].astype(BF16), w_out[l].astype(BF16),
                          w_route, b_route)

    e1 = rt[:, 0].astype(jnp.int32)
    e2 = rt[:, 1].astype(jnp.int32)
    tile_expert, tile_first, tile_valid, tile_next, tile_base, src_tok, p1, p2 = _route_plan(e1, e2, t)
    ys = _moe(n2, w_exp_gate[l], w_exp_up[l], w_exp_down[l], tile_expert, tile_first, tile_valid, tile_next,
              tile_base, src_tok)
    y_p, y_s = _final(ys, x1, g2, rt, final_norm_g, p1, p2, t_p)

    win = min(WINDOW, seq)

    def tail(col0, width, n_rows):
        return jnp.stack([z[(b + 1) * seq - n_rows:(b + 1) * seq, col0:col0 + width] for b in range(n_b)])

    kp = tail(COL_K, KV_DIM, win).reshape(n_b, win, N_KV_HEADS, HEAD_DIM)
    vp = tail(COL_V, KV_DIM, win).reshape(n_b, win, N_KV_HEADS, HEAD_DIM)
    cp = tail(COL_XR, D_RNN, CONV_W - 1)
    rp = hl_p[:, N_CHAINS - 1, :]
    ks = k_all[:, -cw:].reshape(d_b, cw, N_KV_HEADS, HEAD_DIM)
    vs = v_all[:, -cw:].reshape(d_b, cw, N_KV_HEADS, HEAD_DIM)
    cs = z[t_p:, COL_XR:COL_XR + D_RNN].reshape(d_b, d_seq, D_RNN)[:, -(CONV_W - 1):]
    rs = hl_s[0]
    return (y_p.reshape(n_b, seq, D_MODEL), y_s.reshape(d_b, d_seq, D_MODEL),
            kp[None], vp[None], cp[None], rp[None], ks[None], vs[None], cs[None], rs[None])
```

```python
import functools
import math

import jax
import jax.numpy as jnp
from jax import lax
from jax.experimental import pallas as pl
from jax.experimental.pallas import tpu as pltpu

F32 = jnp.float32
BF16 = jnp.bfloat16

D_MODEL = 2048
D_RNN = 2048
RNN_BLOCK = 128
CONV_W = 4
LRU_C = 8.0
N_HEADS = 16
N_KV_HEADS = 4
HEAD_DIM = 128
GQA_GROUP = N_HEADS // N_KV_HEADS
Q_DIM = N_HEADS * HEAD_DIM
KV_DIM = N_KV_HEADS * HEAD_DIM
CHUNK = 64
WINDOW = 128
WIN_CHUNKS = WINDOW // CHUNK
SPAN = (WIN_CHUNKS + 1) * CHUNK
N_BUCKETS = 32
MAX_DISTANCE = 128
N_GROUPS = 4
E_PER_GROUP = 8
N_EXPERTS = N_GROUPS * E_PER_GROUP
D_EXPERT = 512
EPS = 1e-6
NEG_INF = -1e30
D_IN = 2 * D_RNN + Q_DIM + 2 * KV_DIM + 2 * D_MODEL
COL_XR, COL_GR, COL_Q = 0, D_RNN, 2 * D_RNN
COL_K = COL_Q + Q_DIM
COL_V = COL_K + KV_DIM
COL_GA = COL_V + KV_DIM
COL_GB = COL_GA + D_MODEL

LANES = 128
SUBLANES = 8
MOD_ROWS = 32
KEY_TILE = 256
VMEM_LIMIT = 56 * 1024 * 1024
VMEM_LIMIT_BIG = 60 * 1024 * 1024
ADA_TN = 1024
ADA_LATE_TN_CHOICES = (256, 512, 1024, 2048, 4096)

ROW_TILE = 256
IN_TM_CHOICES = (1408, 768, 256)
IN_TN = 1024
MOE_TM = 192
MOE_RING = 3


def _sigmoid(x):
    return 0.5 * jnp.tanh(0.5 * x) + 0.5


def _gelu_tanh(x):
    return 0.5 * x * (1.0 + jnp.tanh(math.sqrt(2.0 / math.pi) * (x + 0.044715 * (x * x * x))))


def _bcast_rows(v, rows):
    n, d = v.shape
    return jnp.broadcast_to(v[:, None, :], (n, MOD_ROWS, d)).reshape(rows, d)


def _rms_modulate(x, gain, scale_seg, shift_seg):
    rows = x.shape[0]
    y = x * lax.rsqrt(jnp.mean(x * x, axis=-1, keepdims=True) + EPS) * gain
    return y * (1.0 + _bcast_rows(scale_seg, rows)) + _bcast_rows(shift_seg, rows)


def _cparams(sem, vmem_limit=VMEM_LIMIT):
    return pltpu.CompilerParams(dimension_semantics=sem, vmem_limit_bytes=vmem_limit)


def _ada_block(c_ref, w_ref, b_ref):
    c = c_ref[...]
    s = (c * _sigmoid(c)).astype(BF16)
    return jnp.dot(s, w_ref[...].astype(BF16), preferred_element_type=F32) + b_ref[...]


def _ada_kernel(c_ref, w_ref, b_ref, o_ref):
    o_ref[...] = _ada_block(c_ref, w_ref, b_ref)


def _ada(c_all, w_ada, b_ada, n_cols):
    rows = c_all.shape[0]
    tn = ADA_TN
    return pl.pallas_call(
        _ada_kernel,
        grid=(n_cols // tn,),
        in_specs=[pl.BlockSpec((rows, D_MODEL), lambda j: (0, 0)),
                  pl.BlockSpec((D_MODEL, tn), lambda j: (0, j)),
                  pl.BlockSpec((1, tn), lambda j: (0, j))],
        out_specs=pl.BlockSpec((rows, tn), lambda j: (0, j)),
        out_shape=jax.ShapeDtypeStruct((rows, n_cols), F32),
        compiler_params=_cparams(("arbitrary",)),
        name="ada",
    )(c_all, w_ada, b_ada.reshape(1, w_ada.shape[1]))


NORM_NSUB = 3


def _norm1_kernel(x0_ref, x1_ref, x2_ref, xs_ref, sc_ref, sh_ref, g_ref, n1_ref, *, n_prompt_tiles):
    tail_is_sample = pl.program_id(0) * NORM_NSUB + (NORM_NSUB - 1) >= n_prompt_tiles
    gain = g_ref[...]
    segs = ROW_TILE // MOD_ROWS
    for r, x_ref in enumerate((x0_ref, x1_ref, x2_ref)):
        for s in range(segs):
            rows = slice(s * MOD_ROWS, (s + 1) * MOD_ROWS)
            x = x_ref[rows, :]
            if r == NORM_NSUB - 1:
                x = jnp.where(tail_is_sample, xs_ref[rows, :], x)
            y = x * lax.rsqrt(jnp.mean(x * x, axis=-1, keepdims=True) + EPS) * gain
            k = r * segs + s
            out_rows = slice(k * MOD_ROWS, (k + 1) * MOD_ROWS)
            n1_ref[out_rows, :] = (y * (1.0 + sc_ref[k:k + 1, :]) + sh_ref[k:k + 1, :]).astype(BF16)


def _norm1(x_p, x_s, sc1, sh1, norm_g):
    n_p = x_p.shape[0] // ROW_TILE
    t = x_p.shape[0] + x_s.shape[0]
    tm = NORM_NSUB * ROW_TILE
    assert x_s.shape[0] == ROW_TILE and t % tm == 0
    seg = tm // MOD_ROWS

    def sub(r):
        return pl.BlockSpec((ROW_TILE, D_MODEL), lambda i: (jnp.minimum(i * NORM_NSUB + r, n_p - 1), 0))

    return pl.pallas_call(
        functools.partial(_norm1_kernel, n_prompt_tiles=n_p),
        grid=(t // tm,),
        in_specs=[sub(0), sub(1), sub(2),
                  pl.BlockSpec((ROW_TILE, D_MODEL), lambda i: (0, 0)),
                  pl.BlockSpec((seg, D_MODEL), lambda i: (i, 0)),
                  pl.BlockSpec((seg, D_MODEL), lambda i: (i, 0)),
                  pl.BlockSpec((1, D_MODEL), lambda i: (0, 0))],
        out_specs=pl.BlockSpec((tm, D_MODEL), lambda i: (i, 0)),
        out_shape=jax.ShapeDtypeStruct((t, D_MODEL), BF16),
        compiler_params=_cparams(("arbitrary",)),
        name="norm1",
    )(x_p, x_p, x_p, x_s, sc1, sh1, norm_g.reshape(1, D_MODEL))


def _inproj_kernel(n1_ref, w_ref, b_ref, c_ref, wa_ref, ba_ref, z_ref, mod_ref, w_bf, *, n_late):
    @pl.when(pl.program_id(1) == 0)
    def _():
        w_bf[...] = w_ref[...].astype(BF16)

    z_ref[...] = jnp.dot(n1_ref[...], w_bf[...], preferred_element_type=F32) + b_ref[...]

    @pl.when(pl.program_id(0) * pl.num_programs(1) + pl.program_id(1) < n_late)
    def _():
        mod_ref[...] = _ada_block(c_ref, wa_ref, ba_ref)


def _inproj(n1, w_in, b_in, c_all, w_ada, b_ada, late_col0):
    t = n1.shape[0]
    tm = next(m for m in IN_TM_CHOICES if t % m == 0)
    n_row = t // tm
    n_steps = (D_IN // IN_TN) * n_row
    rows = c_all.shape[0]
    late_cols = w_ada.shape[1] - late_col0
    late_tn = next(w for w in ADA_LATE_TN_CHOICES if late_cols // w <= n_steps)
    n_late = late_cols // late_tn
    late = lambda j, i: jnp.minimum(j * n_row + i, n_late - 1)
    return pl.pallas_call(
        functools.partial(_inproj_kernel, n_late=n_late),
        grid=(D_IN // IN_TN, n_row),
        in_specs=[pl.BlockSpec((tm, D_MODEL), lambda j, i: (i, 0)),
                  pl.BlockSpec((D_MODEL, IN_TN), lambda j, i: (0, j)),
                  pl.BlockSpec((1, IN_TN), lambda j, i: (0, j)),
                  pl.BlockSpec((rows, D_MODEL), lambda j, i: (0, 0)),
                  pl.BlockSpec((D_MODEL, late_tn), lambda j, i: (0, late_col0 // late_tn + late(j, i))),
                  pl.BlockSpec((1, late_tn), lambda j, i: (0, late_col0 // late_tn + late(j, i)))],
        out_specs=[pl.BlockSpec((tm, IN_TN), lambda j, i: (i, j)),
                   pl.BlockSpec((rows, late_tn), lambda j, i: (0, late(j, i)))],
        out_shape=[jax.ShapeDtypeStruct((t, D_IN), F32),
                   jax.ShapeDtypeStruct((rows, late_cols), F32)],
        scratch_shapes=[pltpu.VMEM((D_MODEL, IN_TN), BF16)],
        compiler_params=_cparams(("arbitrary", "arbitrary")),
        name="inproj",
    )(n1, w_in, b_in.reshape(1, D_IN), c_all, w_ada, b_ada.reshape(1, w_ada.shape[1]))


RG_CW = 256
RG_SLABS = RG_CW // LANES
N_CHAINS = SUBLANES
CHAIN_PAD = 8


def _rglru_kernel(xr_ref, gr_ref, st_ref, h0_ref, wc_ref, bc_ref, wg_ref, bg_ref, lam_ref,
                  hg_ref, hl_ref, ext_scr, a_scr, u_scr, *, n_seq, seq_len, link):
    rows = n_seq * seq_len
    cl = rows // N_CHAINS
    pitch = cl + CHAIN_PAD

    for s in range(n_seq):
        ext_scr[s, 0:SUBLANES, :] = st_ref[s]
        ext_scr[s, SUBLANES:SUBLANES + seq_len, :] = xr_ref[s * seq_len:(s + 1) * seq_len, :]

    z = -lam_ref[...]
    softplus = jnp.maximum(z, 0.0) + jnp.log1p(jnp.exp(-jnp.abs(z)))
    nq = (-0.25 * LRU_C) * softplus

    for c in range(N_CHAINS):
        s, r0 = divmod(c * cl, seq_len)
        xc = bc_ref[...]
        for j in range(CONV_W):
            xc = xc + wc_ref[j:j + 1, :] * ext_scr[s, pl.ds(SUBLANES - (CONV_W - 1) + j + r0, cl), :]
        for sl in range(RG_SLABS):
            lanes = slice(sl * LANES, (sl + 1) * LANES)
            xb = xc[:, lanes]
            g = jnp.dot(xb.astype(BF16), wg_ref[sl].astype(BF16), preferred_element_type=F32) + bg_ref[sl]
            tr = jnp.tanh(0.5 * g[:, :LANES])
            ti = jnp.tanh(0.5 * g[:, LANES:])
            th = jnp.tanh(nq[:, lanes] * (tr + 1.0))
            rcp = 1.0 / (1.0 - th)
            a_scr[sl, c * pitch:c * pitch + cl, :] = (1.0 + th) * rcp
            u_scr[sl, c * pitch:c * pitch + cl, :] = (jnp.sqrt(-th) * rcp) * ((ti + 1.0) * xb)

    def step(t, carry):
        hs, ps = carry
        new_h, new_p = [], []
        for sl in range(RG_SLABS):
            a = a_scr[sl, pl.ds(t, N_CHAINS, stride=pitch), :]
            u = u_scr[sl, pl.ds(t, N_CHAINS, stride=pitch), :]
            h = a * hs[sl] + u
            u_scr[sl, pl.ds(t, N_CHAINS, stride=pitch), :] = h
            new_h.append(h)
            if link:
                p = a * ps[sl]
                a_scr[sl, pl.ds(t, N_CHAINS, stride=pitch), :] = p
                new_p.append(p)
            else:
                new_p.append(ps[sl])
        return tuple(new_h), tuple(new_p)

    h_init = tuple(h0_ref[0, :, sl * LANES:(sl + 1) * LANES] for sl in range(RG_SLABS))
    p_init = tuple(jnp.ones((N_CHAINS, LANES), F32) for _ in range(RG_SLABS))
    h_end, p_end = lax.fori_loop(0, cl, step, (h_init, p_init), unroll=8)

    row = lax.broadcasted_iota(jnp.int32, (N_CHAINS, LANES), 0)
    for sl in range(RG_SLABS):
        lanes = slice(sl * LANES, (sl + 1) * LANES)
        if link:
            def shift_down(v):
                return jnp.where(row == 0, 0.0, pltpu.roll(v, 1, axis=0))
            hh = h_end[sl]
            for _ in range(N_CHAINS - 1):
                hh = h_end[sl] + p_end[sl] * shift_down(hh)
            carry_in = shift_down(hh)
        else:
            hh = h_end[sl]
        hl_ref[0, :, lanes] = hh
        for c in range(N_CHAINS):
            h = u_scr[sl, c * pitch:c * pitch + cl, :]
            if link:
                h = h + a_scr[sl, c * pitch:c * pitch + cl, :] * carry_in[c:c + 1, :]
            gg = _gelu_tanh(gr_ref[c * cl:(c + 1) * cl, lanes])
            hg_ref[c * cl:(c + 1) * cl, lanes] = (h * gg).astype(BF16)


def _rglru(z, state8, h0, w_conv, b_conv, w_gate, b_gate, lam, *, n_blocks, n_seq, seq_len, row_block0, link):
    rows = n_seq * seq_len
    cl = rows // N_CHAINS
    ncb = D_RNN // RG_CW
    gr0 = COL_GR // RG_CW
    kern = functools.partial(_rglru_kernel, n_seq=n_seq, seq_len=seq_len, link=link)
    return pl.pallas_call(
        kern,
        grid=(n_blocks, ncb),
        in_specs=[pl.BlockSpec((rows, RG_CW), lambda b, n: (row_block0 + b, n)),
                  pl.BlockSpec((rows, RG_CW), lambda b, n: (row_block0 + b, gr0 + n)),
                  pl.BlockSpec((n_seq, SUBLANES, RG_CW), lambda b, n: (b, 0, n)),
                  pl.BlockSpec((1, N_CHAINS, RG_CW), lambda b, n: (b, 0, n)),
                  pl.BlockSpec((CONV_W, RG_CW), lambda b, n: (0, n)),
                  pl.BlockSpec((1, RG_CW), lambda b, n: (0, n)),
                  pl.BlockSpec((RG_SLABS, RNN_BLOCK, 2 * RNN_BLOCK), lambda b, n: (n, 0, 0)),
                  pl.BlockSpec((RG_SLABS, 1, 2 * RNN_BLOCK), lambda b, n: (n, 0, 0)),
                  pl.BlockSpec((1, RG_CW), lambda b, n: (0, n))],
        out_specs=[pl.BlockSpec((rows, RG_CW), lambda b, n: (b, n)),
                   pl.BlockSpec((1, N_CHAINS, RG_CW), lambda b, n: (b, 0, n))],
        out_shape=[jax.ShapeDtypeStruct((n_blocks * rows, D_RNN), BF16),
                   jax.ShapeDtypeStruct((n_blocks, N_CHAINS, D_RNN), F32)],
        scratch_shapes=[pltpu.VMEM((n_seq, SUBLANES + seq_len, RG_CW), F32),
                        pltpu.VMEM((RG_SLABS, N_CHAINS * (cl + CHAIN_PAD), LANES), F32),
                        pltpu.VMEM((RG_SLABS, N_CHAINS * (cl + CHAIN_PAD), LANES), F32)],
        compiler_params=_cparams(("arbitrary", "arbitrary")),
        name="rglru_link" if link else "rglru_step",
    )(z, z, state8, h0, w_conv, b_conv.reshape(1, D_RNN), w_gate, b_gate, lam.reshape(1, D_RNN))


LOG2E = math.log2(math.e)


def _build_bias(tbl_ref, bkt_ref, bias_scr, n_q):
    bkt = bkt_ref[...]
    base = jnp.where(bkt < 0, NEG_INF, 0.0)
    for h in range(N_HEADS):
        acc = base
        for bk in range(N_BUCKETS):
            acc = jnp.where(bkt == bk, tbl_ref[bk, h] * LOG2E, acc)
        hk, g = divmod(h, GQA_GROUP)
        bias_scr[hk, g * n_q:(g + 1) * n_q, :] = acc


def _sink_columns(sink_ref, n_q):
    row = lax.broadcasted_iota(jnp.int32, (GQA_GROUP * n_q, 1), 0)
    cols = []
    for hk in range(N_KV_HEADS):
        col = jnp.full((GQA_GROUP * n_q, 1), sink_ref[hk * GQA_GROUP] * LOG2E, F32)
        for g in range(1, GQA_GROUP):
            col = jnp.where(row >= g * n_q, sink_ref[hk * GQA_GROUP + g] * LOG2E, col)
        cols.append(col)
    return cols


def _attend(q_of, k_of, v_of, bias_scr, sinks, key_ok, store, n_q):
    scores = []
    for hk in range(N_KV_HEADS):
        qg = jnp.concatenate([q_of(hk * GQA_GROUP + g) for g in range(GQA_GROUP)], axis=0)
        s = lax.dot_general(qg, k_of(hk), (((1,), (1,)), ((), ())), preferred_element_type=F32)
        s = s * (HEAD_DIM ** -0.5 * LOG2E) + bias_scr[hk]
        if key_ok is not None:
            s = jnp.where(key_ok, s, NEG_INF)
        scores.append(s)
    probs = []
    for hk in range(N_KV_HEADS):
        s, sink = scores[hk], sinks[hk]
        m = jnp.maximum(jnp.max(s, axis=-1, keepdims=True), sink)
        p = jnp.exp2(s - m)
        inv = 1.0 / (jnp.sum(p, axis=-1, keepdims=True) + jnp.exp2(sink - m))
        probs.append((p * inv).astype(BF16))
    for hk in range(N_KV_HEADS):
        o = jnp.dot(probs[hk], v_of(hk), preferred_element_type=F32)
        for g in range(GQA_GROUP):
            store(hk * GQA_GROUP + g, o[g * n_q:(g + 1) * n_q, :].astype(BF16))


ATT_CPB = 8
ATT_LEAD = WIN_CHUNKS * CHUNK
ATT_TAIL = KEY_TILE - SPAN


def _attn_band_kernel(tbl_ref, sink_ref, q_ref, k_ref, v_ref, bkt_ref, o_ref, kpad, vpad, bias_scr, *, seq_len):
    b = pl.program_id(0)
    cg = pl.program_id(1)

    @pl.when((b == 0) & (cg == 0))
    def _():
        _build_bias(tbl_ref, bkt_ref, bias_scr, CHUNK)

    @pl.when(cg == 0)
    def _():
        for ref, pad in ((k_ref, kpad), (v_ref, vpad)):
            pad[0:ATT_LEAD, :] = jnp.zeros((ATT_LEAD, KV_DIM), BF16)
            pad[ATT_LEAD:ATT_LEAD + seq_len, :] = ref[...].astype(BF16)
            pad[ATT_LEAD + seq_len:, :] = jnp.zeros((ATT_TAIL, KV_DIM), BF16)

    kidx = lax.broadcasted_iota(jnp.int32, (1, KEY_TILE), 1)
    sinks = _sink_columns(sink_ref, CHUNK)

    def chunk(c, carry):
        q0 = pl.multiple_of(c * CHUNK, CHUNK)
        start = pl.multiple_of((cg * ATT_CPB + c) * CHUNK, CHUNK)
        key_ok = start + kidx >= ATT_LEAD

        def head_cols(h):
            return slice(h * HEAD_DIM, (h + 1) * HEAD_DIM)

        def store(h, val):
            o_ref[pl.ds(q0, CHUNK), head_cols(h)] = val

        def run(mask):
            _attend(lambda h: q_ref[pl.ds(q0, CHUNK), head_cols(h)].astype(BF16),
                    lambda hk: kpad[pl.ds(start, KEY_TILE), head_cols(hk)],
                    lambda hk: vpad[pl.ds(start, KEY_TILE), head_cols(hk)],
                    bias_scr, sinks, mask, store, CHUNK)

        pl.when(start < ATT_LEAD)(lambda: run(key_ok))
        pl.when(start >= ATT_LEAD)(lambda: run(None))
        return carry

    lax.fori_loop(0, ATT_CPB, chunk, 0)


def _attn_band(z, table, sinks, bkt, *, n_batch, seq_len):
    rows = ATT_CPB * CHUNK
    ng = seq_len // rows
    kern = functools.partial(_attn_band_kernel, seq_len=seq_len)
    smem = pl.BlockSpec(memory_space=pltpu.SMEM)
    pad_rows = ATT_LEAD + seq_len + ATT_TAIL
    return pl.pallas_call(
        kern,
        grid=(n_batch, ng),
        in_specs=[smem, smem,
                  pl.BlockSpec((rows, Q_DIM), lambda b, c: (b * ng + c, COL_Q // Q_DIM)),
                  pl.BlockSpec((seq_len, KV_DIM), lambda b, c: (b, COL_K // KV_DIM)),
                  pl.BlockSpec((seq_len, KV_DIM), lambda b, c: (b, COL_V // KV_DIM)),
                  pl.BlockSpec((CHUNK, KEY_TILE), lambda b, c: (0, 0))],
        out_specs=pl.BlockSpec((rows, Q_DIM), lambda b, c: (b * ng + c, 0)),
        out_shape=jax.ShapeDtypeStruct((n_batch * seq_len, Q_DIM), BF16),
        scratch_shapes=[pltpu.VMEM((pad_rows, KV_DIM), BF16),
                        pltpu.VMEM((pad_rows, KV_DIM), BF16),
                        pltpu.VMEM((N_KV_HEADS, GQA_GROUP * CHUNK, KEY_TILE), F32)],
        compiler_params=_cparams(("arbitrary", "arbitrary")),
        name="attn_band",
    )(table, sinks, z, z, z, bkt)


def _attn_step_kernel(tbl_ref, sink_ref, q_ref, k_ref, v_ref, bkt_ref, o_ref, kbuf, vbuf, bias_scr,
                      *, n_keys, n_q):
    b = pl.program_id(0)

    @pl.when(b == 0)
    def _():
        _build_bias(tbl_ref, bkt_ref, bias_scr, n_q)
        kbuf[n_keys:, :] = jnp.zeros((KEY_TILE - n_keys, KV_DIM), BF16)
        vbuf[n_keys:, :] = jnp.zeros((KEY_TILE - n_keys, KV_DIM), BF16)

    kbuf[0:n_keys, :] = k_ref[0].astype(BF16)
    vbuf[0:n_keys, :] = v_ref[0].astype(BF16)

    def head_cols(h):
        return slice(h * HEAD_DIM, (h + 1) * HEAD_DIM)

    def store(h, val):
        o_ref[:, head_cols(h)] = val

    _attend(lambda h: q_ref[:, head_cols(h)].astype(BF16),
            lambda hk: kbuf[:, head_cols(hk)], lambda hk: vbuf[:, head_cols(hk)],
            bias_scr, _sink_columns(sink_ref, n_q), None, store, n_q)


def _attn_step(z, k_all, v_all, table, sinks, bkt, *, n_batch, n_q, row_block0):
    n_keys = k_all.shape[1]
    kern = functools.partial(_attn_step_kernel, n_keys=n_keys, n_q=n_q)
    smem = pl.BlockSpec(memory_space=pltpu.SMEM)
    return pl.pallas_call(
        kern,
        grid=(n_batch,),
        in_specs=[smem, smem,
                  pl.BlockSpec((n_q, Q_DIM), lambda b: (row_block0 + b, COL_Q // Q_DIM)),
                  pl.BlockSpec((1, n_keys, KV_DIM), lambda b: (b, 0, 0)),
                  pl.BlockSpec((1, n_keys, KV_DIM), lambda b: (b, 0, 0)),
                  pl.BlockSpec((n_q, KEY_TILE), lambda b: (0, 0))],
        out_specs=pl.BlockSpec((n_q, Q_DIM), lambda b: (b, 0)),
        out_shape=jax.ShapeDtypeStruct((n_batch * n_q, Q_DIM), BF16),
        scratch_shapes=[pltpu.VMEM((KEY_TILE, KV_DIM), BF16),
                        pltpu.VMEM((KEY_TILE, KV_DIM), BF16),
                        pltpu.VMEM((N_KV_HEADS, GQA_GROUP * n_q, KEY_TILE), F32)],
        compiler_params=_cparams(("arbitrary",)),
        name="attn_step",
    )(table, sinks, z, k_all, v_all, bkt)


def _t5_bucket(rel):
    nb = N_BUCKETS // 2
    ret = jnp.where(rel > 0, nb, 0)
    n = jnp.abs(rel)
    max_exact = nb // 2
    nf = jnp.maximum(n, 1).astype(jnp.float32)
    large = max_exact + (jnp.log(nf / max_exact) / math.log(MAX_DISTANCE / max_exact)
                         * (nb - max_exact)).astype(jnp.int32)
    large = jnp.minimum(large, nb - 1)
    return ret + jnp.where(n < max_exact, n, large)


def _bucket_map(q_pos, k_pos):
    bkt = _t5_bucket(k_pos[None, :] - q_pos[:, None]).astype(jnp.int32)
    return jnp.pad(bkt, ((0, 0), (0, KEY_TILE - k_pos.shape[0])), constant_values=-1)


ROUTE_LANES = LANES
GATE_TN = 1024


def _outproj_kernel(hgp_ref, hgs_ref, op_ref, os_ref, ga0_ref, ga1_ref, gb0_ref, gb1_ref, xp_ref, xs_ref,
                    g1_ref, sh2_ref, sc2_ref, ng_ref, wr_ref, wa_ref, wo_ref, wrt_ref, brt_ref,
                    x1_ref, n2_ref, rt_ref, *, n_prompt_tiles):
    tm = x1_ref.shape[0]
    is_prompt = pl.program_id(0) < n_prompt_tiles
    hg = jnp.where(is_prompt, hgp_ref[...], hgs_ref[...])
    o = jnp.where(is_prompt, op_ref[...], os_ref[...])
    x = jnp.where(is_prompt, xp_ref[...], xs_ref[...])
    ya = jnp.dot(hg, wr_ref[...], preferred_element_type=F32)
    yb = jnp.dot(o, wa_ref[...], preferred_element_type=F32)
    halves = []
    for k, (ga_ref, gb_ref) in enumerate(((ga0_ref, gb0_ref), (ga1_ref, gb1_ref))):
        cols = slice(k * GATE_TN, (k + 1) * GATE_TN)
        halves.append((_sigmoid(ga_ref[...]) * ya[:, cols] + _sigmoid(gb_ref[...]) * yb[:, cols]).astype(BF16))
    merged = jnp.concatenate(halves, axis=1)
    mix = jnp.dot(merged, wo_ref[...], preferred_element_type=F32)
    x1 = x + _bcast_rows(g1_ref[...], tm) * mix
    x1_ref[...] = x1
    n2 = _rms_modulate(x1, ng_ref[...], sc2_ref[...], sh2_ref[...])
    n2_ref[...] = n2

    lg = jnp.dot(n2.astype(BF16), wrt_ref[...].astype(BF16), preferred_element_type=F32) + brt_ref[...]

    lane = lax.broadcasted_iota(jnp.int32, (tm, ROUTE_LANES), 1)
    lane_f = lane.astype(F32)
    e_lane = lane - N_GROUPS
    lane_group = (e_lane >> 3).astype(F32)

    def first_argmax(vals, vmax):
        return jnp.min(jnp.where(vals == vmax, lane_f, float(ROUTE_LANES)), axis=-1, keepdims=True)

    gl = jnp.where(lane < N_GROUPS, lg, NEG_INF)
    gmax = jnp.max(gl, axis=-1, keepdims=True)
    g_idx = first_argmax(gl, gmax)
    g_w = 1.0 / jnp.sum(jnp.exp(gl - gmax), axis=-1, keepdims=True)
    in_group = jnp.where((e_lane >= 0) & (e_lane < N_EXPERTS), lane_group, -1.0) == g_idx
    el = jnp.where(in_group, lg, NEG_INF)
    v1 = jnp.max(el, axis=-1, keepdims=True)
    i1 = first_argmax(el, v1)
    el2 = jnp.where(lane_f == i1, NEG_INF, el)
    v2 = jnp.max(el2, axis=-1, keepdims=True)
    i2 = first_argmax(el2, v2)
    e21 = jnp.exp(v2 - v1)
    w1 = g_w / (1.0 + e21)
    w2 = g_w * e21 / (1.0 + e21)
    e1 = i1 - float(N_GROUPS)
    e2 = i2 - float(N_GROUPS)
    rt_ref[...] = jnp.where(lane == 0, e1, jnp.where(lane == 1, e2, jnp.where(lane == 2, w1,
                            jnp.where(lane == 3, w2, 0.0))))


def _outproj(hg_p, hg_s, o_p, o_s, z, x_p, x_s, g1, sh2, sc2, norm_g, wr, wa, wo, w_route, b_route):
    tm = ROW_TILE
    t = z.shape[0]
    seg = tm // MOD_ROWS
    n_p = x_p.shape[0] // tm
    row = lambda i: (i, 0)
    fix = lambda i: (0, 0)
    row_p = lambda i: (jnp.minimum(i, n_p - 1), 0)
    row_s = lambda i: (jnp.maximum(i - n_p, 0), 0)
    once = pl.Buffered(1)

    def gate(col):
        return pl.BlockSpec((tm, GATE_TN), lambda i: (i, col // GATE_TN))

    return pl.pallas_call(
        functools.partial(_outproj_kernel, n_prompt_tiles=n_p),
        grid=(t // tm,),
        in_specs=[pl.BlockSpec((tm, D_RNN), row_p), pl.BlockSpec((tm, D_RNN), row_s, pipeline_mode=once),
                  pl.BlockSpec((tm, Q_DIM), row_p), pl.BlockSpec((tm, Q_DIM), row_s, pipeline_mode=once),
                  gate(COL_GA), gate(COL_GA + GATE_TN), gate(COL_GB), gate(COL_GB + GATE_TN),
                  pl.BlockSpec((tm, D_MODEL), row_p), pl.BlockSpec((tm, D_MODEL), row_s, pipeline_mode=once),
                  pl.BlockSpec((seg, D_MODEL), row), pl.BlockSpec((seg, D_MODEL), row),
                  pl.BlockSpec((seg, D_MODEL), row),
                  pl.BlockSpec((1, D_MODEL), fix),
                  pl.BlockSpec((D_RNN, D_MODEL), fix, pipeline_mode=once),
                  pl.BlockSpec((Q_DIM, D_MODEL), fix, pipeline_mode=once),
                  pl.BlockSpec((D_MODEL, D_MODEL), fix, pipeline_mode=once),
                  pl.BlockSpec((D_MODEL, ROUTE_LANES), fix),
                  pl.BlockSpec((1, ROUTE_LANES), fix)],
        out_specs=[pl.BlockSpec((tm, D_MODEL), row), pl.BlockSpec((tm, D_MODEL), row),
                   pl.BlockSpec((tm, ROUTE_LANES), row)],
        out_shape=[jax.ShapeDtypeStruct((t, D_MODEL), F32), jax.ShapeDtypeStruct((t, D_MODEL), F32),
                   jax.ShapeDtypeStruct((t, ROUTE_LANES), F32)],
        compiler_params=_cparams(("arbitrary",), VMEM_LIMIT_BIG),
        name="outproj",
    )(hg_p, hg_s, o_p, o_s, z, z, z, z, x_p, x_s, g1, sh2, sc2, norm_g.reshape(1, D_MODEL),
      wr, wa, wo, w_route, b_route)


def _row_gather(src_hbm, idx_ref, base, dst, sem, n_rows):
    for r in range(n_rows):
        tok = idx_ref[base + r]
        pltpu.make_async_copy(src_hbm.at[pl.ds(tok, 1), :], dst.at[pl.ds(r, 1), :], sem).start()


def _row_gather_wait(src_hbm, dst, sem, n_rows):
    pltpu.make_async_copy(src_hbm.at[pl.ds(0, n_rows), :], dst, sem).wait()


def _moe_kernel(te_ref, tfirst_ref, tvalid_ref, tnext_ref, tbase_ref, src_ref, n2_hbm, wg_hbm, wu_hbm, wd_hbm, y_ref,
                xbuf, xsem, wg_st, wu_st, wd_st, wsem, wg_bf, wu_bf, wd_bf):
    i = pl.program_id(0)
    n_tiles = pl.num_programs(0)
    slot = lax.rem(i, MOE_RING)
    stages = ((wg_hbm, wg_st, wg_bf), (wu_hbm, wu_st, wu_bf), (wd_hbm, wd_st, wd_bf))

    def weight_copy(k, e):
        hbm, st, _ = stages[k]
        return pltpu.make_async_copy(hbm.at[e], st, wsem.at[k])

    def gather_tile(tile):
        tc = jnp.minimum(tile, n_tiles - 1)

        @pl.when((tile < n_tiles) & (tvalid_ref[tc] == 1))
        def _():
            s = lax.rem(tile, MOE_RING)
            _row_gather(n2_hbm, src_ref, tbase_ref[tc], xbuf.at[s], xsem.at[s], MOE_TM)

    @pl.when(i == 0)
    def _():
        for k in range(len(stages)):
            weight_copy(k, te_ref[0]).start(priority=1)
        for ahead in range(MOE_RING - 1):
            gather_tile(ahead)

    @pl.when(tvalid_ref[i] == 1)
    def _():
        gather_tile(i + MOE_RING - 1)

        @pl.when(tfirst_ref[i] == 1)
        def _():
            for k, (_, st, bf) in enumerate(stages):
                weight_copy(k, te_ref[i]).wait()
                bf[...] = st[...].astype(BF16)

            @pl.when(tnext_ref[i] >= 0)
            def _():
                for k in range(len(stages)):
                    weight_copy(k, tnext_ref[i]).start(priority=1)

        _row_gather_wait(n2_hbm, xbuf.at[slot], xsem.at[slot], MOE_TM)
        x = xbuf[slot].astype(BF16)
        hgate = jnp.dot(x, wg_bf[...], preferred_element_type=F32)
        hup = jnp.dot(x, wu_bf[...], preferred_element_type=F32)
        act = (hgate * _sigmoid(hgate) * hup).astype(BF16)
        y_ref[...] = jnp.dot(act, wd_bf[...], preferred_element_type=F32)

    @pl.when(tvalid_ref[i] == 0)
    def _():
        y_ref[...] = jnp.zeros(y_ref.shape, F32)


def _moe(n2, w_gate, w_up, w_down, tile_expert, tile_first, tile_valid, tile_next, tile_base, src_tok):
    n_tiles = tile_expert.shape[0]
    hbm = pl.BlockSpec(memory_space=pl.ANY)
    grid_spec = pltpu.PrefetchScalarGridSpec(
        num_scalar_prefetch=6,
        grid=(n_tiles,),
        in_specs=[hbm, hbm, hbm, hbm],
        out_specs=pl.BlockSpec((MOE_TM, D_MODEL), lambda i, te, tf, tv, tn, tb, st: (i, 0)),
        scratch_shapes=[pltpu.VMEM((MOE_RING, MOE_TM, D_MODEL), F32),
                        pltpu.SemaphoreType.DMA((MOE_RING,)),
                        pltpu.VMEM((D_MODEL, D_EXPERT), F32),
                        pltpu.VMEM((D_MODEL, D_EXPERT), F32),
                        pltpu.VMEM((D_EXPERT, D_MODEL), F32),
                        pltpu.SemaphoreType.DMA((3,)),
                        pltpu.VMEM((D_MODEL, D_EXPERT), BF16),
                        pltpu.VMEM((D_MODEL, D_EXPERT), BF16),
                        pltpu.VMEM((D_EXPERT, D_MODEL), BF16)],
    )
    return pl.pallas_call(
        _moe_kernel,
        grid_spec=grid_spec,
        out_shape=jax.ShapeDtypeStruct((n_tiles * MOE_TM, D_MODEL), F32),
        compiler_params=_cparams(("arbitrary",)),
        name="moe",
    )(tile_expert, tile_first, tile_valid, tile_next, tile_base, src_tok, n2, w_gate, w_up, w_down)


def _route_plan(e1, e2, n_tok):
    experts = jnp.arange(N_EXPERTS, dtype=jnp.int32)
    flat_e = jnp.concatenate([e1, e2])
    onehot = (flat_e[:, None] == experts[None, :]).astype(jnp.int32)
    blk = onehot.astype(F32).reshape(-1, LANES, N_EXPERTS)
    tri = (jnp.arange(LANES)[:, None] >= jnp.arange(LANES)[None, :]).astype(F32)
    within = jnp.einsum("ij,bje->bie", tri, blk)
    before = jnp.cumsum(within[:, -1, :], axis=0) - within[:, -1, :]
    csum = (within + before[:, None, :]).reshape(-1, N_EXPERTS).astype(jnp.int32)
    rank = jnp.sum(csum * onehot, axis=1) - 1
    counts = csum[-1]
    tiles_per = (counts + MOE_TM - 1) // MOE_TM
    tile_end = jnp.cumsum(tiles_per)
    tile_off = tile_end - tiles_per
    slot = jnp.sum(onehot * tile_off[None, :], axis=1) * MOE_TM + rank
    n_tiles = (2 * n_tok) // MOE_TM + N_EXPERTS
    tok = jnp.tile(jnp.arange(n_tok, dtype=jnp.int32), 2)
    _, src_tok = lax.sort((slot, tok), num_keys=1)
    src_tok = jnp.concatenate([src_tok, jnp.zeros((MOE_TM,), jnp.int32)])
    count_off = jnp.cumsum(counts) - counts
    tile_id = jnp.arange(n_tiles, dtype=jnp.int32)
    n_used = tile_end[-1]
    tile_valid = (tile_id < n_used).astype(jnp.int32)
    te = jnp.sum((tile_end[None, :] <= jnp.minimum(tile_id, n_used - 1)[:, None]).astype(jnp.int32), axis=1)
    tile_expert = jnp.minimum(te, N_EXPERTS - 1)
    prev = jnp.concatenate([jnp.full((1,), -1, jnp.int32), tile_expert[:-1]])
    tile_first = (tile_expert != prev).astype(jnp.int32)
    oh_te = (tile_expert[:, None] == experts[None, :]).astype(jnp.int32)

    def of_expert(table):
        return jnp.sum(oh_te * table[None, :], axis=1)

    next_tile = of_expert(tile_end)
    oh_next = (tile_id[None, :] == jnp.minimum(next_tile, n_tiles - 1)[:, None]).astype(jnp.int32)
    tile_next = jnp.where(next_tile < n_used, jnp.sum(oh_next * tile_expert[None, :], axis=1), -1)
    tile_base = jnp.where(tile_valid == 1, of_expert(count_off) + (tile_id - of_expert(tile_off)) * MOE_TM, 0)
    return tile_expert, tile_first, tile_valid, tile_next, tile_base, src_tok, slot[:n_tok], slot[n_tok:]


def _final_kernel(p1_ref, p2_ref, ys_hbm, x1_ref, g2_ref, rt_ref, fg_ref, yp_ref, ysm_ref, ybuf, sem,
                  *, n_prompt_tiles):
    i = pl.program_id(0)
    n_tiles = pl.num_programs(0)
    slot = i % 2

    def start(tile, s):
        _row_gather(ys_hbm, p1_ref, tile * ROW_TILE, ybuf.at[s, 0], sem.at[s], ROW_TILE)
        _row_gather(ys_hbm, p2_ref, tile * ROW_TILE, ybuf.at[s, 1], sem.at[s], ROW_TILE)

    @pl.when(i == 0)
    def _():
        start(0, 0)

    @pl.when(i + 1 < n_tiles)
    def _():
        start(i + 1, 1 - slot)

    _row_gather_wait(ys_hbm, ybuf.at[slot, 0], sem.at[slot], ROW_TILE)
    _row_gather_wait(ys_hbm, ybuf.at[slot, 1], sem.at[slot], ROW_TILE)
    rt = rt_ref[...]
    moe = rt[:, 2:3] * ybuf[slot, 0] + rt[:, 3:4] * ybuf[slot, 1]
    x2 = x1_ref[...] + _bcast_rows(g2_ref[...], ROW_TILE) * moe
    y = x2 * lax.rsqrt(jnp.mean(x2 * x2, axis=-1, keepdims=True) + EPS) * fg_ref[...]

    @pl.when(i < n_prompt_tiles)
    def _():
        yp_ref[...] = y

    @pl.when(i >= n_prompt_tiles)
    def _():
        ysm_ref[...] = y


def _final(ys, x1, g2, rt, final_g, p1, p2, n_prompt_rows):
    t = x1.shape[0]
    tm = ROW_TILE
    seg = tm // MOD_ROWS
    n_p = n_prompt_rows // tm
    grid_spec = pltpu.PrefetchScalarGridSpec(
        num_scalar_prefetch=2,
        grid=(t // tm,),
        in_specs=[pl.BlockSpec(memory_space=pl.ANY),
                  pl.BlockSpec((tm, D_MODEL), lambda i, a, b: (i, 0)),
                  pl.BlockSpec((seg, D_MODEL), lambda i, a, b: (i, 0)),
                  pl.BlockSpec((tm, ROUTE_LANES), lambda i, a, b: (i, 0)),
                  pl.BlockSpec((1, D_MODEL), lambda i, a, b: (0, 0))],
        out_specs=[pl.BlockSpec((tm, D_MODEL), lambda i, a, b: (jnp.minimum(i, n_p - 1), 0)),
                   pl.BlockSpec((tm, D_MODEL), lambda i, a, b: (jnp.maximum(i - n_p, 0), 0))],
        scratch_shapes=[pltpu.VMEM((2, 2, tm, D_MODEL), F32),
                        pltpu.SemaphoreType.DMA((2,))],
    )
    return pl.pallas_call(
        functools.partial(_final_kernel, n_prompt_tiles=n_p),
        grid_spec=grid_spec,
        out_shape=[jax.ShapeDtypeStruct((n_prompt_rows, D_MODEL), F32),
                   jax.ShapeDtypeStruct((t - n_prompt_rows, D_MODEL), F32)],
        compiler_params=_cparams(("arbitrary",)),
        name="final",
    )(p1, p2, ys, x1, g2, rt, final_g.reshape(1, D_MODEL))


def kernel(x_prompt, x_sample, cache_k_win, cache_v_win, state_conv, state_rglru, c_prompt, c_sample, w_ada, b_ada, norm1_g, norm2_g, w_in, b_in, w_conv, b_conv, w_rg_a, b_rg_a, w_rg_x, b_rg_x, lru_lambda, w_rnn_out, w_attn_out, w_out, attn_sinks, w_route_group, b_route_group, w_route_expert, b_route_expert, w_exp_gate, w_exp_up, w_exp_down, rel_bias_table, final_norm_g):
    n_b, seq, _ = x_prompt.shape
    d_b, d_seq, _ = x_sample.shape
    assert w_ada.shape[0] == 1, "single trunk layer"
    assert seq % (ATT_CPB * CHUNK) == 0 and seq % MOD_ROWS == 0 and d_seq == MOD_ROWS
    assert d_seq >= CONV_W - 1 and d_b == N_CHAINS
    t_p, t_s = n_b * seq, d_b * d_seq
    t = t_p + t_s
    assert t_s == ROW_TILE and t_p % ROW_TILE == 0
    cw = cache_k_win.shape[2]
    l = 0

    x_p = x_prompt.reshape(t_p, D_MODEL)
    x_s = x_sample.reshape(t_s, D_MODEL)

    n_c = n_b + d_b
    c_all = jnp.pad(jnp.concatenate([c_prompt, c_sample], axis=0), ((0, -n_c % SUBLANES), (0, 0)))
    def per_segment(m):
        return jnp.concatenate([jnp.repeat(m[:n_b], seq // MOD_ROWS, axis=0),
                                jnp.repeat(m[n_b:n_c], d_seq // MOD_ROWS, axis=0)], axis=0)

    mod_early = _ada(c_all, w_ada[l], b_ada[l], 2 * D_MODEL)
    sh1, sc1 = [per_segment(m) for m in jnp.split(mod_early, 2, axis=-1)]
    n1 = _norm1(x_p, x_s, sc1, sh1, norm1_g[l])
    z, mod_late = _inproj(n1, w_in[l], b_in[l], c_all, w_ada[l], b_ada[l], 2 * D_MODEL)
    g1, sh2, sc2, g2 = [per_segment(m) for m in jnp.split(mod_late, 4, axis=-1)]

    w_gate = jnp.concatenate([w_rg_a[l], w_rg_x[l]], axis=-1)
    b_gate = jnp.concatenate([b_rg_a[l], b_rg_x[l]], axis=-1)[:, None, :]
    hg_p, hl_p = _rglru(z, jnp.zeros((n_b, SUBLANES, D_RNN), F32), jnp.zeros((n_b, N_CHAINS, D_RNN), F32),
                        w_conv[l], b_conv[l], w_gate, b_gate, lru_lambda[l],
                        n_blocks=n_b, n_seq=1, seq_len=seq, row_block0=0, link=True)
    state8 = jnp.pad(state_conv[l], ((0, 0), (SUBLANES - (CONV_W - 1), 0), (0, 0)))
    hg_s, hl_s = _rglru(z, state8, state_rglru[l][None], w_conv[l], b_conv[l], w_gate, b_gate, lru_lambda[l],
                        n_blocks=1, n_seq=d_b, seq_len=d_seq, row_block0=t_p // t_s, link=False)

    sinks = attn_sinks[l]
    bkt_p = _bucket_map(WIN_CHUNKS * CHUNK + jnp.arange(CHUNK), jnp.arange(SPAN))
    o_p = _attn_band(z, rel_bias_table, sinks, bkt_p, n_batch=n_b, seq_len=seq)
    k_new = z[t_p:, COL_K:COL_K + KV_DIM].reshape(d_b, d_seq, KV_DIM)
    v_new = z[t_p:, COL_V:COL_V + KV_DIM].reshape(d_b, d_seq, KV_DIM)
    k_all = jnp.concatenate([cache_k_win[l].reshape(d_b, cw, KV_DIM), k_new], axis=1)
    v_all = jnp.concatenate([cache_v_win[l].reshape(d_b, cw, KV_DIM), v_new], axis=1)
    bkt_s = _bucket_map(cw + jnp.arange(d_seq), jnp.arange(cw + d_seq))
    o_s = _attn_step(z, k_all, v_all, rel_bias_table, sinks, bkt_s, n_batch=d_b, n_q=d_seq, row_block0=t_p // d_seq)

    n_route = N_GROUPS + N_EXPERTS
    w_route = jnp.pad(jnp.concatenate([w_route_group[l], w_route_expert[l]], axis=1),
                      ((0, 0), (0, ROUTE_LANES - n_route)))
    b_route = jnp.pad(jnp.concatenate([b_route_group[l], b_route_expert[l]]),
                      (0, ROUTE_LANES - n_route)).reshape(1, ROUTE_LANES)
    x1, n2, rt = _outproj(hg_p, hg_s, o_p, o_s, z, x_p, x_s, g1, sh2, sc2, norm2_g[l],
                          w_rnn_out[l].astype(BF16), w_attn_out[l].astype(BF16), w_out[l].astype(BF16),
                          w_route, b_route)

    e1 = rt[:, 0].astype(jnp.int32)
    e2 = rt[:, 1].astype(jnp.int32)
    tile_expert, tile_first, tile_valid, tile_next, tile_base, src_tok, p1, p2 = _route_plan(e1, e2, t)
    ys = _moe(n2, w_exp_gate[l], w_exp_up[l], w_exp_down[l], tile_expert, tile_first, tile_valid, tile_next,
              tile_base, src_tok)
    y_p, y_s = _final(ys, x1, g2, rt, final_norm_g, p1, p2, t_p)

    win = min(WINDOW, seq)

    def tail(col0, width, n_rows):
        return jnp.stack([z[(b + 1) * seq - n_rows:(b + 1) * seq, col0:col0 + width] for b in range(n_b)])

    kp = tail(COL_K, KV_DIM, win).reshape(n_b, win, N_KV_HEADS, HEAD_DIM)
    vp = tail(COL_V, KV_DIM, win).reshape(n_b, win, N_KV_HEADS, HEAD_DIM)
    cp = tail(COL_XR, D_RNN, CONV_W - 1)
    rp = hl_p[:, N_CHAINS - 1, :]
    ks = k_all[:, -cw:].reshape(d_b, cw, N_KV_HEADS, HEAD_DIM)
    vs = v_all[:, -cw:].reshape(d_b, cw, N_KV_HEADS, HEAD_DIM)
    cs = z[t_p:, COL_XR:COL_XR + D_RNN].reshape(d_b, d_seq, D_RNN)[:, -(CONV_W - 1):]
    rs = hl_s[0]
    return (y_p.reshape(n_b, seq, D_MODEL), y_s.reshape(d_b, d_seq, D_MODEL),
            kp[None], vp[None], cp[None], rp[None], ks[None], vs[None], cs[None], rs[None])
```

```python
import functools
import math

import jax
import jax.numpy as jnp
from jax import lax
from jax.experimental import pallas as pl
from jax.experimental.pallas import tpu as pltpu

F32 = jnp.float32
BF16 = jnp.bfloat16

D_MODEL = 2048
D_RNN = 2048
RNN_BLOCK = 128
CONV_W = 4
LRU_C = 8.0
N_HEADS = 16
N_KV_HEADS = 4
HEAD_DIM = 128
GQA_GROUP = N_HEADS // N_KV_HEADS
Q_DIM = N_HEADS * HEAD_DIM
KV_DIM = N_KV_HEADS * HEAD_DIM
CHUNK = 64
WINDOW = 128
WIN_CHUNKS = WINDOW // CHUNK
SPAN = (WIN_CHUNKS + 1) * CHUNK
N_BUCKETS = 32
MAX_DISTANCE = 128
N_GROUPS = 4
E_PER_GROUP = 8
N_EXPERTS = N_GROUPS * E_PER_GROUP
D_EXPERT = 512
EPS = 1e-6
NEG_INF = -1e30
D_IN = 2 * D_RNN + Q_DIM + 2 * KV_DIM + 2 * D_MODEL
COL_XR, COL_GR, COL_Q = 0, D_RNN, 2 * D_RNN
COL_K = COL_Q + Q_DIM
COL_V = COL_K + KV_DIM
COL_GA = COL_V + KV_DIM
COL_GB = COL_GA + D_MODEL

LANES = 128
SUBLANES = 8
MOD_ROWS = 32
KEY_TILE = 256
VMEM_LIMIT = 56 * 1024 * 1024
VMEM_LIMIT_BIG = 60 * 1024 * 1024
ADA_TN = 1024

ROW_TILE = 256
IN_TM_CHOICES = (1408, 768, 256)
IN_TN = 1024
MOE_TM = 192
MOE_RING = 3


def _sigmoid(x):
    return 0.5 * jnp.tanh(0.5 * x) + 0.5


def _gelu_tanh(x):
    return 0.5 * x * (1.0 + jnp.tanh(math.sqrt(2.0 / math.pi) * (x + 0.044715 * (x * x * x))))


def _bcast_rows(v, rows):
    n, d = v.shape
    return jnp.broadcast_to(v[:, None, :], (n, MOD_ROWS, d)).reshape(rows, d)


def _rms_modulate(x, gain, scale_seg, shift_seg):
    rows = x.shape[0]
    y = x * lax.rsqrt(jnp.mean(x * x, axis=-1, keepdims=True) + EPS) * gain
    return y * (1.0 + _bcast_rows(scale_seg, rows)) + _bcast_rows(shift_seg, rows)


def _cparams(sem, vmem_limit=VMEM_LIMIT):
    return pltpu.CompilerParams(dimension_semantics=sem, vmem_limit_bytes=vmem_limit)


def _ada_kernel(c_ref, w_ref, b_ref, o_ref):
    c = c_ref[...]
    s = (c * _sigmoid(c)).astype(BF16)
    o_ref[...] = jnp.dot(s, w_ref[...].astype(BF16), preferred_element_type=F32) + b_ref[...]


def _ada(c_all, w_ada, b_ada):
    rows = c_all.shape[0]
    n = w_ada.shape[1]
    tn = ADA_TN
    return pl.pallas_call(
        _ada_kernel,
        grid=(n // tn,),
        in_specs=[pl.BlockSpec((rows, D_MODEL), lambda j: (0, 0)),
                  pl.BlockSpec((D_MODEL, tn), lambda j: (0, j)),
                  pl.BlockSpec((1, tn), lambda j: (0, j))],
        out_specs=pl.BlockSpec((rows, tn), lambda j: (0, j)),
        out_shape=jax.ShapeDtypeStruct((rows, n), F32),
        compiler_params=_cparams(("arbitrary",)),
        name="ada",
    )(c_all, w_ada, b_ada.reshape(1, n))


NORM_NSUB = 3


def _norm1_kernel(x0_ref, x1_ref, x2_ref, xs_ref, sc_ref, sh_ref, g_ref, n1_ref, *, n_prompt_tiles):
    tail_is_sample = pl.program_id(0) * NORM_NSUB + (NORM_NSUB - 1) >= n_prompt_tiles
    gain = g_ref[...]
    segs = ROW_TILE // MOD_ROWS
    for r, x_ref in enumerate((x0_ref, x1_ref, x2_ref)):
        for s in range(segs):
            rows = slice(s * MOD_ROWS, (s + 1) * MOD_ROWS)
            x = x_ref[rows, :]
            if r == NORM_NSUB - 1:
                x = jnp.where(tail_is_sample, xs_ref[rows, :], x)
            y = x * lax.rsqrt(jnp.mean(x * x, axis=-1, keepdims=True) + EPS) * gain
            k = r * segs + s
            out_rows = slice(k * MOD_ROWS, (k + 1) * MOD_ROWS)
            n1_ref[out_rows, :] = (y * (1.0 + sc_ref[k:k + 1, :]) + sh_ref[k:k + 1, :]).astype(BF16)


def _norm1(x_p, x_s, sc1, sh1, norm_g):
    n_p = x_p.shape[0] // ROW_TILE
    t = x_p.shape[0] + x_s.shape[0]
    tm = NORM_NSUB * ROW_TILE
    assert x_s.shape[0] == ROW_TILE and t % tm == 0
    seg = tm // MOD_ROWS

    def sub(r):
        return pl.BlockSpec((ROW_TILE, D_MODEL), lambda i: (jnp.minimum(i * NORM_NSUB + r, n_p - 1), 0))

    return pl.pallas_call(
        functools.partial(_norm1_kernel, n_prompt_tiles=n_p),
        grid=(t // tm,),
        in_specs=[sub(0), sub(1), sub(2),
                  pl.BlockSpec((ROW_TILE, D_MODEL), lambda i: (0, 0)),
                  pl.BlockSpec((seg, D_MODEL), lambda i: (i, 0)),
                  pl.BlockSpec((seg, D_MODEL), lambda i: (i, 0)),
                  pl.BlockSpec((1, D_MODEL), lambda i: (0, 0))],
        out_specs=pl.BlockSpec((tm, D_MODEL), lambda i: (i, 0)),
        out_shape=jax.ShapeDtypeStruct((t, D_MODEL), BF16),
        compiler_params=_cparams(("arbitrary",)),
        name="norm1",
    )(x_p, x_p, x_p, x_s, sc1, sh1, norm_g.reshape(1, D_MODEL))


def _inproj_kernel(n1_ref, w_ref, b_ref, z_ref, w_bf):
    @pl.when(pl.program_id(1) == 0)
    def _():
        w_bf[...] = w_ref[...].astype(BF16)

    z_ref[...] = jnp.dot(n1_ref[...], w_bf[...], preferred_element_type=F32) + b_ref[...]


def _inproj(n1, w_in, b_in):
    t = n1.shape[0]
    tm = next(m for m in IN_TM_CHOICES if t % m == 0)
    return pl.pallas_call(
        _inproj_kernel,
        grid=(D_IN // IN_TN, t // tm),
        in_specs=[pl.BlockSpec((tm, D_MODEL), lambda j, i: (i, 0)),
                  pl.BlockSpec((D_MODEL, IN_TN), lambda j, i: (0, j)),
                  pl.BlockSpec((1, IN_TN), lambda j, i: (0, j))],
        out_specs=pl.BlockSpec((tm, IN_TN), lambda j, i: (i, j)),
        out_shape=jax.ShapeDtypeStruct((t, D_IN), F32),
        scratch_shapes=[pltpu.VMEM((D_MODEL, IN_TN), BF16)],
        compiler_params=_cparams(("arbitrary", "arbitrary")),
        name="inproj",
    )(n1, w_in, b_in.reshape(1, D_IN))


RG_CW = 256
RG_SLABS = RG_CW // LANES
N_CHAINS = SUBLANES
CHAIN_PAD = 8


def _rglru_kernel(xr_ref, gr_ref, st_ref, h0_ref, wc_ref, bc_ref, wg_ref, bg_ref, lam_ref,
                  hg_ref, hl_ref, ext_scr, a_scr, u_scr, *, n_seq, seq_len, link):
    rows = n_seq * seq_len
    cl = rows // N_CHAINS
    pitch = cl + CHAIN_PAD

    for s in range(n_seq):
        ext_scr[s, 0:SUBLANES, :] = st_ref[s]
        ext_scr[s, SUBLANES:SUBLANES + seq_len, :] = xr_ref[s * seq_len:(s + 1) * seq_len, :]

    z = -lam_ref[...]
    softplus = jnp.maximum(z, 0.0) + jnp.log1p(jnp.exp(-jnp.abs(z)))
    nq = (-0.25 * LRU_C) * softplus

    for c in range(N_CHAINS):
        s, r0 = divmod(c * cl, seq_len)
        xc = bc_ref[...]
        for j in range(CONV_W):
            xc = xc + wc_ref[j:j + 1, :] * ext_scr[s, pl.ds(SUBLANES - (CONV_W - 1) + j + r0, cl), :]
        for sl in range(RG_SLABS):
            lanes = slice(sl * LANES, (sl + 1) * LANES)
            xb = xc[:, lanes]
            g = jnp.dot(xb.astype(BF16), wg_ref[sl].astype(BF16), preferred_element_type=F32) + bg_ref[sl]
            tr = jnp.tanh(0.5 * g[:, :LANES])
            ti = jnp.tanh(0.5 * g[:, LANES:])
            th = jnp.tanh(nq[:, lanes] * (tr + 1.0))
            rcp = 1.0 / (1.0 - th)
            a_scr[sl, c * pitch:c * pitch + cl, :] = (1.0 + th) * rcp
            u_scr[sl, c * pitch:c * pitch + cl, :] = (jnp.sqrt(-th) * rcp) * ((ti + 1.0) * xb)

    def step(t, carry):
        hs, ps = carry
        new_h, new_p = [], []
        for sl in range(RG_SLABS):
            a = a_scr[sl, pl.ds(t, N_CHAINS, stride=pitch), :]
            u = u_scr[sl, pl.ds(t, N_CHAINS, stride=pitch), :]
            h = a * hs[sl] + u
            u_scr[sl, pl.ds(t, N_CHAINS, stride=pitch), :] = h
            new_h.append(h)
            if link:
                p = a * ps[sl]
                a_scr[sl, pl.ds(t, N_CHAINS, stride=pitch), :] = p
                new_p.append(p)
            else:
                new_p.append(ps[sl])
        return tuple(new_h), tuple(new_p)

    h_init = tuple(h0_ref[0, :, sl * LANES:(sl + 1) * LANES] for sl in range(RG_SLABS))
    p_init = tuple(jnp.ones((N_CHAINS, LANES), F32) for _ in range(RG_SLABS))
    h_end, p_end = lax.fori_loop(0, cl, step, (h_init, p_init), unroll=8)

    row = lax.broadcasted_iota(jnp.int32, (N_CHAINS, LANES), 0)
    for sl in range(RG_SLABS):
        lanes = slice(sl * LANES, (sl + 1) * LANES)
        if link:
            def shift_down(v):
                return jnp.where(row == 0, 0.0, pltpu.roll(v, 1, axis=0))
            hh = h_end[sl]
            for _ in range(N_CHAINS - 1):
                hh = h_end[sl] + p_end[sl] * shift_down(hh)
            carry_in = shift_down(hh)
        else:
            hh = h_end[sl]
        hl_ref[0, :, lanes] = hh
        for c in range(N_CHAINS):
            h = u_scr[sl, c * pitch:c * pitch + cl, :]
            if link:
                h = h + a_scr[sl, c * pitch:c * pitch + cl, :] * carry_in[c:c + 1, :]
            gg = _gelu_tanh(gr_ref[c * cl:(c + 1) * cl, lanes])
            hg_ref[c * cl:(c + 1) * cl, lanes] = (h * gg).astype(BF16)


def _rglru(z, state8, h0, w_conv, b_conv, w_gate, b_gate, lam, *, n_blocks, n_seq, seq_len, row_block0, link):
    rows = n_seq * seq_len
    cl = rows // N_CHAINS
    ncb = D_RNN // RG_CW
    gr0 = COL_GR // RG_CW
    kern = functools.partial(_rglru_kernel, n_seq=n_seq, seq_len=seq_len, link=link)
    return pl.pallas_call(
        kern,
        grid=(n_blocks, ncb),
        in_specs=[pl.BlockSpec((rows, RG_CW), lambda b, n: (row_block0 + b, n)),
                  pl.BlockSpec((rows, RG_CW), lambda b, n: (row_block0 + b, gr0 + n)),
                  pl.BlockSpec((n_seq, SUBLANES, RG_CW), lambda b, n: (b, 0, n)),
                  pl.BlockSpec((1, N_CHAINS, RG_CW), lambda b, n: (b, 0, n)),
                  pl.BlockSpec((CONV_W, RG_CW), lambda b, n: (0, n)),
                  pl.BlockSpec((1, RG_CW), lambda b, n: (0, n)),
                  pl.BlockSpec((RG_SLABS, RNN_BLOCK, 2 * RNN_BLOCK), lambda b, n: (n, 0, 0)),
                  pl.BlockSpec((RG_SLABS, 1, 2 * RNN_BLOCK), lambda b, n: (n, 0, 0)),
                  pl.BlockSpec((1, RG_CW), lambda b, n: (0, n))],
        out_specs=[pl.BlockSpec((rows, RG_CW), lambda b, n: (b, n)),
                   pl.BlockSpec((1, N_CHAINS, RG_CW), lambda b, n: (b, 0, n))],
        out_shape=[jax.ShapeDtypeStruct((n_blocks * rows, D_RNN), BF16),
                   jax.ShapeDtypeStruct((n_blocks, N_CHAINS, D_RNN), F32)],
        scratch_shapes=[pltpu.VMEM((n_seq, SUBLANES + seq_len, RG_CW), F32),
                        pltpu.VMEM((RG_SLABS, N_CHAINS * (cl + CHAIN_PAD), LANES), F32),
                        pltpu.VMEM((RG_SLABS, N_CHAINS * (cl + CHAIN_PAD), LANES), F32)],
        compiler_params=_cparams(("arbitrary", "arbitrary")),
        name="rglru_link" if link else "rglru_step",
    )(z, z, state8, h0, w_conv, b_conv.reshape(1, D_RNN), w_gate, b_gate, lam.reshape(1, D_RNN))


LOG2E = math.log2(math.e)


def _build_bias(tbl_ref, bkt_ref, bias_scr, n_q):
    bkt = bkt_ref[...]
    base = jnp.where(bkt < 0, NEG_INF, 0.0)
    for h in range(N_HEADS):
        acc = base
        for bk in range(N_BUCKETS):
            acc = jnp.where(bkt == bk, tbl_ref[bk, h] * LOG2E, acc)
        hk, g = divmod(h, GQA_GROUP)
        bias_scr[hk, g * n_q:(g + 1) * n_q, :] = acc


def _sink_columns(sink_ref, n_q):
    row = lax.broadcasted_iota(jnp.int32, (GQA_GROUP * n_q, 1), 0)
    cols = []
    for hk in range(N_KV_HEADS):
        col = jnp.full((GQA_GROUP * n_q, 1), sink_ref[hk * GQA_GROUP] * LOG2E, F32)
        for g in range(1, GQA_GROUP):
            col = jnp.where(row >= g * n_q, sink_ref[hk * GQA_GROUP + g] * LOG2E, col)
        cols.append(col)
    return cols


def _attend(q_of, k_of, v_of, bias_scr, sinks, key_ok, store, n_q):
    scores = []
    for hk in range(N_KV_HEADS):
        qg = jnp.concatenate([q_of(hk * GQA_GROUP + g) for g in range(GQA_GROUP)], axis=0)
        s = lax.dot_general(qg, k_of(hk), (((1,), (1,)), ((), ())), preferred_element_type=F32)
        s = s * (HEAD_DIM ** -0.5 * LOG2E) + bias_scr[hk]
        if key_ok is not None:
            s = jnp.where(key_ok, s, NEG_INF)
        scores.append(s)
    probs = []
    for hk in range(N_KV_HEADS):
        s, sink = scores[hk], sinks[hk]
        m = jnp.maximum(jnp.max(s, axis=-1, keepdims=True), sink)
        p = jnp.exp2(s - m)
        inv = 1.0 / (jnp.sum(p, axis=-1, keepdims=True) + jnp.exp2(sink - m))
        probs.append((p * inv).astype(BF16))
    for hk in range(N_KV_HEADS):
        o = jnp.dot(probs[hk], v_of(hk), preferred_element_type=F32)
        for g in range(GQA_GROUP):
            store(hk * GQA_GROUP + g, o[g * n_q:(g + 1) * n_q, :].astype(BF16))


ATT_CPB = 8
ATT_LEAD = WIN_CHUNKS * CHUNK
ATT_TAIL = KEY_TILE - SPAN


def _attn_band_kernel(tbl_ref, sink_ref, q_ref, k_ref, v_ref, bkt_ref, o_ref, kpad, vpad, bias_scr, *, seq_len):
    b = pl.program_id(0)
    cg = pl.program_id(1)

    @pl.when((b == 0) & (cg == 0))
    def _():
        _build_bias(tbl_ref, bkt_ref, bias_scr, CHUNK)

    @pl.when(cg == 0)
    def _():
        for ref, pad in ((k_ref, kpad), (v_ref, vpad)):
            pad[0:ATT_LEAD, :] = jnp.zeros((ATT_LEAD, KV_DIM), BF16)
            pad[ATT_LEAD:ATT_LEAD + seq_len, :] = ref[...].astype(BF16)
            pad[ATT_LEAD + seq_len:, :] = jnp.zeros((ATT_TAIL, KV_DIM), BF16)

    kidx = lax.broadcasted_iota(jnp.int32, (1, KEY_TILE), 1)
    sinks = _sink_columns(sink_ref, CHUNK)

    def chunk(c, carry):
        q0 = pl.multiple_of(c * CHUNK, CHUNK)
        start = pl.multiple_of((cg * ATT_CPB + c) * CHUNK, CHUNK)
        key_ok = start + kidx >= ATT_LEAD

        def head_cols(h):
            return slice(h * HEAD_DIM, (h + 1) * HEAD_DIM)

        def store(h, val):
            o_ref[pl.ds(q0, CHUNK), head_cols(h)] = val

        def run(mask):
            _attend(lambda h: q_ref[pl.ds(q0, CHUNK), head_cols(h)].astype(BF16),
                    lambda hk: kpad[pl.ds(start, KEY_TILE), head_cols(hk)],
                    lambda hk: vpad[pl.ds(start, KEY_TILE), head_cols(hk)],
                    bias_scr, sinks, mask, store, CHUNK)

        pl.when(start < ATT_LEAD)(lambda: run(key_ok))
        pl.when(start >= ATT_LEAD)(lambda: run(None))
        return carry

    lax.fori_loop(0, ATT_CPB, chunk, 0)


def _attn_band(z, table, sinks, bkt, *, n_batch, seq_len):
    rows = ATT_CPB * CHUNK
    ng = seq_len // rows
    kern = functools.partial(_attn_band_kernel, seq_len=seq_len)
    smem = pl.BlockSpec(memory_space=pltpu.SMEM)
    pad_rows = ATT_LEAD + seq_len + ATT_TAIL
    return pl.pallas_call(
        kern,
        grid=(n_batch, ng),
        in_specs=[smem, smem,
                  pl.BlockSpec((rows, Q_DIM), lambda b, c: (b * ng + c, COL_Q // Q_DIM)),
                  pl.BlockSpec((seq_len, KV_DIM), lambda b, c: (b, COL_K // KV_DIM)),
                  pl.BlockSpec((seq_len, KV_DIM), lambda b, c: (b, COL_V // KV_DIM)),
                  pl.BlockSpec((CHUNK, KEY_TILE), lambda b, c: (0, 0))],
        out_specs=pl.BlockSpec((rows, Q_DIM), lambda b, c: (b * ng + c, 0)),
        out_shape=jax.ShapeDtypeStruct((n_batch * seq_len, Q_DIM), BF16),
        scratch_shapes=[pltpu.VMEM((pad_rows, KV_DIM), BF16),
                        pltpu.VMEM((pad_rows, KV_DIM), BF16),
                        pltpu.VMEM((N_KV_HEADS, GQA_GROUP * CHUNK, KEY_TILE), F32)],
        compiler_params=_cparams(("arbitrary", "arbitrary")),
        name="attn_band",
    )(table, sinks, z, z, z, bkt)


def _attn_step_kernel(tbl_ref, sink_ref, q_ref, k_ref, v_ref, bkt_ref, o_ref, kbuf, vbuf, bias_scr,
                      *, n_keys, n_q):
    b = pl.program_id(0)

    @pl.when(b == 0)
    def _():
        _build_bias(tbl_ref, bkt_ref, bias_scr, n_q)
        kbuf[n_keys:, :] = jnp.zeros((KEY_TILE - n_keys, KV_DIM), BF16)
        vbuf[n_keys:, :] = jnp.zeros((KEY_TILE - n_keys, KV_DIM), BF16)

    kbuf[0:n_keys, :] = k_ref[0].astype(BF16)
    vbuf[0:n_keys, :] = v_ref[0].astype(BF16)

    def head_cols(h):
        return slice(h * HEAD_DIM, (h + 1) * HEAD_DIM)

    def store(h, val):
        o_ref[:, head_cols(h)] = val

    _attend(lambda h: q_ref[:, head_cols(h)].astype(BF16),
            lambda hk: kbuf[:, head_cols(hk)], lambda hk: vbuf[:, head_cols(hk)],
            bias_scr, _sink_columns(sink_ref, n_q), None, store, n_q)


def _attn_step(z, k_all, v_all, table, sinks, bkt, *, n_batch, n_q, row_block0):
    n_keys = k_all.shape[1]
    kern = functools.partial(_attn_step_kernel, n_keys=n_keys, n_q=n_q)
    smem = pl.BlockSpec(memory_space=pltpu.SMEM)
    return pl.pallas_call(
        kern,
        grid=(n_batch,),
        in_specs=[smem, smem,
                  pl.BlockSpec((n_q, Q_DIM), lambda b: (row_block0 + b, COL_Q // Q_DIM)),
                  pl.BlockSpec((1, n_keys, KV_DIM), lambda b: (b, 0, 0)),
                  pl.BlockSpec((1, n_keys, KV_DIM), lambda b: (b, 0, 0)),
                  pl.BlockSpec((n_q, KEY_TILE), lambda b: (0, 0))],
        out_specs=pl.BlockSpec((n_q, Q_DIM), lambda b: (b, 0)),
        out_shape=jax.ShapeDtypeStruct((n_batch * n_q, Q_DIM), BF16),
        scratch_shapes=[pltpu.VMEM((KEY_TILE, KV_DIM), BF16),
                        pltpu.VMEM((KEY_TILE, KV_DIM), BF16),
                        pltpu.VMEM((N_KV_HEADS, GQA_GROUP * n_q, KEY_TILE), F32)],
        compiler_params=_cparams(("arbitrary",)),
        name="attn_step",
    )(table, sinks, z, k_all, v_all, bkt)


def _t5_bucket(rel):
    nb = N_BUCKETS // 2
    ret = jnp.where(rel > 0, nb, 0)
    n = jnp.abs(rel)
    max_exact = nb // 2
    nf = jnp.maximum(n, 1).astype(jnp.float32)
    large = max_exact + (jnp.log(nf / max_exact) / math.log(MAX_DISTANCE / max_exact)
                         * (nb - max_exact)).astype(jnp.int32)
    large = jnp.minimum(large, nb - 1)
    return ret + jnp.where(n < max_exact, n, large)


def _bucket_map(q_pos, k_pos):
    bkt = _t5_bucket(k_pos[None, :] - q_pos[:, None]).astype(jnp.int32)
    return jnp.pad(bkt, ((0, 0), (0, KEY_TILE - k_pos.shape[0])), constant_values=-1)


ROUTE_LANES = LANES
GATE_TN = 1024


def _outproj_kernel(hgp_ref, hgs_ref, op_ref, os_ref, ga0_ref, ga1_ref, gb0_ref, gb1_ref, xp_ref, xs_ref,
                    g1_ref, sh2_ref, sc2_ref, ng_ref, wr_ref, wa_ref, wo_ref, wrt_ref, brt_ref,
                    x1_ref, n2_ref, rt_ref, *, n_prompt_tiles):
    tm = x1_ref.shape[0]
    is_prompt = pl.program_id(0) < n_prompt_tiles
    hg = jnp.where(is_prompt, hgp_ref[...], hgs_ref[...])
    o = jnp.where(is_prompt, op_ref[...], os_ref[...])
    x = jnp.where(is_prompt, xp_ref[...], xs_ref[...])
    ya = jnp.dot(hg, wr_ref[...], preferred_element_type=F32)
    yb = jnp.dot(o, wa_ref[...], preferred_element_type=F32)
    halves = []
    for k, (ga_ref, gb_ref) in enumerate(((ga0_ref, gb0_ref), (ga1_ref, gb1_ref))):
        cols = slice(k * GATE_TN, (k + 1) * GATE_TN)
        halves.append((_sigmoid(ga_ref[...]) * ya[:, cols] + _sigmoid(gb_ref[...]) * yb[:, cols]).astype(BF16))
    merged = jnp.concatenate(halves, axis=1)
    mix = jnp.dot(merged, wo_ref[...], preferred_element_type=F32)
    x1 = x + _bcast_rows(g1_ref[...], tm) * mix
    x1_ref[...] = x1
    n2 = _rms_modulate(x1, ng_ref[...], sc2_ref[...], sh2_ref[...])
    n2_ref[...] = n2

    lg = jnp.dot(n2.astype(BF16), wrt_ref[...].astype(BF16), preferred_element_type=F32) + brt_ref[...]

    lane = lax.broadcasted_iota(jnp.int32, (tm, ROUTE_LANES), 1)
    lane_f = lane.astype(F32)
    e_lane = lane - N_GROUPS
    lane_group = (e_lane >> 3).astype(F32)

    def first_argmax(vals, vmax):
        return jnp.min(jnp.where(vals == vmax, lane_f, float(ROUTE_LANES)), axis=-1, keepdims=True)

    gl = jnp.where(lane < N_GROUPS, lg, NEG_INF)
    gmax = jnp.max(gl, axis=-1, keepdims=True)
    g_idx = first_argmax(gl, gmax)
    g_w = 1.0 / jnp.sum(jnp.exp(gl - gmax), axis=-1, keepdims=True)
    in_group = jnp.where((e_lane >= 0) & (e_lane < N_EXPERTS), lane_group, -1.0) == g_idx
    el = jnp.where(in_group, lg, NEG_INF)
    v1 = jnp.max(el, axis=-1, keepdims=True)
    i1 = first_argmax(el, v1)
    el2 = jnp.where(lane_f == i1, NEG_INF, el)
    v2 = jnp.max(el2, axis=-1, keepdims=True)
    i2 = first_argmax(el2, v2)
    e21 = jnp.exp(v2 - v1)
    w1 = g_w / (1.0 + e21)
    w2 = g_w * e21 / (1.0 + e21)
    e1 = i1 - float(N_GROUPS)
    e2 = i2 - float(N_GROUPS)
    rt_ref[...] = jnp.where(lane == 0, e1, jnp.where(lane == 1, e2, jnp.where(lane == 2, w1,
                            jnp.where(lane == 3, w2, 0.0))))


def _outproj(hg_p, hg_s, o_p, o_s, z, x_p, x_s, g1, sh2, sc2, norm_g, wr, wa, wo, w_route, b_route):
    tm = ROW_TILE
    t = z.shape[0]
    seg = tm // MOD_ROWS
    n_p = x_p.shape[0] // tm
    row = lambda i: (i, 0)
    fix = lambda i: (0, 0)
    row_p = lambda i: (jnp.minimum(i, n_p - 1), 0)
    row_s = lambda i: (jnp.maximum(i - n_p, 0), 0)
    once = pl.Buffered(1)

    def gate(col):
        return pl.BlockSpec((tm, GATE_TN), lambda i: (i, col // GATE_TN))

    return pl.pallas_call(
        functools.partial(_outproj_kernel, n_prompt_tiles=n_p),
        grid=(t // tm,),
        in_specs=[pl.BlockSpec((tm, D_RNN), row_p), pl.BlockSpec((tm, D_RNN), row_s, pipeline_mode=once),
                  pl.BlockSpec((tm, Q_DIM), row_p), pl.BlockSpec((tm, Q_DIM), row_s, pipeline_mode=once),
                  gate(COL_GA), gate(COL_GA + GATE_TN), gate(COL_GB), gate(COL_GB + GATE_TN),
                  pl.BlockSpec((tm, D_MODEL), row_p), pl.BlockSpec((tm, D_MODEL), row_s, pipeline_mode=once),
                  pl.BlockSpec((seg, D_MODEL), row), pl.BlockSpec((seg, D_MODEL), row),
                  pl.BlockSpec((seg, D_MODEL), row),
                  pl.BlockSpec((1, D_MODEL), fix),
                  pl.BlockSpec((D_RNN, D_MODEL), fix, pipeline_mode=once),
                  pl.BlockSpec((Q_DIM, D_MODEL), fix, pipeline_mode=once),
                  pl.BlockSpec((D_MODEL, D_MODEL), fix, pipeline_mode=once),
                  pl.BlockSpec((D_MODEL, ROUTE_LANES), fix),
                  pl.BlockSpec((1, ROUTE_LANES), fix)],
        out_specs=[pl.BlockSpec((tm, D_MODEL), row), pl.BlockSpec((tm, D_MODEL), row),
                   pl.BlockSpec((tm, ROUTE_LANES), row)],
        out_shape=[jax.ShapeDtypeStruct((t, D_MODEL), F32), jax.ShapeDtypeStruct((t, D_MODEL), F32),
                   jax.ShapeDtypeStruct((t, ROUTE_LANES), F32)],
        compiler_params=_cparams(("arbitrary",), VMEM_LIMIT_BIG),
        name="outproj",
    )(hg_p, hg_s, o_p, o_s, z, z, z, z, x_p, x_s, g1, sh2, sc2, norm_g.reshape(1, D_MODEL),
      wr, wa, wo, w_route, b_route)


def _row_gather(src_hbm, idx_ref, base, dst, sem, n_rows):
    for r in range(n_rows):
        tok = idx_ref[base + r]
        pltpu.make_async_copy(src_hbm.at[pl.ds(tok, 1), :], dst.at[pl.ds(r, 1), :], sem).start()


def _row_gather_wait(src_hbm, dst, sem, n_rows):
    pltpu.make_async_copy(src_hbm.at[pl.ds(0, n_rows), :], dst, sem).wait()


def _moe_kernel(te_ref, tfirst_ref, tvalid_ref, tnext_ref, tbase_ref, src_ref, n2_hbm, wg_hbm, wu_hbm, wd_hbm, y_ref,
                xbuf, xsem, wg_st, wu_st, wd_st, wsem, wg_bf, wu_bf, wd_bf):
    i = pl.program_id(0)
    n_tiles = pl.num_programs(0)
    slot = lax.rem(i, MOE_RING)
    stages = ((wg_hbm, wg_st, wg_bf), (wu_hbm, wu_st, wu_bf), (wd_hbm, wd_st, wd_bf))

    def weight_copy(k, e):
        hbm, st, _ = stages[k]
        return pltpu.make_async_copy(hbm.at[e], st, wsem.at[k])

    def gather_tile(tile):
        tc = jnp.minimum(tile, n_tiles - 1)

        @pl.when((tile < n_tiles) & (tvalid_ref[tc] == 1))
        def _():
            s = lax.rem(tile, MOE_RING)
            _row_gather(n2_hbm, src_ref, tbase_ref[tc], xbuf.at[s], xsem.at[s], MOE_TM)

    @pl.when(i == 0)
    def _():
        for k in range(len(stages)):
            weight_copy(k, te_ref[0]).start(priority=1)
        for ahead in range(MOE_RING - 1):
            gather_tile(ahead)

    @pl.when(tvalid_ref[i] == 1)
    def _():
        gather_tile(i + MOE_RING - 1)

        _row_gather_wait(n2_hbm, xbuf.at[slot], xsem.at[slot], MOE_TM)

        def ffn(weight):
            x = xbuf[slot].astype(BF16)
            hgate = jnp.dot(x, weight(0), preferred_element_type=F32)
            hup = jnp.dot(x, weight(1), preferred_element_type=F32)
            act = (hgate * _sigmoid(hgate) * hup).astype(BF16)
            y_ref[...] = jnp.dot(act, weight(2), preferred_element_type=F32)

        @pl.when(tfirst_ref[i] == 1)
        def _():
            for k in range(len(stages)):
                weight_copy(k, te_ref[i]).wait()

            def cast_and_keep(k):
                _, st, bf = stages[k]
                w = st[...].astype(BF16)
                bf[...] = w
                return w

            ffn(cast_and_keep)

            @pl.when(tnext_ref[i] >= 0)
            def _():
                for k in range(len(stages)):
                    weight_copy(k, tnext_ref[i]).start(priority=1)

        @pl.when(tfirst_ref[i] == 0)
        def _():
            ffn(lambda k: stages[k][2][...])

    @pl.when(tvalid_ref[i] == 0)
    def _():
        y_ref[...] = jnp.zeros(y_ref.shape, F32)


def _moe(n2, w_gate, w_up, w_down, tile_expert, tile_first, tile_valid, tile_next, tile_base, src_tok):
    n_tiles = tile_expert.shape[0]
    hbm = pl.BlockSpec(memory_space=pl.ANY)
    grid_spec = pltpu.PrefetchScalarGridSpec(
        num_scalar_prefetch=6,
        grid=(n_tiles,),
        in_specs=[hbm, hbm, hbm, hbm],
        out_specs=pl.BlockSpec((MOE_TM, D_MODEL), lambda i, te, tf, tv, tn, tb, st: (i, 0)),
        scratch_shapes=[pltpu.VMEM((MOE_RING, MOE_TM, D_MODEL), F32),
                        pltpu.SemaphoreType.DMA((MOE_RING,)),
                        pltpu.VMEM((D_MODEL, D_EXPERT), F32),
                        pltpu.VMEM((D_MODEL, D_EXPERT), F32),
                        pltpu.VMEM((D_EXPERT, D_MODEL), F32),
                        pltpu.SemaphoreType.DMA((3,)),
                        pltpu.VMEM((D_MODEL, D_EXPERT), BF16),
                        pltpu.VMEM((D_MODEL, D_EXPERT), BF16),
                        pltpu.VMEM((D_EXPERT, D_MODEL), BF16)],
    )
    return pl.pallas_call(
        _moe_kernel,
        grid_spec=grid_spec,
        out_shape=jax.ShapeDtypeStruct((n_tiles * MOE_TM, D_MODEL), F32),
        compiler_params=_cparams(("arbitrary",)),
        name="moe",
    )(tile_expert, tile_first, tile_valid, tile_next, tile_base, src_tok, n2, w_gate, w_up, w_down)


def _route_plan(e1, e2, n_tok):
    experts = jnp.arange(N_EXPERTS, dtype=jnp.int32)
    flat_e = jnp.concatenate([e1, e2])
    onehot = (flat_e[:, None] == experts[None, :]).astype(jnp.int32)
    blk = onehot.astype(F32).reshape(-1, LANES, N_EXPERTS)
    tri = (jnp.arange(LANES)[:, None] >= jnp.arange(LANES)[None, :]).astype(F32)
    within = jnp.einsum("ij,bje->bie", tri, blk)
    before = jnp.cumsum(within[:, -1, :], axis=0) - within[:, -1, :]
    csum = (within + before[:, None, :]).reshape(-1, N_EXPERTS).astype(jnp.int32)
    rank = jnp.sum(csum * onehot, axis=1) - 1
    counts = csum[-1]
    tiles_per = (counts + MOE_TM - 1) // MOE_TM
    tile_end = jnp.cumsum(tiles_per)
    tile_off = tile_end - tiles_per
    slot = jnp.sum(onehot * tile_off[None, :], axis=1) * MOE_TM + rank
    n_tiles = (2 * n_tok) // MOE_TM + N_EXPERTS
    tok = jnp.tile(jnp.arange(n_tok, dtype=jnp.int32), 2)
    _, src_tok = lax.sort((slot, tok), num_keys=1)
    src_tok = jnp.concatenate([src_tok, jnp.zeros((MOE_TM,), jnp.int32)])
    count_off = jnp.cumsum(counts) - counts
    tile_id = jnp.arange(n_tiles, dtype=jnp.int32)
    n_used = tile_end[-1]
    tile_valid = (tile_id < n_used).astype(jnp.int32)
    te = jnp.sum((tile_end[None, :] <= jnp.minimum(tile_id, n_used - 1)[:, None]).astype(jnp.int32), axis=1)
    tile_expert = jnp.minimum(te, N_EXPERTS - 1)
    prev = jnp.concatenate([jnp.full((1,), -1, jnp.int32), tile_expert[:-1]])
    tile_first = (tile_expert != prev).astype(jnp.int32)
    oh_te = (tile_expert[:, None] == experts[None, :]).astype(jnp.int32)

    def of_expert(table):
        return jnp.sum(oh_te * table[None, :], axis=1)

    next_tile = of_expert(tile_end)
    oh_next = (tile_id[None, :] == jnp.minimum(next_tile, n_tiles - 1)[:, None]).astype(jnp.int32)
    tile_next = jnp.where(next_tile < n_used, jnp.sum(oh_next * tile_expert[None, :], axis=1), -1)
    tile_base = jnp.where(tile_valid == 1, of_expert(count_off) + (tile_id - of_expert(tile_off)) * MOE_TM, 0)
    return tile_expert, tile_first, tile_valid, tile_next, tile_base, src_tok, slot[:n_tok], slot[n_tok:]


def _final_kernel(p1_ref, p2_ref, ys_hbm, x1_ref, g2_ref, rt_ref, fg_ref, yp_ref, ysm_ref, ybuf, sem,
                  *, n_prompt_tiles):
    i = pl.program_id(0)
    n_tiles = pl.num_programs(0)
    slot = i % 2

    def start(tile, s):
        _row_gather(ys_hbm, p1_ref, tile * ROW_TILE, ybuf.at[s, 0], sem.at[s], ROW_TILE)
        _row_gather(ys_hbm, p2_ref, tile * ROW_TILE, ybuf.at[s, 1], sem.at[s], ROW_TILE)

    @pl.when(i == 0)
    def _():
        start(0, 0)

    @pl.when(i + 1 < n_tiles)
    def _():
        start(i + 1, 1 - slot)

    _row_gather_wait(ys_hbm, ybuf.at[slot, 0], sem.at[slot], ROW_TILE)
    _row_gather_wait(ys_hbm, ybuf.at[slot, 1], sem.at[slot], ROW_TILE)
    rt = rt_ref[...]
    moe = rt[:, 2:3] * ybuf[slot, 0] + rt[:, 3:4] * ybuf[slot, 1]
    x2 = x1_ref[...] + _bcast_rows(g2_ref[...], ROW_TILE) * moe
    y = x2 * lax.rsqrt(jnp.mean(x2 * x2, axis=-1, keepdims=True) + EPS) * fg_ref[...]

    @pl.when(i < n_prompt_tiles)
    def _():
        yp_ref[...] = y

    @pl.when(i >= n_prompt_tiles)
    def _():
        ysm_ref[...] = y


def _final(ys, x1, g2, rt, final_g, p1, p2, n_prompt_rows):
    t = x1.shape[0]
    tm = ROW_TILE
    seg = tm // MOD_ROWS
    n_p = n_prompt_rows // tm
    grid_spec = pltpu.PrefetchScalarGridSpec(
        num_scalar_prefetch=2,
        grid=(t // tm,),
        in_specs=[pl.BlockSpec(memory_space=pl.ANY),
                  pl.BlockSpec((tm, D_MODEL), lambda i, a, b: (i, 0)),
                  pl.BlockSpec((seg, D_MODEL), lambda i, a, b: (i, 0)),
                  pl.BlockSpec((tm, ROUTE_LANES), lambda i, a, b: (i, 0)),
                  pl.BlockSpec((1, D_MODEL), lambda i, a, b: (0, 0))],
        out_specs=[pl.BlockSpec((tm, D_MODEL), lambda i, a, b: (jnp.minimum(i, n_p - 1), 0)),
                   pl.BlockSpec((tm, D_MODEL), lambda i, a, b: (jnp.maximum(i - n_p, 0), 0))],
        scratch_shapes=[pltpu.VMEM((2, 2, tm, D_MODEL), F32),
                        pltpu.SemaphoreType.DMA((2,))],
    )
    return pl.pallas_call(
        functools.partial(_final_kernel, n_prompt_tiles=n_p),
        grid_spec=grid_spec,
        out_shape=[jax.ShapeDtypeStruct((n_prompt_rows, D_MODEL), F32),
                   jax.ShapeDtypeStruct((t - n_prompt_rows, D_MODEL), F32)],
        compiler_params=_cparams(("arbitrary",)),
        name="final",
    )(p1, p2, ys, x1, g2, rt, final_g.reshape(1, D_MODEL))


def kernel(x_prompt, x_sample, cache_k_win, cache_v_win, state_conv, state_rglru, c_prompt, c_sample, w_ada, b_ada, norm1_g, norm2_g, w_in, b_in, w_conv, b_conv, w_rg_a, b_rg_a, w_rg_x, b_rg_x, lru_lambda, w_rnn_out, w_attn_out, w_out, attn_sinks, w_route_group, b_route_group, w_route_expert, b_route_expert, w_exp_gate, w_exp_up, w_exp_down, rel_bias_table, final_norm_g):
    n_b, seq, _ = x_prompt.shape
    d_b, d_seq, _ = x_sample.shape
    assert w_ada.shape[0] == 1, "single trunk layer"
    assert seq % (ATT_CPB * CHUNK) == 0 and seq % MOD_ROWS == 0 and d_seq == MOD_ROWS
    assert d_seq >= CONV_W - 1 and d_b == N_CHAINS
    t_p, t_s = n_b * seq, d_b * d_seq
    t = t_p + t_s
    assert t_s == ROW_TILE and t_p % ROW_TILE == 0
    cw = cache_k_win.shape[2]
    l = 0

    x_p = x_prompt.reshape(t_p, D_MODEL)
    x_s = x_sample.reshape(t_s, D_MODEL)

    n_c = n_b + d_b
    c_all = jnp.pad(jnp.concatenate([c_prompt, c_sample], axis=0), ((0, -n_c % SUBLANES), (0, 0)))
    mod = _ada(c_all, w_ada[l], b_ada[l])

    def per_segment(m):
        return jnp.concatenate([jnp.repeat(m[:n_b], seq // MOD_ROWS, axis=0),
                                jnp.repeat(m[n_b:n_c], d_seq // MOD_ROWS, axis=0)], axis=0)

    sh1, sc1, g1, sh2, sc2, g2 = [per_segment(m) for m in jnp.split(mod, 6, axis=-1)]

    z = _inproj(_norm1(x_p, x_s, sc1, sh1, norm1_g[l]), w_in[l], b_in[l])

    w_gate = jnp.concatenate([w_rg_a[l], w_rg_x[l]], axis=-1)
    b_gate = jnp.concatenate([b_rg_a[l], b_rg_x[l]], axis=-1)[:, None, :]
    hg_p, hl_p = _rglru(z, jnp.zeros((n_b, SUBLANES, D_RNN), F32), jnp.zeros((n_b, N_CHAINS, D_RNN), F32),
                        w_conv[l], b_conv[l], w_gate, b_gate, lru_lambda[l],
                        n_blocks=n_b, n_seq=1, seq_len=seq, row_block0=0, link=True)
    state8 = jnp.pad(state_conv[l], ((0, 0), (SUBLANES - (CONV_W - 1), 0), (0, 0)))
    hg_s, hl_s = _rglru(z, state8, state_rglru[l][None], w_conv[l], b_conv[l], w_gate, b_gate, lru_lambda[l],
                        n_blocks=1, n_seq=d_b, seq_len=d_seq, row_block0=t_p // t_s, link=False)

    sinks = attn_sinks[l]
    bkt_p = _bucket_map(WIN_CHUNKS * CHUNK + jnp.arange(CHUNK), jnp.arange(SPAN))
    o_p = _attn_band(z, rel_bias_table, sinks, bkt_p, n_batch=n_b, seq_len=seq)
    k_new = z[t_p:, COL_K:COL_K + KV_DIM].reshape(d_b, d_seq, KV_DIM)
    v_new = z[t_p:, COL_V:COL_V + KV_DIM].reshape(d_b, d_seq, KV_DIM)
    k_all = jnp.concatenate([cache_k_win[l].reshape(d_b, cw, KV_DIM), k_new], axis=1)
    v_all = jnp.concatenate([cache_v_win[l].reshape(d_b, cw, KV_DIM), v_new], axis=1)
    bkt_s = _bucket_map(cw + jnp.arange(d_seq), jnp.arange(cw + d_seq))
    o_s = _attn_step(z, k_all, v_all, rel_bias_table, sinks, bkt_s, n_batch=d_b, n_q=d_seq, row_block0=t_p // d_seq)

    n_route = N_GROUPS + N_EXPERTS
    w_route = jnp.pad(jnp.concatenate([w_route_group[l], w_route_expert[l]], axis=1),
                      ((0, 0), (0, ROUTE_LANES - n_route)))
    b_route = jnp.pad(jnp.concatenate([b_route_group[l], b_route_expert[l]]),
                      (0, ROUTE_LANES - n_route)).reshape(1, ROUTE_LANES)
    x1, n2, rt = _outproj(hg_p, hg_s, o_p, o_s, z, x_p, x_s, g1, sh2, sc2, norm2_g[l],
                          w_rnn_out[l].astype(BF16), w_attn_out[l].astype(BF16), w_out[l].astype(BF16),
                          w_route, b_route)

    e1 = rt[:, 0].astype(jnp.int32)
    e2 = rt[:, 1].astype(jnp.int32)
    tile_expert, tile_first, tile_valid, tile_next, tile_base, src_tok, p1, p2 = _route_plan(e1, e2, t)
    ys = _moe(n2, w_exp_gate[l], w_exp_up[l], w_exp_down[l], tile_expert, tile_first, tile_valid, tile_next,
              tile_base, src_tok)
    y_p, y_s = _final(ys, x1, g2, rt, final_norm_g, p1, p2, t_p)

    win = min(WINDOW, seq)

    def tail(col0, width, n_rows):
        return jnp.stack([z[(b + 1) * seq - n_rows:(b + 1) * seq, col0:col0 + width] for b in range(n_b)])

    kp = tail(COL_K, KV_DIM, win).reshape(n_b, win, N_KV_HEADS, HEAD_DIM)
    vp = tail(COL_V, KV_DIM, win).reshape(n_b, win, N_KV_HEADS, HEAD_DIM)
    cp = tail(COL_XR, D_RNN, CONV_W - 1)
    rp = hl_p[:, N_CHAINS - 1, :]
    ks = k_all[:, -cw:].reshape(d_b, cw, N_KV_HEADS, HEAD_DIM)
    vs = v_all[:, -cw:].reshape(d_b, cw, N_KV_HEADS, HEAD_DIM)
    cs = z[t_p:, COL_XR:COL_XR + D_RNN].reshape(d_b, d_seq, D_RNN)[:, -(CONV_W - 1):]
    rs = hl_s[0]
    return (y_p.reshape(n_b, seq, D_MODEL), y_s.reshape(d_b, d_seq, D_MODEL),
            kp[None], vp[None], cp[None], rp[None], ks[None], vs[None], cs[None], rs[None])
```

```python
import functools
import math

import jax
import jax.numpy as jnp
from jax import lax
from jax.experimental import pallas as pl
from jax.experimental.pallas import tpu as pltpu

F32 = jnp.float32
BF16 = jnp.bfloat16

D_MODEL = 2048
D_RNN = 2048
RNN_BLOCK = 128
CONV_W = 4
LRU_C = 8.0
N_HEADS = 16
N_KV_HEADS = 4
HEAD_DIM = 128
GQA_GROUP = N_HEADS // N_KV_HEADS
Q_DIM = N_HEADS * HEAD_DIM
KV_DIM = N_KV_HEADS * HEAD_DIM
CHUNK = 64
WINDOW = 128
WIN_CHUNKS = WINDOW // CHUNK
SPAN = (WIN_CHUNKS + 1) * CHUNK
N_BUCKETS = 32
MAX_DISTANCE = 128
N_GROUPS = 4
E_PER_GROUP = 8
N_EXPERTS = N_GROUPS * E_PER_GROUP
D_EXPERT = 512
EPS = 1e-6
NEG_INF = -1e30
D_IN = 2 * D_RNN + Q_DIM + 2 * KV_DIM + 2 * D_MODEL
COL_XR, COL_GR, COL_Q = 0, D_RNN, 2 * D_RNN
COL_K = COL_Q + Q_DIM
COL_V = COL_K + KV_DIM
COL_GA = COL_V + KV_DIM
COL_GB = COL_GA + D_MODEL

LANES = 128
SUBLANES = 8
MOD_ROWS = 32
KEY_TILE = 256
VMEM_LIMIT = 56 * 1024 * 1024
VMEM_LIMIT_BIG = 60 * 1024 * 1024
ADA_TN = 1024

ROW_TILE = 256
IN_TM_CHOICES = (1408, 768, 256)
IN_TN = 1024
MOE_TM = 192
MOE_RING = 3


def _sigmoid(x):
    return 0.5 * jnp.tanh(0.5 * x) + 0.5


def _gelu_tanh(x):
    return 0.5 * x * (1.0 + jnp.tanh(math.sqrt(2.0 / math.pi) * (x + 0.044715 * (x * x * x))))


def _bcast_rows(v, rows):
    n, d = v.shape
    return jnp.broadcast_to(v[:, None, :], (n, MOD_ROWS, d)).reshape(rows, d)


def _rms_modulate(x, gain, scale_seg, shift_seg):
    rows = x.shape[0]
    y = x * lax.rsqrt(jnp.mean(x * x, axis=-1, keepdims=True) + EPS) * gain
    return y * (1.0 + _bcast_rows(scale_seg, rows)) + _bcast_rows(shift_seg, rows)


def _cparams(sem, vmem_limit=VMEM_LIMIT):
    return pltpu.CompilerParams(dimension_semantics=sem, vmem_limit_bytes=vmem_limit)


def _ada_kernel(c_ref, w_ref, b_ref, o_ref):
    c = c_ref[...]
    s = (c * _sigmoid(c)).astype(BF16)
    o_ref[...] = jnp.dot(s, w_ref[...].astype(BF16), preferred_element_type=F32) + b_ref[...]


def _ada(c_all, w_ada, b_ada):
    rows = c_all.shape[0]
    n = w_ada.shape[1]
    tn = ADA_TN
    return pl.pallas_call(
        _ada_kernel,
        grid=(n // tn,),
        in_specs=[pl.BlockSpec((rows, D_MODEL), lambda j: (0, 0)),
                  pl.BlockSpec((D_MODEL, tn), lambda j: (0, j)),
                  pl.BlockSpec((1, tn), lambda j: (0, j))],
        out_specs=pl.BlockSpec((rows, tn), lambda j: (0, j)),
        out_shape=jax.ShapeDtypeStruct((rows, n), F32),
        compiler_params=_cparams(("arbitrary",)),
        name="ada",
    )(c_all, w_ada, b_ada.reshape(1, n))


NORM_NSUB = 3


def _norm1_kernel(x0_ref, x1_ref, x2_ref, xs_ref, sc_ref, sh_ref, g_ref, n1_ref, *, n_prompt_tiles):
    tail_is_sample = pl.program_id(0) * NORM_NSUB + (NORM_NSUB - 1) >= n_prompt_tiles
    gain = g_ref[...]
    segs = ROW_TILE // MOD_ROWS
    for r, x_ref in enumerate((x0_ref, x1_ref, x2_ref)):
        for s in range(segs):
            rows = slice(s * MOD_ROWS, (s + 1) * MOD_ROWS)
            x = x_ref[rows, :]
            if r == NORM_NSUB - 1:
                x = jnp.where(tail_is_sample, xs_ref[rows, :], x)
            y = x * lax.rsqrt(jnp.mean(x * x, axis=-1, keepdims=True) + EPS) * gain
            k = r * segs + s
            out_rows = slice(k * MOD_ROWS, (k + 1) * MOD_ROWS)
            n1_ref[out_rows, :] = (y * (1.0 + sc_ref[k:k + 1, :]) + sh_ref[k:k + 1, :]).astype(BF16)


def _norm1(x_p, x_s, sc1, sh1, norm_g):
    n_p = x_p.shape[0] // ROW_TILE
    t = x_p.shape[0] + x_s.shape[0]
    tm = NORM_NSUB * ROW_TILE
    assert x_s.shape[0] == ROW_TILE and t % tm == 0
    seg = tm // MOD_ROWS

    def sub(r):
        return pl.BlockSpec((ROW_TILE, D_MODEL), lambda i: (jnp.minimum(i * NORM_NSUB + r, n_p - 1), 0))

    return pl.pallas_call(
        functools.partial(_norm1_kernel, n_prompt_tiles=n_p),
        grid=(t // tm,),
        in_specs=[sub(0), sub(1), sub(2),
                  pl.BlockSpec((ROW_TILE, D_MODEL), lambda i: (0, 0)),
                  pl.BlockSpec((seg, D_MODEL), lambda i: (i, 0)),
                  pl.BlockSpec((seg, D_MODEL), lambda i: (i, 0)),
                  pl.BlockSpec((1, D_MODEL), lambda i: (0, 0))],
        out_specs=pl.BlockSpec((tm, D_MODEL), lambda i: (i, 0)),
        out_shape=jax.ShapeDtypeStruct((t, D_MODEL), BF16),
        compiler_params=_cparams(("arbitrary",)),
        name="norm1",
    )(x_p, x_p, x_p, x_s, sc1, sh1, norm_g.reshape(1, D_MODEL))


def _inproj_kernel(n1_ref, w_ref, b_ref, z_ref, w_bf):
    @pl.when(pl.program_id(1) == 0)
    def _():
        w_bf[...] = w_ref[...].astype(BF16)

    z_ref[...] = jnp.dot(n1_ref[...], w_bf[...], preferred_element_type=F32) + b_ref[...]


def _inproj(n1, w_in, b_in):
    t = n1.shape[0]
    tm = next(m for m in IN_TM_CHOICES if t % m == 0)
    return pl.pallas_call(
        _inproj_kernel,
        grid=(D_IN // IN_TN, t // tm),
        in_specs=[pl.BlockSpec((tm, D_MODEL), lambda j, i: (i, 0)),
                  pl.BlockSpec((D_MODEL, IN_TN), lambda j, i: (0, j)),
                  pl.BlockSpec((1, IN_TN), lambda j, i: (0, j))],
        out_specs=pl.BlockSpec((tm, IN_TN), lambda j, i: (i, j)),
        out_shape=jax.ShapeDtypeStruct((t, D_IN), F32),
        scratch_shapes=[pltpu.VMEM((D_MODEL, IN_TN), BF16)],
        compiler_params=_cparams(("arbitrary", "arbitrary")),
        name="inproj",
    )(n1, w_in, b_in.reshape(1, D_IN))


RG_CW = 256
RG_SLABS = RG_CW // LANES
N_CHAINS = SUBLANES
CHAIN_PAD = 8


def _rglru_kernel(xr_ref, gr_ref, st_ref, h0_ref, wc_ref, bc_ref, wg_ref, bg_ref, lam_ref,
                  hg_ref, hl_ref, ext_scr, a_scr, u_scr, *, n_seq, seq_len, link):
    rows = n_seq * seq_len
    cl = rows // N_CHAINS
    pitch = cl + CHAIN_PAD

    for s in range(n_seq):
        ext_scr[s, 0:SUBLANES, :] = st_ref[s]
        ext_scr[s, SUBLANES:SUBLANES + seq_len, :] = xr_ref[s * seq_len:(s + 1) * seq_len, :]

    z = -lam_ref[...]
    softplus = jnp.maximum(z, 0.0) + jnp.log1p(jnp.exp(-jnp.abs(z)))
    nq = (-0.25 * LRU_C) * softplus

    for c in range(N_CHAINS):
        s, r0 = divmod(c * cl, seq_len)
        blk = ext_scr[s, pl.ds(SUBLANES + r0, cl), :]
        front = ext_scr[s, pl.ds(r0, SUBLANES), :]
        row8 = lax.broadcasted_iota(jnp.int32, (SUBLANES, RG_CW), 0)
        xc = bc_ref[...] + wc_ref[CONV_W - 1:CONV_W, :] * blk
        for k in range(1, CONV_W):
            rolled = pltpu.roll(blk, k, axis=0)
            head = jnp.where(row8 < k, pltpu.roll(front, k, axis=0), rolled[:SUBLANES, :])
            shifted = jnp.concatenate([head, rolled[SUBLANES:, :]], axis=0)
            xc = xc + wc_ref[CONV_W - 1 - k:CONV_W - k, :] * shifted
        for sl in range(RG_SLABS):
            lanes = slice(sl * LANES, (sl + 1) * LANES)
            xb = xc[:, lanes]
            g = jnp.dot(xb.astype(BF16), wg_ref[sl].astype(BF16), preferred_element_type=F32) + bg_ref[sl]
            tr = jnp.tanh(0.5 * g[:, :LANES])
            ti = jnp.tanh(0.5 * g[:, LANES:])
            th = jnp.tanh(nq[:, lanes] * (tr + 1.0))
            rcp = 1.0 / (1.0 - th)
            a_scr[sl, c * pitch:c * pitch + cl, :] = (1.0 + th) * rcp
            u_scr[sl, c * pitch:c * pitch + cl, :] = (jnp.sqrt(-th) * rcp) * ((ti + 1.0) * xb)

    def step(t, carry):
        hs, ps = carry
        new_h, new_p = [], []
        for sl in range(RG_SLABS):
            a = a_scr[sl, pl.ds(t, N_CHAINS, stride=pitch), :]
            u = u_scr[sl, pl.ds(t, N_CHAINS, stride=pitch), :]
            h = a * hs[sl] + u
            u_scr[sl, pl.ds(t, N_CHAINS, stride=pitch), :] = h
            new_h.append(h)
            if link:
                p = a * ps[sl]
                a_scr[sl, pl.ds(t, N_CHAINS, stride=pitch), :] = p
                new_p.append(p)
            else:
                new_p.append(ps[sl])
        return tuple(new_h), tuple(new_p)

    h_init = tuple(h0_ref[0, :, sl * LANES:(sl + 1) * LANES] for sl in range(RG_SLABS))
    p_init = tuple(jnp.ones((N_CHAINS, LANES), F32) for _ in range(RG_SLABS))
    h_end, p_end = lax.fori_loop(0, cl, step, (h_init, p_init), unroll=8)

    row = lax.broadcasted_iota(jnp.int32, (N_CHAINS, LANES), 0)
    for sl in range(RG_SLABS):
        lanes = slice(sl * LANES, (sl + 1) * LANES)
        if link:
            def shift_down(v):
                return jnp.where(row == 0, 0.0, pltpu.roll(v, 1, axis=0))
            hh = h_end[sl]
            for _ in range(N_CHAINS - 1):
                hh = h_end[sl] + p_end[sl] * shift_down(hh)
            carry_in = shift_down(hh)
        else:
            hh = h_end[sl]
        hl_ref[0, :, lanes] = hh
        for c in range(N_CHAINS):
            h = u_scr[sl, c * pitch:c * pitch + cl, :]
            if link:
                h = h + a_scr[sl, c * pitch:c * pitch + cl, :] * carry_in[c:c + 1, :]
            gg = _gelu_tanh(gr_ref[c * cl:(c + 1) * cl, lanes])
            hg_ref[c * cl:(c + 1) * cl, lanes] = (h * gg).astype(BF16)


def _rglru(z, state8, h0, w_conv, b_conv, w_gate, b_gate, lam, *, n_blocks, n_seq, seq_len, row_block0, link):
    rows = n_seq * seq_len
    cl = rows // N_CHAINS
    ncb = D_RNN // RG_CW
    gr0 = COL_GR // RG_CW
    kern = functools.partial(_rglru_kernel, n_seq=n_seq, seq_len=seq_len, link=link)
    return pl.pallas_call(
        kern,
        grid=(n_blocks, ncb),
        in_specs=[pl.BlockSpec((rows, RG_CW), lambda b, n: (row_block0 + b, n)),
                  pl.BlockSpec((rows, RG_CW), lambda b, n: (row_block0 + b, gr0 + n)),
                  pl.BlockSpec((n_seq, SUBLANES, RG_CW), lambda b, n: (b, 0, n)),
                  pl.BlockSpec((1, N_CHAINS, RG_CW), lambda b, n: (b, 0, n)),
                  pl.BlockSpec((CONV_W, RG_CW), lambda b, n: (0, n)),
                  pl.BlockSpec((1, RG_CW), lambda b, n: (0, n)),
                  pl.BlockSpec((RG_SLABS, RNN_BLOCK, 2 * RNN_BLOCK), lambda b, n: (n, 0, 0)),
                  pl.BlockSpec((RG_SLABS, 1, 2 * RNN_BLOCK), lambda b, n: (n, 0, 0)),
                  pl.BlockSpec((1, RG_CW), lambda b, n: (0, n))],
        out_specs=[pl.BlockSpec((rows, RG_CW), lambda b, n: (b, n)),
                   pl.BlockSpec((1, N_CHAINS, RG_CW), lambda b, n: (b, 0, n))],
        out_shape=[jax.ShapeDtypeStruct((n_blocks * rows, D_RNN), BF16),
                   jax.ShapeDtypeStruct((n_blocks, N_CHAINS, D_RNN), F32)],
        scratch_shapes=[pltpu.VMEM((n_seq, SUBLANES + seq_len, RG_CW), F32),
                        pltpu.VMEM((RG_SLABS, N_CHAINS * (cl + CHAIN_PAD), LANES), F32),
                        pltpu.VMEM((RG_SLABS, N_CHAINS * (cl + CHAIN_PAD), LANES), F32)],
        compiler_params=_cparams(("arbitrary", "arbitrary")),
        name="rglru_link" if link else "rglru_step",
    )(z, z, state8, h0, w_conv, b_conv.reshape(1, D_RNN), w_gate, b_gate, lam.reshape(1, D_RNN))


LOG2E = math.log2(math.e)


def _build_bias(tbl_ref, bkt_ref, bias_scr, n_q):
    bkt = bkt_ref[...]
    base = jnp.where(bkt < 0, NEG_INF, 0.0)
    for h in range(N_HEADS):
        acc = base
        for bk in range(N_BUCKETS):
            acc = jnp.where(bkt == bk, tbl_ref[bk, h] * LOG2E, acc)
        hk, g = divmod(h, GQA_GROUP)
        bias_scr[hk, g * n_q:(g + 1) * n_q, :] = acc


def _sink_columns(sink_ref, n_q):
    row = lax.broadcasted_iota(jnp.int32, (GQA_GROUP * n_q, 1), 0)
    cols = []
    for hk in range(N_KV_HEADS):
        col = jnp.full((GQA_GROUP * n_q, 1), sink_ref[hk * GQA_GROUP] * LOG2E, F32)
        for g in range(1, GQA_GROUP):
            col = jnp.where(row >= g * n_q, sink_ref[hk * GQA_GROUP + g] * LOG2E, col)
        cols.append(col)
    return cols


def _attend(q_of, k_of, v_of, bias_scr, sinks, key_ok, store, n_q):
    scores = []
    for hk in range(N_KV_HEADS):
        qg = jnp.concatenate([q_of(hk * GQA_GROUP + g) for g in range(GQA_GROUP)], axis=0)
        s = lax.dot_general(qg, k_of(hk), (((1,), (1,)), ((), ())), preferred_element_type=F32)
        s = s * (HEAD_DIM ** -0.5 * LOG2E) + bias_scr[hk]
        if key_ok is not None:
            s = jnp.where(key_ok, s, NEG_INF)
        scores.append(s)
    probs = []
    for hk in range(N_KV_HEADS):
        s, sink = scores[hk], sinks[hk]
        m = jnp.maximum(jnp.max(s, axis=-1, keepdims=True), sink)
        p = jnp.exp2(s - m)
        inv = 1.0 / (jnp.sum(p, axis=-1, keepdims=True) + jnp.exp2(sink - m))
        probs.append((p * inv).astype(BF16))
    for hk in range(N_KV_HEADS):
        o = jnp.dot(probs[hk], v_of(hk), preferred_element_type=F32)
        for g in range(GQA_GROUP):
            store(hk * GQA_GROUP + g, o[g * n_q:(g + 1) * n_q, :].astype(BF16))


ATT_CPB = 8
ATT_LEAD = WIN_CHUNKS * CHUNK
ATT_TAIL = KEY_TILE - SPAN


def _attn_band_kernel(tbl_ref, sink_ref, q_ref, k_ref, v_ref, bkt_ref, o_ref, kpad, vpad, bias_scr, *, seq_len):
    b = pl.program_id(0)
    cg = pl.program_id(1)

    @pl.when((b == 0) & (cg == 0))
    def _():
        _build_bias(tbl_ref, bkt_ref, bias_scr, CHUNK)

    @pl.when(cg == 0)
    def _():
        for ref, pad in ((k_ref, kpad), (v_ref, vpad)):
            pad[0:ATT_LEAD, :] = jnp.zeros((ATT_LEAD, KV_DIM), BF16)
            pad[ATT_LEAD:ATT_LEAD + seq_len, :] = ref[...].astype(BF16)
            pad[ATT_LEAD + seq_len:, :] = jnp.zeros((ATT_TAIL, KV_DIM), BF16)

    kidx = lax.broadcasted_iota(jnp.int32, (1, KEY_TILE), 1)
    sinks = _sink_columns(sink_ref, CHUNK)

    def chunk(c, carry):
        q0 = pl.multiple_of(c * CHUNK, CHUNK)
        start = pl.multiple_of((cg * ATT_CPB + c) * CHUNK, CHUNK)
        key_ok = start + kidx >= ATT_LEAD

        def head_cols(h):
            return slice(h * HEAD_DIM, (h + 1) * HEAD_DIM)

        def store(h, val):
            o_ref[pl.ds(q0, CHUNK), head_cols(h)] = val

        def run(mask):
            _attend(lambda h: q_ref[pl.ds(q0, CHUNK), head_cols(h)].astype(BF16),
                    lambda hk: kpad[pl.ds(start, KEY_TILE), head_cols(hk)],
                    lambda hk: vpad[pl.ds(start, KEY_TILE), head_cols(hk)],
                    bias_scr, sinks, mask, store, CHUNK)

        pl.when(start < ATT_LEAD)(lambda: run(key_ok))
        pl.when(start >= ATT_LEAD)(lambda: run(None))
        return carry

    lax.fori_loop(0, ATT_CPB, chunk, 0)


def _attn_band(z, table, sinks, bkt, *, n_batch, seq_len):
    rows = ATT_CPB * CHUNK
    ng = seq_len // rows
    kern = functools.partial(_attn_band_kernel, seq_len=seq_len)
    smem = pl.BlockSpec(memory_space=pltpu.SMEM)
    pad_rows = ATT_LEAD + seq_len + ATT_TAIL
    return pl.pallas_call(
        kern,
        grid=(n_batch, ng),
        in_specs=[smem, smem,
                  pl.BlockSpec((rows, Q_DIM), lambda b, c: (b * ng + c, COL_Q // Q_DIM)),
                  pl.BlockSpec((seq_len, KV_DIM), lambda b, c: (b, COL_K // KV_DIM)),
                  pl.BlockSpec((seq_len, KV_DIM), lambda b, c: (b, COL_V // KV_DIM)),
                  pl.BlockSpec((CHUNK, KEY_TILE), lambda b, c: (0, 0))],
        out_specs=pl.BlockSpec((rows, Q_DIM), lambda b, c: (b * ng + c, 0)),
        out_shape=jax.ShapeDtypeStruct((n_batch * seq_len, Q_DIM), BF16),
        scratch_shapes=[pltpu.VMEM((pad_rows, KV_DIM), BF16),
                        pltpu.VMEM((pad_rows, KV_DIM), BF16),
                        pltpu.VMEM((N_KV_HEADS, GQA_GROUP * CHUNK, KEY_TILE), F32)],
        compiler_params=_cparams(("arbitrary", "arbitrary")),
        name="attn_band",
    )(table, sinks, z, z, z, bkt)


def _attn_step_kernel(tbl_ref, sink_ref, q_ref, k_ref, v_ref, bkt_ref, o_ref, kbuf, vbuf, bias_scr,
                      *, n_keys, n_q):
    b = pl.program_id(0)

    @pl.when(b == 0)
    def _():
        _build_bias(tbl_ref, bkt_ref, bias_scr, n_q)
        kbuf[n_keys:, :] = jnp.zeros((KEY_TILE - n_keys, KV_DIM), BF16)
        vbuf[n_keys:, :] = jnp.zeros((KEY_TILE - n_keys, KV_DIM), BF16)

    kbuf[0:n_keys, :] = k_ref[0].astype(BF16)
    vbuf[0:n_keys, :] = v_ref[0].astype(BF16)

    def head_cols(h):
        return slice(h * HEAD_DIM, (h + 1) * HEAD_DIM)

    def store(h, val):
        o_ref[:, head_cols(h)] = val

    _attend(lambda h: q_ref[:, head_cols(h)].astype(BF16),
            lambda hk: kbuf[:, head_cols(hk)], lambda hk: vbuf[:, head_cols(hk)],
            bias_scr, _sink_columns(sink_ref, n_q), None, store, n_q)


def _attn_step(z, k_all, v_all, table, sinks, bkt, *, n_batch, n_q, row_block0):
    n_keys = k_all.shape[1]
    kern = functools.partial(_attn_step_kernel, n_keys=n_keys, n_q=n_q)
    smem = pl.BlockSpec(memory_space=pltpu.SMEM)
    return pl.pallas_call(
        kern,
        grid=(n_batch,),
        in_specs=[smem, smem,
                  pl.BlockSpec((n_q, Q_DIM), lambda b: (row_block0 + b, COL_Q // Q_DIM)),
                  pl.BlockSpec((1, n_keys, KV_DIM), lambda b: (b, 0, 0)),
                  pl.BlockSpec((1, n_keys, KV_DIM), lambda b: (b, 0, 0)),
                  pl.BlockSpec((n_q, KEY_TILE), lambda b: (0, 0))],
        out_specs=pl.BlockSpec((n_q, Q_DIM), lambda b: (b, 0)),
        out_shape=jax.ShapeDtypeStruct((n_batch * n_q, Q_DIM), BF16),
        scratch_shapes=[pltpu.VMEM((KEY_TILE, KV_DIM), BF16),
                        pltpu.VMEM((KEY_TILE, KV_DIM), BF16),
                        pltpu.VMEM((N_KV_HEADS, GQA_GROUP * n_q, KEY_TILE), F32)],
        compiler_params=_cparams(("arbitrary",)),
        name="attn_step",
    )(table, sinks, z, k_all, v_all, bkt)


def _t5_bucket(rel):
    nb = N_BUCKETS // 2
    ret = jnp.where(rel > 0, nb, 0)
    n = jnp.abs(rel)
    max_exact = nb // 2
    nf = jnp.maximum(n, 1).astype(jnp.float32)
    large = max_exact + (jnp.log(nf / max_exact) / math.log(MAX_DISTANCE / max_exact)
                         * (nb - max_exact)).astype(jnp.int32)
    large = jnp.minimum(large, nb - 1)
    return ret + jnp.where(n < max_exact, n, large)


def _bucket_map(q_pos, k_pos):
    bkt = _t5_bucket(k_pos[None, :] - q_pos[:, None]).astype(jnp.int32)
    return jnp.pad(bkt, ((0, 0), (0, KEY_TILE - k_pos.shape[0])), constant_values=-1)


ROUTE_LANES = LANES
GATE_TN = 1024


def _outproj_kernel(hgp_ref, hgs_ref, op_ref, os_ref, ga0_ref, ga1_ref, gb0_ref, gb1_ref, xp_ref, xs_ref,
                    g1_ref, sh2_ref, sc2_ref, ng_ref, wr_ref, wa_ref, wo_ref, wrt_ref, brt_ref,
                    x1_ref, n2_ref, rt_ref, *, n_prompt_tiles):
    tm = x1_ref.shape[0]
    is_prompt = pl.program_id(0) < n_prompt_tiles
    hg = jnp.where(is_prompt, hgp_ref[...], hgs_ref[...])
    o = jnp.where(is_prompt, op_ref[...], os_ref[...])
    x = jnp.where(is_prompt, xp_ref[...], xs_ref[...])
    ya = jnp.dot(hg, wr_ref[...], preferred_element_type=F32)
    yb = jnp.dot(o, wa_ref[...], preferred_element_type=F32)
    halves = []
    for k, (ga_ref, gb_ref) in enumerate(((ga0_ref, gb0_ref), (ga1_ref, gb1_ref))):
        cols = slice(k * GATE_TN, (k + 1) * GATE_TN)
        halves.append((_sigmoid(ga_ref[...]) * ya[:, cols] + _sigmoid(gb_ref[...]) * yb[:, cols]).astype(BF16))
    merged = jnp.concatenate(halves, axis=1)
    mix = jnp.dot(merged, wo_ref[...], preferred_element_type=F32)
    x1 = x + _bcast_rows(g1_ref[...], tm) * mix
    x1_ref[...] = x1
    n2 = _rms_modulate(x1, ng_ref[...], sc2_ref[...], sh2_ref[...])
    n2_ref[...] = n2

    lg = jnp.dot(n2.astype(BF16), wrt_ref[...].astype(BF16), preferred_element_type=F32) + brt_ref[...]

    lane = lax.broadcasted_iota(jnp.int32, (tm, ROUTE_LANES), 1)
    lane_f = lane.astype(F32)
    e_lane = lane - N_GROUPS
    lane_group = (e_lane >> 3).astype(F32)

    def first_argmax(vals, vmax):
        return jnp.min(jnp.where(vals == vmax, lane_f, float(ROUTE_LANES)), axis=-1, keepdims=True)

    gl = jnp.where(lane < N_GROUPS, lg, NEG_INF)
    gmax = jnp.max(gl, axis=-1, keepdims=True)
    g_idx = first_argmax(gl, gmax)
    g_w = 1.0 / jnp.sum(jnp.exp(gl - gmax), axis=-1, keepdims=True)
    in_group = jnp.where((e_lane >= 0) & (e_lane < N_EXPERTS), lane_group, -1.0) == g_idx
    el = jnp.where(in_group, lg, NEG_INF)
    v1 = jnp.max(el, axis=-1, keepdims=True)
    i1 = first_argmax(el, v1)
    el2 = jnp.where(lane_f == i1, NEG_INF, el)
    v2 = jnp.max(el2, axis=-1, keepdims=True)
    i2 = first_argmax(el2, v2)
    e21 = jnp.exp(v2 - v1)
    w1 = g_w / (1.0 + e21)
    w2 = g_w * e21 / (1.0 + e21)
    e1 = i1 - float(N_GROUPS)
    e2 = i2 - float(N_GROUPS)
    rt_ref[...] = jnp.where(lane == 0, e1, jnp.where(lane == 1, e2, jnp.where(lane == 2, w1,
                            jnp.where(lane == 3, w2, 0.0))))


def _outproj(hg_p, hg_s, o_p, o_s, z, x_p, x_s, g1, sh2, sc2, norm_g, wr, wa, wo, w_route, b_route):
    tm = ROW_TILE
    t = z.shape[0]
    seg = tm // MOD_ROWS
    n_p = x_p.shape[0] // tm
    row = lambda i: (i, 0)
    fix = lambda i: (0, 0)
    row_p = lambda i: (jnp.minimum(i, n_p - 1), 0)
    row_s = lambda i: (jnp.maximum(i - n_p, 0), 0)
    once = pl.Buffered(1)

    def gate(col):
        return pl.BlockSpec((tm, GATE_TN), lambda i: (i, col // GATE_TN))

    return pl.pallas_call(
        functools.partial(_outproj_kernel, n_prompt_tiles=n_p),
        grid=(t // tm,),
        in_specs=[pl.BlockSpec((tm, D_RNN), row_p), pl.BlockSpec((tm, D_RNN), row_s, pipeline_mode=once),
                  pl.BlockSpec((tm, Q_DIM), row_p), pl.BlockSpec((tm, Q_DIM), row_s, pipeline_mode=once),
                  gate(COL_GA), gate(COL_GA + GATE_TN), gate(COL_GB), gate(COL_GB + GATE_TN),
                  pl.BlockSpec((tm, D_MODEL), row_p), pl.BlockSpec((tm, D_MODEL), row_s, pipeline_mode=once),
                  pl.BlockSpec((seg, D_MODEL), row), pl.BlockSpec((seg, D_MODEL), row),
                  pl.BlockSpec((seg, D_MODEL), row),
                  pl.BlockSpec((1, D_MODEL), fix),
                  pl.BlockSpec((D_RNN, D_MODEL), fix, pipeline_mode=once),
                  pl.BlockSpec((Q_DIM, D_MODEL), fix, pipeline_mode=once),
                  pl.BlockSpec((D_MODEL, D_MODEL), fix, pipeline_mode=once),
                  pl.BlockSpec((D_MODEL, ROUTE_LANES), fix),
                  pl.BlockSpec((1, ROUTE_LANES), fix)],
        out_specs=[pl.BlockSpec((tm, D_MODEL), row), pl.BlockSpec((tm, D_MODEL), row),
                   pl.BlockSpec((tm, ROUTE_LANES), row)],
        out_shape=[jax.ShapeDtypeStruct((t, D_MODEL), F32), jax.ShapeDtypeStruct((t, D_MODEL), F32),
                   jax.ShapeDtypeStruct((t, ROUTE_LANES), F32)],
        compiler_params=_cparams(("arbitrary",), VMEM_LIMIT_BIG),
        name="outproj",
    )(hg_p, hg_s, o_p, o_s, z, z, z, z, x_p, x_s, g1, sh2, sc2, norm_g.reshape(1, D_MODEL),
      wr, wa, wo, w_route, b_route)


def _row_gather(src_hbm, idx_ref, base, dst, sem, n_rows):
    for r in range(n_rows):
        tok = idx_ref[base + r]
        pltpu.make_async_copy(src_hbm.at[pl.ds(tok, 1), :], dst.at[pl.ds(r, 1), :], sem).start()


def _row_gather_wait(src_hbm, dst, sem, n_rows):
    pltpu.make_async_copy(src_hbm.at[pl.ds(0, n_rows), :], dst, sem).wait()


def _moe_kernel(te_ref, tfirst_ref, tvalid_ref, tnext_ref, tbase_ref, src_ref, n2_hbm, wg_hbm, wu_hbm, wd_hbm, y_ref,
                xbuf, xsem, wg_st, wu_st, wd_st, wsem, wg_bf, wu_bf, wd_bf):
    i = pl.program_id(0)
    n_tiles = pl.num_programs(0)
    slot = lax.rem(i, MOE_RING)
    stages = ((wg_hbm, wg_st, wg_bf), (wu_hbm, wu_st, wu_bf), (wd_hbm, wd_st, wd_bf))

    def weight_copy(k, e):
        hbm, st, _ = stages[k]
        return pltpu.make_async_copy(hbm.at[e], st, wsem.at[k])

    def gather_tile(tile):
        tc = jnp.minimum(tile, n_tiles - 1)

        @pl.when((tile < n_tiles) & (tvalid_ref[tc] == 1))
        def _():
            s = lax.rem(tile, MOE_RING)
            _row_gather(n2_hbm, src_ref, tbase_ref[tc], xbuf.at[s], xsem.at[s], MOE_TM)

    @pl.when(i == 0)
    def _():
        for k in range(len(stages)):
            weight_copy(k, te_ref[0]).start(priority=1)
        for ahead in range(MOE_RING - 1):
            gather_tile(ahead)

    @pl.when(tvalid_ref[i] == 1)
    def _():
        gather_tile(i + MOE_RING - 1)

        _row_gather_wait(n2_hbm, xbuf.at[slot], xsem.at[slot], MOE_TM)

        def ffn(weight):
            x = xbuf[slot].astype(BF16)
            hgate = jnp.dot(x, weight(0), preferred_element_type=F32)
            hup = jnp.dot(x, weight(1), preferred_element_type=F32)
            act = (hgate * _sigmoid(hgate) * hup).astype(BF16)
            y_ref[...] = jnp.dot(act, weight(2), preferred_element_type=F32)

        @pl.when(tfirst_ref[i] == 1)
        def _():
            for k in range(len(stages)):
                weight_copy(k, te_ref[i]).wait()

            def cast_and_keep(k):
                _, st, bf = stages[k]
                w = st[...].astype(BF16)
                bf[...] = w
                return w

            ffn(cast_and_keep)

            @pl.when(tnext_ref[i] >= 0)
            def _():
                for k in range(len(stages)):
                    weight_copy(k, tnext_ref[i]).start(priority=1)

        @pl.when(tfirst_ref[i] == 0)
        def _():
            ffn(lambda k: stages[k][2][...])

    @pl.when(tvalid_ref[i] == 0)
    def _():
        y_ref[...] = jnp.zeros(y_ref.shape, F32)


def _moe(n2, w_gate, w_up, w_down, tile_expert, tile_first, tile_valid, tile_next, tile_base, src_tok):
    n_tiles = tile_expert.shape[0]
    hbm = pl.BlockSpec(memory_space=pl.ANY)
    grid_spec = pltpu.PrefetchScalarGridSpec(
        num_scalar_prefetch=6,
        grid=(n_tiles,),
        in_specs=[hbm, hbm, hbm, hbm],
        out_specs=pl.BlockSpec((MOE_TM, D_MODEL), lambda i, te, tf, tv, tn, tb, st: (i, 0)),
        scratch_shapes=[pltpu.VMEM((MOE_RING, MOE_TM, D_MODEL), F32),
                        pltpu.SemaphoreType.DMA((MOE_RING,)),
                        pltpu.VMEM((D_MODEL, D_EXPERT), F32),
                        pltpu.VMEM((D_MODEL, D_EXPERT), F32),
                        pltpu.VMEM((D_EXPERT, D_MODEL), F32),
                        pltpu.SemaphoreType.DMA((3,)),
                        pltpu.VMEM((D_MODEL, D_EXPERT), BF16),
                        pltpu.VMEM((D_MODEL, D_EXPERT), BF16),
                        pltpu.VMEM((D_EXPERT, D_MODEL), BF16)],
    )
    return pl.pallas_call(
        _moe_kernel,
        grid_spec=grid_spec,
        out_shape=jax.ShapeDtypeStruct((n_tiles * MOE_TM, D_MODEL), F32),
        compiler_params=_cparams(("arbitrary",)),
        name="moe",
    )(tile_expert, tile_first, tile_valid, tile_next, tile_base, src_tok, n2, w_gate, w_up, w_down)


def _route_plan(e1, e2, n_tok):
    experts = jnp.arange(N_EXPERTS, dtype=jnp.int32)
    flat_e = jnp.concatenate([e1, e2])
    onehot = (flat_e[:, None] == experts[None, :]).astype(jnp.int32)
    blk = onehot.astype(F32).reshape(-1, LANES, N_EXPERTS)
    tri = (jnp.arange(LANES)[:, None] >= jnp.arange(LANES)[None, :]).astype(F32)
    within = jnp.einsum("ij,bje->bie", tri, blk)
    before = jnp.cumsum(within[:, -1, :], axis=0) - within[:, -1, :]
    csum = (within + before[:, None, :]).reshape(-1, N_EXPERTS).astype(jnp.int32)
    rank = jnp.sum(csum * onehot, axis=1) - 1
    counts = csum[-1]
    tiles_per = (counts + MOE_TM - 1) // MOE_TM
    tile_end = jnp.cumsum(tiles_per)
    tile_off = tile_end - tiles_per
    slot = jnp.sum(onehot * tile_off[None, :], axis=1) * MOE_TM + rank
    n_tiles = (2 * n_tok) // MOE_TM + N_EXPERTS
    tok = jnp.tile(jnp.arange(n_tok, dtype=jnp.int32), 2)
    _, src_tok = lax.sort((slot, tok), num_keys=1)
    src_tok = jnp.concatenate([src_tok, jnp.zeros((MOE_TM,), jnp.int32)])
    count_off = jnp.cumsum(counts) - counts
    tile_id = jnp.arange(n_tiles, dtype=jnp.int32)
    n_used = tile_end[-1]
    tile_valid = (tile_id < n_used).astype(jnp.int32)
    te = jnp.sum((tile_end[None, :] <= jnp.minimum(tile_id, n_used - 1)[:, None]).astype(jnp.int32), axis=1)
    tile_expert = jnp.minimum(te, N_EXPERTS - 1)
    prev = jnp.concatenate([jnp.full((1,), -1, jnp.int32), tile_expert[:-1]])
    tile_first = (tile_expert != prev).astype(jnp.int32)
    oh_te = (tile_expert[:, None] == experts[None, :]).astype(jnp.int32)

    def of_expert(table):
        return jnp.sum(oh_te * table[None, :], axis=1)

    next_tile = of_expert(tile_end)
    oh_next = (tile_id[None, :] == jnp.minimum(next_tile, n_tiles - 1)[:, None]).astype(jnp.int32)
    tile_next = jnp.where(next_tile < n_used, jnp.sum(oh_next * tile_expert[None, :], axis=1), -1)
    tile_base = jnp.where(tile_valid == 1, of_expert(count_off) + (tile_id - of_expert(tile_off)) * MOE_TM, 0)
    return tile_expert, tile_first, tile_valid, tile_next, tile_base, src_tok, slot[:n_tok], slot[n_tok:]


def _final_kernel(p1_ref, p2_ref, ys_hbm, x1_ref, g2_ref, rt_ref, fg_ref, yp_ref, ysm_ref, ybuf, sem,
                  *, n_prompt_tiles):
    i = pl.program_id(0)
    n_tiles = pl.num_programs(0)
    slot = i % 2

    def start(tile, s):
        _row_gather(ys_hbm, p1_ref, tile * ROW_TILE, ybuf.at[s, 0], sem.at[s], ROW_TILE)
        _row_gather(ys_hbm, p2_ref, tile * ROW_TILE, ybuf.at[s, 1], sem.at[s], ROW_TILE)

    @pl.when(i == 0)
    def _():
        start(0, 0)

    @pl.when(i + 1 < n_tiles)
    def _():
        start(i + 1, 1 - slot)

    _row_gather_wait(ys_hbm, ybuf.at[slot, 0], sem.at[slot], ROW_TILE)
    _row_gather_wait(ys_hbm, ybuf.at[slot, 1], sem.at[slot], ROW_TILE)
    rt = rt_ref[...]
    moe = rt[:, 2:3] * ybuf[slot, 0] + rt[:, 3:4] * ybuf[slot, 1]
    x2 = x1_ref[...] + _bcast_rows(g2_ref[...], ROW_TILE) * moe
    y = x2 * lax.rsqrt(jnp.mean(x2 * x2, axis=-1, keepdims=True) + EPS) * fg_ref[...]

    @pl.when(i < n_prompt_tiles)
    def _():
        yp_ref[...] = y

    @pl.when(i >= n_prompt_tiles)
    def _():
        ysm_ref[...] = y


def _final(ys, x1, g2, rt, final_g, p1, p2, n_prompt_rows):
    t = x1.shape[0]
    tm = ROW_TILE
    seg = tm // MOD_ROWS
    n_p = n_prompt_rows // tm
    grid_spec = pltpu.PrefetchScalarGridSpec(
        num_scalar_prefetch=2,
        grid=(t // tm,),
        in_specs=[pl.BlockSpec(memory_space=pl.ANY),
                  pl.BlockSpec((tm, D_MODEL), lambda i, a, b: (i, 0)),
                  pl.BlockSpec((seg, D_MODEL), lambda i, a, b: (i, 0)),
                  pl.BlockSpec((tm, ROUTE_LANES), lambda i, a, b: (i, 0)),
                  pl.BlockSpec((1, D_MODEL), lambda i, a, b: (0, 0))],
        out_specs=[pl.BlockSpec((tm, D_MODEL), lambda i, a, b: (jnp.minimum(i, n_p - 1), 0)),
                   pl.BlockSpec((tm, D_MODEL), lambda i, a, b: (jnp.maximum(i - n_p, 0), 0))],
        scratch_shapes=[pltpu.VMEM((2, 2, tm, D_MODEL), F32),
                        pltpu.SemaphoreType.DMA((2,))],
    )
    return pl.pallas_call(
        functools.partial(_final_kernel, n_prompt_tiles=n_p),
        grid_spec=grid_spec,
        out_shape=[jax.ShapeDtypeStruct((n_prompt_rows, D_MODEL), F32),
                   jax.ShapeDtypeStruct((t - n_prompt_rows, D_MODEL), F32)],
        compiler_params=_cparams(("arbitrary",)),
        name="final",
    )(p1, p2, ys, x1, g2, rt, final_g.reshape(1, D_MODEL))


def kernel(x_prompt, x_sample, cache_k_win, cache_v_win, state_conv, state_rglru, c_prompt, c_sample, w_ada, b_ada, norm1_g, norm2_g, w_in, b_in, w_conv, b_conv, w_rg_a, b_rg_a, w_rg_x, b_rg_x, lru_lambda, w_rnn_out, w_attn_out, w_out, attn_sinks, w_route_group, b_route_group, w_route_expert, b_route_expert, w_exp_gate, w_exp_up, w_exp_down, rel_bias_table, final_norm_g):
    n_b, seq, _ = x_prompt.shape
    d_b, d_seq, _ = x_sample.shape
    assert w_ada.shape[0] == 1, "single trunk layer"
    assert seq % (ATT_CPB * CHUNK) == 0 and seq % MOD_ROWS == 0 and d_seq == MOD_ROWS
    assert d_seq >= CONV_W - 1 and d_b == N_CHAINS
    t_p, t_s = n_b * seq, d_b * d_seq
    t = t_p + t_s
    assert t_s == ROW_TILE and t_p % ROW_TILE == 0
    cw = cache_k_win.shape[2]
    l = 0

    x_p = x_prompt.reshape(t_p, D_MODEL)
    x_s = x_sample.reshape(t_s, D_MODEL)

    n_c = n_b + d_b
    c_all = jnp.pad(jnp.concatenate([c_prompt, c_sample], axis=0), ((0, -n_c % SUBLANES), (0, 0)))
    mod = _ada(c_all, w_ada[l], b_ada[l])

    def per_segment(m):
        return jnp.concatenate([jnp.repeat(m[:n_b], seq // MOD_ROWS, axis=0),
                                jnp.repeat(m[n_b:n_c], d_seq // MOD_ROWS, axis=0)], axis=0)

    sh1, sc1, g1, sh2, sc2, g2 = [per_segment(m) for m in jnp.split(mod, 6, axis=-1)]

    z = _inproj(_norm1(x_p, x_s, sc1, sh1, norm1_g[l]), w_in[l], b_in[l])

    w_gate = jnp.concatenate([w_rg_a[l], w_rg_x[l]], axis=-1)
    b_gate = jnp.concatenate([b_rg_a[l], b_rg_x[l]], axis=-1)[:, None, :]
    hg_p, hl_p = _rglru(z, jnp.zeros((n_b, SUBLANES, D_RNN), F32), jnp.zeros((n_b, N_CHAINS, D_RNN), F32),
                        w_conv[l], b_conv[l], w_gate, b_gate, lru_lambda[l],
                        n_blocks=n_b, n_seq=1, seq_len=seq, row_block0=0, link=True)
    state8 = jnp.pad(state_conv[l], ((0, 0), (SUBLANES - (CONV_W - 1), 0), (0, 0)))
    hg_s, hl_s = _rglru(z, state8, state_rglru[l][None], w_conv[l], b_conv[l], w_gate, b_gate, lru_lambda[l],
                        n_blocks=1, n_seq=d_b, seq_len=d_seq, row_block0=t_p // t_s, link=False)

    sinks = attn_sinks[l]
    bkt_p = _bucket_map(WIN_CHUNKS * CHUNK + jnp.arange(CHUNK), jnp.arange(SPAN))
    o_p = _attn_band(z, rel_bias_table, sinks, bkt_p, n_batch=n_b, seq_len=seq)
    k_new = z[t_p:, COL_K:COL_K + KV_DIM].reshape(d_b, d_seq, KV_DIM)
    v_new = z[t_p:, COL_V:COL_V + KV_DIM].reshape(d_b, d_seq, KV_DIM)
    k_all = jnp.concatenate([cache_k_win[l].reshape(d_b, cw, KV_DIM), k_new], axis=1)
    v_all = jnp.concatenate([cache_v_win[l].reshape(d_b, cw, KV_DIM), v_new], axis=1)
    bkt_s = _bucket_map(cw + jnp.arange(d_seq), jnp.arange(cw + d_seq))
    o_s = _attn_step(z, k_all, v_all, rel_bias_table, sinks, bkt_s, n_batch=d_b, n_q=d_seq, row_block0=t_p // d_seq)

    n_route = N_GROUPS + N_EXPERTS
    w_route = jnp.pad(jnp.concatenate([w_route_group[l], w_route_expert[l]], axis=1),
                      ((0, 0), (0, ROUTE_LANES - n_route)))
    b_route = jnp.pad(jnp.concatenate([b_route_group[l], b_route_expert[l]]),
                      (0, ROUTE_LANES - n_route)).reshape(1, ROUTE_LANES)
    x1, n2, rt = _outproj(hg_p, hg_s, o_p, o_s, z, x_p, x_s, g1, sh2, sc2, norm2_g[l],
                          w_rnn_out[l].astype(BF16), w_attn_out[l].astype(BF16), w_out[l].astype(BF16),
                          w_route, b_route)

    e1 = rt[:, 0].astype(jnp.int32)
    e2 = rt[:, 1].astype(jnp.int32)
    tile_expert, tile_first, tile_valid, tile_next, tile_base, src_tok, p1, p2 = _route_plan(e1, e2, t)
    ys = _moe(n2, w_exp_gate[l], w_exp_up[l], w_exp_down[l], tile_expert, tile_first, tile_valid, tile_next,
              tile_base, src_tok)
    y_p, y_s = _final(ys, x1, g2, rt, final_norm_g, p1, p2, t_p)

    win = min(WINDOW, seq)

    def tail(col0, width, n_rows):
        return jnp.stack([z[(b + 1) * seq - n_rows:(b + 1) * seq, col0:col0 + width] for b in range(n_b)])

    kp = tail(COL_K, KV_DIM, win).reshape(n_b, win, N_KV_HEADS, HEAD_DIM)
    vp = tail(COL_V, KV_DIM, win).reshape(n_b, win, N_KV_HEADS, HEAD_DIM)
    cp = tail(COL_XR, D_RNN, CONV_W - 1)
    rp = hl_p[:, N_CHAINS - 1, :]
    ks = k_all[:, -cw:].reshape(d_b, cw, N_KV_HEADS, HEAD_DIM)
    vs = v_all[:, -cw:].reshape(d_b, cw, N_KV_HEADS, HEAD_DIM)
    cs = z[t_p:, COL_XR:COL_XR + D_RNN].reshape(d_b, d_seq, D_RNN)[:, -(CONV_W - 1):]
    rs = hl_s[0]
    return (y_p.reshape(n_b, seq, D_MODEL), y_s.reshape(d_b, d_seq, D_MODEL),
            kp[None], vp[None], cp[None], rp[None], ks[None], vs[None], cs[None], rs[None])
```
